```python
import math
import jax, jax.numpy as jnp
from jax import lax
import numpy as np

D_MODEL = 1024
BATCH = 8
SEQ = 2048
DEPTH = 2
DEC_BATCH = 128
DEC_SEQ = 8
PAST_LEN = 16384
PAGE_SIZE = 128

MIX_W = D_MODEL
GROUP_W = MIX_W // 4
HEAD_DIM = 64
N_HEADS = GROUP_W // HEAD_DIM
RET_CHUNK = 64
GDN_CHUNK = 64
CONV_W = 4
LRU_C = 8.0
POOL_WINDOWS = (2, 4, 8, 16)
POOL_GROUP = GROUP_W // len(POOL_WINDOWS)
POOL_BUF = max(POOL_WINDOWS) - 1
N_GROUPS = 4
EXPERTS_PER_GROUP = 8
N_EXPERTS = N_GROUPS * EXPERTS_PER_GROUP
TOP_K = 2
D_EXPERT = 512
ROPE_BASE = 10000.0
EPS = 1e-6
SPLIT_SIZES = (GROUP_W, GROUP_W, GROUP_W, GROUP_W, 3 * GROUP_W, GROUP_W, N_HEADS, N_HEADS, GROUP_W, GROUP_W, GROUP_W)
N_IN = sum(SPLIT_SIZES)

kernel_name = 'hybrid_ret_gdn_lru_pool_hmoe'


def _rmsnorm(x, g):
    xf = x.astype(jnp.float32)
    y = xf * lax.rsqrt(jnp.mean(xf * xf, axis=-1, keepdims=True) + EPS)
    return (y * g.astype(jnp.float32)).astype(x.dtype)


def _l2norm(x):
    return x * lax.rsqrt(jnp.sum(x * x, axis=-1, keepdims=True) + EPS)


def _rotary(x, pos):
    half = x.shape[-1] // 2
    inv = ROPE_BASE ** (-jnp.arange(half, dtype=jnp.float32) / half)
    ang = pos.astype(jnp.float32)[:, None] * inv[None, :]
    cos = jnp.cos(ang)[None, :, None, :]
    sin = jnp.sin(ang)[None, :, None, :]
    x1, x2 = x[..., :half], x[..., half:]
    return jnp.concatenate([x1 * cos - x2 * sin, x1 * sin + x2 * cos], axis=-1)


def _causal_conv(x, buf, w):
    L = x.shape[1]
    xp = jnp.concatenate([buf, x], axis=1)
    y = sum(xp[:, i:i + L] * w[i] for i in range(CONV_W))
    return y, xp[:, -(CONV_W - 1):]


def _chunk(L, c):
    return c if L % c == 0 else L


def _to_blocks(t, chunk):
    B, L, H = t.shape[:3]
    t = t.reshape((B, L // chunk, chunk, H) + t.shape[3:])
    return jnp.swapaxes(jnp.moveaxis(t, 1, 0), 2, 3)


def _from_blocks(t):
    n, B, H, C, D = t.shape
    return jnp.moveaxis(jnp.swapaxes(t, 2, 3), 0, 1).reshape(B, n * C, H, D)


def _retention(q, k, v, s0, chunk):
    H = q.shape[2]
    f32 = jnp.float32
    log_g = jnp.log1p(-jnp.exp2(-5.0 - jnp.arange(H, dtype=f32)))
    idx = jnp.arange(chunk, dtype=f32)
    diff = idx[:, None] - idx[None, :]
    intra_decay = jnp.where(diff >= 0, jnp.exp(log_g[:, None, None] * jnp.maximum(diff, 0.0)), 0.0)
    q_decay = jnp.exp(log_g[:, None] * (idx + 1.0))[None, :, :, None]
    k_decay = jnp.exp(log_g[:, None] * (chunk - 1.0 - idx))[None, :, :, None]
    s_decay = jnp.exp(log_g * chunk)[None, :, None, None]

    def step(s, blk):
        qb, kb, vb = blk
        scores = jnp.einsum('bhqd,bhkd->bhqk', qb, kb) * intra_decay
        o = jnp.einsum('bhqk,bhkv->bhqv', scores, vb) + jnp.einsum('bhqd,bhdv->bhqv', qb, s) * q_decay
        s = s * s_decay + jnp.einsum('bhkd,bhkv->bhdv', kb * k_decay, vb)
        return s, o

    s, o = lax.scan(step, s0, (_to_blocks(q, chunk), _to_blocks(k, chunk), _to_blocks(v, chunk)))
    return _from_blocks(o), s


def _gated_delta(q, k, v, log_a, beta, s0, chunk):
    dv = v.shape[-1]
    idx = jnp.arange(chunk)
    incl = idx[:, None] >= idx[None, :]
    strict = idx[:, None] > idx[None, :]
    eye = jnp.eye(chunk, dtype=jnp.float32)

    def step(s, blk):
        qb, kb, vb, gb, bb = blk
        gc = jnp.cumsum(gb, axis=-1)
        rel = gc[..., :, None] - gc[..., None, :]
        dec = jnp.where(incl, jnp.exp(jnp.where(incl, rel, 0.0)), 0.0)
        a_mat = jnp.where(strict, bb[..., :, None] * dec * jnp.einsum('bhtd,bhsd->bhts', kb, kb), 0.0)
        rhs = jnp.concatenate([bb[..., None] * vb, (bb * jnp.exp(gc))[..., None] * kb], axis=-1)
        sol = lax.linalg.triangular_solve(eye + a_mat, rhs, left_side=True, lower=True, unit_diagonal=True)
        u = sol[..., :dv] - jnp.einsum('bhtk,bhkv->bhtv', sol[..., dv:], s)
        qk = jnp.einsum('bhtd,bhsd->bhts', qb, kb) * dec
        o = jnp.exp(gc)[..., None] * jnp.einsum('bhtk,bhkv->bhtv', qb, s) + jnp.einsum('bhts,bhsv->bhtv', qk, u)
        g_last = gc[..., -1:]
        s = jnp.exp(g_last)[..., None] * s + jnp.einsum('bhsk,bhsv->bhkv', jnp.exp(g_last - gc)[..., None] * kb, u)
        return s, o

    blks = (_to_blocks(q, chunk), _to_blocks(k, chunk), _to_blocks(v, chunk),
            _to_blocks(log_a, chunk), _to_blocks(beta, chunk))
    s, o = lax.scan(step, s0, blks)
    return _from_blocks(o), s


def _linear_scan(a, b, h0):
    b = b.at[:, 0].add(a[:, 0] * h0)

    def comb(l, r):
        return (l[0] * r[0], r[0] * l[1] + r[1])

    _, h = lax.associative_scan(comb, (a, b), axis=1)
    return h


def _multiscale_pool(u, buf, pos, w, b, scale):
    B, L, C = u.shape
    uu = jnp.concatenate([buf, u], axis=1)
    cs = jnp.concatenate([jnp.zeros((B, 1, C), u.dtype), lax.cumsum(uu, axis=1)], axis=1)
    end = cs[:, POOL_BUF + 1:]
    means = []
    for gi, win in enumerate(POOL_WINDOWS):
        sl = slice(gi * POOL_GROUP, (gi + 1) * POOL_GROUP)
        start = cs[:, POOL_BUF + 1 - win:POOL_BUF + 1 - win + L, sl]
        cnt = jnp.minimum(pos + 1, win).astype(u.dtype)[None, :, None]
        means.append((end[..., sl] - start) / cnt)
    pooled = (jnp.concatenate(means, axis=-1) - u).reshape(B, L, len(POOL_WINDOWS), POOL_GROUP)
    y = jnp.einsum('blgc,gcd->blgd', pooled, w).reshape(B, L, C) + b
    return y * scale, uu[:, -POOL_BUF:]


def _hier_moe(h, wg, bg, we, be, w_gate, w_up, w_down):
    B, L, D = h.shape
    t = h.reshape(B * L, D)
    glog = (t @ wg + bg).astype(jnp.float32)
    gprob = jax.nn.softmax(glog, axis=-1)
    gsel = jnp.argmax(glog, axis=-1)
    p_g = jnp.take_along_axis(gprob, gsel[:, None], axis=-1)[:, 0]
    elog = (jnp.einsum('td,gde->tge', t, we) + be).astype(jnp.float32)
    elog = jnp.take_along_axis(elog, gsel[:, None, None], axis=1)[:, 0]
    top_v, top_i = lax.top_k(elog, TOP_K)
    top_w = jax.nn.softmax(top_v, axis=-1) * p_g[:, None]
    eid = gsel[:, None] * EXPERTS_PER_GROUP + top_i
    comb = jnp.sum(jax.nn.one_hot(eid, N_EXPERTS, dtype=jnp.float32) * top_w[..., None], axis=1).astype(t.dtype)
    out = jnp.zeros_like(t)
    for e in range(N_EXPERTS):
        he = jax.nn.silu(t @ w_gate[e]) * (t @ w_up[e])
        out = out + comb[:, e:e + 1] * (he @ w_down[e])
    return out.reshape(B, L, D)


def _layer(x, pos, st, p):
    (s_ret, s_gdn, s_gconv, s_lru, s_lconv, s_pool) = st
    (norm1_g, w_in, ret_gn_g, gdn_conv_w, gdn_a_log, gdn_dt_bias, gdn_norm_g,
     lru_conv_w, lru_conv_b, lru_wa, lru_ba, lru_wx, lru_bx, lru_lambda,
     pool_w, pool_b, pool_scale, w_out, norm2_g,
     router_group_w, router_group_b, router_expert_w, router_expert_b,
     moe_w_gate, moe_w_up, moe_w_down) = p
    f32 = jnp.float32
    B, L, _ = x.shape
    h = _rmsnorm(x, norm1_g)
    z = (h @ w_in).astype(f32)
    cuts = np.cumsum(SPLIT_SIZES)[:-1].tolist()
    (ret_q, ret_k, ret_v, ret_g, gdn_qkv, gdn_g, gdn_a, gdn_b,
     lru_x, lru_gate, pool_x) = jnp.split(z, cuts, axis=-1)

    def heads(t):
        return t.reshape(B, L, N_HEADS, HEAD_DIM)

    rq = _rotary(heads(ret_q), pos)
    rk = _rotary(heads(ret_k), pos) * HEAD_DIM ** -0.5
    ro, ret_new = _retention(rq, rk, heads(ret_v), s_ret.astype(f32), _chunk(L, RET_CHUNK))
    mu = jnp.mean(ro, axis=-1, keepdims=True)
    var = jnp.mean(jnp.square(ro - mu), axis=-1, keepdims=True)
    ro = ((ro - mu) * lax.rsqrt(var + EPS)).reshape(B, L, GROUP_W) * ret_gn_g.astype(f32) * jax.nn.silu(ret_g)

    qkv, gconv_new = _causal_conv(gdn_qkv, s_gconv.astype(f32), gdn_conv_w.astype(f32))
    qkv = jax.nn.silu(qkv)
    gq, gk, gv = jnp.split(qkv, 3, axis=-1)
    gq = _l2norm(heads(gq)) * HEAD_DIM ** -0.5
    gk = _l2norm(heads(gk))
    beta = jax.nn.sigmoid(gdn_b)
    log_a = -jnp.exp(gdn_a_log.astype(f32)) * jax.nn.softplus(gdn_a + gdn_dt_bias.astype(f32))
    go, gdn_new = _gated_delta(gq, gk, heads(gv), log_a, beta, s_gdn.astype(f32), _chunk(L, GDN_CHUNK))
    go = _rmsnorm(go, gdn_norm_g).reshape(B, L, GROUP_W) * jax.nn.silu(gdn_g)

    xc, lconv_new = _causal_conv(lru_x, s_lconv.astype(f32), lru_conv_w.astype(f32))
    xc = xc + lru_conv_b.astype(f32)
    xb = xc.reshape(B, L, N_HEADS, HEAD_DIM)
    r = jax.nn.sigmoid(jnp.einsum('blhi,hij->blhj', xb, lru_wa.astype(f32)).reshape(B, L, GROUP_W) + lru_ba.astype(f32))
    i = jax.nn.sigmoid(jnp.einsum('blhi,hij->blhj', xb, lru_wx.astype(f32)).reshape(B, L, GROUP_W) + lru_bx.astype(f32))
    log_at = -LRU_C * r * jax.nn.softplus(-lru_lambda.astype(f32))
    hs = _linear_scan(jnp.exp(log_at), jnp.sqrt(-jnp.expm1(2.0 * log_at)) * (i * xc), s_lru.astype(f32))
    lo = hs * jax.nn.gelu(lru_gate)
    lru_new = hs[:, -1]

    po, pool_new = _multiscale_pool(pool_x, s_pool.astype(f32), pos, pool_w.astype(f32),
                                    pool_b.astype(f32), pool_scale.astype(f32))

    mix = jnp.concatenate([ro, go, lo, po], axis=-1).astype(x.dtype)
    x = x + mix @ w_out
    x = x + _hier_moe(_rmsnorm(x, norm2_g), router_group_w, router_group_b, router_expert_w,
                      router_expert_b, moe_w_gate, moe_w_up, moe_w_down)
    dt = x.dtype
    new_st = (ret_new.astype(dt), gdn_new.astype(dt), gconv_new.astype(dt), lru_new.astype(dt),
              lconv_new.astype(dt), pool_new.astype(dt))
    return x, new_st


def _trunk(x, pos, states, params):
    news = []
    for l in range(DEPTH):
        st = tuple(s[l] for s in states)
        p = tuple(w[l] for w in params)
        x, ns = _layer(x, pos, st, p)
        news.append(ns)
    stacked = [jnp.stack([ns[j] for ns in news], axis=0) for j in range(len(states))]
    return x, stacked


def setup_inputs(seed: int = 0) -> dict:
    key = jax.random.key(seed)
    ks = jax.random.split(key, 40)
    f32 = jnp.float32

    def nrm(i, shape, s):
        return jax.random.normal(ks[i], shape, f32) * s

    def uni(i, shape, lo, hi):
        return jax.random.uniform(ks[i], shape, f32, lo, hi)

    D, G, H, HD = D_MODEL, GROUP_W, N_HEADS, HEAD_DIM
    a0 = uni(20, (DEPTH, G), 0.9, 0.999) ** (1.0 / LRU_C)
    dtv = jnp.exp(uni(13, (DEPTH, H), math.log(1e-3), math.log(0.1)))
    return {
        'x_prompt': nrm(0, (BATCH, SEQ, D), 1.0),
        'x_sample': nrm(1, (DEC_BATCH, DEC_SEQ, D), 1.0),
        'state_ret': nrm(2, (DEPTH, DEC_BATCH, H, HD, HD), 0.5),
        'state_gdn': nrm(3, (DEPTH, DEC_BATCH, H, HD, HD), 0.5),
        'state_gdn_conv': nrm(4, (DEPTH, DEC_BATCH, CONV_W - 1, 3 * G), 1.0),
        'state_lru': nrm(5, (DEPTH, DEC_BATCH, G), 0.5),
        'state_lru_conv': nrm(6, (DEPTH, DEC_BATCH, CONV_W - 1, G), 1.0),
        'state_pool': nrm(7, (DEPTH, DEC_BATCH, POOL_BUF, G), 1.0),
        'norm1_g': 1.0 + nrm(8, (DEPTH, D), 0.05),
        'w_in': nrm(9, (DEPTH, D, N_IN), D ** -0.5),
        'ret_gn_g': 1.0 + nrm(10, (DEPTH, G), 0.05),
        'gdn_conv_w': nrm(11, (DEPTH, CONV_W, 3 * G), 0.5),
        'gdn_a_log': jnp.log(uni(12, (DEPTH, H), 1.0, 16.0)),
        'gdn_dt_bias': dtv + jnp.log(-jnp.expm1(-dtv)),
        'gdn_norm_g': 1.0 + nrm(14, (DEPTH, HD), 0.05),
        'lru_conv_w': nrm(15, (DEPTH, CONV_W, G), 0.5),
        'lru_conv_b': nrm(16, (DEPTH, G), 0.02),
        'lru_wa': nrm(17, (DEPTH, H, HD, HD), HD ** -0.5),
        'lru_ba': nrm(18, (DEPTH, G), 0.02),
        'lru_wx': nrm(19, (DEPTH, H, HD, HD), HD ** -0.5),
        'lru_bx': nrm(21, (DEPTH, G), 0.02),
        'lru_lambda': jnp.log(a0) - jnp.log1p(-a0),
        'pool_w': nrm(22, (DEPTH, len(POOL_WINDOWS), POOL_GROUP, POOL_GROUP), POOL_GROUP ** -0.5),
        'pool_b': nrm(23, (DEPTH, G), 0.02),
        'pool_scale': 1.0 + nrm(24, (DEPTH, G), 0.05),
        'w_out': nrm(25, (DEPTH, MIX_W, D), MIX_W ** -0.5),
        'norm2_g': 1.0 + nrm(26, (DEPTH, D), 0.05),
        'router_group_w': nrm(27, (DEPTH, D, N_GROUPS), D ** -0.5),
        'router_group_b': nrm(28, (DEPTH, N_GROUPS), 0.01),
        'router_expert_w': nrm(29, (DEPTH, N_GROUPS, D, EXPERTS_PER_GROUP), D ** -0.5),
        'router_expert_b': nrm(30, (DEPTH, N_GROUPS, EXPERTS_PER_GROUP), 0.01),
        'moe_w_gate': nrm(31, (DEPTH, N_EXPERTS, D, D_EXPERT), D ** -0.5),
        'moe_w_up': nrm(32, (DEPTH, N_EXPERTS, D, D_EXPERT), D ** -0.5),
        'moe_w_down': nrm(33, (DEPTH, N_EXPERTS, D_EXPERT, D), D_EXPERT ** -0.5),
        'final_g': 1.0 + nrm(34, (D,), 0.05),
    }


def reference(x_prompt, x_sample, state_ret, state_gdn, state_gdn_conv, state_lru, state_lru_conv, state_pool,
              norm1_g, w_in, ret_gn_g, gdn_conv_w, gdn_a_log, gdn_dt_bias, gdn_norm_g,
              lru_conv_w, lru_conv_b, lru_wa, lru_ba, lru_wx, lru_bx, lru_lambda,
              pool_w, pool_b, pool_scale, w_out, norm2_g,
              router_group_w, router_group_b, router_expert_w, router_expert_b,
              moe_w_gate, moe_w_up, moe_w_down, final_g):
    params = (norm1_g, w_in, ret_gn_g, gdn_conv_w, gdn_a_log, gdn_dt_bias, gdn_norm_g,
              lru_conv_w, lru_conv_b, lru_wa, lru_ba, lru_wx, lru_bx, lru_lambda,
              pool_w, pool_b, pool_scale, w_out, norm2_g,
              router_group_w, router_group_b, router_expert_w, router_expert_b,
              moe_w_gate, moe_w_up, moe_w_down)
    bp, dt = x_prompt.shape[0], x_prompt.dtype
    zero_states = (jnp.zeros((DEPTH, bp) + state_ret.shape[2:], dt),
                   jnp.zeros((DEPTH, bp) + state_gdn.shape[2:], dt),
                   jnp.zeros((DEPTH, bp) + state_gdn_conv.shape[2:], dt),
                   jnp.zeros((DEPTH, bp) + state_lru.shape[2:], dt),
                   jnp.zeros((DEPTH, bp) + state_lru_conv.shape[2:], dt),
                   jnp.zeros((DEPTH, bp) + state_pool.shape[2:], dt))
    pos_p = jnp.arange(x_prompt.shape[1], dtype=jnp.int32)
    pos_s = PAST_LEN + jnp.arange(x_sample.shape[1], dtype=jnp.int32)
    yp, (ret_p, gdn_p, gdn_conv_p, lru_p, lru_conv_p, pool_p) = _trunk(x_prompt, pos_p, zero_states, params)
    ys, (ret_s, gdn_s, gdn_conv_s, lru_s, lru_conv_s, pool_s) = _trunk(
        x_sample, pos_s, (state_ret, state_gdn, state_gdn_conv, state_lru, state_lru_conv, state_pool), params)
    y_prompt = _rmsnorm(yp, final_g)
    y_sample = _rmsnorm(ys, final_g)
    return (y_prompt, y_sample, ret_p, gdn_p, gdn_conv_p, lru_p, lru_conv_p, pool_p,
            ret_s, gdn_s, gdn_conv_s, lru_s, lru_conv_s, pool_s)
```

```python
import functools

import jax
import jax.numpy as jnp
from jax import lax
from jax.experimental import pallas as pl
from jax.experimental.pallas import tpu as pltpu

F32 = jnp.float32
BF16 = jnp.bfloat16

EPS = 1e-6
HEAD_DIM = 64
N_HEADS = 4
GROUP_W = 256
CONV_W = 4
LRU_C = 8.0
POOL_WINDOWS = (2, 4, 8, 16)
POOL_BUF = 15
ROPE_BASE = 10000.0
N_GROUPS = 4
EXPERTS_PER_GROUP = 8
N_EXPERTS = 32
PAST_LEN = 16384

LANES = 128
HIST = 16
Z_AB = 11 * GROUP_W
NZ = Z_AB + LANES
INV_BASE = 16
INV_PASSES = 3
VMEM_LIMIT = 56 * 1024 * 1024

NN = (((1,), (0,)), ((), ()))
NT = (((1,), (1,)), ((), ()))
TN = (((0,), (0,)), ((), ()))


def _mm(a, b, dims=NN, passes=1):
    ah = a.astype(BF16)
    bh = b.astype(BF16)
    dot = functools.partial(lax.dot_general, dimension_numbers=dims, preferred_element_type=F32)
    if passes == 1:
        return dot(ah, bh)
    al = (a - ah.astype(F32)).astype(BF16)
    bl = (b - bh.astype(F32)).astype(BF16)
    return dot(ah, bh) + (dot(ah, bl) + dot(al, bh))


def _sigmoid(x):
    return 1.0 / (1.0 + jnp.exp(-x))


def _silu(x):
    return x * _sigmoid(x)


def _softplus(x):
    return jnp.maximum(x, 0.0) + jnp.log1p(jnp.exp(-jnp.abs(x)))


def _gelu_tanh(x):
    return 0.5 * x * (1.0 + jnp.tanh(0.7978845608028654 * (x + 0.044715 * (x * x * x))))


def _iota(shape, axis):
    return lax.broadcasted_iota(jnp.int32, shape, axis)


def _shift_rows(x, d, fill):
    return jnp.where(_iota(x.shape, 0) >= d, pltpu.roll(x, d, 0), fill)


def _cumsum_rows(x):
    d = 1
    while d < x.shape[0]:
        x = x + _shift_rows(x, d, 0.0)
        d *= 2
    return x


def _swap_half_heads(x):
    n = x.shape[1]
    lower = (_iota(x.shape, 1) & (HEAD_DIM - 1)) < HEAD_DIM // 2
    return jnp.where(lower, pltpu.roll(x, n - HEAD_DIM // 2, 1), pltpu.roll(x, HEAD_DIM // 2, 1))


def _inv_unit_lower(a):
    c = a.shape[0]
    row = _iota(a.shape, 0)
    col = _iota(a.shape, 1)
    nb = min(INV_BASE, c)
    shift = nb.bit_length() - 1
    d = jnp.where((row >> shift) == (col >> shift), a, 0.0)
    x = jnp.where(row == col, 1.0, 0.0) - d
    p = d
    k = 2
    while k < nb:
        p = _mm(p, p, passes=INV_PASSES)
        x = x + _mm(x, p, passes=INV_PASSES)
        k *= 2
    m = nb
    while m < c:
        s1 = m.bit_length() - 1
        off = jnp.where(((row >> (s1 + 1)) == (col >> (s1 + 1))) & ((row >> s1) != (col >> s1)), a, 0.0)
        x = x - _mm(_mm(x, off, passes=INV_PASSES), x, passes=INV_PASSES)
        m *= 2
    return x


def _norm_win_kernel(x_ref, g_ref, w_ref, z_ref):
    x = x_ref[...]
    h = x * lax.rsqrt(jnp.mean(x * x, axis=-1, keepdims=True) + EPS) * g_ref[...]
    z_ref[...] = jnp.dot(h.astype(BF16), w_ref[...], preferred_element_type=F32)


def _norm_win(x, g, w_bf16, tm):
    t, d = x.shape
    nz = w_bf16.shape[1]
    return pl.pallas_call(
        _norm_win_kernel,
        grid=(t // tm,),
        in_specs=[pl.BlockSpec((tm, d), lambda i: (i, 0)),
                  pl.BlockSpec((1, d), lambda i: (0, 0)),
                  pl.BlockSpec((d, nz), lambda i: (0, 0))],
        out_specs=pl.BlockSpec((tm, nz), lambda i: (i, 0)),
        out_shape=jax.ShapeDtypeStruct((t, nz), F32),
        compiler_params=pltpu.CompilerParams(dimension_semantics=("arbitrary",), vmem_limit_bytes=VMEM_LIMIT),
        name="norm_win",
    )(x, g, w_bf16)


def _retention_tile(zq, zk, zv, cos, sin, intra_ref, qdec, kdec, sdec, s_ref):
    rq = zq * cos + _swap_half_heads(zq) * sin
    rk = (zk * cos + _swap_half_heads(zk) * sin) * (HEAD_DIM ** -0.5)
    kd = rk * kdec
    outs = []
    for h in range(N_HEADS):
        sl = slice(h * HEAD_DIM, (h + 1) * HEAD_DIM)
        q, k, v = rq[:, sl], rk[:, sl], zv[:, sl]
        s = s_ref[h]
        scores = _mm(q, k, NT) * intra_ref[h]
        o = _mm(scores, v) + _mm(q, s) * qdec[:, sl]
        s_ref[h] = s * sdec[h:h + 1, :HEAD_DIM] + _mm(kd[:, sl], v, TN)
        mu = jnp.mean(o, axis=-1, keepdims=True)
        oc = o - mu
        var = jnp.mean(oc * oc, axis=-1, keepdims=True)
        outs.append(oc * lax.rsqrt(var + EPS))
    return jnp.concatenate(outs, axis=-1)


def _head_lane_select(c, h):
    return jnp.where(_iota((c, LANES), 1) == h, 1.0, 0.0).astype(BF16)


def _split3(x):
    x0 = x.astype(BF16)
    r1 = x - x0.astype(F32)
    x1 = r1.astype(BF16)
    x2 = (r1 - x1.astype(F32)).astype(BF16)
    return x0, x1, x2


def _gdn_tile(gq, gk, gv, gab, s_ref):
    c = gq.shape[0]
    gc = _cumsum_rows(gab)
    egc = jnp.exp(gc)
    gparts = _split3(gc)
    row = _iota((c, c), 0)
    col = _iota((c, c), 1)
    incl = row >= col
    strict = row > col
    dotnt = functools.partial(lax.dot_general, dimension_numbers=NT, preferred_element_type=F32)
    outs = []
    for h in range(N_HEADS):
        sl = slice(h * HEAD_DIM, (h + 1) * HEAD_DIM)
        sel = _head_lane_select(c, h)
        grow = dotnt(sel, gparts[0]) + (dotnt(sel, gparts[1]) + dotnt(sel, gparts[2]))
        gcol = gc[:, h:h + 1]
        dec = jnp.where(incl, jnp.exp(jnp.where(incl, gcol - grow, 0.0)), 0.0)
        bcol = gab[:, N_HEADS + h:N_HEADS + h + 1]
        ecol = egc[:, h:h + 1]
        q, k, v = gq[:, sl], gk[:, sl], gv[:, sl]
        a = jnp.where(strict, bcol * dec * _mm(k, k, NT), 0.0)
        tinv = _inv_unit_lower(a)
        s = s_ref[h]
        w = bcol * (v - ecol * _mm(k, s))
        u = _mm(tinv, w, passes=INV_PASSES)
        qk = _mm(q, k, NT) * dec
        o = ecol * _mm(q, s) + _mm(qk, u)
        glast = gc[c - 1:c, h:h + 1]
        s_ref[h] = jnp.exp(glast) * s + _mm(jnp.exp(glast - gcol) * k, u, TN)
        outs.append(o)
    return outs


def _mixer_kernel(nt, ts, pos_base,
                  z_ref, cos_ref, sin_ref, intra_ref, qdec_ref, kdec_ref, sdec_ref, pv_ref, gcw_ref, gabp_ref,
                  wa_ref, wx_ref, pw_ref, sret0, sgdn0, gconv0, lru0, lconv0, pool0,
                  mix_ref, ret_o, gdn_o, gconv_o, lru_o, lconv_o, pool_o,
                  s_ret, s_gdn, ext_g, ext_l, ext_p, h_lru):
    n = pl.program_id(1)
    g = GROUP_W

    @pl.when(n == 0)
    def _():
        s_ret[...] = sret0[0]
        s_gdn[...] = sgdn0[0]
        ext_g[0:HIST, :] = jnp.zeros((HIST, 3 * g), F32)
        ext_g[HIST - (CONV_W - 1):HIST, :] = gconv0[0]
        ext_l[0:HIST, :] = jnp.zeros((HIST, g), F32)
        ext_l[HIST - (CONV_W - 1):HIST, :] = lconv0[0]
        ext_p[0:HIST, :] = jnp.zeros((HIST, g), F32)
        ext_p[HIST - POOL_BUF:HIST, :] = pool0[0]
        h_lru[...] = lru0[0]

    pv = pv_ref[...]
    ret_gn_g, gdn_norm_g = pv[0:1], pv[1:2]
    lru_conv_b, lru_ba, lru_bx, lru_lambda = pv[2:3], pv[3:4], pv[4:5], pv[5:6]
    pool_b, pool_scale = pv[6:7], pv[7:8]
    lru_conv_w = pv[8:12]

    ro = _retention_tile(z_ref[:, 0:g], z_ref[:, g:2 * g], z_ref[:, 2 * g:3 * g], cos_ref[...], sin_ref[...],
                         intra_ref, qdec_ref[...], kdec_ref[...], sdec_ref[...], s_ret)
    mix_ref[:, 0:g] = ro * ret_gn_g * _silu(z_ref[:, 3 * g:4 * g])

    ext_g[HIST:HIST + ts, :] = z_ref[:, 4 * g:7 * g]
    gcw = gcw_ref[...]
    base = HIST - (CONV_W - 1)
    y = ext_g[base:base + ts, :] * gcw[0:1]
    for i in range(1, CONV_W):
        y = y + ext_g[base + i:base + i + ts, :] * gcw[i:i + 1]
    qkv = _silu(y)
    zab = z_ref[:, Z_AB:Z_AB + LANES]
    gabp = gabp_ref[...]
    lane = _iota(zab.shape, 1)
    log_a = -jnp.exp(gabp[0:1]) * _softplus(zab + gabp[1:2])
    gab = jnp.where(lane < N_HEADS, log_a, jnp.where(lane < 2 * N_HEADS, _sigmoid(zab), 0.0))
    gq_parts, gk_parts = [], []
    for h in range(N_HEADS):
        qh = qkv[:, h * HEAD_DIM:(h + 1) * HEAD_DIM]
        kh = qkv[:, g + h * HEAD_DIM:g + (h + 1) * HEAD_DIM]
        gq_parts.append(qh * lax.rsqrt(jnp.sum(qh * qh, axis=-1, keepdims=True) + EPS) * (HEAD_DIM ** -0.5))
        gk_parts.append(kh * lax.rsqrt(jnp.sum(kh * kh, axis=-1, keepdims=True) + EPS))
    go = _gdn_tile(jnp.concatenate(gq_parts, axis=-1), jnp.concatenate(gk_parts, axis=-1), qkv[:, 2 * g:3 * g],
                   gab, s_gdn)
    go = [o * lax.rsqrt(jnp.mean(o * o, axis=-1, keepdims=True) + EPS) for o in go]
    mix_ref[:, g:2 * g] = jnp.concatenate(go, axis=-1) * gdn_norm_g * _silu(z_ref[:, 7 * g:8 * g])

    ext_l[HIST:HIST + ts, :] = z_ref[:, 8 * g:9 * g]
    xc = ext_l[base:base + ts, :] * lru_conv_w[0:1]
    for i in range(1, CONV_W):
        xc = xc + ext_l[base + i:base + i + ts, :] * lru_conv_w[i:i + 1]
    xc = xc + lru_conv_b
    r = _sigmoid(jnp.dot(xc.astype(BF16), wa_ref[...], preferred_element_type=F32) + lru_ba)
    gate_i = _sigmoid(jnp.dot(xc.astype(BF16), wx_ref[...], preferred_element_type=F32) + lru_bx)
    log_at = -LRU_C * r * _softplus(-lru_lambda)
    av = jnp.exp(log_at)
    bv = jnp.sqrt(-jnp.tanh(log_at) * (av * av + 1.0)) * (gate_i * xc)
    bv = bv + jnp.where(_iota(bv.shape, 0) == 0, av * h_lru[...], 0.0)
    d = 1
    while d < ts:
        bv = av * _shift_rows(bv, d, 0.0) + bv
        av = av * _shift_rows(av, d, 1.0)
        d *= 2
    h_lru[...] = bv[ts - 1:ts, :]
    mix_ref[:, 2 * g:3 * g] = bv * _gelu_tanh(z_ref[:, 9 * g:10 * g])

    u = z_ref[:, 10 * g:11 * g]
    ext_p[HIST:HIST + ts, :] = u
    e = ext_p[...]
    sums = []
    step = 1
    for _ in POOL_WINDOWS:
        e = e + pltpu.roll(e, step, 0)
        sums.append(e[HIST:HIST + ts, :])
        step *= 2
    pos1 = (pos_base + 1 + n * ts + _iota((ts, g), 0)).astype(F32)
    lane_g = _iota((ts, g), 1) >> 6
    pooled = sums[-1] / jnp.minimum(pos1, float(POOL_WINDOWS[-1]))
    for gi in range(len(POOL_WINDOWS) - 2, -1, -1):
        pooled = jnp.where(lane_g == gi, sums[gi] / jnp.minimum(pos1, float(POOL_WINDOWS[gi])), pooled)
    pooled = pooled - u
    yp = jnp.dot(pooled.astype(BF16), pw_ref[...], preferred_element_type=F32) + pool_b
    mix_ref[:, 3 * g:4 * g] = yp * pool_scale

    tail_g = ext_g[ts:ts + HIST, :]
    tail_l = ext_l[ts:ts + HIST, :]
    tail_p = ext_p[ts:ts + HIST, :]
    ext_g[0:HIST, :] = tail_g
    ext_l[0:HIST, :] = tail_l
    ext_p[0:HIST, :] = tail_p

    @pl.when(n == nt - 1)
    def _():
        ret_o[0] = s_ret[...]
        gdn_o[0] = s_gdn[...]
        gconv_o[0] = ext_g[HIST - (CONV_W - 1):HIST, :]
        lconv_o[0] = ext_l[HIST - (CONV_W - 1):HIST, :]
        pool_o[0] = ext_p[HIST - POOL_BUF:HIST, :]
        lru_o[0] = h_lru[...]


def _ret_consts(c):
    log_g = jnp.log1p(-jnp.exp2(-5.0 - jnp.arange(N_HEADS, dtype=F32)))
    idx = jnp.arange(c, dtype=F32)
    diff = idx[:, None] - idx[None, :]
    intra = jnp.where(diff >= 0, jnp.exp(log_g[:, None, None] * jnp.maximum(diff, 0.0)), 0.0)
    qdec = jnp.exp(log_g[:, None] * (idx + 1.0))
    kdec = jnp.exp(log_g[:, None] * (c - 1.0 - idx))
    sdec = jnp.exp(log_g * c)
    expand = lambda t: jnp.repeat(t.T, HEAD_DIM, axis=1)
    sdec_tab = jnp.zeros((8, LANES), F32).at[:N_HEADS, :].set(sdec[:, None])
    return intra, expand(qdec), expand(kdec), sdec_tab


def _rope_tables(pos):
    half = HEAD_DIM // 2
    inv = ROPE_BASE ** (-jnp.arange(half, dtype=F32) / half)
    ang = pos.astype(F32)[:, None] * inv[None, :]
    cos, sin = jnp.cos(ang), jnp.sin(ang)
    cos_t = jnp.tile(jnp.concatenate([cos, cos], axis=-1), (1, N_HEADS))
    sin_t = jnp.tile(jnp.concatenate([-sin, sin], axis=-1), (1, N_HEADS))
    return cos_t, sin_t


def _mixer(z, z_row0, b, l, ts, pos_base, states, lp):
    nt = l // ts
    g = GROUP_W
    pos = pos_base + jnp.arange(l, dtype=jnp.int32)
    cos_t, sin_t = _rope_tables(pos)
    intra, qdec, kdec, sdec = _ret_consts(ts)
    s_ret0, s_gdn0, gconv0, lru0, lconv0, pool0 = states
    blk0 = z_row0 // ts
    const2 = lambda bi, n: (0, 0)
    per_b3 = lambda bi, n: (bi, 0, 0)
    per_b4 = lambda bi, n: (bi, 0, 0, 0)
    in_specs = [
        pl.BlockSpec((ts, NZ), lambda bi, n: (blk0 + bi * nt + n, 0)),
        pl.BlockSpec((ts, g), lambda bi, n: (n, 0)),
        pl.BlockSpec((ts, g), lambda bi, n: (n, 0)),
        pl.BlockSpec((N_HEADS, ts, ts), lambda bi, n: (0, 0, 0)),
        pl.BlockSpec((ts, g), const2),
        pl.BlockSpec((ts, g), const2),
        pl.BlockSpec((8, LANES), const2),
        pl.BlockSpec((16, g), const2),
        pl.BlockSpec((CONV_W, 3 * g), const2),
        pl.BlockSpec((8, LANES), const2),
        pl.BlockSpec((g, g), const2),
        pl.BlockSpec((g, g), const2),
        pl.BlockSpec((g, g), const2),
        pl.BlockSpec((1, N_HEADS, HEAD_DIM, HEAD_DIM), per_b4),
        pl.BlockSpec((1, N_HEADS, HEAD_DIM, HEAD_DIM), per_b4),
        pl.BlockSpec((1, CONV_W - 1, 3 * g), per_b3),
        pl.BlockSpec((1, 1, g), per_b3),
        pl.BlockSpec((1, CONV_W - 1, g), per_b3),
        pl.BlockSpec((1, POOL_BUF, g), per_b3),
    ]
    out_specs = [
        pl.BlockSpec((ts, 4 * g), lambda bi, n: (bi * nt + n, 0)),
        pl.BlockSpec((1, N_HEADS, HEAD_DIM, HEAD_DIM), per_b4),
        pl.BlockSpec((1, N_HEADS, HEAD_DIM, HEAD_DIM), per_b4),
        pl.BlockSpec((1, CONV_W - 1, 3 * g), per_b3),
        pl.BlockSpec((1, 1, g), per_b3),
        pl.BlockSpec((1, CONV_W - 1, g), per_b3),
        pl.BlockSpec((1, POOL_BUF, g), per_b3),
    ]
    out_shape = [
        jax.ShapeDtypeStruct((b * l, 4 * g), F32),
        jax.ShapeDtypeStruct((b, N_HEADS, HEAD_DIM, HEAD_DIM), F32),
        jax.ShapeDtypeStruct((b, N_HEADS, HEAD_DIM, HEAD_DIM), F32),
        jax.ShapeDtypeStruct((b, CONV_W - 1, 3 * g), F32),
        jax.ShapeDtypeStruct((b, 1, g), F32),
        jax.ShapeDtypeStruct((b, CONV_W - 1, g), F32),
        jax.ShapeDtypeStruct((b, POOL_BUF, g), F32),
    ]
    scratch = [
        pltpu.VMEM((N_HEADS, HEAD_DIM, HEAD_DIM), F32),
        pltpu.VMEM((N_HEADS, HEAD_DIM, HEAD_DIM), F32),
        pltpu.VMEM((ts + HIST, 3 * g), F32),
        pltpu.VMEM((ts + HIST, g), F32),
        pltpu.VMEM((ts + HIST, g), F32),
        pltpu.VMEM((1, g), F32),
    ]
    outs = pl.pallas_call(
        functools.partial(_mixer_kernel, nt, ts, pos_base),
        grid=(b, nt),
        in_specs=in_specs,
        out_specs=out_specs,
        out_shape=out_shape,
        scratch_shapes=scratch,
        compiler_params=pltpu.CompilerParams(dimension_semantics=("arbitrary", "arbitrary"),
                                             vmem_limit_bytes=VMEM_LIMIT),
        name=f"mixer_ts{ts}",
    )(z, cos_t, sin_t, intra, qdec, kdec, sdec, lp["pvec"], lp["gdn_conv_w"], lp["gabp"],
      lp["lru_wa"], lp["lru_wx"], lp["pool_w"], s_ret0, s_gdn0, gconv0, lru0[:, None, :], lconv0, pool0)
    mix, ret_n, gdn_n, gconv_n, lru_n, lconv_n, pool_n = outs
    return mix, (ret_n, gdn_n, gconv_n, lru_n[:, 0, :], lconv_n, pool_n)


def _wout_route_kernel(mix_ref, x_ref, wo_ref, g_ref, wr_ref, br_ref, x1_ref, h2_ref, rt_ref):
    x1 = x_ref[...] + jnp.dot(mix_ref[...].astype(BF16), wo_ref[...], preferred_element_type=F32)
    x1_ref[...] = x1
    h = x1 * lax.rsqrt(jnp.mean(x1 * x1, axis=-1, keepdims=True) + EPS) * g_ref[...]
    h2_ref[...] = h
    logits = _mm(h, wr_ref[...], passes=3) + br_ref[...]
    lane = _iota(logits.shape, 1)
    lane_f = lane.astype(F32)
    neg = -1e30
    far = 1e9
    is_g = lane < N_GROUPS
    gl = jnp.where(is_g, logits, neg)
    gmax = jnp.max(gl, axis=-1, keepdims=True)
    gsel = jnp.min(jnp.where(gl == gmax, lane_f, far), axis=-1, keepdims=True)
    p_g = 1.0 / jnp.sum(jnp.where(is_g, jnp.exp(gl - gmax), 0.0), axis=-1, keepdims=True)
    e_group = ((lane - N_GROUPS) >> 3).astype(F32)
    in_group = jnp.where(lane >= N_GROUPS, e_group, -1.0) == gsel
    el = jnp.where(in_group, logits, neg)
    m1 = jnp.max(el, axis=-1, keepdims=True)
    i1 = jnp.min(jnp.where(el == m1, lane_f, far), axis=-1, keepdims=True)
    el2 = jnp.where(lane_f == i1, neg, el)
    m2 = jnp.max(el2, axis=-1, keepdims=True)
    i2 = jnp.min(jnp.where(el2 == m2, lane_f, far), axis=-1, keepdims=True)
    e2 = jnp.exp(m2 - m1)
    w1 = p_g / (1.0 + e2)
    w2 = p_g * e2 / (1.0 + e2)
    rt = jnp.where(lane == 0, i1 - N_GROUPS,
                   jnp.where(lane == 1, i2 - N_GROUPS, jnp.where(lane == 2, w1, jnp.where(lane == 3, w2, 0.0))))
    rt_ref[...] = rt


def _wout_route(mix, x, wo_bf16, g, wr, br, tm):
    t, d = x.shape
    row = lambda i: (i, 0)
    const = lambda i: (0, 0)
    return pl.pallas_call(
        _wout_route_kernel,
        grid=(t // tm,),
        in_specs=[pl.BlockSpec((tm, d), row), pl.BlockSpec((tm, d), row), pl.BlockSpec((d, d), const),
                  pl.BlockSpec((1, d), const), pl.BlockSpec((d, LANES), const), pl.BlockSpec((1, LANES), const)],
        out_specs=[pl.BlockSpec((tm, d), row), pl.BlockSpec((tm, d), row), pl.BlockSpec((tm, LANES), row)],
        out_shape=[jax.ShapeDtypeStruct((t, d), F32), jax.ShapeDtypeStruct((t, d), F32),
                   jax.ShapeDtypeStruct((t, LANES), F32)],
        compiler_params=pltpu.CompilerParams(dimension_semantics=("arbitrary",), vmem_limit_bytes=VMEM_LIMIT),
        name="wout_route",
    )(mix, x, wo_bf16, g, wr, br)


def _row_copy(src_hbm, idx, dst_vmem, r, sem):
    return pltpu.make_async_copy(src_hbm.at[pl.ds(idx, 1), :], dst_vmem.at[pl.ds(r, 1), :], sem)


def _gather_kernel(tg, idx_ref, h_hbm, out_ref, sem):
    def start(r, c):
        _row_copy(h_hbm, idx_ref[0, 0, r], out_ref, r, sem.at[r]).start()
        return c

    def wait(r, c):
        _row_copy(h_hbm, idx_ref[0, 0, r], out_ref, r, sem.at[r]).wait()
        return c

    lax.fori_loop(0, tg, start, 0)
    lax.fori_loop(0, tg, wait, 0)


def _gather_rows(h, idx, tg):
    p = idx.shape[0]
    d = h.shape[1]
    return pl.pallas_call(
        functools.partial(_gather_kernel, tg),
        grid=(p // tg,),
        in_specs=[pl.BlockSpec((1, 1, tg), lambda i: (i, 0, 0), memory_space=pltpu.SMEM),
                  pl.BlockSpec(memory_space=pl.ANY)],
        out_specs=pl.BlockSpec((tg, d), lambda i: (i, 0)),
        out_shape=jax.ShapeDtypeStruct((p, d), F32),
        scratch_shapes=[pltpu.SemaphoreType.DMA((tg,))],
        compiler_params=pltpu.CompilerParams(dimension_semantics=("arbitrary",), vmem_limit_bytes=VMEM_LIMIT),
        name="moe_gather",
    )(idx.reshape(p // tg, 1, tg), h)


def _ffn_kernel(te_ref, nu_ref, xs_ref, wg_ref, wu_ref, wd_ref, ys_ref, wg_s, wu_s, wd_s):
    i = pl.program_id(0)
    used = i < nu_ref[0]
    new_expert = (i == 0) | (te_ref[i] != te_ref[jnp.maximum(i - 1, 0)])

    @pl.when(used & new_expert)
    def _():
        wg_s[...] = wg_ref[0].astype(BF16)
        wu_s[...] = wu_ref[0].astype(BF16)
        wd_s[...] = wd_ref[0].astype(BF16)

    @pl.when(used)
    def _():
        x = xs_ref[...].astype(BF16)
        gate = jnp.dot(x, wg_s[...], preferred_element_type=F32)
        up = jnp.dot(x, wu_s[...], preferred_element_type=F32)
        act = (_silu(gate) * up).astype(BF16)
        ys_ref[...] = jnp.dot(act, wd_s[...], preferred_element_type=F32)

    @pl.when(jnp.logical_not(used))
    def _():
        ys_ref[...] = jnp.zeros(ys_ref.shape, F32)


def _expert_ffn(xs, tile_expert, n_used, w_gate, w_up, w_down, tm):
    p, d = xs.shape
    de = w_gate.shape[2]
    row = lambda i, te, nu: (jnp.minimum(i, nu[0] - 1), 0)
    grid_spec = pltpu.PrefetchScalarGridSpec(
        num_scalar_prefetch=2,
        grid=(p // tm,),
        in_specs=[pl.BlockSpec((tm, d), row),
                  pl.BlockSpec((1, d, de), lambda i, te, nu: (te[i], 0, 0)),
                  pl.BlockSpec((1, d, de), lambda i, te, nu: (te[i], 0, 0)),
                  pl.BlockSpec((1, de, d), lambda i, te, nu: (te[i], 0, 0))],
        out_specs=pl.BlockSpec((tm, d), lambda i, te, nu: (i, 0)),
        scratch_shapes=[pltpu.VMEM((d, de), BF16), pltpu.VMEM((d, de), BF16), pltpu.VMEM((de, d), BF16)],
    )
    return pl.pallas_call(
        _ffn_kernel,
        grid_spec=grid_spec,
        out_shape=jax.ShapeDtypeStruct((p, d), F32),
        compiler_params=pltpu.CompilerParams(dimension_semantics=("arbitrary",), vmem_limit_bytes=VMEM_LIMIT),
        name="expert_ffn",
    )(tile_expert, n_used, xs, w_gate, w_up, w_down)


def _combine_kernel(tm, final_norm, p1_ref, p2_ref, x1_ref, rt_ref, fg_ref, ys_hbm, out_ref, buf1, buf2, sem1, sem2):
    def start(r, c):
        _row_copy(ys_hbm, p1_ref[0, 0, r], buf1, r, sem1.at[r]).start()
        _row_copy(ys_hbm, p2_ref[0, 0, r], buf2, r, sem2.at[r]).start()
        return c

    def wait(r, c):
        _row_copy(ys_hbm, p1_ref[0, 0, r], buf1, r, sem1.at[r]).wait()
        _row_copy(ys_hbm, p2_ref[0, 0, r], buf2, r, sem2.at[r]).wait()
        return c

    lax.fori_loop(0, tm, start, 0)
    lax.fori_loop(0, tm, wait, 0)
    rt = rt_ref[...]
    x = x1_ref[...] + rt[:, 2:3] * buf1[...] + rt[:, 3:4] * buf2[...]
    if final_norm:
        x = x * lax.rsqrt(jnp.mean(x * x, axis=-1, keepdims=True) + EPS) * fg_ref[...]
    out_ref[...] = x


def _combine(x1, rt, ys, pos1, pos2, final_g, final_norm, tm):
    t, d = x1.shape
    nb = t // tm
    row = lambda i: (i, 0)
    smem_blk = pl.BlockSpec((1, 1, tm), lambda i: (i, 0, 0), memory_space=pltpu.SMEM)
    return pl.pallas_call(
        functools.partial(_combine_kernel, tm, final_norm),
        grid=(nb,),
        in_specs=[smem_blk, smem_blk, pl.BlockSpec((tm, d), row), pl.BlockSpec((tm, LANES), row),
                  pl.BlockSpec((1, d), lambda i: (0, 0)), pl.BlockSpec(memory_space=pl.ANY)],
        out_specs=pl.BlockSpec((tm, d), row),
        out_shape=jax.ShapeDtypeStruct((t, d), F32),
        scratch_shapes=[pltpu.VMEM((tm, d), F32), pltpu.VMEM((tm, d), F32),
                        pltpu.SemaphoreType.DMA((tm,)), pltpu.SemaphoreType.DMA((tm,))],
        compiler_params=pltpu.CompilerParams(dimension_semantics=("arbitrary",), vmem_limit_bytes=VMEM_LIMIT),
        name="moe_combine",
    )(pos1.reshape(nb, 1, tm), pos2.reshape(nb, 1, tm), x1, rt, final_g, ys)


def _route_plan(rt, tm, n_tiles):
    t = rt.shape[0]
    eid = rt[:, 0:2].astype(jnp.int32)
    e_flat = jnp.concatenate([eid[:, 0], eid[:, 1]])
    onehot = (e_flat[:, None] == jnp.arange(N_EXPERTS, dtype=jnp.int32)[None, :]).astype(jnp.int32)
    csum = jnp.cumsum(onehot, axis=0)
    rank = jnp.sum(csum * onehot, axis=1) - 1
    counts = csum[-1]
    padded = ((counts + tm - 1) // tm) * tm
    ends = jnp.cumsum(padded)
    starts = ends - padded
    pos = jnp.sum(onehot * starts[None, :], axis=1) + rank
    tok = jnp.concatenate([jnp.arange(t, dtype=jnp.int32)] * 2)
    src = jnp.zeros((n_tiles * tm,), jnp.int32).at[pos].set(tok, unique_indices=True)
    n_used = (ends[-1] // tm).astype(jnp.int32)
    tile_start = jnp.arange(n_tiles, dtype=jnp.int32) * tm
    tile_expert = jnp.sum((tile_start[:, None] >= ends[None, :]).astype(jnp.int32), axis=1)
    last_expert = jnp.sum((tile_start[n_used - 1] >= ends).astype(jnp.int32))
    tile_expert = jnp.where(jnp.arange(n_tiles) < n_used, tile_expert, last_expert).astype(jnp.int32)
    return src, pos[:t], pos[t:], tile_expert, n_used.reshape(1)


def _block_diag(w):
    h, a, b = w.shape
    out = jnp.zeros((h * a, h * b), w.dtype)
    for i in range(h):
        out = out.at[i * a:(i + 1) * a, i * b:(i + 1) * b].set(w[i])
    return out


def _layer_params(l, p):
    g = GROUP_W
    w_in = p["w_in"][l]
    d = w_in.shape[0]
    w_perm = jnp.concatenate([w_in[:, :8 * g], w_in[:, 8 * g + 8:], w_in[:, 8 * g:8 * g + 8],
                              jnp.zeros((d, LANES - 8), w_in.dtype)], axis=1)
    pvec = jnp.zeros((16, g), F32)
    rows = [p["ret_gn_g"][l], jnp.tile(p["gdn_norm_g"][l], N_HEADS), p["lru_conv_b"][l], p["lru_ba"][l],
            p["lru_bx"][l], p["lru_lambda"][l], p["pool_b"][l], p["pool_scale"][l]]
    pvec = pvec.at[0:8].set(jnp.stack(rows)).at[8:12].set(p["lru_conv_w"][l])
    gabp = jnp.zeros((8, LANES), F32).at[0, :N_HEADS].set(p["gdn_a_log"][l]).at[1, :N_HEADS].set(p["gdn_dt_bias"][l])
    wr = jnp.zeros((d, LANES), F32).at[:, :N_GROUPS].set(p["router_group_w"][l])
    wr = wr.at[:, N_GROUPS:N_GROUPS + N_EXPERTS].set(
        jnp.transpose(p["router_expert_w"][l], (1, 0, 2)).reshape(d, N_EXPERTS))
    br = jnp.zeros((1, LANES), F32).at[0, :N_GROUPS].set(p["router_group_b"][l])
    br = br.at[0, N_GROUPS:N_GROUPS + N_EXPERTS].set(p["router_expert_b"][l].reshape(N_EXPERTS))
    return dict(
        norm1_g=p["norm1_g"][l][None, :], w_in=w_perm.astype(BF16), pvec=pvec, gdn_conv_w=p["gdn_conv_w"][l],
        gabp=gabp, lru_wa=_block_diag(p["lru_wa"][l]).astype(BF16), lru_wx=_block_diag(p["lru_wx"][l]).astype(BF16),
        pool_w=_block_diag(p["pool_w"][l]).astype(BF16), w_out=p["w_out"][l].astype(BF16),
        norm2_g=p["norm2_g"][l][None, :], wr=wr, br=br,
        w_gate=p["moe_w_gate"][l], w_up=p["moe_w_up"][l], w_down=p["moe_w_down"][l])


def kernel(x_prompt, x_sample, state_ret, state_gdn, state_gdn_conv, state_lru, state_lru_conv, state_pool,
           norm1_g, w_in, ret_gn_g, gdn_conv_w, gdn_a_log, gdn_dt_bias, gdn_norm_g,
           lru_conv_w, lru_conv_b, lru_wa, lru_ba, lru_wx, lru_bx, lru_lambda,
           pool_w, pool_b, pool_scale, w_out, norm2_g,
           router_group_w, router_group_b, router_expert_w, router_expert_b,
           moe_w_gate, moe_w_up, moe_w_down, final_g):
    params = dict(norm1_g=norm1_g, w_in=w_in, ret_gn_g=ret_gn_g, gdn_conv_w=gdn_conv_w, gdn_a_log=gdn_a_log,
                  gdn_dt_bias=gdn_dt_bias, gdn_norm_g=gdn_norm_g, lru_conv_w=lru_conv_w, lru_conv_b=lru_conv_b,
                  lru_wa=lru_wa, lru_ba=lru_ba, lru_wx=lru_wx, lru_bx=lru_bx, lru_lambda=lru_lambda,
                  pool_w=pool_w, pool_b=pool_b, pool_scale=pool_scale, w_out=w_out, norm2_g=norm2_g,
                  router_group_w=router_group_w, router_group_b=router_group_b,
                  router_expert_w=router_expert_w, router_expert_b=router_expert_b,
                  moe_w_gate=moe_w_gate, moe_w_up=moe_w_up, moe_w_down=moe_w_down)
    bp, lp_, d = x_prompt.shape
    bs, ls, _ = x_sample.shape
    depth = w_in.shape[0]
    tp, ts_ = bp * lp_, bs * ls
    t = tp + ts_
    tile_p = min(256, lp_)
    tm = 256
    n_tiles = (2 * t + N_EXPERTS * (tm - 1) + tm - 1) // tm

    x = jnp.concatenate([x_prompt.reshape(tp, d), x_sample.reshape(ts_, d)], axis=0)
    zero_states = (jnp.zeros((bp,) + state_ret.shape[2:], F32), jnp.zeros((bp,) + state_gdn.shape[2:], F32),
                   jnp.zeros((bp,) + state_gdn_conv.shape[2:], F32), jnp.zeros((bp,) + state_lru.shape[2:], F32),
                   jnp.zeros((bp,) + state_lru_conv.shape[2:], F32), jnp.zeros((bp,) + state_pool.shape[2:], F32))
    sample_states = (state_ret, state_gdn, state_gdn_conv, state_lru, state_lru_conv, state_pool)
    new_p, new_s = [], []
    for l in range(depth):
        lp = _layer_params(l, params)
        z = _norm_win(x, lp["norm1_g"], lp["w_in"], tm)
        mix_p, st_p = _mixer(z, 0, bp, lp_, tile_p, 0, zero_states, lp)
        mix_s, st_s = _mixer(z, tp, bs, ls, ls, PAST_LEN, tuple(s[l] for s in sample_states), lp)
        new_p.append(st_p)
        new_s.append(st_s)
        mix = jnp.concatenate([mix_p, mix_s], axis=0)
        x1, h2, rt = _wout_route(mix, x, lp["w_out"], lp["norm2_g"], lp["wr"], lp["br"], tm)
        src, pos1, pos2, tile_expert, n_used = _route_plan(rt, tm, n_tiles)
        xs = _gather_rows(h2, src, tm)
        ys = _expert_ffn(xs, tile_expert, n_used, lp["w_gate"], lp["w_up"], lp["w_down"], tm)
        x = _combine(x1, rt, ys, pos1, pos2, final_g[None, :], l == depth - 1, tm)
    y_prompt = x[:tp].reshape(bp, lp_, d)
    y_sample = x[tp:].reshape(bs, ls, d)
    stack = lambda news: [jnp.stack([ns[j] for ns in news], axis=0) for j in range(6)]
    return (y_prompt, y_sample, *stack(new_p), *stack(new_s))
```

```python
import functools

import jax
import jax.numpy as jnp
from jax import lax
from jax.experimental import pallas as pl
from jax.experimental.pallas import tpu as pltpu

F32 = jnp.float32
BF16 = jnp.bfloat16

EPS = 1e-6
HEAD_DIM = 64
N_HEADS = 4
GROUP_W = 256
CONV_W = 4
LRU_C = 8.0
POOL_WINDOWS = (2, 4, 8, 16)
POOL_BUF = 15
ROPE_BASE = 10000.0
N_GROUPS = 4
EXPERTS_PER_GROUP = 8
N_EXPERTS = 32
PAST_LEN = 16384

LANES = 128
SUBLANES = 8
HIST = 16
Z_AB = 11 * GROUP_W
NZ = Z_AB + LANES
INV_BASE = 8
RESID_PASSES = 3
SAMPLE_SEQS_PER_STEP = 1
VMEM_LIMIT = 56 * 1024 * 1024

NN = (((1,), (0,)), ((), ()))
NT = (((1,), (1,)), ((), ()))
TN = (((0,), (0,)), ((), ()))


def _mm(a, b, dims=NN, passes=1):
    ah = a.astype(BF16)
    bh = b.astype(BF16)
    dot = functools.partial(lax.dot_general, dimension_numbers=dims, preferred_element_type=F32)
    if passes == 1:
        return dot(ah, bh)
    al = (a - ah.astype(F32)).astype(BF16)
    bl = (b - bh.astype(F32)).astype(BF16)
    return dot(ah, bh) + (dot(ah, bl) + dot(al, bh))


def _sigmoid(x):
    return 1.0 / (1.0 + jnp.exp(-x))


def _silu(x):
    return x * _sigmoid(x)


def _softplus(x):
    return jnp.maximum(x, 0.0) + jnp.log1p(jnp.exp(-jnp.abs(x)))


def _gelu_tanh(x):
    return 0.5 * x * (1.0 + jnp.tanh(0.7978845608028654 * (x + 0.044715 * (x * x * x))))


def _iota(shape, axis):
    return lax.broadcasted_iota(jnp.int32, shape, axis)


def _shift_rows(x, d, fill):
    return jnp.where(_iota(x.shape, 0) >= d, pltpu.roll(x, d, 0), fill)


def _cumsum_rows(x):
    d = 1
    while d < x.shape[0]:
        x = x + _shift_rows(x, d, 0.0)
        d *= 2
    return x


def _swap_half_heads(x):
    n = x.shape[1]
    lower = (_iota(x.shape, 1) & (HEAD_DIM - 1)) < HEAD_DIM // 2
    return jnp.where(lower, pltpu.roll(x, n - HEAD_DIM // 2, 1), pltpu.roll(x, HEAD_DIM // 2, 1))


def _inv_unit_lower(a):
    c = a.shape[0]
    row = _iota(a.shape, 0)
    col = _iota(a.shape, 1)
    nb = min(INV_BASE, c)
    shift = nb.bit_length() - 1
    d = jnp.where((row >> shift) == (col >> shift), a, 0.0)
    x = jnp.where(row == col, 1.0, 0.0) - d
    p = d
    k = 2
    while k < nb:
        p = _mm(p, p)
        x = x + _mm(x, p)
        k *= 2
    m = nb
    while m < c:
        s1 = m.bit_length() - 1
        off = jnp.where(((row >> (s1 + 1)) == (col >> (s1 + 1))) & ((row >> s1) != (col >> s1)), a, 0.0)
        x = x - _mm(_mm(x, off), x)
        m *= 2
    return x


def _solve_unit_lower(a, tinv, w):
    u = _mm(tinv, w)
    resid = w - (u + _mm(a, u, passes=RESID_PASSES))
    return u + _mm(tinv, resid)


def _norm_win_kernel(x_ref, g_ref, w_ref, z_ref):
    x = x_ref[...]
    h = x * lax.rsqrt(jnp.mean(x * x, axis=-1, keepdims=True) + EPS) * g_ref[...]
    z_ref[...] = jnp.dot(h.astype(BF16), w_ref[...], preferred_element_type=F32)


def _norm_win(x, g, w_bf16, tm):
    t, d = x.shape
    nz = w_bf16.shape[1]
    return pl.pallas_call(
        _norm_win_kernel,
        grid=(t // tm,),
        in_specs=[pl.BlockSpec((tm, d), lambda i: (i, 0)),
                  pl.BlockSpec((1, d), lambda i: (0, 0)),
                  pl.BlockSpec((d, nz), lambda i: (0, 0))],
        out_specs=pl.BlockSpec((tm, nz), lambda i: (i, 0)),
        out_shape=jax.ShapeDtypeStruct((t, nz), F32),
        compiler_params=pltpu.CompilerParams(dimension_semantics=("arbitrary",), vmem_limit_bytes=VMEM_LIMIT),
        name="norm_win",
    )(x, g, w_bf16)


def _retention_tile(zq, zk, zv, cos, sin, intra_ref, qdec, kdec, sdec, s_ref):
    rq = zq * cos + _swap_half_heads(zq) * sin
    rk = (zk * cos + _swap_half_heads(zk) * sin) * (HEAD_DIM ** -0.5)
    kd = rk * kdec
    outs = []
    for h in range(N_HEADS):
        sl = slice(h * HEAD_DIM, (h + 1) * HEAD_DIM)
        q, k, v = rq[:, sl], rk[:, sl], zv[:, sl]
        s = s_ref[h]
        scores = _mm(q, k, NT) * intra_ref[h]
        o = _mm(scores, v) + _mm(q, s) * qdec[:, sl]
        s_ref[h] = s * sdec[h:h + 1, :HEAD_DIM] + _mm(kd[:, sl], v, TN)
        mu = jnp.mean(o, axis=-1, keepdims=True)
        oc = o - mu
        var = jnp.mean(oc * oc, axis=-1, keepdims=True)
        outs.append(oc * lax.rsqrt(var + EPS))
    return jnp.concatenate(outs, axis=-1)


def _head_lane_select(c, h):
    return jnp.where(_iota((c, LANES), 1) == h, 1.0, 0.0).astype(BF16)


def _split3(x):
    x0 = x.astype(BF16)
    r1 = x - x0.astype(F32)
    x1 = r1.astype(BF16)
    x2 = (r1 - x1.astype(F32)).astype(BF16)
    return x0, x1, x2


def _gdn_tile(gq, gk, gv, gab, s_ref):
    c = gq.shape[0]
    gc = _cumsum_rows(gab)
    egc = jnp.exp(gc)
    gparts = _split3(gc)
    row = _iota((c, c), 0)
    col = _iota((c, c), 1)
    incl = row >= col
    strict = row > col
    dotnt = functools.partial(lax.dot_general, dimension_numbers=NT, preferred_element_type=F32)
    outs = []
    for h in range(N_HEADS):
        sl = slice(h * HEAD_DIM, (h + 1) * HEAD_DIM)
        sel = _head_lane_select(c, h)
        grow = dotnt(sel, gparts[0]) + (dotnt(sel, gparts[1]) + dotnt(sel, gparts[2]))
        gcol = gc[:, h:h + 1]
        dec = jnp.where(incl, jnp.exp(jnp.where(incl, gcol - grow, 0.0)), 0.0)
        bcol = gab[:, N_HEADS + h:N_HEADS + h + 1]
        ecol = egc[:, h:h + 1]
        q, k, v = gq[:, sl], gk[:, sl], gv[:, sl]
        a = jnp.where(strict, bcol * dec * _mm(k, k, NT), 0.0)
        tinv = _inv_unit_lower(a)
        s = s_ref[h]
        w = bcol * (v - ecol * _mm(k, s))
        u = _solve_unit_lower(a, tinv, w)
        qk = _mm(q, k, NT) * dec
        o = ecol * _mm(q, s) + _mm(qk, u)
        glast = gc[c - 1:c, h:h + 1]
        s_ref[h] = jnp.exp(glast) * s + _mm(jnp.exp(glast - gcol) * k, u, TN)
        outs.append(o)
    return outs


def _mixer_kernel(nt, ts, nb, pos_base,
                  z_ref, cos_ref, sin_ref, intra_ref, qdec_ref, kdec_ref, sdec_ref, pv_ref, gcw_ref, gabp_ref,
                  wa_ref, wx_ref, pw_ref, sret0, sgdn0, gconv0, lru0, lconv0, pool0,
                  mix_ref, ret_o, gdn_o, gconv_o, lru_o, lconv_o, pool_o,
                  s_ret, s_gdn, ext_g, ext_l, ext_p, h_lru):
    n = pl.program_id(1)
    g = GROUP_W
    base = HIST - (CONV_W - 1)

    @pl.when(n == 0)
    def _():
        s_ret[...] = sret0[...]
        s_gdn[...] = sgdn0[...]
        ext_g[:, 0:HIST, :] = jnp.zeros((nb, HIST, 3 * g), F32)
        ext_g[:, base:HIST, :] = gconv0[...]
        ext_l[:, 0:HIST, :] = jnp.zeros((nb, HIST, g), F32)
        ext_l[:, base:HIST, :] = lconv0[...]
        ext_p[:, 0:HIST, :] = jnp.zeros((nb, HIST, g), F32)
        ext_p[:, HIST - POOL_BUF:HIST, :] = pool0[...]
        h_lru[...] = lru0[...]

    pv = pv_ref[...]
    ret_gn_g, gdn_norm_g = pv[0:1], pv[1:2]
    lru_conv_b, lru_ba, lru_bx, lru_lambda = pv[2:3], pv[3:4], pv[4:5], pv[5:6]
    pool_b, pool_scale = pv[6:7], pv[7:8]
    lru_conv_w = pv[8:12]
    gcw = gcw_ref[...]
    gabp = gabp_ref[...]

    for j in range(nb):
        rows = slice(j * ts, (j + 1) * ts)
        eg, el, ep = ext_g.at[j], ext_l.at[j], ext_p.at[j]

        ro = _retention_tile(z_ref[rows, 0:g], z_ref[rows, g:2 * g], z_ref[rows, 2 * g:3 * g], cos_ref[...],
                             sin_ref[...], intra_ref, qdec_ref[...], kdec_ref[...], sdec_ref[...], s_ret.at[j])
        mix_ref[rows, 0:g] = ro * ret_gn_g * _silu(z_ref[rows, 3 * g:4 * g])

        eg[HIST:HIST + ts, :] = z_ref[rows, 4 * g:7 * g]
        y = eg[base:base + ts, :] * gcw[0:1]
        for i in range(1, CONV_W):
            y = y + eg[base + i:base + i + ts, :] * gcw[i:i + 1]
        qkv = _silu(y)
        zab = z_ref[rows, Z_AB:Z_AB + LANES]
        lane = _iota(zab.shape, 1)
        log_a = -jnp.exp(gabp[0:1]) * _softplus(zab + gabp[1:2])
        gab = jnp.where(lane < N_HEADS, log_a, jnp.where(lane < 2 * N_HEADS, _sigmoid(zab), 0.0))
        gq_parts, gk_parts = [], []
        for h in range(N_HEADS):
            qh = qkv[:, h * HEAD_DIM:(h + 1) * HEAD_DIM]
            kh = qkv[:, g + h * HEAD_DIM:g + (h + 1) * HEAD_DIM]
            gq_parts.append(qh * lax.rsqrt(jnp.sum(qh * qh, axis=-1, keepdims=True) + EPS) * (HEAD_DIM ** -0.5))
            gk_parts.append(kh * lax.rsqrt(jnp.sum(kh * kh, axis=-1, keepdims=True) + EPS))
        go = _gdn_tile(jnp.concatenate(gq_parts, axis=-1), jnp.concatenate(gk_parts, axis=-1),
                       qkv[:, 2 * g:3 * g], gab, s_gdn.at[j])
        go = [o * lax.rsqrt(jnp.mean(o * o, axis=-1, keepdims=True) + EPS) for o in go]
        mix_ref[rows, g:2 * g] = jnp.concatenate(go, axis=-1) * gdn_norm_g * _silu(z_ref[rows, 7 * g:8 * g])

        el[HIST:HIST + ts, :] = z_ref[rows, 8 * g:9 * g]
        xc = el[base:base + ts, :] * lru_conv_w[0:1]
        for i in range(1, CONV_W):
            xc = xc + el[base + i:base + i + ts, :] * lru_conv_w[i:i + 1]
        xc = xc + lru_conv_b
        r = _sigmoid(jnp.dot(xc.astype(BF16), wa_ref[...], preferred_element_type=F32) + lru_ba)
        gate_i = _sigmoid(jnp.dot(xc.astype(BF16), wx_ref[...], preferred_element_type=F32) + lru_bx)
        log_at = -LRU_C * r * _softplus(-lru_lambda)
        av = jnp.exp(log_at)
        bv = jnp.sqrt(-jnp.tanh(log_at) * (av * av + 1.0)) * (gate_i * xc)
        bv = bv + jnp.where(_iota(bv.shape, 0) == 0, av * h_lru[j], 0.0)
        d = 1
        while d < ts:
            bv = av * _shift_rows(bv, d, 0.0) + bv
            av = av * _shift_rows(av, d, 1.0)
            d *= 2
        h_lru[j] = bv[ts - 1:ts, :]
        mix_ref[rows, 2 * g:3 * g] = bv * _gelu_tanh(z_ref[rows, 9 * g:10 * g])

        u = z_ref[rows, 10 * g:11 * g]
        ep[HIST:HIST + ts, :] = u
        e = ep[...]
        sums = []
        step = 1
        for _ in POOL_WINDOWS:
            e = e + pltpu.roll(e, step, 0)
            sums.append(e[HIST:HIST + ts, :])
            step *= 2
        pos1 = (pos_base + 1 + n * ts + _iota((ts, g), 0)).astype(F32)
        lane_g = _iota((ts, g), 1) >> 6
        pooled = sums[-1] / jnp.minimum(pos1, float(POOL_WINDOWS[-1]))
        for gi in range(len(POOL_WINDOWS) - 2, -1, -1):
            pooled = jnp.where(lane_g == gi, sums[gi] / jnp.minimum(pos1, float(POOL_WINDOWS[gi])), pooled)
        pooled = pooled - u
        yp = jnp.dot(pooled.astype(BF16), pw_ref[...], preferred_element_type=F32) + pool_b
        mix_ref[rows, 3 * g:4 * g] = yp * pool_scale

        tail_g = eg[ts:ts + HIST, :]
        tail_l = el[ts:ts + HIST, :]
        tail_p = ep[ts:ts + HIST, :]
        eg[0:HIST, :] = tail_g
        el[0:HIST, :] = tail_l
        ep[0:HIST, :] = tail_p

    @pl.when(n == nt - 1)
    def _():
        ret_o[...] = s_ret[...]
        gdn_o[...] = s_gdn[...]
        gconv_o[...] = ext_g[:, base:HIST, :]
        lconv_o[...] = ext_l[:, base:HIST, :]
        pool_o[...] = ext_p[:, HIST - POOL_BUF:HIST, :]
        lru_o[...] = h_lru[...]


def _ret_consts(c):
    log_g = jnp.log1p(-jnp.exp2(-5.0 - jnp.arange(N_HEADS, dtype=F32)))
    idx = jnp.arange(c, dtype=F32)
    diff = idx[:, None] - idx[None, :]
    intra = jnp.where(diff >= 0, jnp.exp(log_g[:, None, None] * jnp.maximum(diff, 0.0)), 0.0)
    qdec = jnp.exp(log_g[:, None] * (idx + 1.0))
    kdec = jnp.exp(log_g[:, None] * (c - 1.0 - idx))
    sdec = jnp.exp(log_g * c)
    expand = lambda t: jnp.repeat(t.T, HEAD_DIM, axis=1)
    sdec_tab = jnp.zeros((8, LANES), F32).at[:N_HEADS, :].set(sdec[:, None])
    return intra, expand(qdec), expand(kdec), sdec_tab


def _rope_tables(pos):
    half = HEAD_DIM // 2
    inv = ROPE_BASE ** (-jnp.arange(half, dtype=F32) / half)
    ang = pos.astype(F32)[:, None] * inv[None, :]
    cos, sin = jnp.cos(ang), jnp.sin(ang)
    cos_t = jnp.tile(jnp.concatenate([cos, cos], axis=-1), (1, N_HEADS))
    sin_t = jnp.tile(jnp.concatenate([-sin, sin], axis=-1), (1, N_HEADS))
    return cos_t, sin_t


def _mixer(z, z_row0, b, l, ts, nb, pos_base, states, lp):
    nt = l // ts
    assert nb == 1 or nt == 1
    g = GROUP_W
    pos = pos_base + jnp.arange(l, dtype=jnp.int32)
    cos_t, sin_t = _rope_tables(pos)
    intra, qdec, kdec, sdec = _ret_consts(ts)
    s_ret0, s_gdn0, gconv0, lru0, lconv0, pool0 = states
    blk0 = z_row0 // (nb * ts)
    const2 = lambda bi, n: (0, 0)
    per_b3 = lambda bi, n: (bi, 0, 0)
    per_b4 = lambda bi, n: (bi, 0, 0, 0)
    in_specs = [
        pl.BlockSpec((nb * ts, NZ), lambda bi, n: (blk0 + bi * nt + n, 0)),
        pl.BlockSpec((ts, g), lambda bi, n: (n, 0)),
        pl.BlockSpec((ts, g), lambda bi, n: (n, 0)),
        pl.BlockSpec((N_HEADS, ts, ts), lambda bi, n: (0, 0, 0)),
        pl.BlockSpec((ts, g), const2),
        pl.BlockSpec((ts, g), const2),
        pl.BlockSpec((8, LANES), const2),
        pl.BlockSpec((16, g), const2),
        pl.BlockSpec((CONV_W, 3 * g), const2),
        pl.BlockSpec((8, LANES), const2),
        pl.BlockSpec((g, g), const2),
        pl.BlockSpec((g, g), const2),
        pl.BlockSpec((g, g), const2),
        pl.BlockSpec((nb, N_HEADS, HEAD_DIM, HEAD_DIM), per_b4),
        pl.BlockSpec((nb, N_HEADS, HEAD_DIM, HEAD_DIM), per_b4),
        pl.BlockSpec((nb, CONV_W - 1, 3 * g), per_b3),
        pl.BlockSpec((nb, 1, g), per_b3),
        pl.BlockSpec((nb, CONV_W - 1, g), per_b3),
        pl.BlockSpec((nb, POOL_BUF, g), per_b3),
    ]
    out_specs = [
        pl.BlockSpec((nb * ts, 4 * g), lambda bi, n: (bi * nt + n, 0)),
        pl.BlockSpec((nb, N_HEADS, HEAD_DIM, HEAD_DIM), per_b4),
        pl.BlockSpec((nb, N_HEADS, HEAD_DIM, HEAD_DIM), per_b4),
        pl.BlockSpec((nb, CONV_W - 1, 3 * g), per_b3),
        pl.BlockSpec((nb, 1, g), per_b3),
        pl.BlockSpec((nb, CONV_W - 1, g), per_b3),
        pl.BlockSpec((nb, POOL_BUF, g), per_b3),
    ]
    out_shape = [
        jax.ShapeDtypeStruct((b * l, 4 * g), F32),
        jax.ShapeDtypeStruct((b, N_HEADS, HEAD_DIM, HEAD_DIM), F32),
        jax.ShapeDtypeStruct((b, N_HEADS, HEAD_DIM, HEAD_DIM), F32),
        jax.ShapeDtypeStruct((b, CONV_W - 1, 3 * g), F32),
        jax.ShapeDtypeStruct((b, 1, g), F32),
        jax.ShapeDtypeStruct((b, CONV_W - 1, g), F32),
        jax.ShapeDtypeStruct((b, POOL_BUF, g), F32),
    ]
    scratch = [
        pltpu.VMEM((nb, N_HEADS, HEAD_DIM, HEAD_DIM), F32),
        pltpu.VMEM((nb, N_HEADS, HEAD_DIM, HEAD_DIM), F32),
        pltpu.VMEM((nb, ts + HIST, 3 * g), F32),
        pltpu.VMEM((nb, ts + HIST, g), F32),
        pltpu.VMEM((nb, ts + HIST, g), F32),
        pltpu.VMEM((nb, 1, g), F32),
    ]
    outs = pl.pallas_call(
        functools.partial(_mixer_kernel, nt, ts, nb, pos_base),
        grid=(b // nb, nt),
        in_specs=in_specs,
        out_specs=out_specs,
        out_shape=out_shape,
        scratch_shapes=scratch,
        compiler_params=pltpu.CompilerParams(dimension_semantics=("arbitrary", "arbitrary"),
                                             vmem_limit_bytes=VMEM_LIMIT),
        name=f"mixer_ts{ts}",
    )(z, cos_t, sin_t, intra, qdec, kdec, sdec, lp["pvec"], lp["gdn_conv_w"], lp["gabp"],
      lp["lru_wa"], lp["lru_wx"], lp["pool_w"], s_ret0, s_gdn0, gconv0, lru0[:, None, :], lconv0, pool0)
    mix, ret_n, gdn_n, gconv_n, lru_n, lconv_n, pool_n = outs
    return mix, (ret_n, gdn_n, gconv_n, lru_n[:, 0, :], lconv_n, pool_n)


def _to_row_tiles(ref, x):
    for j in range(SUBLANES):
        ref[:, j, :] = x[:, j * LANES:(j + 1) * LANES]


def _from_row_tiles(ref):
    return jnp.concatenate([ref[:, j, :] for j in range(SUBLANES)], axis=-1)


def _wout_route_kernel(mix_ref, x_ref, wo_ref, g_ref, wr_ref, br_ref, x1_ref, h2_ref, rt_ref):
    x1 = x_ref[...] + jnp.dot(mix_ref[...].astype(BF16), wo_ref[...], preferred_element_type=F32)
    x1_ref[...] = x1
    h = x1 * lax.rsqrt(jnp.mean(x1 * x1, axis=-1, keepdims=True) + EPS) * g_ref[...]
    _to_row_tiles(h2_ref, h)
    logits = _mm(h, wr_ref[...], passes=3) + br_ref[...]
    lane = _iota(logits.shape, 1)
    lane_f = lane.astype(F32)
    neg = -1e30
    far = 1e9
    is_g = lane < N_GROUPS
    gl = jnp.where(is_g, logits, neg)
    gmax = jnp.max(gl, axis=-1, keepdims=True)
    gsel = jnp.min(jnp.where(gl == gmax, lane_f, far), axis=-1, keepdims=True)
    p_g = 1.0 / jnp.sum(jnp.where(is_g, jnp.exp(gl - gmax), 0.0), axis=-1, keepdims=True)
    e_group = ((lane - N_GROUPS) >> 3).astype(F32)
    in_group = jnp.where(lane >= N_GROUPS, e_group, -1.0) == gsel
    el = jnp.where(in_group, logits, neg)
    m1 = jnp.max(el, axis=-1, keepdims=True)
    i1 = jnp.min(jnp.where(el == m1, lane_f, far), axis=-1, keepdims=True)
    el2 = jnp.where(lane_f == i1, neg, el)
    m2 = jnp.max(el2, axis=-1, keepdims=True)
    i2 = jnp.min(jnp.where(el2 == m2, lane_f, far), axis=-1, keepdims=True)
    e2 = jnp.exp(m2 - m1)
    w1 = p_g / (1.0 + e2)
    w2 = p_g * e2 / (1.0 + e2)
    rt = jnp.where(lane == 0, i1 - N_GROUPS,
                   jnp.where(lane == 1, i2 - N_GROUPS, jnp.where(lane == 2, w1, jnp.where(lane == 3, w2, 0.0))))
    rt_ref[...] = rt


def _wout_route(mix, x, wo_bf16, g, wr, br, tm):
    t, d = x.shape
    row = lambda i: (i, 0)
    const = lambda i: (0, 0)
    return pl.pallas_call(
        _wout_route_kernel,
        grid=(t // tm,),
        in_specs=[pl.BlockSpec((tm, d), row), pl.BlockSpec((tm, d), row), pl.BlockSpec((d, d), const),
                  pl.BlockSpec((1, d), const), pl.BlockSpec((d, LANES), const), pl.BlockSpec((1, LANES), const)],
        out_specs=[pl.BlockSpec((tm, d), row), pl.BlockSpec((tm, SUBLANES, LANES), lambda i: (i, 0, 0)),
                   pl.BlockSpec((tm, LANES), row)],
        out_shape=[jax.ShapeDtypeStruct((t, d), F32), jax.ShapeDtypeStruct((t, SUBLANES, LANES), F32),
                   jax.ShapeDtypeStruct((t, LANES), F32)],
        compiler_params=pltpu.CompilerParams(dimension_semantics=("arbitrary",), vmem_limit_bytes=VMEM_LIMIT),
        name="wout_route",
    )(mix, x, wo_bf16, g, wr, br)


def _row_copy(src, i, dst, j, sem):
    return pltpu.make_async_copy(src.at[pl.ds(i, 1)], dst.at[pl.ds(j, 1)], sem)


def _route_plan_kernel(tm, n_tiles, rt_ref, pos_ref, tmeta_ref, emeta_ref, cnt_s, start_s, carry_s):
    ph = pl.program_id(0)
    i = pl.program_id(1)
    rt = rt_ref[...]
    lane = _iota((tm, LANES), 1)
    lane_f = lane.astype(F32)
    oh1 = jnp.where(lane_f == rt[:, 0:1], 1.0, 0.0)
    oh2 = jnp.where(lane_f == rt[:, 1:2], 1.0, 0.0)
    oh = oh1 + oh2
    colsum = jnp.sum(oh, axis=0, keepdims=True)

    @pl.when((ph == 0) & (i == 0))
    def _():
        cnt_s[...] = jnp.zeros((1, LANES), F32)

    @pl.when(ph == 0)
    def _():
        cnt_s[...] = cnt_s[...] + colsum

    @pl.when((ph == 1) & (i == 0))
    def _():
        cnt = cnt_s[...]
        ntile = jnp.floor((cnt + (tm - 1.0)) * (1.0 / tm))
        before = jnp.where(_iota((LANES, LANES), 0) < _iota((LANES, LANES), 1), 1.0, 0.0)
        st_tiles = _mm(jnp.broadcast_to(ntile, (SUBLANES, LANES)), before)[0:1]
        end_tiles = st_tiles + ntile
        start_s[...] = st_tiles * tm
        carry_s[...] = jnp.zeros((1, LANES), F32)
        is_e1 = _iota((1, LANES), 1) < N_EXPERTS
        n_used = jnp.sum(jnp.where(is_e1, ntile, 0.0), axis=-1, keepdims=True)
        last = jnp.sum(jnp.where(is_e1 & (end_tiles <= n_used - 1.0), 1.0, 0.0), axis=-1, keepdims=True)
        ti = _iota((n_tiles, LANES), 0).astype(F32)
        lane_t = _iota((n_tiles, LANES), 1)
        te = jnp.sum(jnp.where((lane_t < N_EXPERTS) & (end_tiles <= ti), 1.0, 0.0), axis=-1, keepdims=True)
        te = jnp.minimum(te, last)
        tmeta_ref[...] = jnp.where(lane_t == 0, te, jnp.where(lane_t == 1, n_used, 0.0))
        srow = _iota((SUBLANES, LANES), 0)
        emeta_ref[...] = jnp.where(srow == 0, ntile, jnp.where(srow == 1, (end_tiles - 1.0) * tm, 0.0))

    @pl.when(ph == 1)
    def _():
        earlier = jnp.where(_iota((tm, tm), 0) > _iota((tm, tm), 1), 1.0, 0.0)
        base = start_s[...] + carry_s[...] + _mm(earlier, oh)
        p1 = jnp.sum(oh1 * base, axis=-1, keepdims=True)
        p2 = jnp.sum(oh2 * base, axis=-1, keepdims=True)
        pos_ref[...] = jnp.where(lane == 0, p1, jnp.where(lane == 1, p2, 0.0))
        carry_s[...] = carry_s[...] + colsum


def _route_plan(rt, tm, n_tiles):
    t = rt.shape[0]
    pos, tmeta, emeta = pl.pallas_call(
        functools.partial(_route_plan_kernel, tm, n_tiles),
        grid=(2, t // tm),
        in_specs=[pl.BlockSpec((tm, LANES), lambda ph, i: (i, 0))],
        out_specs=[pl.BlockSpec((tm, LANES), lambda ph, i: (ph * i, 0)),
                   pl.BlockSpec((n_tiles, LANES), lambda ph, i: (0, 0)),
                   pl.BlockSpec((SUBLANES, LANES), lambda ph, i: (0, 0))],
        out_shape=[jax.ShapeDtypeStruct((t, LANES), F32), jax.ShapeDtypeStruct((n_tiles, LANES), F32),
                   jax.ShapeDtypeStruct((SUBLANES, LANES), F32)],
        scratch_shapes=[pltpu.VMEM((1, LANES), F32), pltpu.VMEM((1, LANES), F32), pltpu.VMEM((1, LANES), F32)],
        compiler_params=pltpu.CompilerParams(dimension_semantics=("arbitrary", "arbitrary"),
                                             vmem_limit_bytes=VMEM_LIMIT),
        name="route_plan",
    )(rt)
    pos1 = pos[:, 0].astype(jnp.int32)
    pos2 = pos[:, 1].astype(jnp.int32)
    tile_expert = tmeta[:, 0].astype(jnp.int32)
    n_used = tmeta[0:1, 1].astype(jnp.int32)
    has_rows = (emeta[0, :N_EXPERTS] > 0).astype(jnp.int32)
    last_tile_row = jnp.maximum(emeta[1, :N_EXPERTS], 0.0).astype(jnp.int32)
    return pos1, pos2, tile_expert, n_used, has_rows, last_tile_row


def _scatter_kernel(tm, n_tiles, min_tiles, zon_ref, zrow_ref, nu_ref, p1_ref, p2_ref, h_ref, xs_hbm,
                    zbuf, sem1, sem2, zsem, tsem):
    i = pl.program_id(0)

    @pl.when(i == 0)
    def _():
        zbuf[...] = jnp.zeros(zbuf.shape, F32)

        def zero_copy(e):
            return pltpu.make_async_copy(zbuf, xs_hbm.at[pl.ds(zrow_ref[e], tm)], zsem.at[e])

        def tail_copy(k):
            return pltpu.make_async_copy(zbuf, xs_hbm.at[pl.ds((min_tiles + k) * tm, tm)], tsem.at[k])

        def zstart(e, c):
            @pl.when(zon_ref[e] > 0)
            def _():
                zero_copy(e).start()
            return c

        def zwait(e, c):
            @pl.when(zon_ref[e] > 0)
            def _():
                zero_copy(e).wait()
            return c

        def tstart(k, c):
            @pl.when(min_tiles + k >= nu_ref[0])
            def _():
                tail_copy(k).start()
            return c

        def twait(k, c):
            @pl.when(min_tiles + k >= nu_ref[0])
            def _():
                tail_copy(k).wait()
            return c

        lax.fori_loop(0, N_EXPERTS, zstart, 0)
        lax.fori_loop(0, n_tiles - min_tiles, tstart, 0)
        lax.fori_loop(0, N_EXPERTS, zwait, 0)
        lax.fori_loop(0, n_tiles - min_tiles, twait, 0)

    def start(r, c):
        _row_copy(h_ref, r, xs_hbm, p1_ref[0, 0, r], sem1.at[r]).start()
        _row_copy(h_ref, r, xs_hbm, p2_ref[0, 0, r], sem2.at[r]).start()
        return c

    def wait(r, c):
        _row_copy(h_ref, r, xs_hbm, p1_ref[0, 0, r], sem1.at[r]).wait()
        _row_copy(h_ref, r, xs_hbm, p2_ref[0, 0, r], sem2.at[r]).wait()
        return c

    lax.fori_loop(0, tm, start, 0, unroll=8)
    lax.fori_loop(0, tm, wait, 0, unroll=8)


def _scatter_rows(h2, pos1, pos2, has_rows, last_tile_row, n_used, n_tiles, tm):
    t = h2.shape[0]
    nb = t // tm
    min_tiles = (2 * t) // tm
    smem_blk = pl.BlockSpec((1, 1, tm), lambda i, zon, zrow, nu: (i, 0, 0), memory_space=pltpu.SMEM)
    grid_spec = pltpu.PrefetchScalarGridSpec(
        num_scalar_prefetch=3,
        grid=(nb,),
        in_specs=[smem_blk, smem_blk, pl.BlockSpec((tm, SUBLANES, LANES), lambda i, zon, zrow, nu: (i, 0, 0))],
        out_specs=pl.BlockSpec(memory_space=pl.ANY),
        scratch_shapes=[pltpu.VMEM((tm, SUBLANES, LANES), F32), pltpu.SemaphoreType.DMA((tm,)),
                        pltpu.SemaphoreType.DMA((tm,)), pltpu.SemaphoreType.DMA((N_EXPERTS,)),
                        pltpu.SemaphoreType.DMA((n_tiles - min_tiles,))],
    )
    return pl.pallas_call(
        functools.partial(_scatter_kernel, tm, n_tiles, min_tiles),
        grid_spec=grid_spec,
        out_shape=jax.ShapeDtypeStruct((n_tiles * tm, SUBLANES, LANES), F32),
        compiler_params=pltpu.CompilerParams(dimension_semantics=("arbitrary",), vmem_limit_bytes=VMEM_LIMIT),
        name="moe_scatter",
    )(has_rows, last_tile_row, n_used, pos1.reshape(nb, 1, tm), pos2.reshape(nb, 1, tm), h2)


def _ffn_kernel(te_ref, nu_ref, xs_ref, wg_ref, wu_ref, wd_ref, ys_ref, wg_s, wu_s, wd_s):
    i = pl.program_id(0)
    used = i < nu_ref[0]
    new_expert = (i == 0) | (te_ref[i] != te_ref[jnp.maximum(i - 1, 0)])

    @pl.when(used & new_expert)
    def _():
        wg_s[...] = wg_ref[0].astype(BF16)
        wu_s[...] = wu_ref[0].astype(BF16)
        wd_s[...] = wd_ref[0].astype(BF16)

    @pl.when(used)
    def _():
        x = _from_row_tiles(xs_ref).astype(BF16)
        gate = jnp.dot(x, wg_s[...], preferred_element_type=F32)
        up = jnp.dot(x, wu_s[...], preferred_element_type=F32)
        act = (_silu(gate) * up).astype(BF16)
        _to_row_tiles(ys_ref, jnp.dot(act, wd_s[...], preferred_element_type=F32))

    @pl.when(jnp.logical_not(used))
    def _():
        ys_ref[...] = jnp.zeros(ys_ref.shape, F32)


def _expert_ffn(xs, tile_expert, n_used, w_gate, w_up, w_down, tm):
    p = xs.shape[0]
    _, d, de = w_gate.shape
    row = lambda i, te, nu: (jnp.minimum(i, nu[0] - 1), 0, 0)
    grid_spec = pltpu.PrefetchScalarGridSpec(
        num_scalar_prefetch=2,
        grid=(p // tm,),
        in_specs=[pl.BlockSpec((tm, SUBLANES, LANES), row),
                  pl.BlockSpec((1, d, de), lambda i, te, nu: (te[i], 0, 0)),
                  pl.BlockSpec((1, d, de), lambda i, te, nu: (te[i], 0, 0)),
                  pl.BlockSpec((1, de, d), lambda i, te, nu: (te[i], 0, 0))],
        out_specs=pl.BlockSpec((tm, SUBLANES, LANES), lambda i, te, nu: (i, 0, 0)),
        scratch_shapes=[pltpu.VMEM((d, de), BF16), pltpu.VMEM((d, de), BF16), pltpu.VMEM((de, d), BF16)],
    )
    return pl.pallas_call(
        _ffn_kernel,
        grid_spec=grid_spec,
        out_shape=jax.ShapeDtypeStruct((p, SUBLANES, LANES), F32),
        compiler_params=pltpu.CompilerParams(dimension_semantics=("arbitrary",), vmem_limit_bytes=VMEM_LIMIT),
        name="expert_ffn",
    )(tile_expert, n_used, xs, w_gate, w_up, w_down)


def _combine_kernel(tm, final_norm, p1_ref, p2_ref, x1_ref, rt_ref, fg_ref, ys_hbm, out_ref, buf1, buf2, sem1, sem2):
    def start(r, c):
        _row_copy(ys_hbm, p1_ref[0, 0, r], buf1, r, sem1.at[r]).start()
        _row_copy(ys_hbm, p2_ref[0, 0, r], buf2, r, sem2.at[r]).start()
        return c

    def wait(r, c):
        _row_copy(ys_hbm, p1_ref[0, 0, r], buf1, r, sem1.at[r]).wait()
        _row_copy(ys_hbm, p2_ref[0, 0, r], buf2, r, sem2.at[r]).wait()
        return c

    lax.fori_loop(0, tm, start, 0, unroll=8)
    lax.fori_loop(0, tm, wait, 0, unroll=8)
    rt = rt_ref[...]
    x = x1_ref[...] + rt[:, 2:3] * _from_row_tiles(buf1) + rt[:, 3:4] * _from_row_tiles(buf2)
    if final_norm:
        x = x * lax.rsqrt(jnp.mean(x * x, axis=-1, keepdims=True) + EPS) * fg_ref[...]
    out_ref[...] = x


def _combine(x1, rt, ys, pos1, pos2, final_g, final_norm, tm):
    t, d = x1.shape
    nb = t // tm
    row = lambda i: (i, 0)
    smem_blk = pl.BlockSpec((1, 1, tm), lambda i: (i, 0, 0), memory_space=pltpu.SMEM)
    return pl.pallas_call(
        functools.partial(_combine_kernel, tm, final_norm),
        grid=(nb,),
        in_specs=[smem_blk, smem_blk, pl.BlockSpec((tm, d), row), pl.BlockSpec((tm, LANES), row),
                  pl.BlockSpec((1, d), lambda i: (0, 0)), pl.BlockSpec(memory_space=pl.ANY)],
        out_specs=pl.BlockSpec((tm, d), row),
        out_shape=jax.ShapeDtypeStruct((t, d), F32),
        scratch_shapes=[pltpu.VMEM((tm, SUBLANES, LANES), F32), pltpu.VMEM((tm, SUBLANES, LANES), F32),
                        pltpu.SemaphoreType.DMA((tm,)), pltpu.SemaphoreType.DMA((tm,))],
        compiler_params=pltpu.CompilerParams(dimension_semantics=("arbitrary",), vmem_limit_bytes=VMEM_LIMIT),
        name="moe_combine",
    )(pos1.reshape(nb, 1, tm), pos2.reshape(nb, 1, tm), x1, rt, final_g, ys)


def _block_diag(w):
    h, a, b = w.shape
    out = jnp.zeros((h * a, h * b), w.dtype)
    for i in range(h):
        out = out.at[i * a:(i + 1) * a, i * b:(i + 1) * b].set(w[i])
    return out


def _layer_params(l, p):
    g = GROUP_W
    w_in = p["w_in"][l]
    d = w_in.shape[0]
    w_perm = jnp.concatenate([w_in[:, :8 * g], w_in[:, 8 * g + 8:], w_in[:, 8 * g:8 * g + 8],
                              jnp.zeros((d, LANES - 8), w_in.dtype)], axis=1)
    pvec = jnp.zeros((16, g), F32)
    rows = [p["ret_gn_g"][l], jnp.tile(p["gdn_norm_g"][l], N_HEADS), p["lru_conv_b"][l], p["lru_ba"][l],
            p["lru_bx"][l], p["lru_lambda"][l], p["pool_b"][l], p["pool_scale"][l]]
    pvec = pvec.at[0:8].set(jnp.stack(rows)).at[8:12].set(p["lru_conv_w"][l])
    gabp = jnp.zeros((8, LANES), F32).at[0, :N_HEADS].set(p["gdn_a_log"][l]).at[1, :N_HEADS].set(p["gdn_dt_bias"][l])
    wr = jnp.zeros((d, LANES), F32).at[:, :N_GROUPS].set(p["router_group_w"][l])
    wr = wr.at[:, N_GROUPS:N_GROUPS + N_EXPERTS].set(
        jnp.transpose(p["router_expert_w"][l], (1, 0, 2)).reshape(d, N_EXPERTS))
    br = jnp.zeros((1, LANES), F32).at[0, :N_GROUPS].set(p["router_group_b"][l])
    br = br.at[0, N_GROUPS:N_GROUPS + N_EXPERTS].set(p["router_expert_b"][l].reshape(N_EXPERTS))
    return dict(
        norm1_g=p["norm1_g"][l][None, :], w_in=w_perm.astype(BF16), pvec=pvec, gdn_conv_w=p["gdn_conv_w"][l],
        gabp=gabp, lru_wa=_block_diag(p["lru_wa"][l]).astype(BF16), lru_wx=_block_diag(p["lru_wx"][l]).astype(BF16),
        pool_w=_block_diag(p["pool_w"][l]).astype(BF16), w_out=p["w_out"][l].astype(BF16),
        norm2_g=p["norm2_g"][l][None, :], wr=wr, br=br,
        w_gate=p["moe_w_gate"][l], w_up=p["moe_w_up"][l], w_down=p["moe_w_down"][l])


def kernel(x_prompt, x_sample, state_ret, state_gdn, state_gdn_conv, state_lru, state_lru_conv, state_pool,
           norm1_g, w_in, ret_gn_g, gdn_conv_w, gdn_a_log, gdn_dt_bias, gdn_norm_g,
           lru_conv_w, lru_conv_b, lru_wa, lru_ba, lru_wx, lru_bx, lru_lambda,
           pool_w, pool_b, pool_scale, w_out, norm2_g,
           router_group_w, router_group_b, router_expert_w, router_expert_b,
           moe_w_gate, moe_w_up, moe_w_down, final_g):
    params = dict(norm1_g=norm1_g, w_in=w_in, ret_gn_g=ret_gn_g, gdn_conv_w=gdn_conv_w, gdn_a_log=gdn_a_log,
                  gdn_dt_bias=gdn_dt_bias, gdn_norm_g=gdn_norm_g, lru_conv_w=lru_conv_w, lru_conv_b=lru_conv_b,
                  lru_wa=lru_wa, lru_ba=lru_ba, lru_wx=lru_wx, lru_bx=lru_bx, lru_lambda=lru_lambda,
                  pool_w=pool_w, pool_b=pool_b, pool_scale=pool_scale, w_out=w_out, norm2_g=norm2_g,
                  router_group_w=router_group_w, router_group_b=router_group_b,
                  router_expert_w=router_expert_w, router_expert_b=router_expert_b,
                  moe_w_gate=moe_w_gate, moe_w_up=moe_w_up, moe_w_down=moe_w_down)
    bp, lp_, d = x_prompt.shape
    bs, ls, _ = x_sample.shape
    depth = w_in.shape[0]
    tp, ts_ = bp * lp_, bs * ls
    t = tp + ts_
    tile_p = min(256, lp_)
    tm = 256
    n_tiles = (2 * t + N_EXPERTS * (tm - 1) + tm - 1) // tm

    x = jnp.concatenate([x_prompt.reshape(tp, d), x_sample.reshape(ts_, d)], axis=0)
    zero_states = (jnp.zeros((bp,) + state_ret.shape[2:], F32), jnp.zeros((bp,) + state_gdn.shape[2:], F32),
                   jnp.zeros((bp,) + state_gdn_conv.shape[2:], F32), jnp.zeros((bp,) + state_lru.shape[2:], F32),
                   jnp.zeros((bp,) + state_lru_conv.shape[2:], F32), jnp.zeros((bp,) + state_pool.shape[2:], F32))
    sample_states = (state_ret, state_gdn, state_gdn_conv, state_lru, state_lru_conv, state_pool)
    new_p, new_s = [], []
    for l in range(depth):
        lp = _layer_params(l, params)
        z = _norm_win(x, lp["norm1_g"], lp["w_in"], tm)
        mix_p, st_p = _mixer(z, 0, bp, lp_, tile_p, 1, 0, zero_states, lp)
        mix_s, st_s = _mixer(z, tp, bs, ls, ls, SAMPLE_SEQS_PER_STEP, PAST_LEN,
                             tuple(s[l] for s in sample_states), lp)
        new_p.append(st_p)
        new_s.append(st_s)
        mix = jnp.concatenate([mix_p, mix_s], axis=0)
        x1, h2, rt = _wout_route(mix, x, lp["w_out"], lp["norm2_g"], lp["wr"], lp["br"], tm)
        pos1, pos2, tile_expert, n_used, has_rows, last_tile_row = _route_plan(rt, tm, n_tiles)
        xs = _scatter_rows(h2, pos1, pos2, has_rows, last_tile_row, n_used, n_tiles, tm)
        ys = _expert_ffn(xs, tile_expert, n_used, lp["w_gate"], lp["w_up"], lp["w_down"], tm)
        x = _combine(x1, rt, ys, pos1, pos2, final_g[None, :], l == depth - 1, tm)
    y_prompt = x[:tp].reshape(bp, lp_, d)
    y_sample = x[tp:].reshape(bs, ls, d)
    stack = lambda news: [jnp.stack([ns[j] for ns in news], axis=0) for j in range(6)]
    return (y_prompt, y_sample, *stack(new_p), *stack(new_s))
```

```python
import functools

import jax
import jax.numpy as jnp
from jax import lax
from jax.experimental import pallas as pl
from jax.experimental.pallas import tpu as pltpu

F32 = jnp.float32
BF16 = jnp.bfloat16

EPS = 1e-6
HEAD_DIM = 64
N_HEADS = 4
GROUP_W = 256
CONV_W = 4
LRU_C = 8.0
POOL_WINDOWS = (2, 4, 8, 16)
POOL_BUF = 15
ROPE_BASE = 10000.0
N_GROUPS = 4
EXPERTS_PER_GROUP = 8
N_EXPERTS = 32
PAST_LEN = 16384

LANES = 128
SUBLANES = 8
HIST = 16
Z_AB = 11 * GROUP_W
NZ = Z_AB + LANES
INV_BASE = 8
RESID_PASSES = 3
SAMPLE_SEQS_PER_STEP = 16
VMEM_LIMIT = 56 * 1024 * 1024

NN = (((1,), (0,)), ((), ()))
NT = (((1,), (1,)), ((), ()))
TN = (((0,), (0,)), ((), ()))


def _mm(a, b, dims=NN, passes=1):
    ah = a.astype(BF16)
    bh = b.astype(BF16)
    dot = functools.partial(lax.dot_general, dimension_numbers=dims, preferred_element_type=F32)
    if passes == 1:
        return dot(ah, bh)
    al = (a - ah.astype(F32)).astype(BF16)
    bl = (b - bh.astype(F32)).astype(BF16)
    return dot(ah, bh) + (dot(ah, bl) + dot(al, bh))


def _sigmoid(x):
    return 1.0 / (1.0 + jnp.exp(-x))


def _silu(x):
    return x * _sigmoid(x)


def _softplus(x):
    return jnp.maximum(x, 0.0) + jnp.log1p(jnp.exp(-jnp.abs(x)))


def _gelu_tanh(x):
    return 0.5 * x * (1.0 + jnp.tanh(0.7978845608028654 * (x + 0.044715 * (x * x * x))))


def _iota(shape, axis):
    return lax.broadcasted_iota(jnp.int32, shape, axis)


def _shift_rows(x, d, fill, seg=None):
    t = _iota(x.shape, 0)
    if seg is not None:
        t = t & (seg - 1)
    return jnp.where(t >= d, pltpu.roll(x, d, 0), fill)


def _cumsum_rows(x, seg=None):
    d = 1
    while d < (seg or x.shape[0]):
        x = x + _shift_rows(x, d, 0.0, seg)
        d *= 2
    return x


def _swap_half_heads(x):
    n = x.shape[1]
    lower = (_iota(x.shape, 1) & (HEAD_DIM - 1)) < HEAD_DIM // 2
    return jnp.where(lower, pltpu.roll(x, n - HEAD_DIM // 2, 1), pltpu.roll(x, HEAD_DIM // 2, 1))


def _inv_unit_lower(mats, c=None):
    shape = mats[0].shape
    c = c or shape[0]
    row = _iota(shape, 0)
    col = _iota(shape, 1)
    nb = min(INV_BASE, c)
    shift = nb.bit_length() - 1
    in_base = (row >> shift) == (col >> shift)
    ps = [jnp.where(in_base, a, 0.0) for a in mats]
    eye = jnp.where(row == col, 1.0, 0.0)
    xs = [eye - d for d in ps]
    k = 2
    while k < nb:
        ps = [_mm(p, p) for p in ps]
        xs = [x + _mm(x, p) for x, p in zip(xs, ps)]
        k *= 2
    m = nb
    while m < c:
        s1 = m.bit_length() - 1
        in_off = ((row >> (s1 + 1)) == (col >> (s1 + 1))) & ((row >> s1) != (col >> s1))
        ys = [_mm(x, jnp.where(in_off, a, 0.0)) for x, a in zip(xs, mats)]
        xs = [x - _mm(y, x) for x, y in zip(xs, ys)]
        m *= 2
    return xs


def _solve_unit_lower(mats, tinvs, ws):
    us = [_mm(t, w) for t, w in zip(tinvs, ws)]
    rs = [w - (u + _mm(a, u, passes=RESID_PASSES)) for a, u, w in zip(mats, us, ws)]
    return [u + _mm(t, r) for u, t, r in zip(us, tinvs, rs)]


def _pair_specs(na, tm, d):
    return [pl.BlockSpec((tm, d), lambda i: (jnp.minimum(i, na - 1), 0)),
            pl.BlockSpec((tm, d), lambda i: (jnp.maximum(i - na, 0), 0))]


def _pair_tile(na, a_ref, b_ref):
    return jnp.where(pl.program_id(0) < na, a_ref[...], b_ref[...])


def _norm_win_kernel(na, xa_ref, xb_ref, g_ref, w_ref, z_ref):
    x = _pair_tile(na, xa_ref, xb_ref)
    h = x * lax.rsqrt(jnp.mean(x * x, axis=-1, keepdims=True) + EPS) * g_ref[...]
    z_ref[...] = jnp.dot(h.astype(BF16), w_ref[...], preferred_element_type=F32)


def _norm_win(x_pair, g, w_bf16, tm):
    xa, xb = x_pair
    d = xa.shape[1]
    na, t = xa.shape[0] // tm, xa.shape[0] + xb.shape[0]
    nz = w_bf16.shape[1]
    return pl.pallas_call(
        functools.partial(_norm_win_kernel, na),
        grid=(t // tm,),
        in_specs=_pair_specs(na, tm, d) + [pl.BlockSpec((1, d), lambda i: (0, 0)),
                                           pl.BlockSpec((d, nz), lambda i: (0, 0))],
        out_specs=pl.BlockSpec((tm, nz), lambda i: (i, 0)),
        out_shape=jax.ShapeDtypeStruct((t, nz), F32),
        compiler_params=pltpu.CompilerParams(dimension_semantics=("arbitrary",), vmem_limit_bytes=VMEM_LIMIT),
        name="norm_win",
    )(xa, xb, g, w_bf16)


def _retention_tile(zq, zk, zv, cos, sin, intra_ref, qdec, kdec, sdec, s_ref):
    rq = zq * cos + _swap_half_heads(zq) * sin
    rk = (zk * cos + _swap_half_heads(zk) * sin) * (HEAD_DIM ** -0.5)
    kd = rk * kdec
    heads = range(N_HEADS)
    sls = [slice(h * HEAD_DIM, (h + 1) * HEAD_DIM) for h in heads]
    ss = [s_ref[h] for h in heads]
    scores = [_mm(rq[:, sl], rk[:, sl], NT) * intra_ref[h] for h, sl in zip(heads, sls)]
    os_ = [_mm(sc, zv[:, sl]) + _mm(rq[:, sl], s) * qdec[:, sl] for sc, s, sl in zip(scores, ss, sls)]
    s_new = [s * sdec[h:h + 1, :HEAD_DIM] + _mm(kd[:, sl], zv[:, sl], TN) for h, s, sl in zip(heads, ss, sls)]
    for h in heads:
        s_ref[h] = s_new[h]
    outs = []
    for o in os_:
        mu = jnp.mean(o, axis=-1, keepdims=True)
        oc = o - mu
        var = jnp.mean(oc * oc, axis=-1, keepdims=True)
        outs.append(oc * lax.rsqrt(var + EPS))
    return jnp.concatenate(outs, axis=-1)


def _head_lane_select(c, h):
    return jnp.where(_iota((c, LANES), 1) == h, 1.0, 0.0).astype(BF16)


def _split3(x):
    x0 = x.astype(BF16)
    r1 = x - x0.astype(F32)
    x1 = r1.astype(BF16)
    x2 = (r1 - x1.astype(F32)).astype(BF16)
    return x0, x1, x2


def _gdn_tile(gq, gk, gv, gab, s_ref):
    c = gq.shape[0]
    gc = _cumsum_rows(gab)
    egc = jnp.exp(gc)
    gparts = _split3(gc)
    row = _iota((c, c), 0)
    col = _iota((c, c), 1)
    incl = row >= col
    strict = row > col
    dotnt = functools.partial(lax.dot_general, dimension_numbers=NT, preferred_element_type=F32)
    heads = range(N_HEADS)
    sls = [slice(h * HEAD_DIM, (h + 1) * HEAD_DIM) for h in heads]
    ss = [s_ref[h] for h in heads]
    gcols = [gc[:, h:h + 1] for h in heads]
    bcols = [gab[:, N_HEADS + h:N_HEADS + h + 1] for h in heads]
    ecols = [egc[:, h:h + 1] for h in heads]
    decs = []
    for h in heads:
        sel = _head_lane_select(c, h)
        grow = dotnt(sel, gparts[0]) + (dotnt(sel, gparts[1]) + dotnt(sel, gparts[2]))
        decs.append(jnp.where(incl, jnp.exp(jnp.where(incl, gcols[h] - grow, 0.0)), 0.0))
    mats = [jnp.where(strict, bcols[h] * decs[h] * _mm(gk[:, sl], gk[:, sl], NT), 0.0) for h, sl in zip(heads, sls)]
    tinvs = _inv_unit_lower(mats)
    ws = [bcols[h] * (gv[:, sl] - ecols[h] * _mm(gk[:, sl], ss[h])) for h, sl in zip(heads, sls)]
    us = _solve_unit_lower(mats, tinvs, ws)
    qks = [_mm(gq[:, sl], gk[:, sl], NT) * decs[h] for h, sl in zip(heads, sls)]
    outs = [ecols[h] * _mm(gq[:, sl], ss[h]) + _mm(qks[h], us[h]) for h, sl in zip(heads, sls)]
    glasts = [gc[c - 1:c, h:h + 1] for h in heads]
    s_new = [jnp.exp(glasts[h]) * ss[h] + _mm(jnp.exp(glasts[h] - gcols[h]) * gk[:, sl], us[h], TN)
             for h, sl in zip(heads, sls)]
    for h in heads:
        s_ref[h] = s_new[h]
    return outs


def _mixer_kernel(nt, ts, nb, pos_base,
                  z_ref, cos_ref, sin_ref, intra_ref, qdec_ref, kdec_ref, sdec_ref, pv_ref, gcw_ref, gabp_ref,
                  wa_ref, wx_ref, pw_ref, sret0, sgdn0, gconv0, lru0, lconv0, pool0,
                  mix_ref, ret_o, gdn_o, gconv_o, lru_o, lconv_o, pool_o,
                  s_ret, s_gdn, ext_g, ext_l, ext_p, h_lru):
    n = pl.program_id(1)
    g = GROUP_W
    base = HIST - (CONV_W - 1)

    @pl.when(n == 0)
    def _():
        s_ret[...] = sret0[...]
        s_gdn[...] = sgdn0[...]
        ext_g[:, 0:HIST, :] = jnp.zeros((nb, HIST, 3 * g), F32)
        ext_g[:, base:HIST, :] = gconv0[...]
        ext_l[:, 0:HIST, :] = jnp.zeros((nb, HIST, g), F32)
        ext_l[:, base:HIST, :] = lconv0[...]
        ext_p[:, 0:HIST, :] = jnp.zeros((nb, HIST, g), F32)
        ext_p[:, HIST - POOL_BUF:HIST, :] = pool0[...]
        h_lru[...] = lru0[...]

    pv = pv_ref[...]
    ret_gn_g, gdn_norm_g = pv[0:1], pv[1:2]
    lru_conv_b, lru_ba, lru_bx, lru_lambda = pv[2:3], pv[3:4], pv[4:5], pv[5:6]
    pool_b, pool_scale = pv[6:7], pv[7:8]
    lru_conv_w = pv[8:12]
    gcw = gcw_ref[...]
    gabp = gabp_ref[...]

    for j in range(nb):
        rows = slice(j * ts, (j + 1) * ts)
        eg, el, ep = ext_g.at[j], ext_l.at[j], ext_p.at[j]

        ro = _retention_tile(z_ref[rows, 0:g], z_ref[rows, g:2 * g], z_ref[rows, 2 * g:3 * g], cos_ref[...],
                             sin_ref[...], intra_ref, qdec_ref[...], kdec_ref[...], sdec_ref[...], s_ret.at[j])
        mix_ref[rows, 0:g] = ro * ret_gn_g * _silu(z_ref[rows, 3 * g:4 * g])

        eg[HIST:HIST + ts, :] = z_ref[rows, 4 * g:7 * g]
        y = eg[base:base + ts, :] * gcw[0:1]
        for i in range(1, CONV_W):
            y = y + eg[base + i:base + i + ts, :] * gcw[i:i + 1]
        qkv = _silu(y)
        zab = z_ref[rows, Z_AB:Z_AB + LANES]
        lane = _iota(zab.shape, 1)
        log_a = -jnp.exp(gabp[0:1]) * _softplus(zab + gabp[1:2])
        gab = jnp.where(lane < N_HEADS, log_a, jnp.where(lane < 2 * N_HEADS, _sigmoid(zab), 0.0))
        gq_parts, gk_parts = [], []
        for h in range(N_HEADS):
            qh = qkv[:, h * HEAD_DIM:(h + 1) * HEAD_DIM]
            kh = qkv[:, g + h * HEAD_DIM:g + (h + 1) * HEAD_DIM]
            gq_parts.append(qh * lax.rsqrt(jnp.sum(qh * qh, axis=-1, keepdims=True) + EPS) * (HEAD_DIM ** -0.5))
            gk_parts.append(kh * lax.rsqrt(jnp.sum(kh * kh, axis=-1, keepdims=True) + EPS))
        go = _gdn_tile(jnp.concatenate(gq_parts, axis=-1), jnp.concatenate(gk_parts, axis=-1),
                       qkv[:, 2 * g:3 * g], gab, s_gdn.at[j])
        go = [o * lax.rsqrt(jnp.mean(o * o, axis=-1, keepdims=True) + EPS) for o in go]
        mix_ref[rows, g:2 * g] = jnp.concatenate(go, axis=-1) * gdn_norm_g * _silu(z_ref[rows, 7 * g:8 * g])

        el[HIST:HIST + ts, :] = z_ref[rows, 8 * g:9 * g]
        xc = el[base:base + ts, :] * lru_conv_w[0:1]
        for i in range(1, CONV_W):
            xc = xc + el[base + i:base + i + ts, :] * lru_conv_w[i:i + 1]
        xc = xc + lru_conv_b
        r = _sigmoid(jnp.dot(xc.astype(BF16), wa_ref[...], preferred_element_type=F32) + lru_ba)
        gate_i = _sigmoid(jnp.dot(xc.astype(BF16), wx_ref[...], preferred_element_type=F32) + lru_bx)
        log_at = -LRU_C * r * _softplus(-lru_lambda)
        av = jnp.exp(log_at)
        bv = jnp.sqrt(-jnp.tanh(log_at) * (av * av + 1.0)) * (gate_i * xc)
        bv = bv + jnp.where(_iota(bv.shape, 0) == 0, av * h_lru[j], 0.0)
        d = 1
        while d < ts:
            bv = av * _shift_rows(bv, d, 0.0) + bv
            av = av * _shift_rows(av, d, 1.0)
            d *= 2
        h_lru[j] = bv[ts - 1:ts, :]
        mix_ref[rows, 2 * g:3 * g] = bv * _gelu_tanh(z_ref[rows, 9 * g:10 * g])

        u = z_ref[rows, 10 * g:11 * g]
        ep[HIST:HIST + ts, :] = u
        e = ep[...]
        sums = []
        step = 1
        for _ in POOL_WINDOWS:
            e = e + pltpu.roll(e, step, 0)
            sums.append(e[HIST:HIST + ts, :])
            step *= 2
        pos1 = (pos_base + 1 + n * ts + _iota((ts, g), 0)).astype(F32)
        lane_g = _iota((ts, g), 1) >> 6
        pooled = sums[-1] / jnp.minimum(pos1, float(POOL_WINDOWS[-1]))
        for gi in range(len(POOL_WINDOWS) - 2, -1, -1):
            pooled = jnp.where(lane_g == gi, sums[gi] / jnp.minimum(pos1, float(POOL_WINDOWS[gi])), pooled)
        pooled = pooled - u
        yp = jnp.dot(pooled.astype(BF16), pw_ref[...], preferred_element_type=F32) + pool_b
        mix_ref[rows, 3 * g:4 * g] = yp * pool_scale

        tail_g = eg[ts:ts + HIST, :]
        tail_l = el[ts:ts + HIST, :]
        tail_p = ep[ts:ts + HIST, :]
        eg[0:HIST, :] = tail_g
        el[0:HIST, :] = tail_l
        ep[0:HIST, :] = tail_p

    @pl.when(n == nt - 1)
    def _():
        ret_o[...] = s_ret[...]
        gdn_o[...] = s_gdn[...]
        gconv_o[...] = ext_g[:, base:HIST, :]
        lconv_o[...] = ext_l[:, base:HIST, :]
        pool_o[...] = ext_p[:, HIST - POOL_BUF:HIST, :]
        lru_o[...] = h_lru[...]


def _decode_mixer_kernel(ts, nb, pos_base,
                         z_ref, cos_ref, sin_ref, intra_ref, qdec_ref, kdec_ref, sdec_ref, pv_ref, gcw_ref, gabp_ref,
                         wa_ref, wx_ref, pw_ref, sret0, sgdn0, gconv0, lru0, lconv0, pool0,
                         mix_ref, ret_o, gdn_o, gconv_o, lru_o, lconv_o, pool_o,
                         ext_g, ext_l, ext_p):
    g = GROUP_W
    r_all = nb * ts
    base = HIST - (CONV_W - 1)
    seg_shift = ts.bit_length() - 1
    row2 = _iota((r_all, r_all), 0)
    col2 = _iota((r_all, r_all), 1)
    same = (row2 >> seg_shift) == (col2 >> seg_shift)
    incl = same & (row2 >= col2)
    strict = same & (row2 > col2)
    seq_l = _iota((r_all, LANES), 0) >> seg_shift
    half_l = _iota((r_all, LANES), 1) >> 6
    exp_masks = [((seq_l >> 1) == c) & ((seq_l & 1) == half_l) for c in range(nb // 2)]

    def expand(x):
        x2 = jnp.concatenate([x, x], axis=-1)
        return jnp.concatenate([jnp.where(m, x2, 0.0) for m in exp_masks], axis=-1)

    def stacked(state_ref, h):
        return jnp.concatenate([state_ref[j, h] for j in range(nb)], axis=0)

    def unstack(out_ref, h, sv):
        for j in range(nb):
            out_ref[j, h] = sv[j * HEAD_DIM:(j + 1) * HEAD_DIM, :]

    pv = pv_ref[...]
    ret_gn_g, gdn_norm_g = pv[0:1], pv[1:2]
    lru_conv_b, lru_ba, lru_bx, lru_lambda = pv[2:3], pv[3:4], pv[4:5], pv[5:6]
    pool_b, pool_scale = pv[6:7], pv[7:8]
    lru_conv_w = pv[8:12]
    gcw = gcw_ref[...]
    gabp = gabp_ref[...]

    ext_g[:, 0:HIST, :] = jnp.zeros((nb, HIST, 3 * g), F32)
    ext_g[:, base:HIST, :] = gconv0[...]
    ext_l[:, 0:HIST, :] = jnp.zeros((nb, HIST, g), F32)
    ext_l[:, base:HIST, :] = lconv0[...]
    ext_p[:, 0:HIST, :] = jnp.zeros((nb, HIST, g), F32)
    ext_p[:, HIST - POOL_BUF:HIST, :] = pool0[...]
    y_parts, xc_parts, sum_parts, h0_parts = [], [], [[] for _ in POOL_WINDOWS], []
    for j in range(nb):
        rows = slice(j * ts, (j + 1) * ts)
        eg, el, ep = ext_g.at[j], ext_l.at[j], ext_p.at[j]
        eg[HIST:HIST + ts, :] = z_ref[rows, 4 * g:7 * g]
        el[HIST:HIST + ts, :] = z_ref[rows, 8 * g:9 * g]
        ep[HIST:HIST + ts, :] = z_ref[rows, 10 * g:11 * g]
        y = eg[base:base + ts, :] * gcw[0:1]
        xc = el[base:base + ts, :] * lru_conv_w[0:1]
        for i in range(1, CONV_W):
            y = y + eg[base + i:base + i + ts, :] * gcw[i:i + 1]
            xc = xc + el[base + i:base + i + ts, :] * lru_conv_w[i:i + 1]
        y_parts.append(y)
        xc_parts.append(xc)
        e = ep[...]
        step = 1
        for wi in range(len(POOL_WINDOWS)):
            e = e + pltpu.roll(e, step, 0)
            sum_parts[wi].append(e[HIST:HIST + ts, :])
            step *= 2
        h0_parts.append(jnp.broadcast_to(lru0[j], (ts, g)))
        gconv_o[j] = eg[ts + base:ts + HIST, :]
        lconv_o[j] = el[ts + base:ts + HIST, :]
        pool_o[j] = ep[ts + HIST - POOL_BUF:ts + HIST, :]

    zq, zk, zv = z_ref[:, 0:g], z_ref[:, g:2 * g], z_ref[:, 2 * g:3 * g]
    cos, sin = cos_ref[...], sin_ref[...]
    qdec, kdec, sdec = qdec_ref[...], kdec_ref[...], sdec_ref[...]
    rq = zq * cos + _swap_half_heads(zq) * sin
    rk = (zk * cos + _swap_half_heads(zk) * sin) * (HEAD_DIM ** -0.5)
    kd = rk * kdec
    outs = []
    for h in range(N_HEADS):
        sl = slice(h * HEAD_DIM, (h + 1) * HEAD_DIM)
        q, k, v = rq[:, sl], rk[:, sl], zv[:, sl]
        sv = stacked(sret0, h)
        scores = _mm(q, k, NT) * intra_ref[h]
        o = _mm(scores, v) + _mm(expand(q), sv) * qdec[:, sl]
        unstack(ret_o, h, sv * sdec[h:h + 1, :HEAD_DIM] + _mm(expand(kd[:, sl]), v, TN))
        mu = jnp.mean(o, axis=-1, keepdims=True)
        oc = o - mu
        var = jnp.mean(oc * oc, axis=-1, keepdims=True)
        outs.append(oc * lax.rsqrt(var + EPS))
    mix_ref[:, 0:g] = jnp.concatenate(outs, axis=-1) * ret_gn_g * _silu(z_ref[:, 3 * g:4 * g])

    qkv = _silu(jnp.concatenate(y_parts, axis=0))
    zab = z_ref[:, Z_AB:Z_AB + LANES]
    lane = _iota(zab.shape, 1)
    log_a = -jnp.exp(gabp[0:1]) * _softplus(zab + gabp[1:2])
    gab = jnp.where(lane < N_HEADS, log_a, jnp.where(lane < 2 * N_HEADS, _sigmoid(zab), 0.0))
    gc = _cumsum_rows(gab, seg=ts)
    egc = jnp.exp(gc)
    gparts = _split3(gc)
    dotnn = functools.partial(lax.dot_general, dimension_numbers=NN, preferred_element_type=F32)
    dotnt = functools.partial(lax.dot_general, dimension_numbers=NT, preferred_element_type=F32)
    same_b = jnp.where(same, 1.0, 0.0).astype(BF16)
    a0, a1, a2 = _split3(gab)
    glast = dotnn(same_b, a0) + (dotnn(same_b, a1) + dotnn(same_b, a2))
    pick = jnp.where(((_iota((nb * HEAD_DIM, r_all), 0) >> 6) == (_iota((nb * HEAD_DIM, r_all), 1) >> seg_shift))
                     & ((_iota((nb * HEAD_DIM, r_all), 1) & (ts - 1)) == 0), 1.0, 0.0).astype(BF16)
    e0, e1, e2 = _split3(jnp.exp(glast))
    eglast_s = dotnn(pick, e0) + (dotnn(pick, e1) + dotnn(pick, e2))
    heads = range(N_HEADS)
    qs, ks, vs, decs = [], [], [], []
    for h in heads:
        qh = qkv[:, h * HEAD_DIM:(h + 1) * HEAD_DIM]
        kh = qkv[:, g + h * HEAD_DIM:g + (h + 1) * HEAD_DIM]
        qs.append(qh * lax.rsqrt(jnp.sum(qh * qh, axis=-1, keepdims=True) + EPS) * (HEAD_DIM ** -0.5))
        ks.append(kh * lax.rsqrt(jnp.sum(kh * kh, axis=-1, keepdims=True) + EPS))
        vs.append(qkv[:, 2 * g + h * HEAD_DIM:2 * g + (h + 1) * HEAD_DIM])
        sel = _head_lane_select(r_all, h)
        grow = dotnt(sel, gparts[0]) + (dotnt(sel, gparts[1]) + dotnt(sel, gparts[2]))
        decs.append(jnp.where(incl, jnp.exp(jnp.where(incl, gc[:, h:h + 1] - grow, 0.0)), 0.0))
    gcols = [gc[:, h:h + 1] for h in heads]
    bcols = [gab[:, N_HEADS + h:N_HEADS + h + 1] for h in heads]
    ecols = [egc[:, h:h + 1] for h in heads]
    svs = [stacked(sgdn0, h) for h in heads]
    mats = [jnp.where(strict, bcols[h] * decs[h] * _mm(ks[h], ks[h], NT), 0.0) for h in heads]
    tinvs = _inv_unit_lower(mats, ts)
    ws = [bcols[h] * (vs[h] - ecols[h] * _mm(expand(ks[h]), svs[h])) for h in heads]
    us = _solve_unit_lower(mats, tinvs, ws)
    os_ = [ecols[h] * _mm(expand(qs[h]), svs[h]) + _mm(_mm(qs[h], ks[h], NT) * decs[h], us[h]) for h in heads]
    for h in heads:
        kdx = jnp.exp(glast[:, h:h + 1] - gcols[h]) * ks[h]
        unstack(gdn_o, h, eglast_s[:, h:h + 1] * svs[h] + _mm(expand(kdx), us[h], TN))
    outs = [o * lax.rsqrt(jnp.mean(o * o, axis=-1, keepdims=True) + EPS) for o in os_]
    mix_ref[:, g:2 * g] = jnp.concatenate(outs, axis=-1) * gdn_norm_g * _silu(z_ref[:, 7 * g:8 * g])

    xc = jnp.concatenate(xc_parts, axis=0) + lru_conv_b
    r = _sigmoid(jnp.dot(xc.astype(BF16), wa_ref[...], preferred_element_type=F32) + lru_ba)
    gate_i = _sigmoid(jnp.dot(xc.astype(BF16), wx_ref[...], preferred_element_type=F32) + lru_bx)
    log_at = -LRU_C * r * _softplus(-lru_lambda)
    av = jnp.exp(log_at)
    bv = jnp.sqrt(-jnp.tanh(log_at) * (av * av + 1.0)) * (gate_i * xc)
    first = (_iota(bv.shape, 0) & (ts - 1)) == 0
    bv = bv + jnp.where(first, av * jnp.concatenate(h0_parts, axis=0), 0.0)
    d = 1
    while d < ts:
        bv = av * _shift_rows(bv, d, 0.0, ts) + bv
        av = av * _shift_rows(av, d, 1.0, ts)
        d *= 2
    for j in range(nb):
        lru_o[j] = bv[(j + 1) * ts - 1:(j + 1) * ts, :]
    mix_ref[:, 2 * g:3 * g] = bv * _gelu_tanh(z_ref[:, 9 * g:10 * g])

    u_in = z_ref[:, 10 * g:11 * g]
    pos1 = (pos_base + 1 + (_iota((r_all, g), 0) & (ts - 1))).astype(F32)
    lane_g = _iota((r_all, g), 1) >> 6
    sums = [jnp.concatenate(parts, axis=0) for parts in sum_parts]
    pooled = sums[-1] / jnp.minimum(pos1, float(POOL_WINDOWS[-1]))
    for gi in range(len(POOL_WINDOWS) - 2, -1, -1):
        pooled = jnp.where(lane_g == gi, sums[gi] / jnp.minimum(pos1, float(POOL_WINDOWS[gi])), pooled)
    pooled = pooled - u_in
    yp = jnp.dot(pooled.astype(BF16), pw_ref[...], preferred_element_type=F32) + pool_b
    mix_ref[:, 3 * g:4 * g] = yp * pool_scale


def _ret_consts(c):
    log_g = jnp.log1p(-jnp.exp2(-5.0 - jnp.arange(N_HEADS, dtype=F32)))
    idx = jnp.arange(c, dtype=F32)
    diff = idx[:, None] - idx[None, :]
    intra = jnp.where(diff >= 0, jnp.exp(log_g[:, None, None] * jnp.maximum(diff, 0.0)), 0.0)
    qdec = jnp.exp(log_g[:, None] * (idx + 1.0))
    kdec = jnp.exp(log_g[:, None] * (c - 1.0 - idx))
    sdec = jnp.exp(log_g * c)
    expand = lambda t: jnp.repeat(t.T, HEAD_DIM, axis=1)
    sdec_tab = jnp.zeros((8, LANES), F32).at[:N_HEADS, :].set(sdec[:, None])
    return intra, expand(qdec), expand(kdec), sdec_tab


def _rope_tables(pos):
    half = HEAD_DIM // 2
    inv = ROPE_BASE ** (-jnp.arange(half, dtype=F32) / half)
    ang = pos.astype(F32)[:, None] * inv[None, :]
    cos, sin = jnp.cos(ang), jnp.sin(ang)
    cos_t = jnp.tile(jnp.concatenate([cos, cos], axis=-1), (1, N_HEADS))
    sin_t = jnp.tile(jnp.concatenate([-sin, sin], axis=-1), (1, N_HEADS))
    return cos_t, sin_t


def _mixer(z, z_row0, b, l, ts, nb, pos_base, states, layer, lp):
    nt = l // ts
    assert nb == 1 or (nt == 1 and nb % 2 == 0 and ts == INV_BASE)
    g = GROUP_W
    pos = pos_base + jnp.arange(l, dtype=jnp.int32)
    cos_t, sin_t = _rope_tables(pos)
    intra, qdec, kdec, sdec = _ret_consts(ts)
    rt_ = nb * ts
    if nb > 1:
        cos_t, sin_t, qdec, kdec = (jnp.tile(a, (nb, 1)) for a in (cos_t, sin_t, qdec, kdec))
        intra = jnp.einsum("ab,hij->haibj", jnp.eye(nb, dtype=F32), intra).reshape(N_HEADS, rt_, rt_)
    s_ret0, s_gdn0, gconv0, lru0, lconv0, pool0 = states
    blk0 = z_row0 // rt_
    const2 = lambda bi, n: (0, 0)
    per_b3 = lambda bi, n: (bi, 0, 0)
    per_b4 = lambda bi, n: (bi, 0, 0, 0)
    st3 = lambda bi, n: (layer, bi, 0, 0)
    st4 = lambda bi, n: (layer, bi, 0, 0, 0)
    in_specs = [
        pl.BlockSpec((rt_, NZ), lambda bi, n: (blk0 + bi * nt + n, 0)),
        pl.BlockSpec((rt_, g), lambda bi, n: (n, 0)),
        pl.BlockSpec((rt_, g), lambda bi, n: (n, 0)),
        pl.BlockSpec((N_HEADS, rt_, rt_), lambda bi, n: (0, 0, 0)),
        pl.BlockSpec((rt_, g), const2),
        pl.BlockSpec((rt_, g), const2),
        pl.BlockSpec((8, LANES), const2),
        pl.BlockSpec((16, g), const2),
        pl.BlockSpec((CONV_W, 3 * g), const2),
        pl.BlockSpec((8, LANES), const2),
        pl.BlockSpec((g, g), const2),
        pl.BlockSpec((g, g), const2),
        pl.BlockSpec((g, g), const2),
        pl.BlockSpec((None, nb, N_HEADS, HEAD_DIM, HEAD_DIM), st4),
        pl.BlockSpec((None, nb, N_HEADS, HEAD_DIM, HEAD_DIM), st4),
        pl.BlockSpec((None, nb, CONV_W - 1, 3 * g), st3),
        pl.BlockSpec((None, nb, 1, g), st3),
        pl.BlockSpec((None, nb, CONV_W - 1, g), st3),
        pl.BlockSpec((None, nb, POOL_BUF, g), st3),
    ]
    out_specs = [
        pl.BlockSpec((rt_, 4 * g), lambda bi, n: (bi * nt + n, 0)),
        pl.BlockSpec((nb, N_HEADS, HEAD_DIM, HEAD_DIM), per_b4),
        pl.BlockSpec((nb, N_HEADS, HEAD_DIM, HEAD_DIM), per_b4),
        pl.BlockSpec((nb, CONV_W - 1, 3 * g), per_b3),
        pl.BlockSpec((nb, 1, g), per_b3),
        pl.BlockSpec((nb, CONV_W - 1, g), per_b3),
        pl.BlockSpec((nb, POOL_BUF, g), per_b3),
    ]
    out_shape = [
        jax.ShapeDtypeStruct((b * l, 4 * g), F32),
        jax.ShapeDtypeStruct((b, N_HEADS, HEAD_DIM, HEAD_DIM), F32),
        jax.ShapeDtypeStruct((b, N_HEADS, HEAD_DIM, HEAD_DIM), F32),
        jax.ShapeDtypeStruct((b, CONV_W - 1, 3 * g), F32),
        jax.ShapeDtypeStruct((b, 1, g), F32),
        jax.ShapeDtypeStruct((b, CONV_W - 1, g), F32),
        jax.ShapeDtypeStruct((b, POOL_BUF, g), F32),
    ]
    ext = [pltpu.VMEM((nb, ts + HIST, 3 * g), F32), pltpu.VMEM((nb, ts + HIST, g), F32),
           pltpu.VMEM((nb, ts + HIST, g), F32)]
    if nb > 1:
        body = functools.partial(_decode_mixer_kernel, ts, nb, pos_base)
        scratch = ext
    else:
        body = functools.partial(_mixer_kernel, nt, ts, nb, pos_base)
        scratch = [pltpu.VMEM((nb, N_HEADS, HEAD_DIM, HEAD_DIM), F32),
                   pltpu.VMEM((nb, N_HEADS, HEAD_DIM, HEAD_DIM), F32)] + ext + [pltpu.VMEM((nb, 1, g), F32)]
    outs = pl.pallas_call(
        body,
        grid=(b // nb, nt),
        in_specs=in_specs,
        out_specs=out_specs,
        out_shape=out_shape,
        scratch_shapes=scratch,
        compiler_params=pltpu.CompilerParams(dimension_semantics=("arbitrary", "arbitrary"),
                                             vmem_limit_bytes=VMEM_LIMIT),
        name=f"mixer_ts{ts}",
    )(z, cos_t, sin_t, intra, qdec, kdec, sdec, lp["pvec"], lp["gdn_conv_w"], lp["gabp"],
      lp["lru_wa"], lp["lru_wx"], lp["pool_w"], s_ret0, s_gdn0, gconv0, lru0[:, :, None, :], lconv0, pool0)
    mix, ret_n, gdn_n, gconv_n, lru_n, lconv_n, pool_n = outs
    return mix, (ret_n, gdn_n, gconv_n, lru_n[:, 0, :], lconv_n, pool_n)


def _to_row_tiles(ref, x):
    for j in range(SUBLANES):
        ref[:, j, :] = x[:, j * LANES:(j + 1) * LANES]


def _from_row_tiles(ref):
    return jnp.concatenate([ref[:, j, :] for j in range(SUBLANES)], axis=-1)


def _wout_route_kernel(na, ma_ref, mb_ref, xa_ref, xb_ref, wo_ref, g_ref, wr_ref, br_ref, x1_ref, h2_ref, rt_ref):
    mix = _pair_tile(na, ma_ref, mb_ref)
    x1 = _pair_tile(na, xa_ref, xb_ref) + jnp.dot(mix.astype(BF16), wo_ref[...], preferred_element_type=F32)
    x1_ref[...] = x1
    h = x1 * lax.rsqrt(jnp.mean(x1 * x1, axis=-1, keepdims=True) + EPS) * g_ref[...]
    _to_row_tiles(h2_ref, h)
    logits = _mm(h, wr_ref[...], passes=3) + br_ref[...]
    lane = _iota(logits.shape, 1)
    lane_f = lane.astype(F32)
    neg = -1e30
    far = 1e9
    is_g = lane < N_GROUPS
    gl = jnp.where(is_g, logits, neg)
    gmax = jnp.max(gl, axis=-1, keepdims=True)
    gsel = jnp.min(jnp.where(gl == gmax, lane_f, far), axis=-1, keepdims=True)
    p_g = 1.0 / jnp.sum(jnp.where(is_g, jnp.exp(gl - gmax), 0.0), axis=-1, keepdims=True)
    e_group = ((lane - N_GROUPS) >> 3).astype(F32)
    in_group = jnp.where(lane >= N_GROUPS, e_group, -1.0) == gsel
    el = jnp.where(in_group, logits, neg)
    m1 = jnp.max(el, axis=-1, keepdims=True)
    i1 = jnp.min(jnp.where(el == m1, lane_f, far), axis=-1, keepdims=True)
    el2 = jnp.where(lane_f == i1, neg, el)
    m2 = jnp.max(el2, axis=-1, keepdims=True)
    i2 = jnp.min(jnp.where(el2 == m2, lane_f, far), axis=-1, keepdims=True)
    e2 = jnp.exp(m2 - m1)
    w1 = p_g / (1.0 + e2)
    w2 = p_g * e2 / (1.0 + e2)
    rt = jnp.where(lane == 0, i1 - N_GROUPS,
                   jnp.where(lane == 1, i2 - N_GROUPS, jnp.where(lane == 2, w1, jnp.where(lane == 3, w2, 0.0))))
    rt_ref[...] = rt


def _wout_route(mix_pair, x_pair, wo_bf16, g, wr, br, tm):
    d = x_pair[0].shape[1]
    na, t = x_pair[0].shape[0] // tm, x_pair[0].shape[0] + x_pair[1].shape[0]
    row = lambda i: (i, 0)
    const = lambda i: (0, 0)
    return pl.pallas_call(
        functools.partial(_wout_route_kernel, na),
        grid=(t // tm,),
        in_specs=_pair_specs(na, tm, d) + _pair_specs(na, tm, d) + [
            pl.BlockSpec((d, d), const), pl.BlockSpec((1, d), const), pl.BlockSpec((d, LANES), const),
            pl.BlockSpec((1, LANES), const)],
        out_specs=[pl.BlockSpec((tm, d), row), pl.BlockSpec((tm, SUBLANES, LANES), lambda i: (i, 0, 0)),
                   pl.BlockSpec((tm, LANES), row)],
        out_shape=[jax.ShapeDtypeStruct((t, d), F32), jax.ShapeDtypeStruct((t, SUBLANES, LANES), F32),
                   jax.ShapeDtypeStruct((t, LANES), F32)],
        compiler_params=pltpu.CompilerParams(dimension_semantics=("arbitrary",), vmem_limit_bytes=VMEM_LIMIT),
        name="wout_route",
    )(*mix_pair, *x_pair, wo_bf16, g, wr, br)


def _row_copy(src, i, dst, j, sem):
    return pltpu.make_async_copy(src.at[pl.ds(i, 1)], dst.at[pl.ds(j, 1)], sem)


def _route_plan_kernel(tm, n_tiles, rt_ref, pos_ref, tmeta_ref, emeta_ref, cnt_s, start_s, carry_s):
    ph = pl.program_id(0)
    i = pl.program_id(1)
    rt = rt_ref[...]
    lane = _iota((tm, LANES), 1)
    lane_f = lane.astype(F32)
    oh1 = jnp.where(lane_f == rt[:, 0:1], 1.0, 0.0)
    oh2 = jnp.where(lane_f == rt[:, 1:2], 1.0, 0.0)
    oh = oh1 + oh2
    colsum = jnp.sum(oh, axis=0, keepdims=True)

    @pl.when((ph == 0) & (i == 0))
    def _():
        cnt_s[...] = jnp.zeros((1, LANES), F32)

    @pl.when(ph == 0)
    def _():
        cnt_s[...] = cnt_s[...] + colsum

    @pl.when((ph == 1) & (i == 0))
    def _():
        cnt = cnt_s[...]
        ntile = jnp.floor((cnt + (tm - 1.0)) * (1.0 / tm))
        before = jnp.where(_iota((LANES, LANES), 0) < _iota((LANES, LANES), 1), 1.0, 0.0)
        st_tiles = _mm(jnp.broadcast_to(ntile, (SUBLANES, LANES)), before)[0:1]
        end_tiles = st_tiles + ntile
        start_s[...] = st_tiles * tm
        carry_s[...] = jnp.zeros((1, LANES), F32)
        is_e1 = _iota((1, LANES), 1) < N_EXPERTS
        n_used = jnp.sum(jnp.where(is_e1, ntile, 0.0), axis=-1, keepdims=True)
        last = jnp.sum(jnp.where(is_e1 & (end_tiles <= n_used - 1.0), 1.0, 0.0), axis=-1, keepdims=True)
        ti = _iota((n_tiles, LANES), 0).astype(F32)
        lane_t = _iota((n_tiles, LANES), 1)
        te = jnp.sum(jnp.where((lane_t < N_EXPERTS) & (end_tiles <= ti), 1.0, 0.0), axis=-1, keepdims=True)
        te = jnp.minimum(te, last)
        tmeta_ref[...] = jnp.where(lane_t == 0, te, jnp.where(lane_t == 1, n_used, 0.0))
        srow = _iota((SUBLANES, LANES), 0)
        emeta_ref[...] = jnp.where(srow == 0, ntile, jnp.where(srow == 1, (end_tiles - 1.0) * tm, 0.0))

    @pl.when(ph == 1)
    def _():
        earlier = jnp.where(_iota((tm, tm), 0) > _iota((tm, tm), 1), 1.0, 0.0)
        base = start_s[...] + carry_s[...] + _mm(earlier, oh)
        p1 = jnp.sum(oh1 * base, axis=-1, keepdims=True)
        p2 = jnp.sum(oh2 * base, axis=-1, keepdims=True)
        pos_ref[...] = jnp.where(lane == 0, p1, jnp.where(lane == 1, p2, 0.0))
        carry_s[...] = carry_s[...] + colsum


def _route_plan(rt, tm, n_tiles):
    t = rt.shape[0]
    pos, tmeta, emeta = pl.pallas_call(
        functools.partial(_route_plan_kernel, tm, n_tiles),
        grid=(2, t // tm),
        in_specs=[pl.BlockSpec((tm, LANES), lambda ph, i: (i, 0))],
        out_specs=[pl.BlockSpec((tm, LANES), lambda ph, i: (ph * i, 0)),
                   pl.BlockSpec((n_tiles, LANES), lambda ph, i: (0, 0)),
                   pl.BlockSpec((SUBLANES, LANES), lambda ph, i: (0, 0))],
        out_shape=[jax.ShapeDtypeStruct((t, LANES), F32), jax.ShapeDtypeStruct((n_tiles, LANES), F32),
                   jax.ShapeDtypeStruct((SUBLANES, LANES), F32)],
        scratch_shapes=[pltpu.VMEM((1, LANES), F32), pltpu.VMEM((1, LANES), F32), pltpu.VMEM((1, LANES), F32)],
        compiler_params=pltpu.CompilerParams(dimension_semantics=("arbitrary", "arbitrary"),
                                             vmem_limit_bytes=VMEM_LIMIT),
        name="route_plan",
    )(rt)
    pos1 = pos[:, 0].astype(jnp.int32)
    pos2 = pos[:, 1].astype(jnp.int32)
    tile_expert = tmeta[:, 0].astype(jnp.int32)
    n_used = tmeta[0:1, 1].astype(jnp.int32)
    has_rows = (emeta[0, :N_EXPERTS] > 0).astype(jnp.int32)
    last_tile_row = jnp.maximum(emeta[1, :N_EXPERTS], 0.0).astype(jnp.int32)
    return pos1, pos2, tile_expert, n_used, has_rows, last_tile_row


def _scatter_kernel(tm, n_tiles, min_tiles, zon_ref, zrow_ref, nu_ref, p1_ref, p2_ref, h_ref, xs_hbm,
                    zbuf, sem1, sem2, zsem, tsem):
    i = pl.program_id(0)

    @pl.when(i == 0)
    def _():
        zbuf[...] = jnp.zeros(zbuf.shape, F32)

        def zero_copy(e):
            return pltpu.make_async_copy(zbuf, xs_hbm.at[pl.ds(zrow_ref[e], tm)], zsem.at[e])

        def tail_copy(k):
            return pltpu.make_async_copy(zbuf, xs_hbm.at[pl.ds((min_tiles + k) * tm, tm)], tsem.at[k])

        def zstart(e, c):
            @pl.when(zon_ref[e] > 0)
            def _():
                zero_copy(e).start()
            return c

        def zwait(e, c):
            @pl.when(zon_ref[e] > 0)
            def _():
                zero_copy(e).wait()
            return c

        def tstart(k, c):
            @pl.when(min_tiles + k >= nu_ref[0])
            def _():
                tail_copy(k).start()
            return c

        def twait(k, c):
            @pl.when(min_tiles + k >= nu_ref[0])
            def _():
                tail_copy(k).wait()
            return c

        lax.fori_loop(0, N_EXPERTS, zstart, 0)
        lax.fori_loop(0, n_tiles - min_tiles, tstart, 0)
        lax.fori_loop(0, N_EXPERTS, zwait, 0)
        lax.fori_loop(0, n_tiles - min_tiles, twait, 0)

    def start(r, c):
        _row_copy(h_ref, r, xs_hbm, p1_ref[0, 0, r], sem1.at[r]).start()
        _row_copy(h_ref, r, xs_hbm, p2_ref[0, 0, r], sem2.at[r]).start()
        return c

    def wait(r, c):
        _row_copy(h_ref, r, xs_hbm, p1_ref[0, 0, r], sem1.at[r]).wait()
        _row_copy(h_ref, r, xs_hbm, p2_ref[0, 0, r], sem2.at[r]).wait()
        return c

    lax.fori_loop(0, tm, start, 0, unroll=8)
    lax.fori_loop(0, tm, wait, 0, unroll=8)


def _scatter_rows(h2, pos1, pos2, has_rows, last_tile_row, n_used, n_tiles, tm):
    t = h2.shape[0]
    nb = t // tm
    min_tiles = (2 * t) // tm
    smem_blk = pl.BlockSpec((1, 1, tm), lambda i, zon, zrow, nu: (i, 0, 0), memory_space=pltpu.SMEM)
    grid_spec = pltpu.PrefetchScalarGridSpec(
        num_scalar_prefetch=3,
        grid=(nb,),
        in_specs=[smem_blk, smem_blk, pl.BlockSpec((tm, SUBLANES, LANES), lambda i, zon, zrow, nu: (i, 0, 0))],
        out_specs=pl.BlockSpec(memory_space=pl.ANY),
        scratch_shapes=[pltpu.VMEM((tm, SUBLANES, LANES), F32), pltpu.SemaphoreType.DMA((tm,)),
                        pltpu.SemaphoreType.DMA((tm,)), pltpu.SemaphoreType.DMA((N_EXPERTS,)),
                        pltpu.SemaphoreType.DMA((n_tiles - min_tiles,))],
    )
    return pl.pallas_call(
        functools.partial(_scatter_kernel, tm, n_tiles, min_tiles),
        grid_spec=grid_spec,
        out_shape=jax.ShapeDtypeStruct((n_tiles * tm, SUBLANES, LANES), F32),
        compiler_params=pltpu.CompilerParams(dimension_semantics=("arbitrary",), vmem_limit_bytes=VMEM_LIMIT),
        name="moe_scatter",
    )(has_rows, last_tile_row, n_used, pos1.reshape(nb, 1, tm), pos2.reshape(nb, 1, tm), h2)


def _ffn_kernel(te_ref, nu_ref, xs_ref, wg_ref, wu_ref, wd_ref, ys_ref, wg_s, wu_s, wd_s):
    i = pl.program_id(0)
    used = i < nu_ref[0]
    new_expert = (i == 0) | (te_ref[i] != te_ref[jnp.maximum(i - 1, 0)])

    @pl.when(used & new_expert)
    def _():
        wg_s[...] = wg_ref[...].astype(BF16)
        wu_s[...] = wu_ref[...].astype(BF16)
        wd_s[...] = wd_ref[...].astype(BF16)

    @pl.when(used)
    def _():
        x = _from_row_tiles(xs_ref).astype(BF16)
        gate = jnp.dot(x, wg_s[...], preferred_element_type=F32)
        up = jnp.dot(x, wu_s[...], preferred_element_type=F32)
        act = (_silu(gate) * up).astype(BF16)
        _to_row_tiles(ys_ref, jnp.dot(act, wd_s[...], preferred_element_type=F32))

    @pl.when(jnp.logical_not(used))
    def _():
        ys_ref[...] = jnp.zeros(ys_ref.shape, F32)


def _expert_ffn(xs, tile_expert, n_used, w_gate, w_up, w_down, layer, tm):
    p = xs.shape[0]
    _, _, d, de = w_gate.shape
    row = lambda i, te, nu: (jnp.minimum(i, nu[0] - 1), 0, 0)
    wsel = lambda i, te, nu: (layer, te[i], 0, 0)
    grid_spec = pltpu.PrefetchScalarGridSpec(
        num_scalar_prefetch=2,
        grid=(p // tm,),
        in_specs=[pl.BlockSpec((tm, SUBLANES, LANES), row),
                  pl.BlockSpec((None, None, d, de), wsel),
                  pl.BlockSpec((None, None, d, de), wsel),
                  pl.BlockSpec((None, None, de, d), wsel)],
        out_specs=pl.BlockSpec((tm, SUBLANES, LANES), lambda i, te, nu: (i, 0, 0)),
        scratch_shapes=[pltpu.VMEM((d, de), BF16), pltpu.VMEM((d, de), BF16), pltpu.VMEM((de, d), BF16)],
    )
    return pl.pallas_call(
        _ffn_kernel,
        grid_spec=grid_spec,
        out_shape=jax.ShapeDtypeStruct((p, SUBLANES, LANES), F32),
        compiler_params=pltpu.CompilerParams(dimension_semantics=("arbitrary",), vmem_limit_bytes=VMEM_LIMIT),
        name="expert_ffn",
    )(tile_expert, n_used, xs, w_gate, w_up, w_down)


def _combine_kernel(tm, na, final_norm, p1_ref, p2_ref, x1_ref, rt_ref, fg_ref, ys_hbm, outa_ref, outb_ref,
                    buf1, buf2, sem1, sem2):
    def start(r, c):
        _row_copy(ys_hbm, p1_ref[0, 0, r], buf1, r, sem1.at[r]).start()
        _row_copy(ys_hbm, p2_ref[0, 0, r], buf2, r, sem2.at[r]).start()
        return c

    def wait(r, c):
        _row_copy(ys_hbm, p1_ref[0, 0, r], buf1, r, sem1.at[r]).wait()
        _row_copy(ys_hbm, p2_ref[0, 0, r], buf2, r, sem2.at[r]).wait()
        return c

    lax.fori_loop(0, tm, start, 0, unroll=8)
    lax.fori_loop(0, tm, wait, 0, unroll=8)
    rt = rt_ref[...]
    x = x1_ref[...] + rt[:, 2:3] * _from_row_tiles(buf1) + rt[:, 3:4] * _from_row_tiles(buf2)
    if final_norm:
        x = x * lax.rsqrt(jnp.mean(x * x, axis=-1, keepdims=True) + EPS) * fg_ref[...]

    @pl.when(pl.program_id(0) < na)
    def _():
        outa_ref[...] = x

    @pl.when(pl.program_id(0) >= na)
    def _():
        outb_ref[...] = x


def _combine(x1, rt, ys, pos1, pos2, final_g, final_norm, ta, tm):
    t, d = x1.shape
    nb = t // tm
    na = ta // tm
    row = lambda i: (i, 0)
    smem_blk = pl.BlockSpec((1, 1, tm), lambda i: (i, 0, 0), memory_space=pltpu.SMEM)
    return pl.pallas_call(
        functools.partial(_combine_kernel, tm, na, final_norm),
        grid=(nb,),
        in_specs=[smem_blk, smem_blk, pl.BlockSpec((tm, d), row), pl.BlockSpec((tm, LANES), row),
                  pl.BlockSpec((1, d), lambda i: (0, 0)), pl.BlockSpec(memory_space=pl.ANY)],
        out_specs=[pl.BlockSpec((tm, d), lambda i: (jnp.minimum(i, na - 1), 0)),
                   pl.BlockSpec((tm, d), lambda i: (jnp.maximum(i - na, 0), 0))],
        out_shape=[jax.ShapeDtypeStruct((ta, d), F32), jax.ShapeDtypeStruct((t - ta, d), F32)],
        scratch_shapes=[pltpu.VMEM((tm, SUBLANES, LANES), F32), pltpu.VMEM((tm, SUBLANES, LANES), F32),
                        pltpu.SemaphoreType.DMA((tm,)), pltpu.SemaphoreType.DMA((tm,))],
        compiler_params=pltpu.CompilerParams(dimension_semantics=("arbitrary",), vmem_limit_bytes=VMEM_LIMIT),
        name="moe_combine",
    )(pos1.reshape(nb, 1, tm), pos2.reshape(nb, 1, tm), x1, rt, final_g, ys)


def _block_diag(w):
    h, a, b = w.shape
    out = jnp.zeros((h * a, h * b), w.dtype)
    for i in range(h):
        out = out.at[i * a:(i + 1) * a, i * b:(i + 1) * b].set(w[i])
    return out


def _layer_params(l, p):
    g = GROUP_W
    w_in = p["w_in"][l]
    d = w_in.shape[0]
    w_perm = jnp.concatenate([w_in[:, :8 * g], w_in[:, 8 * g + 8:], w_in[:, 8 * g:8 * g + 8],
                              jnp.zeros((d, LANES - 8), w_in.dtype)], axis=1)
    pvec = jnp.zeros((16, g), F32)
    rows = [p["ret_gn_g"][l], jnp.tile(p["gdn_norm_g"][l], N_HEADS), p["lru_conv_b"][l], p["lru_ba"][l],
            p["lru_bx"][l], p["lru_lambda"][l], p["pool_b"][l], p["pool_scale"][l]]
    pvec = pvec.at[0:8].set(jnp.stack(rows)).at[8:12].set(p["lru_conv_w"][l])
    gabp = jnp.zeros((8, LANES), F32).at[0, :N_HEADS].set(p["gdn_a_log"][l]).at[1, :N_HEADS].set(p["gdn_dt_bias"][l])
    wr = jnp.zeros((d, LANES), F32).at[:, :N_GROUPS].set(p["router_group_w"][l])
    wr = wr.at[:, N_GROUPS:N_GROUPS + N_EXPERTS].set(
        jnp.transpose(p["router_expert_w"][l], (1, 0, 2)).reshape(d, N_EXPERTS))
    br = jnp.zeros((1, LANES), F32).at[0, :N_GROUPS].set(p["router_group_b"][l])
    br = br.at[0, N_GROUPS:N_GROUPS + N_EXPERTS].set(p["router_expert_b"][l].reshape(N_EXPERTS))
    return dict(
        norm1_g=p["norm1_g"][l][None, :], w_in=w_perm.astype(BF16), pvec=pvec, gdn_conv_w=p["gdn_conv_w"][l],
        gabp=gabp, lru_wa=_block_diag(p["lru_wa"][l]).astype(BF16), lru_wx=_block_diag(p["lru_wx"][l]).astype(BF16),
        pool_w=_block_diag(p["pool_w"][l]).astype(BF16), w_out=p["w_out"][l].astype(BF16),
        norm2_g=p["norm2_g"][l][None, :], wr=wr, br=br)


def kernel(x_prompt, x_sample, state_ret, state_gdn, state_gdn_conv, state_lru, state_lru_conv, state_pool,
           norm1_g, w_in, ret_gn_g, gdn_conv_w, gdn_a_log, gdn_dt_bias, gdn_norm_g,
           lru_conv_w, lru_conv_b, lru_wa, lru_ba, lru_wx, lru_bx, lru_lambda,
           pool_w, pool_b, pool_scale, w_out, norm2_g,
           router_group_w, router_group_b, router_expert_w, router_expert_b,
           moe_w_gate, moe_w_up, moe_w_down, final_g):
    params = dict(norm1_g=norm1_g, w_in=w_in, ret_gn_g=ret_gn_g, gdn_conv_w=gdn_conv_w, gdn_a_log=gdn_a_log,
                  gdn_dt_bias=gdn_dt_bias, gdn_norm_g=gdn_norm_g, lru_conv_w=lru_conv_w, lru_conv_b=lru_conv_b,
                  lru_wa=lru_wa, lru_ba=lru_ba, lru_wx=lru_wx, lru_bx=lru_bx, lru_lambda=lru_lambda,
                  pool_w=pool_w, pool_b=pool_b, pool_scale=pool_scale, w_out=w_out, norm2_g=norm2_g,
                  router_group_w=router_group_w, router_group_b=router_group_b,
                  router_expert_w=router_expert_w, router_expert_b=router_expert_b)
    bp, lp_, d = x_prompt.shape
    bs, ls, _ = x_sample.shape
    depth = w_in.shape[0]
    tp, ts_ = bp * lp_, bs * ls
    t = tp + ts_
    tile_p = min(256, lp_)
    tm = 256
    n_tiles = (2 * t + N_EXPERTS * (tm - 1) + tm - 1) // tm

    x = (x_prompt.reshape(tp, d), x_sample.reshape(ts_, d))
    sample_states = (state_ret, state_gdn, state_gdn_conv, state_lru, state_lru_conv, state_pool)
    zero_states = tuple(jnp.zeros((1, bp) + s.shape[2:], F32) for s in sample_states)
    new_p, new_s = [], []
    for l in range(depth):
        lp = _layer_params(l, params)
        z = _norm_win(x, lp["norm1_g"], lp["w_in"], tm)
        mix_p, st_p = _mixer(z, 0, bp, lp_, tile_p, 1, 0, zero_states, 0, lp)
        mix_s, st_s = _mixer(z, tp, bs, ls, ls, SAMPLE_SEQS_PER_STEP, PAST_LEN, sample_states, l, lp)
        new_p.append(st_p)
        new_s.append(st_s)
        x1, h2, rt = _wout_route((mix_p, mix_s), x, lp["w_out"], lp["norm2_g"], lp["wr"], lp["br"], tm)
        pos1, pos2, tile_expert, n_used, has_rows, last_tile_row = _route_plan(rt, tm, n_tiles)
        xs = _scatter_rows(h2, pos1, pos2, has_rows, last_tile_row, n_used, n_tiles, tm)
        ys = _expert_ffn(xs, tile_expert, n_used, moe_w_gate, moe_w_up, moe_w_down, l, tm)
        x = _combine(x1, rt, ys, pos1, pos2, final_g[None, :], l == depth - 1, tp, tm)
    y_prompt = x[0].reshape(bp, lp_, d)
    y_sample = x[1].reshape(bs, ls, d)
    stack = lambda news: [jnp.stack([ns[j] for ns in news], axis=0) for j in range(6)]
    return (y_prompt, y_sample, *stack(new_p), *stack(new_s))
```

```python
import functools

import jax
import jax.numpy as jnp
from jax import lax
from jax.experimental import pallas as pl
from jax.experimental.pallas import tpu as pltpu

F32 = jnp.float32
BF16 = jnp.bfloat16

EPS = 1e-6
HEAD_DIM = 64
N_HEADS = 4
GROUP_W = 256
CONV_W = 4
LRU_C = 8.0
POOL_WINDOWS = (2, 4, 8, 16)
POOL_BUF = 15
ROPE_BASE = 10000.0
N_GROUPS = 4
EXPERTS_PER_GROUP = 8
N_EXPERTS = 32
PAST_LEN = 16384

LANES = 128
SUBLANES = 8
HIST = 16
Z_AB = 11 * GROUP_W
NZ = Z_AB + LANES
INV_BASE = 8
RESID_PASSES = 3
SAMPLE_SEQS_PER_STEP = 16
VMEM_LIMIT = 56 * 1024 * 1024

NN = (((1,), (0,)), ((), ()))
NT = (((1,), (1,)), ((), ()))
TN = (((0,), (0,)), ((), ()))


def _mm(a, b, dims=NN, passes=1):
    ah = a.astype(BF16)
    bh = b.astype(BF16)
    dot = functools.partial(lax.dot_general, dimension_numbers=dims, preferred_element_type=F32)
    if passes == 1:
        return dot(ah, bh)
    al = (a - ah.astype(F32)).astype(BF16)
    bl = (b - bh.astype(F32)).astype(BF16)
    return dot(ah, bh) + (dot(ah, bl) + dot(al, bh))


def _sigmoid(x):
    return 1.0 / (1.0 + jnp.exp(-x))


def _silu(x):
    return x * _sigmoid(x)


def _softplus(x):
    return jnp.maximum(x, 0.0) + jnp.log1p(jnp.exp(-jnp.abs(x)))


def _gelu_tanh(x):
    return 0.5 * x * (1.0 + jnp.tanh(0.7978845608028654 * (x + 0.044715 * (x * x * x))))


def _iota(shape, axis):
    return lax.broadcasted_iota(jnp.int32, shape, axis)


def _shift_rows(x, d, fill, seg=None):
    t = _iota(x.shape, 0)
    if seg is not None:
        t = t & (seg - 1)
    return jnp.where(t >= d, pltpu.roll(x, d, 0), fill)


def _cumsum_rows(x, seg=None):
    d = 1
    while d < (seg or x.shape[0]):
        x = x + _shift_rows(x, d, 0.0, seg)
        d *= 2
    return x


def _swap_half_heads(x):
    n = x.shape[1]
    lower = (_iota(x.shape, 1) & (HEAD_DIM - 1)) < HEAD_DIM // 2
    return jnp.where(lower, pltpu.roll(x, n - HEAD_DIM // 2, 1), pltpu.roll(x, HEAD_DIM // 2, 1))


def _inv_unit_lower(mats, c=None):
    shape = mats[0].shape
    c = c or shape[0]
    row = _iota(shape, 0)
    col = _iota(shape, 1)
    nb = min(INV_BASE, c)
    shift = nb.bit_length() - 1
    in_base = (row >> shift) == (col >> shift)
    ps = [jnp.where(in_base, a, 0.0) for a in mats]
    eye = jnp.where(row == col, 1.0, 0.0)
    xs = [eye - d for d in ps]
    k = 2
    while k < nb:
        ps = [_mm(p, p) for p in ps]
        xs = [x + _mm(x, p) for x, p in zip(xs, ps)]
        k *= 2
    m = nb
    while m < c:
        s1 = m.bit_length() - 1
        in_off = ((row >> (s1 + 1)) == (col >> (s1 + 1))) & ((row >> s1) != (col >> s1))
        ys = [_mm(x, jnp.where(in_off, a, 0.0)) for x, a in zip(xs, mats)]
        xs = [x - _mm(y, x) for x, y in zip(xs, ys)]
        m *= 2
    return xs


def _solve_unit_lower(mats, tinvs, ws):
    us = [_mm(t, w) for t, w in zip(tinvs, ws)]
    rs = [w - (u + _mm(a, u, passes=RESID_PASSES)) for a, u, w in zip(mats, us, ws)]
    return [u + _mm(t, r) for u, t, r in zip(us, tinvs, rs)]


def _pair_specs(na, tm, d):
    return [pl.BlockSpec((tm, d), lambda i: (jnp.minimum(i, na - 1), 0)),
            pl.BlockSpec((tm, d), lambda i: (jnp.maximum(i - na, 0), 0))]


def _pair_tile(na, a_ref, b_ref):
    return jnp.where(pl.program_id(0) < na, a_ref[...], b_ref[...])


def _norm_win_kernel(na, xa_ref, xb_ref, g_ref, w_ref, z_ref):
    x = _pair_tile(na, xa_ref, xb_ref)
    h = x * lax.rsqrt(jnp.mean(x * x, axis=-1, keepdims=True) + EPS) * g_ref[...]
    z_ref[...] = jnp.dot(h.astype(BF16), w_ref[...], preferred_element_type=F32)


def _norm_win(x_pair, g, w_bf16, tm):
    xa, xb = x_pair
    d = xa.shape[1]
    na, t = xa.shape[0] // tm, xa.shape[0] + xb.shape[0]
    nz = w_bf16.shape[1]
    return pl.pallas_call(
        functools.partial(_norm_win_kernel, na),
        grid=(t // tm,),
        in_specs=_pair_specs(na, tm, d) + [pl.BlockSpec((1, d), lambda i: (0, 0)),
                                           pl.BlockSpec((d, nz), lambda i: (0, 0))],
        out_specs=pl.BlockSpec((tm, nz), lambda i: (i, 0)),
        out_shape=jax.ShapeDtypeStruct((t, nz), F32),
        compiler_params=pltpu.CompilerParams(dimension_semantics=("arbitrary",), vmem_limit_bytes=VMEM_LIMIT),
        name="norm_win",
    )(xa, xb, g, w_bf16)


def _retention_tile(zq, zk, zv, cos, sin, intra_ref, qdec, kdec, sdec, s_ref):
    rq = zq * cos + _swap_half_heads(zq) * sin
    rk = (zk * cos + _swap_half_heads(zk) * sin) * (HEAD_DIM ** -0.5)
    kd = rk * kdec
    heads = range(N_HEADS)
    sls = [slice(h * HEAD_DIM, (h + 1) * HEAD_DIM) for h in heads]
    ss = [s_ref[h] for h in heads]
    scores = [_mm(rq[:, sl], rk[:, sl], NT) * intra_ref[h] for h, sl in zip(heads, sls)]
    os_ = [_mm(sc, zv[:, sl]) + _mm(rq[:, sl], s) * qdec[:, sl] for sc, s, sl in zip(scores, ss, sls)]
    s_new = [s * sdec[h:h + 1, :HEAD_DIM] + _mm(kd[:, sl], zv[:, sl], TN) for h, s, sl in zip(heads, ss, sls)]
    for h in heads:
        s_ref[h] = s_new[h]
    outs = []
    for o in os_:
        mu = jnp.mean(o, axis=-1, keepdims=True)
        oc = o - mu
        var = jnp.mean(oc * oc, axis=-1, keepdims=True)
        outs.append(oc * lax.rsqrt(var + EPS))
    return jnp.concatenate(outs, axis=-1)


def _head_lane_select(c, h):
    return jnp.where(_iota((c, LANES), 1) == h, 1.0, 0.0).astype(BF16)


def _split3(x):
    x0 = x.astype(BF16)
    r1 = x - x0.astype(F32)
    x1 = r1.astype(BF16)
    x2 = (r1 - x1.astype(F32)).astype(BF16)
    return x0, x1, x2


def _gdn_tile(gq, gk, gv, gab, s_ref):
    c = gq.shape[0]
    gc = _cumsum_rows(gab)
    egc = jnp.exp(gc)
    gparts = _split3(gc)
    row = _iota((c, c), 0)
    col = _iota((c, c), 1)
    incl = row >= col
    strict = row > col
    dotnt = functools.partial(lax.dot_general, dimension_numbers=NT, preferred_element_type=F32)
    heads = range(N_HEADS)
    sls = [slice(h * HEAD_DIM, (h + 1) * HEAD_DIM) for h in heads]
    ss = [s_ref[h] for h in heads]
    gcols = [gc[:, h:h + 1] for h in heads]
    bcols = [gab[:, N_HEADS + h:N_HEADS + h + 1] for h in heads]
    ecols = [egc[:, h:h + 1] for h in heads]
    decs = []
    for h in heads:
        sel = _head_lane_select(c, h)
        grow = dotnt(sel, gparts[0]) + (dotnt(sel, gparts[1]) + dotnt(sel, gparts[2]))
        decs.append(jnp.where(incl, jnp.exp(jnp.where(incl, gcols[h] - grow, 0.0)), 0.0))
    mats = [jnp.where(strict, bcols[h] * decs[h] * _mm(gk[:, sl], gk[:, sl], NT), 0.0) for h, sl in zip(heads, sls)]
    tinvs = _inv_unit_lower(mats)
    ws = [bcols[h] * (gv[:, sl] - ecols[h] * _mm(gk[:, sl], ss[h])) for h, sl in zip(heads, sls)]
    us = _solve_unit_lower(mats, tinvs, ws)
    qks = [_mm(gq[:, sl], gk[:, sl], NT) * decs[h] for h, sl in zip(heads, sls)]
    outs = [ecols[h] * _mm(gq[:, sl], ss[h]) + _mm(qks[h], us[h]) for h, sl in zip(heads, sls)]
    glasts = [gc[c - 1:c, h:h + 1] for h in heads]
    s_new = [jnp.exp(glasts[h]) * ss[h] + _mm(jnp.exp(glasts[h] - gcols[h]) * gk[:, sl], us[h], TN)
             for h, sl in zip(heads, sls)]
    for h in heads:
        s_ref[h] = s_new[h]
    return outs


def _mixer_kernel(nt, ts, nb, pos_base,
                  z_ref, cos_ref, sin_ref, intra_ref, qdec_ref, kdec_ref, sdec_ref, pv_ref, gcw_ref, gabp_ref,
                  wa_ref, wx_ref, pw_ref, sret0, sgdn0, gconv0, lru0, lconv0, pool0,
                  mix_ref, ret_o, gdn_o, gconv_o, lru_o, lconv_o, pool_o,
                  s_ret, s_gdn, ext_g, ext_l, ext_p, h_lru):
    n = pl.program_id(1)
    g = GROUP_W
    base = HIST - (CONV_W - 1)

    @pl.when(n == 0)
    def _():
        s_ret[...] = sret0[...]
        s_gdn[...] = sgdn0[...]
        ext_g[:, 0:HIST, :] = jnp.zeros((nb, HIST, 3 * g), F32)
        ext_g[:, base:HIST, :] = gconv0[...]
        ext_l[:, 0:HIST, :] = jnp.zeros((nb, HIST, g), F32)
        ext_l[:, base:HIST, :] = lconv0[...]
        ext_p[:, 0:HIST, :] = jnp.zeros((nb, HIST, g), F32)
        ext_p[:, HIST - POOL_BUF:HIST, :] = pool0[...]
        h_lru[...] = lru0[...]

    pv = pv_ref[...]
    ret_gn_g, gdn_norm_g = pv[0:1], pv[1:2]
    lru_conv_b, lru_ba, lru_bx, lru_lambda = pv[2:3], pv[3:4], pv[4:5], pv[5:6]
    pool_b, pool_scale = pv[6:7], pv[7:8]
    lru_conv_w = pv[8:12]
    gcw = gcw_ref[...]
    gabp = gabp_ref[...]

    for j in range(nb):
        rows = slice(j * ts, (j + 1) * ts)
        eg, el, ep = ext_g.at[j], ext_l.at[j], ext_p.at[j]

        ro = _retention_tile(z_ref[rows, 0:g], z_ref[rows, g:2 * g], z_ref[rows, 2 * g:3 * g], cos_ref[...],
                             sin_ref[...], intra_ref, qdec_ref[...], kdec_ref[...], sdec_ref[...], s_ret.at[j])
        mix_ref[rows, 0:g] = ro * ret_gn_g * _silu(z_ref[rows, 3 * g:4 * g])

        eg[HIST:HIST + ts, :] = z_ref[rows, 4 * g:7 * g]
        y = eg[base:base + ts, :] * gcw[0:1]
        for i in range(1, CONV_W):
            y = y + eg[base + i:base + i + ts, :] * gcw[i:i + 1]
        qkv = _silu(y)
        zab = z_ref[rows, Z_AB:Z_AB + LANES]
        lane = _iota(zab.shape, 1)
        log_a = -jnp.exp(gabp[0:1]) * _softplus(zab + gabp[1:2])
        gab = jnp.where(lane < N_HEADS, log_a, jnp.where(lane < 2 * N_HEADS, _sigmoid(zab), 0.0))
        gq_parts, gk_parts = [], []
        for h in range(N_HEADS):
            qh = qkv[:, h * HEAD_DIM:(h + 1) * HEAD_DIM]
            kh = qkv[:, g + h * HEAD_DIM:g + (h + 1) * HEAD_DIM]
            gq_parts.append(qh * lax.rsqrt(jnp.sum(qh * qh, axis=-1, keepdims=True) + EPS) * (HEAD_DIM ** -0.5))
            gk_parts.append(kh * lax.rsqrt(jnp.sum(kh * kh, axis=-1, keepdims=True) + EPS))
        go = _gdn_tile(jnp.concatenate(gq_parts, axis=-1), jnp.concatenate(gk_parts, axis=-1),
                       qkv[:, 2 * g:3 * g], gab, s_gdn.at[j])
        go = [o * lax.rsqrt(jnp.mean(o * o, axis=-1, keepdims=True) + EPS) for o in go]
        mix_ref[rows, g:2 * g] = jnp.concatenate(go, axis=-1) * gdn_norm_g * _silu(z_ref[rows, 7 * g:8 * g])

        el[HIST:HIST + ts, :] = z_ref[rows, 8 * g:9 * g]
        xc = el[base:base + ts, :] * lru_conv_w[0:1]
        for i in range(1, CONV_W):
            xc = xc + el[base + i:base + i + ts, :] * lru_conv_w[i:i + 1]
        xc = xc + lru_conv_b
        r = _sigmoid(jnp.dot(xc.astype(BF16), wa_ref[...], preferred_element_type=F32) + lru_ba)
        gate_i = _sigmoid(jnp.dot(xc.astype(BF16), wx_ref[...], preferred_element_type=F32) + lru_bx)
        log_at = -LRU_C * r * _softplus(-lru_lambda)
        av = jnp.exp(log_at)
        bv = jnp.sqrt(-jnp.tanh(log_at) * (av * av + 1.0)) * (gate_i * xc)
        bv = bv + jnp.where(_iota(bv.shape, 0) == 0, av * h_lru[j], 0.0)
        d = 1
        while d < ts:
            bv = av * _shift_rows(bv, d, 0.0) + bv
            av = av * _shift_rows(av, d, 1.0)
            d *= 2
        h_lru[j] = bv[ts - 1:ts, :]
        mix_ref[rows, 2 * g:3 * g] = bv * _gelu_tanh(z_ref[rows, 9 * g:10 * g])

        u = z_ref[rows, 10 * g:11 * g]
        ep[HIST:HIST + ts, :] = u
        e = ep[...]
        sums = []
        step = 1
        for _ in POOL_WINDOWS:
            e = e + pltpu.roll(e, step, 0)
            sums.append(e[HIST:HIST + ts, :])
            step *= 2
        pos1 = (pos_base + 1 + n * ts + _iota((ts, g), 0)).astype(F32)
        lane_g = _iota((ts, g), 1) >> 6
        pooled = sums[-1] / jnp.minimum(pos1, float(POOL_WINDOWS[-1]))
        for gi in range(len(POOL_WINDOWS) - 2, -1, -1):
            pooled = jnp.where(lane_g == gi, sums[gi] / jnp.minimum(pos1, float(POOL_WINDOWS[gi])), pooled)
        pooled = pooled - u
        yp = jnp.dot(pooled.astype(BF16), pw_ref[...], preferred_element_type=F32) + pool_b
        mix_ref[rows, 3 * g:4 * g] = yp * pool_scale

        tail_g = eg[ts:ts + HIST, :]
        tail_l = el[ts:ts + HIST, :]
        tail_p = ep[ts:ts + HIST, :]
        eg[0:HIST, :] = tail_g
        el[0:HIST, :] = tail_l
        ep[0:HIST, :] = tail_p

    @pl.when(n == nt - 1)
    def _():
        ret_o[...] = s_ret[...]
        gdn_o[...] = s_gdn[...]
        gconv_o[...] = ext_g[:, base:HIST, :]
        lconv_o[...] = ext_l[:, base:HIST, :]
        pool_o[...] = ext_p[:, HIST - POOL_BUF:HIST, :]
        lru_o[...] = h_lru[...]


def _decode_mixer_kernel(ts, nb, pos_base,
                         z_ref, cos_ref, sin_ref, intra_ref, qdec_ref, kdec_ref, sdec_ref, pv_ref, gcw_ref, gabp_ref,
                         wa_ref, wx_ref, pw_ref, sret0, sgdn0, gconv0, lru0, lconv0, pool0,
                         mix_ref, ret_o, gdn_o, gconv_o, lru_o, lconv_o, pool_o,
                         ext_g, ext_l, ext_p):
    g = GROUP_W
    r_all = nb * ts
    base = HIST - (CONV_W - 1)
    seg_shift = ts.bit_length() - 1
    row2 = _iota((r_all, r_all), 0)
    col2 = _iota((r_all, r_all), 1)
    same = (row2 >> seg_shift) == (col2 >> seg_shift)
    incl = same & (row2 >= col2)
    strict = same & (row2 > col2)
    seq_l = _iota((r_all, LANES), 0) >> seg_shift
    half_l = _iota((r_all, LANES), 1) >> 6
    exp_masks = [((seq_l >> 1) == c) & ((seq_l & 1) == half_l) for c in range(nb // 2)]

    def expand(x):
        x2 = jnp.concatenate([x, x], axis=-1)
        return jnp.concatenate([jnp.where(m, x2, 0.0) for m in exp_masks], axis=-1)

    def stacked(state_ref, h):
        return jnp.concatenate([state_ref[j, h] for j in range(nb)], axis=0)

    def unstack(out_ref, h, sv):
        for j in range(nb):
            out_ref[j, h] = sv[j * HEAD_DIM:(j + 1) * HEAD_DIM, :]

    pv = pv_ref[...]
    ret_gn_g, gdn_norm_g = pv[0:1], pv[1:2]
    lru_conv_b, lru_ba, lru_bx, lru_lambda = pv[2:3], pv[3:4], pv[4:5], pv[5:6]
    pool_b, pool_scale = pv[6:7], pv[7:8]
    lru_conv_w = pv[8:12]
    gcw = gcw_ref[...]
    gabp = gabp_ref[...]

    ext_g[:, 0:HIST, :] = jnp.zeros((nb, HIST, 3 * g), F32)
    ext_g[:, base:HIST, :] = gconv0[...]
    ext_l[:, 0:HIST, :] = jnp.zeros((nb, HIST, g), F32)
    ext_l[:, base:HIST, :] = lconv0[...]
    ext_p[:, 0:HIST, :] = jnp.zeros((nb, HIST, g), F32)
    ext_p[:, HIST - POOL_BUF:HIST, :] = pool0[...]
    y_parts, xc_parts, sum_parts, h0_parts = [], [], [[] for _ in POOL_WINDOWS], []
    for j in range(nb):
        rows = slice(j * ts, (j + 1) * ts)
        eg, el, ep = ext_g.at[j], ext_l.at[j], ext_p.at[j]
        eg[HIST:HIST + ts, :] = z_ref[rows, 4 * g:7 * g]
        el[HIST:HIST + ts, :] = z_ref[rows, 8 * g:9 * g]
        ep[HIST:HIST + ts, :] = z_ref[rows, 10 * g:11 * g]
        y = eg[base:base + ts, :] * gcw[0:1]
        xc = el[base:base + ts, :] * lru_conv_w[0:1]
        for i in range(1, CONV_W):
            y = y + eg[base + i:base + i + ts, :] * gcw[i:i + 1]
            xc = xc + el[base + i:base + i + ts, :] * lru_conv_w[i:i + 1]
        y_parts.append(y)
        xc_parts.append(xc)
        e = ep[...]
        step = 1
        for wi in range(len(POOL_WINDOWS)):
            e = e + pltpu.roll(e, step, 0)
            sum_parts[wi].append(e[HIST:HIST + ts, :])
            step *= 2
        h0_parts.append(jnp.broadcast_to(lru0[j], (ts, g)))
        gconv_o[j] = eg[ts + base:ts + HIST, :]
        lconv_o[j] = el[ts + base:ts + HIST, :]
        pool_o[j] = ep[ts + HIST - POOL_BUF:ts + HIST, :]

    zq, zk, zv = z_ref[:, 0:g], z_ref[:, g:2 * g], z_ref[:, 2 * g:3 * g]
    cos, sin = cos_ref[...], sin_ref[...]
    qdec, kdec, sdec = qdec_ref[...], kdec_ref[...], sdec_ref[...]
    rq = zq * cos + _swap_half_heads(zq) * sin
    rk = (zk * cos + _swap_half_heads(zk) * sin) * (HEAD_DIM ** -0.5)
    kd = rk * kdec
    outs = []
    for h in range(N_HEADS):
        sl = slice(h * HEAD_DIM, (h + 1) * HEAD_DIM)
        q, k, v = rq[:, sl], rk[:, sl], zv[:, sl]
        sv = stacked(sret0, h)
        scores = _mm(q, k, NT) * intra_ref[h]
        o = _mm(scores, v) + _mm(expand(q), sv) * qdec[:, sl]
        unstack(ret_o, h, sv * sdec[h:h + 1, :HEAD_DIM] + _mm(expand(kd[:, sl]), v, TN))
        mu = jnp.mean(o, axis=-1, keepdims=True)
        oc = o - mu
        var = jnp.mean(oc * oc, axis=-1, keepdims=True)
        outs.append(oc * lax.rsqrt(var + EPS))
    mix_ref[:, 0:g] = jnp.concatenate(outs, axis=-1) * ret_gn_g * _silu(z_ref[:, 3 * g:4 * g])

    qkv = _silu(jnp.concatenate(y_parts, axis=0))
    zab = z_ref[:, Z_AB:Z_AB + LANES]
    lane = _iota(zab.shape, 1)
    log_a = -jnp.exp(gabp[0:1]) * _softplus(zab + gabp[1:2])
    gab = jnp.where(lane < N_HEADS, log_a, jnp.where(lane < 2 * N_HEADS, _sigmoid(zab), 0.0))
    gc = _cumsum_rows(gab, seg=ts)
    egc = jnp.exp(gc)
    gparts = _split3(gc)
    dotnn = functools.partial(lax.dot_general, dimension_numbers=NN, preferred_element_type=F32)
    dotnt = functools.partial(lax.dot_general, dimension_numbers=NT, preferred_element_type=F32)
    same_b = jnp.where(same, 1.0, 0.0).astype(BF16)
    a0, a1, a2 = _split3(gab)
    glast = dotnn(same_b, a0) + (dotnn(same_b, a1) + dotnn(same_b, a2))
    pick = jnp.where(((_iota((nb * HEAD_DIM, r_all), 0) >> 6) == (_iota((nb * HEAD_DIM, r_all), 1) >> seg_shift))
                     & ((_iota((nb * HEAD_DIM, r_all), 1) & (ts - 1)) == 0), 1.0, 0.0).astype(BF16)
    e0, e1, e2 = _split3(jnp.exp(glast))
    eglast_s = dotnn(pick, e0) + (dotnn(pick, e1) + dotnn(pick, e2))
    heads = range(N_HEADS)
    qs, ks, vs, decs = [], [], [], []
    for h in heads:
        qh = qkv[:, h * HEAD_DIM:(h + 1) * HEAD_DIM]
        kh = qkv[:, g + h * HEAD_DIM:g + (h + 1) * HEAD_DIM]
        qs.append(qh * lax.rsqrt(jnp.sum(qh * qh, axis=-1, keepdims=True) + EPS) * (HEAD_DIM ** -0.5))
        ks.append(kh * lax.rsqrt(jnp.sum(kh * kh, axis=-1, keepdims=True) + EPS))
        vs.append(qkv[:, 2 * g + h * HEAD_DIM:2 * g + (h + 1) * HEAD_DIM])
        sel = _head_lane_select(r_all, h)
        grow = dotnt(sel, gparts[0]) + (dotnt(sel, gparts[1]) + dotnt(sel, gparts[2]))
        decs.append(jnp.where(incl, jnp.exp(jnp.where(incl, gc[:, h:h + 1] - grow, 0.0)), 0.0))
    gcols = [gc[:, h:h + 1] for h in heads]
    bcols = [gab[:, N_HEADS + h:N_HEADS + h + 1] for h in heads]
    ecols = [egc[:, h:h + 1] for h in heads]
    svs = [stacked(sgdn0, h) for h in heads]
    mats = [jnp.where(strict, bcols[h] * decs[h] * _mm(ks[h], ks[h], NT), 0.0) for h in heads]
    tinvs = _inv_unit_lower(mats, ts)
    ws = [bcols[h] * (vs[h] - ecols[h] * _mm(expand(ks[h]), svs[h])) for h in heads]
    us = _solve_unit_lower(mats, tinvs, ws)
    os_ = [ecols[h] * _mm(expand(qs[h]), svs[h]) + _mm(_mm(qs[h], ks[h], NT) * decs[h], us[h]) for h in heads]
    for h in heads:
        kdx = jnp.exp(glast[:, h:h + 1] - gcols[h]) * ks[h]
        unstack(gdn_o, h, eglast_s[:, h:h + 1] * svs[h] + _mm(expand(kdx), us[h], TN))
    outs = [o * lax.rsqrt(jnp.mean(o * o, axis=-1, keepdims=True) + EPS) for o in os_]
    mix_ref[:, g:2 * g] = jnp.concatenate(outs, axis=-1) * gdn_norm_g * _silu(z_ref[:, 7 * g:8 * g])

    xc = jnp.concatenate(xc_parts, axis=0) + lru_conv_b
    r = _sigmoid(jnp.dot(xc.astype(BF16), wa_ref[...], preferred_element_type=F32) + lru_ba)
    gate_i = _sigmoid(jnp.dot(xc.astype(BF16), wx_ref[...], preferred_element_type=F32) + lru_bx)
    log_at = -LRU_C * r * _softplus(-lru_lambda)
    av = jnp.exp(log_at)
    bv = jnp.sqrt(-jnp.tanh(log_at) * (av * av + 1.0)) * (gate_i * xc)
    first = (_iota(bv.shape, 0) & (ts - 1)) == 0
    bv = bv + jnp.where(first, av * jnp.concatenate(h0_parts, axis=0), 0.0)
    d = 1
    while d < ts:
        bv = av * _shift_rows(bv, d, 0.0, ts) + bv
        av = av * _shift_rows(av, d, 1.0, ts)
        d *= 2
    for j in range(nb):
        lru_o[j] = bv[(j + 1) * ts - 1:(j + 1) * ts, :]
    mix_ref[:, 2 * g:3 * g] = bv * _gelu_tanh(z_ref[:, 9 * g:10 * g])

    u_in = z_ref[:, 10 * g:11 * g]
    pos1 = (pos_base + 1 + (_iota((r_all, g), 0) & (ts - 1))).astype(F32)
    lane_g = _iota((r_all, g), 1) >> 6
    sums = [jnp.concatenate(parts, axis=0) for parts in sum_parts]
    pooled = sums[-1] / jnp.minimum(pos1, float(POOL_WINDOWS[-1]))
    for gi in range(len(POOL_WINDOWS) - 2, -1, -1):
        pooled = jnp.where(lane_g == gi, sums[gi] / jnp.minimum(pos1, float(POOL_WINDOWS[gi])), pooled)
    pooled = pooled - u_in
    yp = jnp.dot(pooled.astype(BF16), pw_ref[...], preferred_element_type=F32) + pool_b
    mix_ref[:, 3 * g:4 * g] = yp * pool_scale


def _ret_consts(c):
    log_g = jnp.log1p(-jnp.exp2(-5.0 - jnp.arange(N_HEADS, dtype=F32)))
    idx = jnp.arange(c, dtype=F32)
    diff = idx[:, None] - idx[None, :]
    intra = jnp.where(diff >= 0, jnp.exp(log_g[:, None, None] * jnp.maximum(diff, 0.0)), 0.0)
    qdec = jnp.exp(log_g[:, None] * (idx + 1.0))
    kdec = jnp.exp(log_g[:, None] * (c - 1.0 - idx))
    sdec = jnp.exp(log_g * c)
    expand = lambda t: jnp.repeat(t.T, HEAD_DIM, axis=1)
    sdec_tab = jnp.zeros((8, LANES), F32).at[:N_HEADS, :].set(sdec[:, None])
    return intra, expand(qdec), expand(kdec), sdec_tab


def _rope_tables(pos):
    half = HEAD_DIM // 2
    inv = ROPE_BASE ** (-jnp.arange(half, dtype=F32) / half)
    ang = pos.astype(F32)[:, None] * inv[None, :]
    cos, sin = jnp.cos(ang), jnp.sin(ang)
    cos_t = jnp.tile(jnp.concatenate([cos, cos], axis=-1), (1, N_HEADS))
    sin_t = jnp.tile(jnp.concatenate([-sin, sin], axis=-1), (1, N_HEADS))
    return cos_t, sin_t


def _mixer(z, z_row0, b, l, ts, nb, pos_base, states, layer, lp):
    nt = l // ts
    assert nb == 1 or (nt == 1 and nb % 2 == 0 and ts == INV_BASE)
    g = GROUP_W
    pos = pos_base + jnp.arange(l, dtype=jnp.int32)
    cos_t, sin_t = _rope_tables(pos)
    intra, qdec, kdec, sdec = _ret_consts(ts)
    rt_ = nb * ts
    if nb > 1:
        cos_t, sin_t, qdec, kdec = (jnp.tile(a, (nb, 1)) for a in (cos_t, sin_t, qdec, kdec))
        intra = jnp.einsum("ab,hij->haibj", jnp.eye(nb, dtype=F32), intra).reshape(N_HEADS, rt_, rt_)
    s_ret0, s_gdn0, gconv0, lru0, lconv0, pool0 = states
    blk0 = z_row0 // rt_
    const2 = lambda bi, n: (0, 0)
    per_b3 = lambda bi, n: (bi, 0, 0)
    per_b4 = lambda bi, n: (bi, 0, 0, 0)
    st3 = lambda bi, n: (layer, bi, 0, 0)
    st4 = lambda bi, n: (layer, bi, 0, 0, 0)
    in_specs = [
        pl.BlockSpec((rt_, NZ), lambda bi, n: (blk0 + bi * nt + n, 0)),
        pl.BlockSpec((rt_, g), lambda bi, n: (n, 0)),
        pl.BlockSpec((rt_, g), lambda bi, n: (n, 0)),
        pl.BlockSpec((N_HEADS, rt_, rt_), lambda bi, n: (0, 0, 0)),
        pl.BlockSpec((rt_, g), const2),
        pl.BlockSpec((rt_, g), const2),
        pl.BlockSpec((8, LANES), const2),
        pl.BlockSpec((16, g), const2),
        pl.BlockSpec((CONV_W, 3 * g), const2),
        pl.BlockSpec((8, LANES), const2),
        pl.BlockSpec((g, g), const2),
        pl.BlockSpec((g, g), const2),
        pl.BlockSpec((g, g), const2),
        pl.BlockSpec((None, nb, N_HEADS, HEAD_DIM, HEAD_DIM), st4),
        pl.BlockSpec((None, nb, N_HEADS, HEAD_DIM, HEAD_DIM), st4),
        pl.BlockSpec((None, nb, CONV_W - 1, 3 * g), st3),
        pl.BlockSpec((None, nb, 1, g), st3),
        pl.BlockSpec((None, nb, CONV_W - 1, g), st3),
        pl.BlockSpec((None, nb, POOL_BUF, g), st3),
    ]
    out_specs = [
        pl.BlockSpec((rt_, 4 * g), lambda bi, n: (bi * nt + n, 0)),
        pl.BlockSpec((nb, N_HEADS, HEAD_DIM, HEAD_DIM), per_b4),
        pl.BlockSpec((nb, N_HEADS, HEAD_DIM, HEAD_DIM), per_b4),
        pl.BlockSpec((nb, CONV_W - 1, 3 * g), per_b3),
        pl.BlockSpec((nb, 1, g), per_b3),
        pl.BlockSpec((nb, CONV_W - 1, g), per_b3),
        pl.BlockSpec((nb, POOL_BUF, g), per_b3),
    ]
    out_shape = [
        jax.ShapeDtypeStruct((b * l, 4 * g), F32),
        jax.ShapeDtypeStruct((b, N_HEADS, HEAD_DIM, HEAD_DIM), F32),
        jax.ShapeDtypeStruct((b, N_HEADS, HEAD_DIM, HEAD_DIM), F32),
        jax.ShapeDtypeStruct((b, CONV_W - 1, 3 * g), F32),
        jax.ShapeDtypeStruct((b, 1, g), F32),
        jax.ShapeDtypeStruct((b, CONV_W - 1, g), F32),
        jax.ShapeDtypeStruct((b, POOL_BUF, g), F32),
    ]
    ext = [pltpu.VMEM((nb, ts + HIST, 3 * g), F32), pltpu.VMEM((nb, ts + HIST, g), F32),
           pltpu.VMEM((nb, ts + HIST, g), F32)]
    if nb > 1:
        body = functools.partial(_decode_mixer_kernel, ts, nb, pos_base)
        scratch = ext
    else:
        body = functools.partial(_mixer_kernel, nt, ts, nb, pos_base)
        scratch = [pltpu.VMEM((nb, N_HEADS, HEAD_DIM, HEAD_DIM), F32),
                   pltpu.VMEM((nb, N_HEADS, HEAD_DIM, HEAD_DIM), F32)] + ext + [pltpu.VMEM((nb, 1, g), F32)]
    outs = pl.pallas_call(
        body,
        grid=(b // nb, nt),
        in_specs=in_specs,
        out_specs=out_specs,
        out_shape=out_shape,
        scratch_shapes=scratch,
        compiler_params=pltpu.CompilerParams(dimension_semantics=("arbitrary", "arbitrary"),
                                             vmem_limit_bytes=VMEM_LIMIT),
        name=f"mixer_ts{ts}",
    )(z, cos_t, sin_t, intra, qdec, kdec, sdec, lp["pvec"], lp["gdn_conv_w"], lp["gabp"],
      lp["lru_wa"], lp["lru_wx"], lp["pool_w"], s_ret0, s_gdn0, gconv0, lru0[:, :, None, :], lconv0, pool0)
    mix, ret_n, gdn_n, gconv_n, lru_n, lconv_n, pool_n = outs
    return mix, (ret_n, gdn_n, gconv_n, lru_n[:, 0, :], lconv_n, pool_n)


def _swap_part_and_sublane(parts):
    m = parts[0].shape[0]
    r = _iota((m, LANES), 0)
    parts = list(parts)
    s = SUBLANES // 2
    while s >= 1:
        low = (r & s) == 0
        nxt = list(parts)
        for j in range(SUBLANES):
            if j & s == 0:
                a, b = parts[j], parts[j + s]
                nxt[j] = jnp.where(low, a, pltpu.roll(b, s, 0))
                nxt[j + s] = jnp.where(low, pltpu.roll(a, m - s, 0), b)
        parts = nxt
        s //= 2
    return parts


def _to_row_tiles(ref, x):
    m = x.shape[0]
    tiles = _swap_part_and_sublane([x[:, j * LANES:(j + 1) * LANES] for j in range(SUBLANES)])
    for k in range(SUBLANES):
        ref[:, k] = tiles[k].reshape(m // SUBLANES, SUBLANES, LANES)


def _from_row_tiles(ref):
    m = ref.shape[0] * SUBLANES
    tiles = [ref[:, k].reshape(m, LANES) for k in range(SUBLANES)]
    return jnp.concatenate(_swap_part_and_sublane(tiles), axis=-1)


def _row_tile_shape(m):
    return (m // SUBLANES, SUBLANES, SUBLANES, LANES)


def _wout_route_kernel(na, ma_ref, mb_ref, xa_ref, xb_ref, wo_ref, g_ref, wr_ref, br_ref, x1_ref, h2_ref, rt_ref):
    mix = _pair_tile(na, ma_ref, mb_ref)
    x1 = _pair_tile(na, xa_ref, xb_ref) + jnp.dot(mix.astype(BF16), wo_ref[...], preferred_element_type=F32)
    x1_ref[...] = x1
    h = x1 * lax.rsqrt(jnp.mean(x1 * x1, axis=-1, keepdims=True) + EPS) * g_ref[...]
    _to_row_tiles(h2_ref, h)
    logits = _mm(h, wr_ref[...], passes=3) + br_ref[...]
    lane = _iota(logits.shape, 1)
    lane_f = lane.astype(F32)
    neg = -1e30
    far = 1e9
    is_g = lane < N_GROUPS
    gl = jnp.where(is_g, logits, neg)
    gmax = jnp.max(gl, axis=-1, keepdims=True)
    gsel = jnp.min(jnp.where(gl == gmax, lane_f, far), axis=-1, keepdims=True)
    p_g = 1.0 / jnp.sum(jnp.where(is_g, jnp.exp(gl - gmax), 0.0), axis=-1, keepdims=True)
    e_group = ((lane - N_GROUPS) >> 3).astype(F32)
    in_group = jnp.where(lane >= N_GROUPS, e_group, -1.0) == gsel
    el = jnp.where(in_group, logits, neg)
    m1 = jnp.max(el, axis=-1, keepdims=True)
    i1 = jnp.min(jnp.where(el == m1, lane_f, far), axis=-1, keepdims=True)
    el2 = jnp.where(lane_f == i1, neg, el)
    m2 = jnp.max(el2, axis=-1, keepdims=True)
    i2 = jnp.min(jnp.where(el2 == m2, lane_f, far), axis=-1, keepdims=True)
    e2 = jnp.exp(m2 - m1)
    w1 = p_g / (1.0 + e2)
    w2 = p_g * e2 / (1.0 + e2)
    rt = jnp.where(lane == 0, i1 - N_GROUPS,
                   jnp.where(lane == 1, i2 - N_GROUPS, jnp.where(lane == 2, w1, jnp.where(lane == 3, w2, 0.0))))
    rt_ref[...] = rt


def _wout_route(mix_pair, x_pair, wo_bf16, g, wr, br, tm):
    d = x_pair[0].shape[1]
    na, t = x_pair[0].shape[0] // tm, x_pair[0].shape[0] + x_pair[1].shape[0]
    row = lambda i: (i, 0)
    const = lambda i: (0, 0)
    return pl.pallas_call(
        functools.partial(_wout_route_kernel, na),
        grid=(t // tm,),
        in_specs=_pair_specs(na, tm, d) + _pair_specs(na, tm, d) + [
            pl.BlockSpec((d, d), const), pl.BlockSpec((1, d), const), pl.BlockSpec((d, LANES), const),
            pl.BlockSpec((1, LANES), const)],
        out_specs=[pl.BlockSpec((tm, d), row), pl.BlockSpec(_row_tile_shape(tm), lambda i: (i, 0, 0, 0)),
                   pl.BlockSpec((tm, LANES), row)],
        out_shape=[jax.ShapeDtypeStruct((t, d), F32), jax.ShapeDtypeStruct(_row_tile_shape(t), F32),
                   jax.ShapeDtypeStruct((t, LANES), F32)],
        compiler_params=pltpu.CompilerParams(dimension_semantics=("arbitrary",), vmem_limit_bytes=VMEM_LIMIT),
        name="wout_route",
    )(*mix_pair, *x_pair, wo_bf16, g, wr, br)


def _tile_of_row(ref4, grp, k):
    return ref4.at[grp, pl.ds(k, 1)]


def _for_each_row(tm, fn):
    def body(grp, c):
        for k in range(SUBLANES):
            fn(grp, k, grp * SUBLANES + k)
        return c

    lax.fori_loop(0, tm // SUBLANES, body, 0)


def _route_plan_kernel(tr, tm, n_tiles, rt_ref, pos_ref, tmeta_ref, emeta_ref, cnt_s, start_s, carry_s):
    ph = pl.program_id(0)
    i = pl.program_id(1)
    rt = rt_ref[...]
    lane = _iota((tr, LANES), 1)
    lane_f = lane.astype(F32)
    oh1 = jnp.where(lane_f == rt[:, 0:1], 1.0, 0.0)
    oh2 = jnp.where(lane_f == rt[:, 1:2], 1.0, 0.0)
    oh = oh1 + oh2
    colsum = jnp.sum(oh, axis=0, keepdims=True)

    @pl.when((ph == 0) & (i == 0))
    def _():
        cnt_s[...] = jnp.zeros((1, LANES), F32)

    @pl.when(ph == 0)
    def _():
        cnt_s[...] = cnt_s[...] + colsum

    @pl.when((ph == 1) & (i == 0))
    def _():
        cnt = cnt_s[...]
        ntile = jnp.floor((cnt + (tm - 1.0)) * (1.0 / tm))
        before = jnp.where(_iota((LANES, LANES), 0) < _iota((LANES, LANES), 1), 1.0, 0.0)
        st_tiles = _mm(jnp.broadcast_to(ntile, (SUBLANES, LANES)), before)[0:1]
        end_tiles = st_tiles + ntile
        start_s[...] = st_tiles * tm
        carry_s[...] = jnp.zeros((1, LANES), F32)
        is_e1 = _iota((1, LANES), 1) < N_EXPERTS
        n_used = jnp.sum(jnp.where(is_e1, ntile, 0.0), axis=-1, keepdims=True)
        last = jnp.sum(jnp.where(is_e1 & (end_tiles <= n_used - 1.0), 1.0, 0.0), axis=-1, keepdims=True)
        ti = _iota((n_tiles, LANES), 0).astype(F32)
        lane_t = _iota((n_tiles, LANES), 1)
        te = jnp.sum(jnp.where((lane_t < N_EXPERTS) & (end_tiles <= ti), 1.0, 0.0), axis=-1, keepdims=True)
        te = jnp.minimum(te, last)
        tmeta_ref[...] = jnp.where(lane_t == 0, te, jnp.where(lane_t == 1, n_used, 0.0))
        srow = _iota((SUBLANES, LANES), 0)
        emeta_ref[...] = jnp.where(srow == 0, ntile, jnp.where(srow == 1, (end_tiles - 1.0) * tm, 0.0))

    @pl.when(ph == 1)
    def _():
        earlier = jnp.where(_iota((tr, tr), 0) > _iota((tr, tr), 1), 1.0, 0.0)
        base = start_s[...] + carry_s[...] + _mm(earlier, oh)
        p1 = jnp.sum(oh1 * base, axis=-1, keepdims=True)
        p2 = jnp.sum(oh2 * base, axis=-1, keepdims=True)
        pos_ref[...] = jnp.where(lane == 0, p1, jnp.where(lane == 1, p2, 0.0))
        carry_s[...] = carry_s[...] + colsum


def _route_plan(rt, tm, n_tiles):
    t = rt.shape[0]
    tr = next(c for c in (1024, 512, 256, tm) if t % c == 0)
    pos, tmeta, emeta = pl.pallas_call(
        functools.partial(_route_plan_kernel, tr, tm, n_tiles),
        grid=(2, t // tr),
        in_specs=[pl.BlockSpec((tr, LANES), lambda ph, i: (i, 0))],
        out_specs=[pl.BlockSpec((tr, LANES), lambda ph, i: (ph * i, 0)),
                   pl.BlockSpec((n_tiles, LANES), lambda ph, i: (0, 0)),
                   pl.BlockSpec((SUBLANES, LANES), lambda ph, i: (0, 0))],
        out_shape=[jax.ShapeDtypeStruct((t, LANES), F32), jax.ShapeDtypeStruct((n_tiles, LANES), F32),
                   jax.ShapeDtypeStruct((SUBLANES, LANES), F32)],
        scratch_shapes=[pltpu.VMEM((1, LANES), F32), pltpu.VMEM((1, LANES), F32), pltpu.VMEM((1, LANES), F32)],
        compiler_params=pltpu.CompilerParams(dimension_semantics=("arbitrary", "arbitrary"),
                                             vmem_limit_bytes=VMEM_LIMIT),
        name="route_plan",
    )(rt)
    pos1 = pos[:, 0].astype(jnp.int32)
    pos2 = pos[:, 1].astype(jnp.int32)
    tile_expert = tmeta[:, 0].astype(jnp.int32)
    n_used = tmeta[0:1, 1].astype(jnp.int32)
    has_rows = (emeta[0, :N_EXPERTS] > 0).astype(jnp.int32)
    last_tile_row = jnp.maximum(emeta[1, :N_EXPERTS], 0.0).astype(jnp.int32)
    return pos1, pos2, tile_expert, n_used, has_rows, last_tile_row


def _scatter_kernel(tm, n_tiles, min_tiles, zon_ref, zrow_ref, nu_ref, p1_ref, p2_ref, h_ref, xs_hbm,
                    zbuf, sem1, sem2, zsem, tsem):
    i = pl.program_id(0)

    @pl.when(i == 0)
    def _():
        zbuf[...] = jnp.zeros(zbuf.shape, F32)

        def zero_copy(e):
            return pltpu.make_async_copy(zbuf, xs_hbm.at[pl.ds(zrow_ref[e], tm)], zsem.at[e])

        def tail_copy(k):
            return pltpu.make_async_copy(zbuf, xs_hbm.at[pl.ds((min_tiles + k) * tm, tm)], tsem.at[k])

        def zstart(e, c):
            @pl.when(zon_ref[e] > 0)
            def _():
                zero_copy(e).start()
            return c

        def zwait(e, c):
            @pl.when(zon_ref[e] > 0)
            def _():
                zero_copy(e).wait()
            return c

        def tstart(k, c):
            @pl.when(min_tiles + k >= nu_ref[0])
            def _():
                tail_copy(k).start()
            return c

        def twait(k, c):
            @pl.when(min_tiles + k >= nu_ref[0])
            def _():
                tail_copy(k).wait()
            return c

        lax.fori_loop(0, N_EXPERTS, zstart, 0)
        lax.fori_loop(0, n_tiles - min_tiles, tstart, 0)
        lax.fori_loop(0, N_EXPERTS, zwait, 0)
        lax.fori_loop(0, n_tiles - min_tiles, twait, 0)

    def copies(grp, k, r):
        src = _tile_of_row(h_ref, grp, k)
        return (pltpu.make_async_copy(src, xs_hbm.at[pl.ds(p1_ref[0, 0, r], 1)], sem1.at[r]),
                pltpu.make_async_copy(src, xs_hbm.at[pl.ds(p2_ref[0, 0, r], 1)], sem2.at[r]))

    def start(grp, k, r):
        c1, c2 = copies(grp, k, r)
        c1.start(priority=0)
        c2.start(priority=1)

    def wait(grp, k, r):
        c1, c2 = copies(grp, k, r)
        c1.wait()
        c2.wait()

    _for_each_row(tm, start)
    _for_each_row(tm, wait)


def _scatter_rows(h2, pos1, pos2, has_rows, last_tile_row, n_used, n_tiles, tm):
    t = h2.shape[0] * SUBLANES
    nb = t // tm
    min_tiles = (2 * t) // tm
    smem_blk = pl.BlockSpec((1, 1, tm), lambda i, zon, zrow, nu: (i, 0, 0), memory_space=pltpu.SMEM)
    grid_spec = pltpu.PrefetchScalarGridSpec(
        num_scalar_prefetch=3,
        grid=(nb,),
        in_specs=[smem_blk, smem_blk, pl.BlockSpec(_row_tile_shape(tm), lambda i, zon, zrow, nu: (i, 0, 0, 0))],
        out_specs=pl.BlockSpec(memory_space=pl.ANY),
        scratch_shapes=[pltpu.VMEM((tm, SUBLANES, LANES), F32), pltpu.SemaphoreType.DMA((tm,)),
                        pltpu.SemaphoreType.DMA((tm,)), pltpu.SemaphoreType.DMA((N_EXPERTS,)),
                        pltpu.SemaphoreType.DMA((n_tiles - min_tiles,))],
    )
    return pl.pallas_call(
        functools.partial(_scatter_kernel, tm, n_tiles, min_tiles),
        grid_spec=grid_spec,
        out_shape=jax.ShapeDtypeStruct((n_tiles * tm, SUBLANES, LANES), F32),
        compiler_params=pltpu.CompilerParams(dimension_semantics=("arbitrary",), vmem_limit_bytes=VMEM_LIMIT),
        name="moe_scatter",
    )(has_rows, last_tile_row, n_used, pos1.reshape(nb, 1, tm), pos2.reshape(nb, 1, tm), h2)


def _ffn_kernel(te_ref, nu_ref, xs_ref, wg_ref, wu_ref, wd_ref, ys_ref, wg_s, wu_s, wd_s):
    i = pl.program_id(0)
    used = i < nu_ref[0]
    new_expert = (i == 0) | (te_ref[i] != te_ref[jnp.maximum(i - 1, 0)])

    @pl.when(used & new_expert)
    def _():
        wg_s[...] = wg_ref[...].astype(BF16)
        wu_s[...] = wu_ref[...].astype(BF16)
        wd_s[...] = wd_ref[...].astype(BF16)

    @pl.when(used)
    def _():
        x = _from_row_tiles(xs_ref).astype(BF16)
        gate = jnp.dot(x, wg_s[...], preferred_element_type=F32)
        up = jnp.dot(x, wu_s[...], preferred_element_type=F32)
        act = (_silu(gate) * up).astype(BF16)
        _to_row_tiles(ys_ref, jnp.dot(act, wd_s[...], preferred_element_type=F32))

    @pl.when(jnp.logical_not(used))
    def _():
        ys_ref[...] = jnp.zeros(ys_ref.shape, F32)


def _expert_ffn(xs, tile_expert, n_used, w_gate, w_up, w_down, layer, tm):
    p = xs.shape[0]
    xs = xs.reshape(_row_tile_shape(p))
    _, _, d, de = w_gate.shape
    row = lambda i, te, nu: (jnp.minimum(i, nu[0] - 1), 0, 0, 0)
    wsel = lambda i, te, nu: (layer, te[i], 0, 0)
    grid_spec = pltpu.PrefetchScalarGridSpec(
        num_scalar_prefetch=2,
        grid=(p // tm,),
        in_specs=[pl.BlockSpec(_row_tile_shape(tm), row),
                  pl.BlockSpec((None, None, d, de), wsel),
                  pl.BlockSpec((None, None, d, de), wsel),
                  pl.BlockSpec((None, None, de, d), wsel)],
        out_specs=pl.BlockSpec(_row_tile_shape(tm), lambda i, te, nu: (i, 0, 0, 0)),
        scratch_shapes=[pltpu.VMEM((d, de), BF16), pltpu.VMEM((d, de), BF16), pltpu.VMEM((de, d), BF16)],
    )
    return pl.pallas_call(
        _ffn_kernel,
        grid_spec=grid_spec,
        out_shape=jax.ShapeDtypeStruct(_row_tile_shape(p), F32),
        compiler_params=pltpu.CompilerParams(dimension_semantics=("arbitrary",), vmem_limit_bytes=VMEM_LIMIT),
        name="expert_ffn",
    )(tile_expert, n_used, xs, w_gate, w_up, w_down)


def _combine_kernel(tm, na, final_norm, p1_ref, p2_ref, x1_ref, rt_ref, fg_ref, ys_hbm, outa_ref, outb_ref,
                    buf1, buf2, sem1, sem2):
    def copies(grp, k, r):
        return (pltpu.make_async_copy(ys_hbm.at[pl.ds(p1_ref[0, 0, r], 1)], _tile_of_row(buf1, grp, k), sem1.at[r]),
                pltpu.make_async_copy(ys_hbm.at[pl.ds(p2_ref[0, 0, r], 1)], _tile_of_row(buf2, grp, k), sem2.at[r]))

    def start(grp, k, r):
        c1, c2 = copies(grp, k, r)
        c1.start(priority=0)
        c2.start(priority=1)

    def wait(grp, k, r):
        c1, c2 = copies(grp, k, r)
        c1.wait()
        c2.wait()

    _for_each_row(tm, start)
    _for_each_row(tm, wait)
    rt = rt_ref[...]
    x = x1_ref[...] + rt[:, 2:3] * _from_row_tiles(buf1) + rt[:, 3:4] * _from_row_tiles(buf2)
    if final_norm:
        x = x * lax.rsqrt(jnp.mean(x * x, axis=-1, keepdims=True) + EPS) * fg_ref[...]

    @pl.when(pl.program_id(0) < na)
    def _():
        outa_ref[...] = x

    @pl.when(pl.program_id(0) >= na)
    def _():
        outb_ref[...] = x


def _combine(x1, rt, ys, pos1, pos2, final_g, final_norm, ta, tm):
    t, d = x1.shape
    nb = t // tm
    na = ta // tm
    row = lambda i: (i, 0)
    smem_blk = pl.BlockSpec((1, 1, tm), lambda i: (i, 0, 0), memory_space=pltpu.SMEM)
    return pl.pallas_call(
        functools.partial(_combine_kernel, tm, na, final_norm),
        grid=(nb,),
        in_specs=[smem_blk, smem_blk, pl.BlockSpec((tm, d), row), pl.BlockSpec((tm, LANES), row),
                  pl.BlockSpec((1, d), lambda i: (0, 0)), pl.BlockSpec(memory_space=pl.ANY)],
        out_specs=[pl.BlockSpec((tm, d), lambda i: (jnp.minimum(i, na - 1), 0)),
                   pl.BlockSpec((tm, d), lambda i: (jnp.maximum(i - na, 0), 0))],
        out_shape=[jax.ShapeDtypeStruct((ta, d), F32), jax.ShapeDtypeStruct((t - ta, d), F32)],
        scratch_shapes=[pltpu.VMEM(_row_tile_shape(tm), F32), pltpu.VMEM(_row_tile_shape(tm), F32),
                        pltpu.SemaphoreType.DMA((tm,)), pltpu.SemaphoreType.DMA((tm,))],
        compiler_params=pltpu.CompilerParams(dimension_semantics=("arbitrary",), vmem_limit_bytes=VMEM_LIMIT),
        name="moe_combine",
    )(pos1.reshape(nb, 1, tm), pos2.reshape(nb, 1, tm), x1, rt, final_g, ys.reshape(-1, SUBLANES, LANES))


def _block_diag(w):
    h, a, b = w.shape
    return jnp.einsum("hij,hk->hikj", w, jnp.eye(h, dtype=w.dtype)).reshape(h * a, h * b)


def _layer_params(l, p):
    g = GROUP_W
    w_in = p["w_in"][l]
    d = w_in.shape[0]
    w_perm = jnp.concatenate([w_in[:, :8 * g], w_in[:, 8 * g + 8:], w_in[:, 8 * g:8 * g + 8],
                              jnp.zeros((d, LANES - 8), w_in.dtype)], axis=1)
    rows = [p["ret_gn_g"][l], jnp.tile(p["gdn_norm_g"][l], N_HEADS), p["lru_conv_b"][l], p["lru_ba"][l],
            p["lru_bx"][l], p["lru_lambda"][l], p["pool_b"][l], p["pool_scale"][l]]
    pvec = jnp.concatenate([jnp.stack(rows), p["lru_conv_w"][l], jnp.zeros((4, g), F32)], axis=0)
    lane_pad = lambda v, n: jnp.concatenate([v, jnp.zeros(v.shape[:-1] + (LANES - n,), F32)], axis=-1)
    gabp = jnp.concatenate([lane_pad(jnp.stack([p["gdn_a_log"][l], p["gdn_dt_bias"][l]]), N_HEADS),
                            jnp.zeros((6, LANES), F32)], axis=0)
    n_route = N_GROUPS + N_EXPERTS
    wr = lane_pad(jnp.concatenate([p["router_group_w"][l],
                                   jnp.transpose(p["router_expert_w"][l], (1, 0, 2)).reshape(d, N_EXPERTS)],
                                  axis=1), n_route)
    br = lane_pad(jnp.concatenate([p["router_group_b"][l], p["router_expert_b"][l].reshape(N_EXPERTS)])[None, :],
                  n_route)
    return dict(
        norm1_g=p["norm1_g"][l][None, :], w_in=w_perm.astype(BF16), pvec=pvec, gdn_conv_w=p["gdn_conv_w"][l],
        gabp=gabp, lru_wa=_block_diag(p["lru_wa"][l]).astype(BF16), lru_wx=_block_diag(p["lru_wx"][l]).astype(BF16),
        pool_w=_block_diag(p["pool_w"][l]).astype(BF16), w_out=p["w_out"][l].astype(BF16),
        norm2_g=p["norm2_g"][l][None, :], wr=wr, br=br)


def kernel(x_prompt, x_sample, state_ret, state_gdn, state_gdn_conv, state_lru, state_lru_conv, state_pool,
           norm1_g, w_in, ret_gn_g, gdn_conv_w, gdn_a_log, gdn_dt_bias, gdn_norm_g,
           lru_conv_w, lru_conv_b, lru_wa, lru_ba, lru_wx, lru_bx, lru_lambda,
           pool_w, pool_b, pool_scale, w_out, norm2_g,
           router_group_w, router_group_b, router_expert_w, router_expert_b,
           moe_w_gate, moe_w_up, moe_w_down, final_g):
    params = dict(norm1_g=norm1_g, w_in=w_in, ret_gn_g=ret_gn_g, gdn_conv_w=gdn_conv_w, gdn_a_log=gdn_a_log,
                  gdn_dt_bias=gdn_dt_bias, gdn_norm_g=gdn_norm_g, lru_conv_w=lru_conv_w, lru_conv_b=lru_conv_b,
                  lru_wa=lru_wa, lru_ba=lru_ba, lru_wx=lru_wx, lru_bx=lru_bx, lru_lambda=lru_lambda,
                  pool_w=pool_w, pool_b=pool_b, pool_scale=pool_scale, w_out=w_out, norm2_g=norm2_g,
                  router_group_w=router_group_w, router_group_b=router_group_b,
                  router_expert_w=router_expert_w, router_expert_b=router_expert_b)
    bp, lp_, d = x_prompt.shape
    bs, ls, _ = x_sample.shape
    depth = w_in.shape[0]
    tp, ts_ = bp * lp_, bs * ls
    t = tp + ts_
    tile_p = min(256, lp_)
    tm = 256
    n_tiles = (2 * t + N_EXPERTS * (tm - 1) + tm - 1) // tm

    x = (x_prompt.reshape(tp, d), x_sample.reshape(ts_, d))
    sample_states = (state_ret, state_gdn, state_gdn_conv, state_lru, state_lru_conv, state_pool)
    zero_states = tuple(jnp.zeros((1, bp) + s.shape[2:], F32) for s in sample_states)
    new_p, new_s = [], []
    for l in range(depth):
        lp = _layer_params(l, params)
        z = _norm_win(x, lp["norm1_g"], lp["w_in"], tm)
        mix_p, st_p = _mixer(z, 0, bp, lp_, tile_p, 1, 0, zero_states, 0, lp)
        mix_s, st_s = _mixer(z, tp, bs, ls, ls, SAMPLE_SEQS_PER_STEP, PAST_LEN, sample_states, l, lp)
        new_p.append(st_p)
        new_s.append(st_s)
        x1, h2, rt = _wout_route((mix_p, mix_s), x, lp["w_out"], lp["norm2_g"], lp["wr"], lp["br"], tm)
        pos1, pos2, tile_expert, n_used, has_rows, last_tile_row = _route_plan(rt, tm, n_tiles)
        xs = _scatter_rows(h2, pos1, pos2, has_rows, last_tile_row, n_used, n_tiles, tm)
        ys = _expert_ffn(xs, tile_expert, n_used, moe_w_gate, moe_w_up, moe_w_down, l, tm)
        x = _combine(x1, rt, ys, pos1, pos2, final_g[None, :], l == depth - 1, tp, tm)
    y_prompt = x[0].reshape(bp, lp_, d)
    y_sample = x[1].reshape(bs, ls, d)
    stack = lambda news: [jnp.stack([ns[j] for ns in news], axis=0) for j in range(6)]
    return (y_prompt, y_sample, *stack(new_p), *stack(new_s))
```

```python
import functools

import jax
import jax.numpy as jnp
from jax import lax
from jax.experimental import pallas as pl
from jax.experimental.pallas import tpu as pltpu

F32 = jnp.float32
BF16 = jnp.bfloat16

EPS = 1e-6
HEAD_DIM = 64
N_HEADS = 4
GROUP_W = 256
CONV_W = 4
LRU_C = 8.0
POOL_WINDOWS = (2, 4, 8, 16)
POOL_BUF = 15
ROPE_BASE = 10000.0
N_GROUPS = 4
EXPERTS_PER_GROUP = 8
N_EXPERTS = 32
PAST_LEN = 16384

LANES = 128
SUBLANES = 8
HIST = 16
Z_AB = 11 * GROUP_W
NZ = Z_AB + LANES
INV_BASE = 8
RESID_PASSES = 3
SAMPLE_SEQS_PER_STEP = 16
VMEM_LIMIT = 56 * 1024 * 1024

NN = (((1,), (0,)), ((), ()))
NT = (((1,), (1,)), ((), ()))
TN = (((0,), (0,)), ((), ()))


def _mm(a, b, dims=NN, passes=1):
    ah = a.astype(BF16)
    bh = b.astype(BF16)
    dot = functools.partial(lax.dot_general, dimension_numbers=dims, preferred_element_type=F32)
    if passes == 1:
        return dot(ah, bh)
    al = (a - ah.astype(F32)).astype(BF16)
    bl = (b - bh.astype(F32)).astype(BF16)
    return dot(ah, bh) + (dot(ah, bl) + dot(al, bh))


def _sigmoid(x):
    return 1.0 / (1.0 + jnp.exp(-x))


def _silu(x):
    return x * _sigmoid(x)


def _softplus(x):
    return jnp.maximum(x, 0.0) + jnp.log1p(jnp.exp(-jnp.abs(x)))


def _gelu_tanh(x):
    return 0.5 * x * (1.0 + jnp.tanh(0.7978845608028654 * (x + 0.044715 * (x * x * x))))


def _iota(shape, axis):
    return lax.broadcasted_iota(jnp.int32, shape, axis)


def _shift_rows(x, d, fill, seg=None):
    t = _iota(x.shape, 0)
    if seg is not None:
        t = t & (seg - 1)
    return jnp.where(t >= d, pltpu.roll(x, d, 0), fill)


def _cumsum_rows(x, seg=None):
    d = 1
    while d < (seg or x.shape[0]):
        x = x + _shift_rows(x, d, 0.0, seg)
        d *= 2
    return x


def _swap_half_heads(x):
    n = x.shape[1]
    lower = (_iota(x.shape, 1) & (HEAD_DIM - 1)) < HEAD_DIM // 2
    return jnp.where(lower, pltpu.roll(x, n - HEAD_DIM // 2, 1), pltpu.roll(x, HEAD_DIM // 2, 1))


def _inv_unit_lower(mats, c=None):
    shape = mats[0].shape
    c = c or shape[0]
    row = _iota(shape, 0)
    col = _iota(shape, 1)
    nb = min(INV_BASE, c)
    shift = nb.bit_length() - 1
    in_base = (row >> shift) == (col >> shift)
    ps = [jnp.where(in_base, a, 0.0) for a in mats]
    eye = jnp.where(row == col, 1.0, 0.0)
    xs = [eye - d for d in ps]
    k = 2
    while k < nb:
        ps = [_mm(p, p) for p in ps]
        xs = [x + _mm(x, p) for x, p in zip(xs, ps)]
        k *= 2
    m = nb
    while m < c:
        s1 = m.bit_length() - 1
        in_off = ((row >> (s1 + 1)) == (col >> (s1 + 1))) & ((row >> s1) != (col >> s1))
        ys = [_mm(x, jnp.where(in_off, a, 0.0)) for x, a in zip(xs, mats)]
        xs = [x - _mm(y, x) for x, y in zip(xs, ys)]
        m *= 2
    return xs


def _solve_unit_lower(mats, tinvs, ws):
    us = [_mm(t, w) for t, w in zip(tinvs, ws)]
    rs = [w - (u + _mm(a, u, passes=RESID_PASSES)) for a, u, w in zip(mats, us, ws)]
    return [u + _mm(t, r) for u, t, r in zip(us, tinvs, rs)]


def _pair_specs(na, tm, d):
    return [pl.BlockSpec((tm, d), lambda i: (jnp.minimum(i, na - 1), 0)),
            pl.BlockSpec((tm, d), lambda i: (jnp.maximum(i - na, 0), 0))]


def _pair_tile(na, a_ref, b_ref):
    return jnp.where(pl.program_id(0) < na, a_ref[...], b_ref[...])


def _norm_win_kernel(na, xa_ref, xb_ref, g_ref, w_ref, z_ref):
    x = _pair_tile(na, xa_ref, xb_ref)
    h = x * lax.rsqrt(jnp.mean(x * x, axis=-1, keepdims=True) + EPS) * g_ref[...]
    z_ref[...] = jnp.dot(h.astype(BF16), w_ref[...], preferred_element_type=F32)


def _norm_win(x_pair, g, w_bf16, tm):
    xa, xb = x_pair
    d = xa.shape[1]
    na, t = xa.shape[0] // tm, xa.shape[0] + xb.shape[0]
    nz = w_bf16.shape[1]
    return pl.pallas_call(
        functools.partial(_norm_win_kernel, na),
        grid=(t // tm,),
        in_specs=_pair_specs(na, tm, d) + [pl.BlockSpec((1, d), lambda i: (0, 0)),
                                           pl.BlockSpec((d, nz), lambda i: (0, 0))],
        out_specs=pl.BlockSpec((tm, nz), lambda i: (i, 0)),
        out_shape=jax.ShapeDtypeStruct((t, nz), F32),
        compiler_params=pltpu.CompilerParams(dimension_semantics=("arbitrary",), vmem_limit_bytes=VMEM_LIMIT),
        name="norm_win",
    )(xa, xb, g, w_bf16)


def _retention_tile(zq, zk, zv, cos, sin, intra_ref, qdec, kdec, sdec, s_ref):
    rq = zq * cos + _swap_half_heads(zq) * sin
    rk = (zk * cos + _swap_half_heads(zk) * sin) * (HEAD_DIM ** -0.5)
    kd = rk * kdec
    heads = range(N_HEADS)
    sls = [slice(h * HEAD_DIM, (h + 1) * HEAD_DIM) for h in heads]
    ss = [s_ref[h] for h in heads]
    scores = [_mm(rq[:, sl], rk[:, sl], NT) * intra_ref[h] for h, sl in zip(heads, sls)]
    os_ = [_mm(sc, zv[:, sl]) + _mm(rq[:, sl], s) * qdec[:, sl] for sc, s, sl in zip(scores, ss, sls)]
    s_new = [s * sdec[h:h + 1, :HEAD_DIM] + _mm(kd[:, sl], zv[:, sl], TN) for h, s, sl in zip(heads, ss, sls)]
    for h in heads:
        s_ref[h] = s_new[h]
    outs = []
    for o in os_:
        mu = jnp.mean(o, axis=-1, keepdims=True)
        oc = o - mu
        var = jnp.mean(oc * oc, axis=-1, keepdims=True)
        outs.append(oc * lax.rsqrt(var + EPS))
    return jnp.concatenate(outs, axis=-1)


def _split3(x):
    x0 = x.astype(BF16)
    r1 = x - x0.astype(F32)
    x1 = r1.astype(BF16)
    x2 = (r1 - x1.astype(F32)).astype(BF16)
    return x0, x1, x2


def _gdn_tile(gq, gk, gv, gab, s_ref):
    c = gq.shape[0]
    gc = _cumsum_rows(gab)
    egc = jnp.exp(gc)
    row = _iota((c, c), 0)
    col = _iota((c, c), 1)
    incl = row >= col
    strict = row > col
    heads = range(N_HEADS)
    sls = [slice(h * HEAD_DIM, (h + 1) * HEAD_DIM) for h in heads]
    ss = [s_ref[h] for h in heads]
    gcols = [gc[:, h:h + 1] for h in heads]
    bcols = [gab[:, N_HEADS + h:N_HEADS + h + 1] for h in heads]
    ecols = [egc[:, h:h + 1] for h in heads]
    gct = gc.T
    decs = [jnp.where(incl, jnp.exp(jnp.where(incl, gcols[h] - gct[h:h + 1, :], 0.0)), 0.0) for h in heads]
    mats = [jnp.where(strict, bcols[h] * decs[h] * _mm(gk[:, sl], gk[:, sl], NT), 0.0) for h, sl in zip(heads, sls)]
    tinvs = _inv_unit_lower(mats)
    ws = [bcols[h] * (gv[:, sl] - ecols[h] * _mm(gk[:, sl], ss[h])) for h, sl in zip(heads, sls)]
    us = _solve_unit_lower(mats, tinvs, ws)
    qks = [_mm(gq[:, sl], gk[:, sl], NT) * decs[h] for h, sl in zip(heads, sls)]
    outs = [ecols[h] * _mm(gq[:, sl], ss[h]) + _mm(qks[h], us[h]) for h, sl in zip(heads, sls)]
    glasts = [gc[c - 1:c, h:h + 1] for h in heads]
    s_new = [jnp.exp(glasts[h]) * ss[h] + _mm(jnp.exp(glasts[h] - gcols[h]) * gk[:, sl], us[h], TN)
             for h, sl in zip(heads, sls)]
    for h in heads:
        s_ref[h] = s_new[h]
    return outs


def _mixer_kernel(nt, ts, nb, pos_base,
                  z_ref, cos_ref, sin_ref, intra_ref, qdec_ref, kdec_ref, sdec_ref, pv_ref, gcw_ref, gabp_ref,
                  wa_ref, wx_ref, pw_ref, sret0, sgdn0, gconv0, lru0, lconv0, pool0,
                  mix_ref, ret_o, gdn_o, gconv_o, lru_o, lconv_o, pool_o,
                  s_ret, s_gdn, ext_g, ext_l, ext_p, h_lru):
    n = pl.program_id(1)
    g = GROUP_W
    base = HIST - (CONV_W - 1)

    @pl.when(n == 0)
    def _():
        s_ret[...] = sret0[...]
        s_gdn[...] = sgdn0[...]
        ext_g[:, 0:HIST, :] = jnp.zeros((nb, HIST, 3 * g), F32)
        ext_g[:, base:HIST, :] = gconv0[...]
        ext_l[:, 0:HIST, :] = jnp.zeros((nb, HIST, g), F32)
        ext_l[:, base:HIST, :] = lconv0[...]
        ext_p[:, 0:HIST, :] = jnp.zeros((nb, HIST, g), F32)
        ext_p[:, HIST - POOL_BUF:HIST, :] = pool0[...]
        h_lru[...] = lru0[...]

    pv = pv_ref[...]
    ret_gn_g, gdn_norm_g = pv[0:1], pv[1:2]
    lru_conv_b, lru_ba, lru_bx, lru_lambda = pv[2:3], pv[3:4], pv[4:5], pv[5:6]
    pool_b, pool_scale = pv[6:7], pv[7:8]
    lru_conv_w = pv[8:12]
    gcw = gcw_ref[...]
    gabp = gabp_ref[...]

    for j in range(nb):
        rows = slice(j * ts, (j + 1) * ts)
        eg, el, ep = ext_g.at[j], ext_l.at[j], ext_p.at[j]

        ro = _retention_tile(z_ref[rows, 0:g], z_ref[rows, g:2 * g], z_ref[rows, 2 * g:3 * g], cos_ref[...],
                             sin_ref[...], intra_ref, qdec_ref[...], kdec_ref[...], sdec_ref[...], s_ret.at[j])
        mix_ref[rows, 0:g] = ro * ret_gn_g * _silu(z_ref[rows, 3 * g:4 * g])

        eg[HIST:HIST + ts, :] = z_ref[rows, 4 * g:7 * g]
        y = eg[base:base + ts, :] * gcw[0:1]
        for i in range(1, CONV_W):
            y = y + eg[base + i:base + i + ts, :] * gcw[i:i + 1]
        qkv = _silu(y)
        zab = z_ref[rows, Z_AB:Z_AB + LANES]
        lane = _iota(zab.shape, 1)
        log_a = -jnp.exp(gabp[0:1]) * _softplus(zab + gabp[1:2])
        gab = jnp.where(lane < N_HEADS, log_a, jnp.where(lane < 2 * N_HEADS, _sigmoid(zab), 0.0))
        gq_parts, gk_parts = [], []
        for h in range(N_HEADS):
            qh = qkv[:, h * HEAD_DIM:(h + 1) * HEAD_DIM]
            kh = qkv[:, g + h * HEAD_DIM:g + (h + 1) * HEAD_DIM]
            gq_parts.append(qh * lax.rsqrt(jnp.sum(qh * qh, axis=-1, keepdims=True) + EPS) * (HEAD_DIM ** -0.5))
            gk_parts.append(kh * lax.rsqrt(jnp.sum(kh * kh, axis=-1, keepdims=True) + EPS))
        go = _gdn_tile(jnp.concatenate(gq_parts, axis=-1), jnp.concatenate(gk_parts, axis=-1),
                       qkv[:, 2 * g:3 * g], gab, s_gdn.at[j])
        go = [o * lax.rsqrt(jnp.mean(o * o, axis=-1, keepdims=True) + EPS) for o in go]
        mix_ref[rows, g:2 * g] = jnp.concatenate(go, axis=-1) * gdn_norm_g * _silu(z_ref[rows, 7 * g:8 * g])

        el[HIST:HIST + ts, :] = z_ref[rows, 8 * g:9 * g]
        xc = el[base:base + ts, :] * lru_conv_w[0:1]
        for i in range(1, CONV_W):
            xc = xc + el[base + i:base + i + ts, :] * lru_conv_w[i:i + 1]
        xc = xc + lru_conv_b
        r = _sigmoid(jnp.dot(xc.astype(BF16), wa_ref[...], preferred_element_type=F32) + lru_ba)
        gate_i = _sigmoid(jnp.dot(xc.astype(BF16), wx_ref[...], preferred_element_type=F32) + lru_bx)
        log_at = -LRU_C * r * _softplus(-lru_lambda)
        av = jnp.exp(log_at)
        bv = jnp.sqrt(-jnp.tanh(log_at) * (av * av + 1.0)) * (gate_i * xc)
        bv = bv + jnp.where(_iota(bv.shape, 0) == 0, av * h_lru[j], 0.0)
        d = 1
        while d < ts:
            bv = av * _shift_rows(bv, d, 0.0) + bv
            av = av * _shift_rows(av, d, 1.0)
            d *= 2
        h_lru[j] = bv[ts - 1:ts, :]
        mix_ref[rows, 2 * g:3 * g] = bv * _gelu_tanh(z_ref[rows, 9 * g:10 * g])

        u = z_ref[rows, 10 * g:11 * g]
        ep[HIST:HIST + ts, :] = u
        e = ep[...]
        sums = []
        step = 1
        for _ in POOL_WINDOWS:
            e = e + pltpu.roll(e, step, 0)
            sums.append(e[HIST:HIST + ts, :])
            step *= 2
        pos1 = (pos_base + 1 + n * ts + _iota((ts, g), 0)).astype(F32)
        lane_g = _iota((ts, g), 1) >> 6
        pooled = sums[-1] / jnp.minimum(pos1, float(POOL_WINDOWS[-1]))
        for gi in range(len(POOL_WINDOWS) - 2, -1, -1):
            pooled = jnp.where(lane_g == gi, sums[gi] / jnp.minimum(pos1, float(POOL_WINDOWS[gi])), pooled)
        pooled = pooled - u
        yp = jnp.dot(pooled.astype(BF16), pw_ref[...], preferred_element_type=F32) + pool_b
        mix_ref[rows, 3 * g:4 * g] = yp * pool_scale

        tail_g = eg[ts:ts + HIST, :]
        tail_l = el[ts:ts + HIST, :]
        tail_p = ep[ts:ts + HIST, :]
        eg[0:HIST, :] = tail_g
        el[0:HIST, :] = tail_l
        ep[0:HIST, :] = tail_p

    @pl.when(n == nt - 1)
    def _():
        ret_o[...] = s_ret[...]
        gdn_o[...] = s_gdn[...]
        gconv_o[...] = ext_g[:, base:HIST, :]
        lconv_o[...] = ext_l[:, base:HIST, :]
        pool_o[...] = ext_p[:, HIST - POOL_BUF:HIST, :]
        lru_o[...] = h_lru[...]


def _decode_mixer_kernel(ts, nb, pos_base,
                         z_ref, cos_ref, sin_ref, intra_ref, qdec_ref, kdec_ref, sdec_ref, pv_ref, gcw_ref, gabp_ref,
                         wa_ref, wx_ref, pw_ref, sret0, sgdn0, gconv0, lru0, lconv0, pool0,
                         mix_ref, ret_o, gdn_o, gconv_o, lru_o, lconv_o, pool_o,
                         ext_g, ext_l, ext_p):
    g = GROUP_W
    r_all = nb * ts
    base = HIST - (CONV_W - 1)
    seg_shift = ts.bit_length() - 1
    row2 = _iota((r_all, r_all), 0)
    col2 = _iota((r_all, r_all), 1)
    same = (row2 >> seg_shift) == (col2 >> seg_shift)
    incl = same & (row2 >= col2)
    strict = same & (row2 > col2)
    seq_l = _iota((r_all, LANES), 0) >> seg_shift
    half_l = _iota((r_all, LANES), 1) >> 6
    exp_masks = [((seq_l >> 1) == c) & ((seq_l & 1) == half_l) for c in range(nb // 2)]

    def expand(x):
        x2 = jnp.concatenate([x, x], axis=-1)
        return jnp.concatenate([jnp.where(m, x2, 0.0) for m in exp_masks], axis=-1)

    def stacked(state_ref, h):
        return jnp.concatenate([state_ref[j, h] for j in range(nb)], axis=0)

    def unstack(out_ref, h, sv):
        for j in range(nb):
            out_ref[j, h] = sv[j * HEAD_DIM:(j + 1) * HEAD_DIM, :]

    pv = pv_ref[...]
    ret_gn_g, gdn_norm_g = pv[0:1], pv[1:2]
    lru_conv_b, lru_ba, lru_bx, lru_lambda = pv[2:3], pv[3:4], pv[4:5], pv[5:6]
    pool_b, pool_scale = pv[6:7], pv[7:8]
    lru_conv_w = pv[8:12]
    gcw = gcw_ref[...]
    gabp = gabp_ref[...]

    ext_g[:, 0:HIST, :] = jnp.zeros((nb, HIST, 3 * g), F32)
    ext_g[:, base:HIST, :] = gconv0[...]
    ext_l[:, 0:HIST, :] = jnp.zeros((nb, HIST, g), F32)
    ext_l[:, base:HIST, :] = lconv0[...]
    ext_p[:, 0:HIST, :] = jnp.zeros((nb, HIST, g), F32)
    ext_p[:, HIST - POOL_BUF:HIST, :] = pool0[...]
    y_parts, xc_parts, sum_parts, h0_parts = [], [], [[] for _ in POOL_WINDOWS], []
    for j in range(nb):
        rows = slice(j * ts, (j + 1) * ts)
        eg, el, ep = ext_g.at[j], ext_l.at[j], ext_p.at[j]
        eg[HIST:HIST + ts, :] = z_ref[rows, 4 * g:7 * g]
        el[HIST:HIST + ts, :] = z_ref[rows, 8 * g:9 * g]
        ep[HIST:HIST + ts, :] = z_ref[rows, 10 * g:11 * g]
        y = eg[base:base + ts, :] * gcw[0:1]
        xc = el[base:base + ts, :] * lru_conv_w[0:1]
        for i in range(1, CONV_W):
            y = y + eg[base + i:base + i + ts, :] * gcw[i:i + 1]
            xc = xc + el[base + i:base + i + ts, :] * lru_conv_w[i:i + 1]
        y_parts.append(y)
        xc_parts.append(xc)
        e = ep[...]
        step = 1
        for wi in range(len(POOL_WINDOWS)):
            e = e + pltpu.roll(e, step, 0)
            sum_parts[wi].append(e[HIST:HIST + ts, :])
            step *= 2
        h0_parts.append(jnp.broadcast_to(lru0[j], (ts, g)))
        gconv_o[j] = eg[ts + base:ts + HIST, :]
        lconv_o[j] = el[ts + base:ts + HIST, :]
        pool_o[j] = ep[ts + HIST - POOL_BUF:ts + HIST, :]

    zq, zk, zv = z_ref[:, 0:g], z_ref[:, g:2 * g], z_ref[:, 2 * g:3 * g]
    cos, sin = cos_ref[...], sin_ref[...]
    qdec, kdec, sdec = qdec_ref[...], kdec_ref[...], sdec_ref[...]
    rq = zq * cos + _swap_half_heads(zq) * sin
    rk = (zk * cos + _swap_half_heads(zk) * sin) * (HEAD_DIM ** -0.5)
    kd = rk * kdec
    outs = []
    for h in range(N_HEADS):
        sl = slice(h * HEAD_DIM, (h + 1) * HEAD_DIM)
        q, k, v = rq[:, sl], rk[:, sl], zv[:, sl]
        sv = stacked(sret0, h)
        scores = _mm(q, k, NT) * intra_ref[h]
        o = _mm(scores, v) + _mm(expand(q), sv) * qdec[:, sl]
        unstack(ret_o, h, sv * sdec[h:h + 1, :HEAD_DIM] + _mm(expand(kd[:, sl]), v, TN))
        mu = jnp.mean(o, axis=-1, keepdims=True)
        oc = o - mu
        var = jnp.mean(oc * oc, axis=-1, keepdims=True)
        outs.append(oc * lax.rsqrt(var + EPS))
    mix_ref[:, 0:g] = jnp.concatenate(outs, axis=-1) * ret_gn_g * _silu(z_ref[:, 3 * g:4 * g])

    qkv = _silu(jnp.concatenate(y_parts, axis=0))
    zab = z_ref[:, Z_AB:Z_AB + LANES]
    lane = _iota(zab.shape, 1)
    log_a = -jnp.exp(gabp[0:1]) * _softplus(zab + gabp[1:2])
    gab = jnp.where(lane < N_HEADS, log_a, jnp.where(lane < 2 * N_HEADS, _sigmoid(zab), 0.0))
    gc = _cumsum_rows(gab, seg=ts)
    egc = jnp.exp(gc)
    gct = gc.T
    dotnn = functools.partial(lax.dot_general, dimension_numbers=NN, preferred_element_type=F32)
    same_b = jnp.where(same, 1.0, 0.0).astype(BF16)
    a0, a1, a2 = _split3(gab)
    glast = dotnn(same_b, a0) + (dotnn(same_b, a1) + dotnn(same_b, a2))
    pick = jnp.where(((_iota((nb * HEAD_DIM, r_all), 0) >> 6) == (_iota((nb * HEAD_DIM, r_all), 1) >> seg_shift))
                     & ((_iota((nb * HEAD_DIM, r_all), 1) & (ts - 1)) == 0), 1.0, 0.0).astype(BF16)
    e0, e1, e2 = _split3(jnp.exp(glast))
    eglast_s = dotnn(pick, e0) + (dotnn(pick, e1) + dotnn(pick, e2))
    heads = range(N_HEADS)
    qs, ks, vs, decs = [], [], [], []
    for h in heads:
        qh = qkv[:, h * HEAD_DIM:(h + 1) * HEAD_DIM]
        kh = qkv[:, g + h * HEAD_DIM:g + (h + 1) * HEAD_DIM]
        qs.append(qh * lax.rsqrt(jnp.sum(qh * qh, axis=-1, keepdims=True) + EPS) * (HEAD_DIM ** -0.5))
        ks.append(kh * lax.rsqrt(jnp.sum(kh * kh, axis=-1, keepdims=True) + EPS))
        vs.append(qkv[:, 2 * g + h * HEAD_DIM:2 * g + (h + 1) * HEAD_DIM])
        decs.append(jnp.where(incl, jnp.exp(jnp.where(incl, gc[:, h:h + 1] - gct[h:h + 1, :], 0.0)), 0.0))
    gcols = [gc[:, h:h + 1] for h in heads]
    bcols = [gab[:, N_HEADS + h:N_HEADS + h + 1] for h in heads]
    ecols = [egc[:, h:h + 1] for h in heads]
    svs = [stacked(sgdn0, h) for h in heads]
    mats = [jnp.where(strict, bcols[h] * decs[h] * _mm(ks[h], ks[h], NT), 0.0) for h in heads]
    tinvs = _inv_unit_lower(mats, ts)
    ws = [bcols[h] * (vs[h] - ecols[h] * _mm(expand(ks[h]), svs[h])) for h in heads]
    us = _solve_unit_lower(mats, tinvs, ws)
    os_ = [ecols[h] * _mm(expand(qs[h]), svs[h]) + _mm(_mm(qs[h], ks[h], NT) * decs[h], us[h]) for h in heads]
    for h in heads:
        kdx = jnp.exp(glast[:, h:h + 1] - gcols[h]) * ks[h]
        unstack(gdn_o, h, eglast_s[:, h:h + 1] * svs[h] + _mm(expand(kdx), us[h], TN))
    outs = [o * lax.rsqrt(jnp.mean(o * o, axis=-1, keepdims=True) + EPS) for o in os_]
    mix_ref[:, g:2 * g] = jnp.concatenate(outs, axis=-1) * gdn_norm_g * _silu(z_ref[:, 7 * g:8 * g])

    xc = jnp.concatenate(xc_parts, axis=0) + lru_conv_b
    r = _sigmoid(jnp.dot(xc.astype(BF16), wa_ref[...], preferred_element_type=F32) + lru_ba)
    gate_i = _sigmoid(jnp.dot(xc.astype(BF16), wx_ref[...], preferred_element_type=F32) + lru_bx)
    log_at = -LRU_C * r * _softplus(-lru_lambda)
    av = jnp.exp(log_at)
    bv = jnp.sqrt(-jnp.tanh(log_at) * (av * av + 1.0)) * (gate_i * xc)
    first = (_iota(bv.shape, 0) & (ts - 1)) == 0
    bv = bv + jnp.where(first, av * jnp.concatenate(h0_parts, axis=0), 0.0)
    d = 1
    while d < ts:
        bv = av * _shift_rows(bv, d, 0.0, ts) + bv
        av = av * _shift_rows(av, d, 1.0, ts)
        d *= 2
    for j in range(nb):
        lru_o[j] = bv[(j + 1) * ts - 1:(j + 1) * ts, :]
    mix_ref[:, 2 * g:3 * g] = bv * _gelu_tanh(z_ref[:, 9 * g:10 * g])

    u_in = z_ref[:, 10 * g:11 * g]
    pos1 = (pos_base + 1 + (_iota((r_all, g), 0) & (ts - 1))).astype(F32)
    lane_g = _iota((r_all, g), 1) >> 6
    sums = [jnp.concatenate(parts, axis=0) for parts in sum_parts]
    pooled = sums[-1] / jnp.minimum(pos1, float(POOL_WINDOWS[-1]))
    for gi in range(len(POOL_WINDOWS) - 2, -1, -1):
        pooled = jnp.where(lane_g == gi, sums[gi] / jnp.minimum(pos1, float(POOL_WINDOWS[gi])), pooled)
    pooled = pooled - u_in
    yp = jnp.dot(pooled.astype(BF16), pw_ref[...], preferred_element_type=F32) + pool_b
    mix_ref[:, 3 * g:4 * g] = yp * pool_scale


N_STATES = 6


def _with_state_stack(body, nt, n_prev, n_in, *refs):
    ins = refs[:n_in]
    prevs = refs[n_in:n_in + (N_STATES if n_prev else 0)]
    k = n_in + len(prevs)
    mix_ref, stacks, scratch = refs[k], refs[k + 1:k + 1 + N_STATES], refs[k + 1 + N_STATES:]

    @pl.when(pl.program_id(1) == nt - 1)
    def _():
        for o, p in zip(stacks, prevs):
            o[0:n_prev] = p[...]

    body(*ins, mix_ref, *[o.at[n_prev] for o in stacks], *scratch)


def _ret_consts(c):
    log_g = jnp.log1p(-jnp.exp2(-5.0 - jnp.arange(N_HEADS, dtype=F32)))
    idx = jnp.arange(c, dtype=F32)
    diff = idx[:, None] - idx[None, :]
    intra = jnp.where(diff >= 0, jnp.exp(log_g[:, None, None] * jnp.maximum(diff, 0.0)), 0.0)
    qdec = jnp.exp(log_g[:, None] * (idx + 1.0))
    kdec = jnp.exp(log_g[:, None] * (c - 1.0 - idx))
    sdec = jnp.exp(log_g * c)
    expand = lambda t: jnp.repeat(t.T, HEAD_DIM, axis=1)
    sdec_tab = jnp.zeros((8, LANES), F32).at[:N_HEADS, :].set(sdec[:, None])
    return intra, expand(qdec), expand(kdec), sdec_tab


def _rope_tables(pos):
    half = HEAD_DIM // 2
    inv = ROPE_BASE ** (-jnp.arange(half, dtype=F32) / half)
    ang = pos.astype(F32)[:, None] * inv[None, :]
    cos, sin = jnp.cos(ang), jnp.sin(ang)
    cos_t = jnp.tile(jnp.concatenate([cos, cos], axis=-1), (1, N_HEADS))
    sin_t = jnp.tile(jnp.concatenate([-sin, sin], axis=-1), (1, N_HEADS))
    return cos_t, sin_t


def _mixer(z, z_row0, b, l, ts, nb, pos_base, states, layer, prev, lp):
    nt = l // ts
    assert nb == 1 or (nt == 1 and nb % 2 == 0 and ts == INV_BASE)
    g = GROUP_W
    pos = pos_base + jnp.arange(l, dtype=jnp.int32)
    cos_t, sin_t = _rope_tables(pos)
    intra, qdec, kdec, sdec = _ret_consts(ts)
    rt_ = nb * ts
    if nb > 1:
        cos_t, sin_t, qdec, kdec = (jnp.tile(a, (nb, 1)) for a in (cos_t, sin_t, qdec, kdec))
        intra = jnp.einsum("ab,hij->haibj", jnp.eye(nb, dtype=F32), intra).reshape(N_HEADS, rt_, rt_)
    s_ret0, s_gdn0, gconv0, lru0, lconv0, pool0 = states
    blk0 = z_row0 // rt_
    const2 = lambda bi, n: (0, 0)
    per_b3 = lambda bi, n: (bi, 0, 0)
    per_b4 = lambda bi, n: (bi, 0, 0, 0)
    st3 = lambda bi, n: (layer, bi, 0, 0)
    st4 = lambda bi, n: (layer, bi, 0, 0, 0)
    in_specs = [
        pl.BlockSpec((rt_, NZ), lambda bi, n: (blk0 + bi * nt + n, 0)),
        pl.BlockSpec((rt_, g), lambda bi, n: (n, 0)),
        pl.BlockSpec((rt_, g), lambda bi, n: (n, 0)),
        pl.BlockSpec((N_HEADS, rt_, rt_), lambda bi, n: (0, 0, 0)),
        pl.BlockSpec((rt_, g), const2),
        pl.BlockSpec((rt_, g), const2),
        pl.BlockSpec((8, LANES), const2),
        pl.BlockSpec((16, g), const2),
        pl.BlockSpec((CONV_W, 3 * g), const2),
        pl.BlockSpec((8, LANES), const2),
        pl.BlockSpec((g, g), const2),
        pl.BlockSpec((g, g), const2),
        pl.BlockSpec((g, g), const2),
        pl.BlockSpec((None, nb, N_HEADS, HEAD_DIM, HEAD_DIM), st4),
        pl.BlockSpec((None, nb, N_HEADS, HEAD_DIM, HEAD_DIM), st4),
        pl.BlockSpec((None, nb, CONV_W - 1, 3 * g), st3),
        pl.BlockSpec((None, nb, 1, g), st3),
        pl.BlockSpec((None, nb, CONV_W - 1, g), st3),
        pl.BlockSpec((None, nb, POOL_BUF, g), st3),
    ]
    st_shapes = [(N_HEADS, HEAD_DIM, HEAD_DIM), (N_HEADS, HEAD_DIM, HEAD_DIM), (CONV_W - 1, 3 * g), (1, g),
                 (CONV_W - 1, g), (POOL_BUF, g)]
    n_prev = 0 if prev is None else prev[0].shape[0]
    stack_spec = lambda layers, s: pl.BlockSpec((layers, nb) + s, lambda bi, n: (0, bi) + (0,) * len(s))
    n_in = len(in_specs)
    if n_prev:
        in_specs = in_specs + [stack_spec(n_prev, s) for s in st_shapes]
    out_specs = [pl.BlockSpec((rt_, 4 * g), lambda bi, n: (bi * nt + n, 0))] + [
        stack_spec(n_prev + 1, s) for s in st_shapes]
    out_shape = [jax.ShapeDtypeStruct((b * l, 4 * g), F32)] + [
        jax.ShapeDtypeStruct((n_prev + 1, b) + s, F32) for s in st_shapes]
    ext = [pltpu.VMEM((nb, ts + HIST, 3 * g), F32), pltpu.VMEM((nb, ts + HIST, g), F32),
           pltpu.VMEM((nb, ts + HIST, g), F32)]
    if nb > 1:
        body = functools.partial(_decode_mixer_kernel, ts, nb, pos_base)
        scratch = ext
    else:
        body = functools.partial(_mixer_kernel, nt, ts, nb, pos_base)
        scratch = [pltpu.VMEM((nb, N_HEADS, HEAD_DIM, HEAD_DIM), F32),
                   pltpu.VMEM((nb, N_HEADS, HEAD_DIM, HEAD_DIM), F32)] + ext + [pltpu.VMEM((nb, 1, g), F32)]
    outs = pl.pallas_call(
        functools.partial(_with_state_stack, body, nt, n_prev, n_in),
        grid=(b // nb, nt),
        in_specs=in_specs,
        out_specs=out_specs,
        out_shape=out_shape,
        scratch_shapes=scratch,
        compiler_params=pltpu.CompilerParams(dimension_semantics=("arbitrary", "arbitrary"),
                                             vmem_limit_bytes=VMEM_LIMIT),
        name=f"mixer_ts{ts}",
    )(z, cos_t, sin_t, intra, qdec, kdec, sdec, lp["pvec"], lp["gdn_conv_w"], lp["gabp"],
      lp["lru_wa"], lp["lru_wx"], lp["pool_w"], s_ret0, s_gdn0, gconv0, lru0[:, :, None, :], lconv0, pool0,
      *(prev or ()))
    return outs[0], tuple(outs[1:])


def _swap_part_and_sublane(parts):
    m = parts[0].shape[0]
    r = _iota((m, LANES), 0)
    parts = list(parts)
    s = SUBLANES // 2
    while s >= 1:
        low = (r & s) == 0
        nxt = list(parts)
        for j in range(SUBLANES):
            if j & s == 0:
                a, b = parts[j], parts[j + s]
                nxt[j] = jnp.where(low, a, pltpu.roll(b, s, 0))
                nxt[j + s] = jnp.where(low, pltpu.roll(a, m - s, 0), b)
        parts = nxt
        s //= 2
    return parts


def _to_row_tiles(ref, x):
    m = x.shape[0]
    tiles = _swap_part_and_sublane([x[:, j * LANES:(j + 1) * LANES] for j in range(SUBLANES)])
    for k in range(SUBLANES):
        ref[:, k] = tiles[k].reshape(m // SUBLANES, SUBLANES, LANES)


def _from_row_tiles(ref):
    m = ref.shape[0] * SUBLANES
    tiles = [ref[:, k].reshape(m, LANES) for k in range(SUBLANES)]
    return jnp.concatenate(_swap_part_and_sublane(tiles), axis=-1)


def _row_tile_shape(m):
    return (m // SUBLANES, SUBLANES, SUBLANES, LANES)


def _wout_route_kernel(na, ma_ref, mb_ref, xa_ref, xb_ref, wo_ref, g_ref, wr_ref, br_ref, x1_ref, h2_ref, rt_ref):
    mix = _pair_tile(na, ma_ref, mb_ref)
    x1 = _pair_tile(na, xa_ref, xb_ref) + jnp.dot(mix.astype(BF16), wo_ref[...], preferred_element_type=F32)
    x1_ref[...] = x1
    h = x1 * lax.rsqrt(jnp.mean(x1 * x1, axis=-1, keepdims=True) + EPS) * g_ref[...]
    _to_row_tiles(h2_ref, h)
    logits = _mm(h, wr_ref[...], passes=3) + br_ref[...]
    lane = _iota(logits.shape, 1)
    lane_f = lane.astype(F32)
    neg = -1e30
    far = 1e9
    is_g = lane < N_GROUPS
    gl = jnp.where(is_g, logits, neg)
    gmax = jnp.max(gl, axis=-1, keepdims=True)
    gsel = jnp.min(jnp.where(gl == gmax, lane_f, far), axis=-1, keepdims=True)
    p_g = 1.0 / jnp.sum(jnp.where(is_g, jnp.exp(gl - gmax), 0.0), axis=-1, keepdims=True)
    e_group = ((lane - N_GROUPS) >> 3).astype(F32)
    in_group = jnp.where(lane >= N_GROUPS, e_group, -1.0) == gsel
    el = jnp.where(in_group, logits, neg)
    m1 = jnp.max(el, axis=-1, keepdims=True)
    i1 = jnp.min(jnp.where(el == m1, lane_f, far), axis=-1, keepdims=True)
    el2 = jnp.where(lane_f == i1, neg, el)
    m2 = jnp.max(el2, axis=-1, keepdims=True)
    i2 = jnp.min(jnp.where(el2 == m2, lane_f, far), axis=-1, keepdims=True)
    e2 = jnp.exp(m2 - m1)
    w1 = p_g / (1.0 + e2)
    w2 = p_g * e2 / (1.0 + e2)
    rt = jnp.where(lane == 0, i1 - N_GROUPS,
                   jnp.where(lane == 1, i2 - N_GROUPS, jnp.where(lane == 2, w1, jnp.where(lane == 3, w2, 0.0))))
    rt_ref[...] = rt


def _wout_route(mix_pair, x_pair, wo_bf16, g, wr, br, tm):
    d = x_pair[0].shape[1]
    na, t = x_pair[0].shape[0] // tm, x_pair[0].shape[0] + x_pair[1].shape[0]
    row = lambda i: (i, 0)
    const = lambda i: (0, 0)
    return pl.pallas_call(
        functools.partial(_wout_route_kernel, na),
        grid=(t // tm,),
        in_specs=_pair_specs(na, tm, d) + _pair_specs(na, tm, d) + [
            pl.BlockSpec((d, d), const), pl.BlockSpec((1, d), const), pl.BlockSpec((d, LANES), const),
            pl.BlockSpec((1, LANES), const)],
        out_specs=[pl.BlockSpec((tm, d), row), pl.BlockSpec(_row_tile_shape(tm), lambda i: (i, 0, 0, 0)),
                   pl.BlockSpec((tm, LANES), row)],
        out_shape=[jax.ShapeDtypeStruct((t, d), F32), jax.ShapeDtypeStruct(_row_tile_shape(t), F32),
                   jax.ShapeDtypeStruct((t, LANES), F32)],
        compiler_params=pltpu.CompilerParams(dimension_semantics=("arbitrary",), vmem_limit_bytes=VMEM_LIMIT),
        name="wout_route",
    )(*mix_pair, *x_pair, wo_bf16, g, wr, br)


def _tile_of_row(ref4, grp, k):
    return ref4.at[grp, pl.ds(k, 1)]


def _for_each_row(tm, fn):
    def body(grp, c):
        for k in range(SUBLANES):
            fn(grp, k, grp * SUBLANES + k)
        return c

    lax.fori_loop(0, tm // SUBLANES, body, 0)


def _route_plan_kernel(tr, tm, n_tiles, rt_ref, pos_ref, tmeta_ref, emeta_ref, cnt_s, start_s, carry_s):
    ph = pl.program_id(0)
    i = pl.program_id(1)
    rt = rt_ref[...]
    lane = _iota((tr, LANES), 1)
    lane_f = lane.astype(F32)
    oh1 = jnp.where(lane_f == rt[:, 0:1], 1.0, 0.0)
    oh2 = jnp.where(lane_f == rt[:, 1:2], 1.0, 0.0)
    oh = oh1 + oh2
    colsum = jnp.sum(oh, axis=0, keepdims=True)

    @pl.when((ph == 0) & (i == 0))
    def _():
        cnt_s[...] = jnp.zeros((1, LANES), F32)

    @pl.when(ph == 0)
    def _():
        cnt_s[...] = cnt_s[...] + colsum

    @pl.when((ph == 1) & (i == 0))
    def _():
        cnt = cnt_s[...]
        ntile = jnp.floor((cnt + (tm - 1.0)) * (1.0 / tm))
        before = jnp.where(_iota((LANES, LANES), 0) < _iota((LANES, LANES), 1), 1.0, 0.0)
        st_tiles = _mm(jnp.broadcast_to(ntile, (SUBLANES, LANES)), before)[0:1]
        end_tiles = st_tiles + ntile
        start_s[...] = st_tiles * tm
        carry_s[...] = jnp.zeros((1, LANES), F32)
        is_e1 = _iota((1, LANES), 1) < N_EXPERTS
        n_used = jnp.sum(jnp.where(is_e1, ntile, 0.0), axis=-1, keepdims=True)
        last = jnp.sum(jnp.where(is_e1 & (end_tiles <= n_used - 1.0), 1.0, 0.0), axis=-1, keepdims=True)
        ti = _iota((n_tiles, LANES), 0).astype(F32)
        lane_t = _iota((n_tiles, LANES), 1)
        te = jnp.sum(jnp.where((lane_t < N_EXPERTS) & (end_tiles <= ti), 1.0, 0.0), axis=-1, keepdims=True)
        te = jnp.minimum(te, last)
        tmeta_ref[...] = jnp.where(lane_t == 0, te, jnp.where(lane_t == 1, n_used, 0.0))
        srow = _iota((SUBLANES, LANES), 0)
        emeta_ref[...] = jnp.where(srow == 0, ntile, jnp.where(srow == 1, (end_tiles - 1.0) * tm, 0.0))

    @pl.when(ph == 1)
    def _():
        earlier = jnp.where(_iota((tr, tr), 0) > _iota((tr, tr), 1), 1.0, 0.0)
        base = start_s[...] + carry_s[...] + _mm(earlier, oh)
        p1 = jnp.sum(oh1 * base, axis=-1, keepdims=True)
        p2 = jnp.sum(oh2 * base, axis=-1, keepdims=True)
        pos_ref[...] = jnp.where(lane == 0, p1, jnp.where(lane == 1, p2, 0.0))
        carry_s[...] = carry_s[...] + colsum


def _route_plan(rt, tm, n_tiles):
    t = rt.shape[0]
    tr = next(c for c in (1024, 512, 256, tm) if t % c == 0)
    pos, tmeta, emeta = pl.pallas_call(
        functools.partial(_route_plan_kernel, tr, tm, n_tiles),
        grid=(2, t // tr),
        in_specs=[pl.BlockSpec((tr, LANES), lambda ph, i: (i, 0))],
        out_specs=[pl.BlockSpec((tr, LANES), lambda ph, i: (ph * i, 0)),
                   pl.BlockSpec((n_tiles, LANES), lambda ph, i: (0, 0)),
                   pl.BlockSpec((SUBLANES, LANES), lambda ph, i: (0, 0))],
        out_shape=[jax.ShapeDtypeStruct((t, LANES), F32), jax.ShapeDtypeStruct((n_tiles, LANES), F32),
                   jax.ShapeDtypeStruct((SUBLANES, LANES), F32)],
        scratch_shapes=[pltpu.VMEM((1, LANES), F32), pltpu.VMEM((1, LANES), F32), pltpu.VMEM((1, LANES), F32)],
        compiler_params=pltpu.CompilerParams(dimension_semantics=("arbitrary", "arbitrary"),
                                             vmem_limit_bytes=VMEM_LIMIT),
        name="route_plan",
    )(rt)
    pos1 = pos[:, 0].astype(jnp.int32)
    pos2 = pos[:, 1].astype(jnp.int32)
    tile_expert = tmeta[:, 0].astype(jnp.int32)
    n_used = tmeta[0:1, 1].astype(jnp.int32)
    has_rows = (emeta[0, :N_EXPERTS] > 0).astype(jnp.int32)
    last_tile_row = jnp.maximum(emeta[1, :N_EXPERTS], 0.0).astype(jnp.int32)
    return pos1, pos2, tile_expert, n_used, has_rows, last_tile_row


def _scatter_kernel(tm, n_tiles, min_tiles, zon_ref, zrow_ref, nu_ref, p1_ref, p2_ref, h_ref, xs_hbm,
                    zbuf, sem1, sem2, zsem, tsem):
    i = pl.program_id(0)

    @pl.when(i == 0)
    def _():
        zbuf[...] = jnp.zeros(zbuf.shape, F32)

        def zero_copy(e):
            return pltpu.make_async_copy(zbuf, xs_hbm.at[pl.ds(zrow_ref[e], tm)], zsem.at[e])

        def tail_copy(k):
            return pltpu.make_async_copy(zbuf, xs_hbm.at[pl.ds((min_tiles + k) * tm, tm)], tsem.at[k])

        def zstart(e, c):
            @pl.when(zon_ref[e] > 0)
            def _():
                zero_copy(e).start()
            return c

        def zwait(e, c):
            @pl.when(zon_ref[e] > 0)
            def _():
                zero_copy(e).wait()
            return c

        def tstart(k, c):
            @pl.when(min_tiles + k >= nu_ref[0])
            def _():
                tail_copy(k).start()
            return c

        def twait(k, c):
            @pl.when(min_tiles + k >= nu_ref[0])
            def _():
                tail_copy(k).wait()
            return c

        lax.fori_loop(0, N_EXPERTS, zstart, 0)
        lax.fori_loop(0, n_tiles - min_tiles, tstart, 0)
        lax.fori_loop(0, N_EXPERTS, zwait, 0)
        lax.fori_loop(0, n_tiles - min_tiles, twait, 0)

    def copies(grp, k, r):
        src = _tile_of_row(h_ref, grp, k)
        return (pltpu.make_async_copy(src, xs_hbm.at[pl.ds(p1_ref[0, 0, r], 1)], sem1),
                pltpu.make_async_copy(src, xs_hbm.at[pl.ds(p2_ref[0, 0, r], 1)], sem2))

    def start(grp, k, r):
        c1, c2 = copies(grp, k, r)
        c1.start(priority=0)
        c2.start(priority=1)

    def wait(grp, k, r):
        c1, c2 = copies(grp, k, r)
        c1.wait()
        c2.wait()

    _for_each_row(tm, start)
    _for_each_row(tm, wait)


def _scatter_rows(h2, pos1, pos2, has_rows, last_tile_row, n_used, n_tiles, tm):
    t = h2.shape[0] * SUBLANES
    nb = t // tm
    min_tiles = (2 * t) // tm
    smem_blk = pl.BlockSpec((1, 1, tm), lambda i, zon, zrow, nu: (i, 0, 0), memory_space=pltpu.SMEM)
    grid_spec = pltpu.PrefetchScalarGridSpec(
        num_scalar_prefetch=3,
        grid=(nb,),
        in_specs=[smem_blk, smem_blk, pl.BlockSpec(_row_tile_shape(tm), lambda i, zon, zrow, nu: (i, 0, 0, 0))],
        out_specs=pl.BlockSpec(memory_space=pl.ANY),
        scratch_shapes=[pltpu.VMEM((tm, SUBLANES, LANES), F32), pltpu.SemaphoreType.DMA,
                        pltpu.SemaphoreType.DMA, pltpu.SemaphoreType.DMA((N_EXPERTS,)),
                        pltpu.SemaphoreType.DMA((n_tiles - min_tiles,))],
    )
    return pl.pallas_call(
        functools.partial(_scatter_kernel, tm, n_tiles, min_tiles),
        grid_spec=grid_spec,
        out_shape=jax.ShapeDtypeStruct((n_tiles * tm, SUBLANES, LANES), F32),
        compiler_params=pltpu.CompilerParams(dimension_semantics=("arbitrary",), vmem_limit_bytes=VMEM_LIMIT),
        name="moe_scatter",
    )(has_rows, last_tile_row, n_used, pos1.reshape(nb, 1, tm), pos2.reshape(nb, 1, tm), h2)


def _ffn_kernel(te_ref, nu_ref, xs_ref, wg_ref, wu_ref, wd_ref, ys_ref, wg_s, wu_s, wd_s):
    i = pl.program_id(0)
    used = i < nu_ref[0]
    new_expert = (i == 0) | (te_ref[i] != te_ref[jnp.maximum(i - 1, 0)])

    @pl.when(used & new_expert)
    def _():
        wg_s[...] = wg_ref[...].astype(BF16)
        wu_s[...] = wu_ref[...].astype(BF16)
        wd_s[...] = wd_ref[...].astype(BF16)

    @pl.when(used)
    def _():
        x = _from_row_tiles(xs_ref).astype(BF16)
        gate = jnp.dot(x, wg_s[...], preferred_element_type=F32)
        up = jnp.dot(x, wu_s[...], preferred_element_type=F32)
        act = (_silu(gate) * up).astype(BF16)
        _to_row_tiles(ys_ref, jnp.dot(act, wd_s[...], preferred_element_type=F32))

    @pl.when(jnp.logical_not(used))
    def _():
        ys_ref[...] = jnp.zeros(ys_ref.shape, F32)


def _expert_ffn(xs, tile_expert, n_used, w_gate, w_up, w_down, layer, tm):
    p = xs.shape[0]
    xs = xs.reshape(_row_tile_shape(p))
    _, _, d, de = w_gate.shape
    row = lambda i, te, nu: (jnp.minimum(i, nu[0] - 1), 0, 0, 0)
    wsel = lambda i, te, nu: (layer, te[i], 0, 0)
    grid_spec = pltpu.PrefetchScalarGridSpec(
        num_scalar_prefetch=2,
        grid=(p // tm,),
        in_specs=[pl.BlockSpec(_row_tile_shape(tm), row),
                  pl.BlockSpec((None, None, d, de), wsel),
                  pl.BlockSpec((None, None, d, de), wsel),
                  pl.BlockSpec((None, None, de, d), wsel)],
        out_specs=pl.BlockSpec(_row_tile_shape(tm), lambda i, te, nu: (i, 0, 0, 0)),
        scratch_shapes=[pltpu.VMEM((d, de), BF16), pltpu.VMEM((d, de), BF16), pltpu.VMEM((de, d), BF16)],
    )
    return pl.pallas_call(
        _ffn_kernel,
        grid_spec=grid_spec,
        out_shape=jax.ShapeDtypeStruct(_row_tile_shape(p), F32),
        compiler_params=pltpu.CompilerParams(dimension_semantics=("arbitrary",), vmem_limit_bytes=VMEM_LIMIT),
        name="expert_ffn",
    )(tile_expert, n_used, xs, w_gate, w_up, w_down)


def _combine_kernel(tm, n_steps, na, final_norm, p1_ref, p2_ref, p1n_ref, p2n_ref, x1_ref, rt_ref, fg_ref, ys_hbm,
                    outa_ref, outb_ref, buf1, buf2, sem1, sem2):
    i = pl.program_id(0)
    slot = i % 2

    def copies(pa_ref, pb_ref, s, grp, k, r):
        return (pltpu.make_async_copy(ys_hbm.at[pl.ds(pa_ref[0, 0, r], 1)], _tile_of_row(buf1.at[s], grp, k),
                                      sem1.at[s]),
                pltpu.make_async_copy(ys_hbm.at[pl.ds(pb_ref[0, 0, r], 1)], _tile_of_row(buf2.at[s], grp, k),
                                      sem2.at[s]))

    def start(pa_ref, pb_ref, s):
        def go(grp, k, r):
            c1, c2 = copies(pa_ref, pb_ref, s, grp, k, r)
            c1.start(priority=0)
            c2.start(priority=1)
        _for_each_row(tm, go)

    @pl.when(i == 0)
    def _():
        start(p1_ref, p2_ref, 0)

    @pl.when(i + 1 < n_steps)
    def _():
        start(p1n_ref, p2n_ref, 1 - slot)

    def wait(grp, k, r):
        c1, c2 = copies(p1_ref, p2_ref, slot, grp, k, r)
        c1.wait()
        c2.wait()

    _for_each_row(tm, wait)
    rt = rt_ref[...]
    x = x1_ref[...] + rt[:, 2:3] * _from_row_tiles(buf1.at[slot]) + rt[:, 3:4] * _from_row_tiles(buf2.at[slot])
    if final_norm:
        x = x * lax.rsqrt(jnp.mean(x * x, axis=-1, keepdims=True) + EPS) * fg_ref[...]

    @pl.when(pl.program_id(0) < na)
    def _():
        outa_ref[...] = x

    @pl.when(pl.program_id(0) >= na)
    def _():
        outb_ref[...] = x


def _combine(x1, rt, ys, pos1, pos2, final_g, final_norm, ta, tm):
    t, d = x1.shape
    nb = t // tm
    na = ta // tm
    row = lambda i: (i, 0)
    smem_blk = pl.BlockSpec((1, 1, tm), lambda i: (i, 0, 0), memory_space=pltpu.SMEM)
    smem_next = pl.BlockSpec((1, 1, tm), lambda i: (jnp.minimum(i + 1, nb - 1), 0, 0), memory_space=pltpu.SMEM)
    p1, p2 = pos1.reshape(nb, 1, tm), pos2.reshape(nb, 1, tm)
    return pl.pallas_call(
        functools.partial(_combine_kernel, tm, nb, na, final_norm),
        grid=(nb,),
        in_specs=[smem_blk, smem_blk, smem_next, smem_next, pl.BlockSpec((tm, d), row),
                  pl.BlockSpec((tm, LANES), row), pl.BlockSpec((1, d), lambda i: (0, 0)),
                  pl.BlockSpec(memory_space=pl.ANY)],
        out_specs=[pl.BlockSpec((tm, d), lambda i: (jnp.minimum(i, na - 1), 0)),
                   pl.BlockSpec((tm, d), lambda i: (jnp.maximum(i - na, 0), 0))],
        out_shape=[jax.ShapeDtypeStruct((ta, d), F32), jax.ShapeDtypeStruct((t - ta, d), F32)],
        scratch_shapes=[pltpu.VMEM((2,) + _row_tile_shape(tm), F32), pltpu.VMEM((2,) + _row_tile_shape(tm), F32),
                        pltpu.SemaphoreType.DMA((2,)), pltpu.SemaphoreType.DMA((2,))],
        compiler_params=pltpu.CompilerParams(dimension_semantics=("arbitrary",), vmem_limit_bytes=VMEM_LIMIT),
        name="moe_combine",
    )(p1, p2, p1, p2, x1, rt, final_g, ys.reshape(-1, SUBLANES, LANES))


def _block_diag(w):
    h, a, b = w.shape
    return jnp.einsum("hij,hk->hikj", w, jnp.eye(h, dtype=w.dtype)).reshape(h * a, h * b)


def _layer_params(l, p):
    g = GROUP_W
    w_in = p["w_in"][l]
    d = w_in.shape[0]
    w_perm = jnp.concatenate([w_in[:, :8 * g], w_in[:, 8 * g + 8:], w_in[:, 8 * g:8 * g + 8],
                              jnp.zeros((d, LANES - 8), w_in.dtype)], axis=1)
    rows = [p["ret_gn_g"][l], jnp.tile(p["gdn_norm_g"][l], N_HEADS), p["lru_conv_b"][l], p["lru_ba"][l],
            p["lru_bx"][l], p["lru_lambda"][l], p["pool_b"][l], p["pool_scale"][l]]
    pvec = jnp.concatenate([jnp.stack(rows), p["lru_conv_w"][l], jnp.zeros((4, g), F32)], axis=0)
    lane_pad = lambda v, n: jnp.concatenate([v, jnp.zeros(v.shape[:-1] + (LANES - n,), F32)], axis=-1)
    gabp = jnp.concatenate([lane_pad(jnp.stack([p["gdn_a_log"][l], p["gdn_dt_bias"][l]]), N_HEADS),
                            jnp.zeros((6, LANES), F32)], axis=0)
    n_route = N_GROUPS + N_EXPERTS
    wr = lane_pad(jnp.concatenate([p["router_group_w"][l],
                                   jnp.transpose(p["router_expert_w"][l], (1, 0, 2)).reshape(d, N_EXPERTS)],
                                  axis=1), n_route)
    br = lane_pad(jnp.concatenate([p["router_group_b"][l], p["router_expert_b"][l].reshape(N_EXPERTS)])[None, :],
                  n_route)
    return dict(
        norm1_g=p["norm1_g"][l][None, :], w_in=w_perm.astype(BF16), pvec=pvec, gdn_conv_w=p["gdn_conv_w"][l],
        gabp=gabp, lru_wa=_block_diag(p["lru_wa"][l]).astype(BF16), lru_wx=_block_diag(p["lru_wx"][l]).astype(BF16),
        pool_w=_block_diag(p["pool_w"][l]).astype(BF16), w_out=p["w_out"][l].astype(BF16),
        norm2_g=p["norm2_g"][l][None, :], wr=wr, br=br)


def kernel(x_prompt, x_sample, state_ret, state_gdn, state_gdn_conv, state_lru, state_lru_conv, state_pool,
           norm1_g, w_in, ret_gn_g, gdn_conv_w, gdn_a_log, gdn_dt_bias, gdn_norm_g,
           lru_conv_w, lru_conv_b, lru_wa, lru_ba, lru_wx, lru_bx, lru_lambda,
           pool_w, pool_b, pool_scale, w_out, norm2_g,
           router_group_w, router_group_b, router_expert_w, router_expert_b,
           moe_w_gate, moe_w_up, moe_w_down, final_g):
    params = dict(norm1_g=norm1_g, w_in=w_in, ret_gn_g=ret_gn_g, gdn_conv_w=gdn_conv_w, gdn_a_log=gdn_a_log,
                  gdn_dt_bias=gdn_dt_bias, gdn_norm_g=gdn_norm_g, lru_conv_w=lru_conv_w, lru_conv_b=lru_conv_b,
                  lru_wa=lru_wa, lru_ba=lru_ba, lru_wx=lru_wx, lru_bx=lru_bx, lru_lambda=lru_lambda,
                  pool_w=pool_w, pool_b=pool_b, pool_scale=pool_scale, w_out=w_out, norm2_g=norm2_g,
                  router_group_w=router_group_w, router_group_b=router_group_b,
                  router_expert_w=router_expert_w, router_expert_b=router_expert_b)
    bp, lp_, d = x_prompt.shape
    bs, ls, _ = x_sample.shape
    depth = w_in.shape[0]
    tp, ts_ = bp * lp_, bs * ls
    t = tp + ts_
    tile_p = min(256, lp_)
    tm = 256
    n_tiles = (2 * t + N_EXPERTS * (tm - 1) + tm - 1) // tm

    x = (x_prompt.reshape(tp, d), x_sample.reshape(ts_, d))
    sample_states = (state_ret, state_gdn, state_gdn_conv, state_lru, state_lru_conv, state_pool)
    zero_states = tuple(jnp.zeros((1, bp) + s.shape[2:], F32) for s in sample_states)
    new_p = new_s = None
    for l in range(depth):
        lp = _layer_params(l, params)
        z = _norm_win(x, lp["norm1_g"], lp["w_in"], tm)
        mix_p, new_p = _mixer(z, 0, bp, lp_, tile_p, 1, 0, zero_states, 0, new_p, lp)
        mix_s, new_s = _mixer(z, tp, bs, ls, ls, SAMPLE_SEQS_PER_STEP, PAST_LEN, sample_states, l, new_s, lp)
        x1, h2, rt = _wout_route((mix_p, mix_s), x, lp["w_out"], lp["norm2_g"], lp["wr"], lp["br"], tm)
        pos1, pos2, tile_expert, n_used, has_rows, last_tile_row = _route_plan(rt, tm, n_tiles)
        xs = _scatter_rows(h2, pos1, pos2, has_rows, last_tile_row, n_used, n_tiles, tm)
        ys = _expert_ffn(xs, tile_expert, n_used, moe_w_gate, moe_w_up, moe_w_down, l, tm)
        x = _combine(x1, rt, ys, pos1, pos2, final_g[None, :], l == depth - 1, tp, tm)
    y_prompt = x[0].reshape(bp, lp_, d)
    y_sample = x[1].reshape(bs, ls, d)
    unpack = lambda st: (st[0], st[1], st[2], st[3][:, :, 0, :], st[4], st[5])
    return (y_prompt, y_sample, *unpack(new_p), *unpack(new_s))
```

```python
import functools

import jax
import jax.numpy as jnp
from jax import lax
from jax.experimental import pallas as pl
from jax.experimental.pallas import tpu as pltpu

F32 = jnp.float32
BF16 = jnp.bfloat16

EPS = 1e-6
HEAD_DIM = 64
N_HEADS = 4
GROUP_W = 256
CONV_W = 4
LRU_C = 8.0
POOL_WINDOWS = (2, 4, 8, 16)
POOL_BUF = 15
ROPE_BASE = 10000.0
N_GROUPS = 4
EXPERTS_PER_GROUP = 8
N_EXPERTS = 32
PAST_LEN = 16384

LANES = 128
SUBLANES = 8
HIST = 16
Z_AB = 11 * GROUP_W
NZ = Z_AB + LANES
INV_BASE = 8
RESID_PASSES = 3
SAMPLE_SEQS_PER_STEP = 16
VMEM_LIMIT = 56 * 1024 * 1024

NN = (((1,), (0,)), ((), ()))
NT = (((1,), (1,)), ((), ()))
TN = (((0,), (0,)), ((), ()))


def _mm(a, b, dims=NN, passes=1):
    ah = a.astype(BF16)
    bh = b.astype(BF16)
    dot = functools.partial(lax.dot_general, dimension_numbers=dims, preferred_element_type=F32)
    if passes == 1:
        return dot(ah, bh)
    al = (a - ah.astype(F32)).astype(BF16)
    bl = (b - bh.astype(F32)).astype(BF16)
    return dot(ah, bh) + (dot(ah, bl) + dot(al, bh))


def _sigmoid(x):
    return 1.0 / (1.0 + jnp.exp(-x))


def _silu(x):
    return x * _sigmoid(x)


def _softplus(x):
    return jnp.maximum(x, 0.0) + jnp.log1p(jnp.exp(-jnp.abs(x)))


def _gelu_tanh(x):
    return 0.5 * x * (1.0 + jnp.tanh(0.7978845608028654 * (x + 0.044715 * (x * x * x))))


def _iota(shape, axis):
    return lax.broadcasted_iota(jnp.int32, shape, axis)


def _shift_rows(x, d, fill, seg=None):
    t = _iota(x.shape, 0)
    if seg is not None:
        t = t & (seg - 1)
    return jnp.where(t >= d, pltpu.roll(x, d, 0), fill)


def _cumsum_rows(x, seg=None):
    d = 1
    while d < (seg or x.shape[0]):
        x = x + _shift_rows(x, d, 0.0, seg)
        d *= 2
    return x


def _swap_half_heads(x):
    n = x.shape[1]
    lower = (_iota(x.shape, 1) & (HEAD_DIM - 1)) < HEAD_DIM // 2
    return jnp.where(lower, pltpu.roll(x, n - HEAD_DIM // 2, 1), pltpu.roll(x, HEAD_DIM // 2, 1))


def _inv_unit_lower(mats, c=None):
    shape = mats[0].shape
    c = c or shape[0]
    row = _iota(shape, 0)
    col = _iota(shape, 1)
    nb = min(INV_BASE, c)
    shift = nb.bit_length() - 1
    in_base = (row >> shift) == (col >> shift)
    ps = [jnp.where(in_base, a, 0.0) for a in mats]
    eye = jnp.where(row == col, 1.0, 0.0)
    xs = [eye - d for d in ps]
    k = 2
    while k < nb:
        ps = [_mm(p, p) for p in ps]
        xs = [x + _mm(x, p) for x, p in zip(xs, ps)]
        k *= 2
    m = nb
    while m < c:
        s1 = m.bit_length() - 1
        in_off = ((row >> (s1 + 1)) == (col >> (s1 + 1))) & ((row >> s1) != (col >> s1))
        ys = [_mm(x, jnp.where(in_off, a, 0.0)) for x, a in zip(xs, mats)]
        xs = [x - _mm(y, x) for x, y in zip(xs, ys)]
        m *= 2
    return xs


def _solve_unit_lower(mats, tinvs, ws):
    us = [_mm(t, w) for t, w in zip(tinvs, ws)]
    rs = [w - (u + _mm(a, u, passes=RESID_PASSES)) for a, u, w in zip(mats, us, ws)]
    return [u + _mm(t, r) for u, t, r in zip(us, tinvs, rs)]


def _pair_specs(na, tm, d):
    return [pl.BlockSpec((tm, d), lambda i: (jnp.minimum(i, na - 1), 0)),
            pl.BlockSpec((tm, d), lambda i: (jnp.maximum(i - na, 0), 0))]


def _pair_tile(na, a_ref, b_ref):
    return jnp.where(pl.program_id(0) < na, a_ref[...], b_ref[...])


def _norm_win_kernel(na, xa_ref, xb_ref, g_ref, w_ref, z_ref):
    x = _pair_tile(na, xa_ref, xb_ref)
    h = x * lax.rsqrt(jnp.mean(x * x, axis=-1, keepdims=True) + EPS) * g_ref[...]
    z_ref[...] = jnp.dot(h.astype(BF16), w_ref[...], preferred_element_type=F32)


def _norm_win(x_pair, g, w_bf16, tm):
    xa, xb = x_pair
    d = xa.shape[1]
    na, t = xa.shape[0] // tm, xa.shape[0] + xb.shape[0]
    nz = w_bf16.shape[1]
    return pl.pallas_call(
        functools.partial(_norm_win_kernel, na),
        grid=(t // tm,),
        in_specs=_pair_specs(na, tm, d) + [pl.BlockSpec((1, d), lambda i: (0, 0)),
                                           pl.BlockSpec((d, nz), lambda i: (0, 0))],
        out_specs=pl.BlockSpec((tm, nz), lambda i: (i, 0)),
        out_shape=jax.ShapeDtypeStruct((t, nz), F32),
        compiler_params=pltpu.CompilerParams(dimension_semantics=("arbitrary",), vmem_limit_bytes=VMEM_LIMIT),
        name="norm_win",
    )(xa, xb, g, w_bf16)


def _retention_tile(zq, zk, zv, cos, sin, intra_ref, qdec, kdec, sdec, s_ref):
    rq = zq * cos + _swap_half_heads(zq) * sin
    rk = (zk * cos + _swap_half_heads(zk) * sin) * (HEAD_DIM ** -0.5)
    kd = rk * kdec
    heads = range(N_HEADS)
    sls = [slice(h * HEAD_DIM, (h + 1) * HEAD_DIM) for h in heads]
    ss = [s_ref[h] for h in heads]
    scores = [_mm(rq[:, sl], rk[:, sl], NT) * intra_ref[h] for h, sl in zip(heads, sls)]
    os_ = [_mm(sc, zv[:, sl]) + _mm(rq[:, sl], s) * qdec[:, sl] for sc, s, sl in zip(scores, ss, sls)]
    s_new = [s * sdec[h:h + 1, :HEAD_DIM] + _mm(kd[:, sl], zv[:, sl], TN) for h, s, sl in zip(heads, ss, sls)]
    for h in heads:
        s_ref[h] = s_new[h]
    outs = []
    for o in os_:
        mu = jnp.mean(o, axis=-1, keepdims=True)
        oc = o - mu
        var = jnp.mean(oc * oc, axis=-1, keepdims=True)
        outs.append(oc * lax.rsqrt(var + EPS))
    return jnp.concatenate(outs, axis=-1)


def _split3(x):
    x0 = x.astype(BF16)
    r1 = x - x0.astype(F32)
    x1 = r1.astype(BF16)
    x2 = (r1 - x1.astype(F32)).astype(BF16)
    return x0, x1, x2


def _gdn_tile(gq, gk, gv, gab, s_ref):
    c = gq.shape[0]
    gc = _cumsum_rows(gab)
    egc = jnp.exp(gc)
    row = _iota((c, c), 0)
    col = _iota((c, c), 1)
    incl = row >= col
    strict = row > col
    heads = range(N_HEADS)
    sls = [slice(h * HEAD_DIM, (h + 1) * HEAD_DIM) for h in heads]
    ss = [s_ref[h] for h in heads]
    gcols = [gc[:, h:h + 1] for h in heads]
    bcols = [gab[:, N_HEADS + h:N_HEADS + h + 1] for h in heads]
    ecols = [egc[:, h:h + 1] for h in heads]
    gct = gc.T
    decs = [jnp.where(incl, jnp.exp(jnp.where(incl, gcols[h] - gct[h:h + 1, :], 0.0)), 0.0) for h in heads]
    mats = [jnp.where(strict, bcols[h] * decs[h] * _mm(gk[:, sl], gk[:, sl], NT), 0.0) for h, sl in zip(heads, sls)]
    tinvs = _inv_unit_lower(mats)
    ws = [bcols[h] * (gv[:, sl] - ecols[h] * _mm(gk[:, sl], ss[h])) for h, sl in zip(heads, sls)]
    us = _solve_unit_lower(mats, tinvs, ws)
    qks = [_mm(gq[:, sl], gk[:, sl], NT) * decs[h] for h, sl in zip(heads, sls)]
    outs = [ecols[h] * _mm(gq[:, sl], ss[h]) + _mm(qks[h], us[h]) for h, sl in zip(heads, sls)]
    glasts = [gc[c - 1:c, h:h + 1] for h in heads]
    s_new = [jnp.exp(glasts[h]) * ss[h] + _mm(jnp.exp(glasts[h] - gcols[h]) * gk[:, sl], us[h], TN)
             for h, sl in zip(heads, sls)]
    for h in heads:
        s_ref[h] = s_new[h]
    return outs


def _mixer_kernel(nt, ts, nb, pos_base,
                  z_ref, cos_ref, sin_ref, intra_ref, qdec_ref, kdec_ref, sdec_ref, pv_ref, gcw_ref, gabp_ref,
                  wa_ref, wx_ref, pw_ref, sret0, sgdn0, gconv0, lru0, lconv0, pool0,
                  mix_ref, ret_o, gdn_o, gconv_o, lru_o, lconv_o, pool_o,
                  s_ret, s_gdn, ext_g, ext_l, ext_p, h_lru):
    n = pl.program_id(1)
    g = GROUP_W
    base = HIST - (CONV_W - 1)

    @pl.when(n == 0)
    def _():
        s_ret[...] = sret0[...]
        s_gdn[...] = sgdn0[...]
        ext_g[:, 0:HIST, :] = jnp.zeros((nb, HIST, 3 * g), F32)
        ext_g[:, base:HIST, :] = gconv0[...]
        ext_l[:, 0:HIST, :] = jnp.zeros((nb, HIST, g), F32)
        ext_l[:, base:HIST, :] = lconv0[...]
        ext_p[:, 0:HIST, :] = jnp.zeros((nb, HIST, g), F32)
        ext_p[:, HIST - POOL_BUF:HIST, :] = pool0[...]
        h_lru[...] = lru0[...]

    pv = pv_ref[...]
    ret_gn_g, gdn_norm_g = pv[0:1], pv[1:2]
    lru_conv_b, lru_ba, lru_bx, lru_lambda = pv[2:3], pv[3:4], pv[4:5], pv[5:6]
    pool_b, pool_scale = pv[6:7], pv[7:8]
    lru_conv_w = pv[8:12]
    gcw = gcw_ref[...]
    gabp = gabp_ref[...]

    for j in range(nb):
        rows = slice(j * ts, (j + 1) * ts)
        eg, el, ep = ext_g.at[j], ext_l.at[j], ext_p.at[j]

        ro = _retention_tile(z_ref[rows, 0:g], z_ref[rows, g:2 * g], z_ref[rows, 2 * g:3 * g], cos_ref[...],
                             sin_ref[...], intra_ref, qdec_ref[...], kdec_ref[...], sdec_ref[...], s_ret.at[j])
        mix_ref[rows, 0:g] = ro * ret_gn_g * _silu(z_ref[rows, 3 * g:4 * g])

        eg[HIST:HIST + ts, :] = z_ref[rows, 4 * g:7 * g]
        y = eg[base:base + ts, :] * gcw[0:1]
        for i in range(1, CONV_W):
            y = y + eg[base + i:base + i + ts, :] * gcw[i:i + 1]
        qkv = _silu(y)
        zab = z_ref[rows, Z_AB:Z_AB + LANES]
        lane = _iota(zab.shape, 1)
        log_a = -jnp.exp(gabp[0:1]) * _softplus(zab + gabp[1:2])
        gab = jnp.where(lane < N_HEADS, log_a, jnp.where(lane < 2 * N_HEADS, _sigmoid(zab), 0.0))
        gq_parts, gk_parts = [], []
        for h in range(N_HEADS):
            qh = qkv[:, h * HEAD_DIM:(h + 1) * HEAD_DIM]
            kh = qkv[:, g + h * HEAD_DIM:g + (h + 1) * HEAD_DIM]
            gq_parts.append(qh * lax.rsqrt(jnp.sum(qh * qh, axis=-1, keepdims=True) + EPS) * (HEAD_DIM ** -0.5))
            gk_parts.append(kh * lax.rsqrt(jnp.sum(kh * kh, axis=-1, keepdims=True) + EPS))
        go = _gdn_tile(jnp.concatenate(gq_parts, axis=-1), jnp.concatenate(gk_parts, axis=-1),
                       qkv[:, 2 * g:3 * g], gab, s_gdn.at[j])
        go = [o * lax.rsqrt(jnp.mean(o * o, axis=-1, keepdims=True) + EPS) for o in go]
        mix_ref[rows, g:2 * g] = jnp.concatenate(go, axis=-1) * gdn_norm_g * _silu(z_ref[rows, 7 * g:8 * g])

        el[HIST:HIST + ts, :] = z_ref[rows, 8 * g:9 * g]
        xc = el[base:base + ts, :] * lru_conv_w[0:1]
        for i in range(1, CONV_W):
            xc = xc + el[base + i:base + i + ts, :] * lru_conv_w[i:i + 1]
        xc = xc + lru_conv_b
        r = _sigmoid(jnp.dot(xc.astype(BF16), wa_ref[...], preferred_element_type=F32) + lru_ba)
        gate_i = _sigmoid(jnp.dot(xc.astype(BF16), wx_ref[...], preferred_element_type=F32) + lru_bx)
        log_at = -LRU_C * r * _softplus(-lru_lambda)
        av = jnp.exp(log_at)
        bv = jnp.sqrt(-jnp.tanh(log_at) * (av * av + 1.0)) * (gate_i * xc)
        bv = bv + jnp.where(_iota(bv.shape, 0) == 0, av * h_lru[j], 0.0)
        d = 1
        while d < ts:
            bv = av * _shift_rows(bv, d, 0.0) + bv
            av = av * _shift_rows(av, d, 1.0)
            d *= 2
        h_lru[j] = bv[ts - 1:ts, :]
        mix_ref[rows, 2 * g:3 * g] = bv * _gelu_tanh(z_ref[rows, 9 * g:10 * g])

        u = z_ref[rows, 10 * g:11 * g]
        ep[HIST:HIST + ts, :] = u
        e = ep[...]
        sums = []
        step = 1
        for _ in POOL_WINDOWS:
            e = e + pltpu.roll(e, step, 0)
            sums.append(e[HIST:HIST + ts, :])
            step *= 2
        pos1 = (pos_base + 1 + n * ts + _iota((ts, g), 0)).astype(F32)
        lane_g = _iota((ts, g), 1) >> 6
        pooled = sums[-1] / jnp.minimum(pos1, float(POOL_WINDOWS[-1]))
        for gi in range(len(POOL_WINDOWS) - 2, -1, -1):
            pooled = jnp.where(lane_g == gi, sums[gi] / jnp.minimum(pos1, float(POOL_WINDOWS[gi])), pooled)
        pooled = pooled - u
        yp = jnp.dot(pooled.astype(BF16), pw_ref[...], preferred_element_type=F32) + pool_b
        mix_ref[rows, 3 * g:4 * g] = yp * pool_scale

        tail_g = eg[ts:ts + HIST, :]
        tail_l = el[ts:ts + HIST, :]
        tail_p = ep[ts:ts + HIST, :]
        eg[0:HIST, :] = tail_g
        el[0:HIST, :] = tail_l
        ep[0:HIST, :] = tail_p

    @pl.when(n == nt - 1)
    def _():
        ret_o[...] = s_ret[...]
        gdn_o[...] = s_gdn[...]
        gconv_o[...] = ext_g[:, base:HIST, :]
        lconv_o[...] = ext_l[:, base:HIST, :]
        pool_o[...] = ext_p[:, HIST - POOL_BUF:HIST, :]
        lru_o[...] = h_lru[...]


def _decode_mixer_kernel(ts, nb, pos_base,
                         z_ref, cos_ref, sin_ref, intra_ref, qdec_ref, kdec_ref, sdec_ref, pv_ref, gcw_ref, gabp_ref,
                         wa_ref, wx_ref, pw_ref, sret0, sgdn0, gconv0, lru0, lconv0, pool0,
                         mix_ref, ret_o, gdn_o, gconv_o, lru_o, lconv_o, pool_o,
                         ext_g, ext_l, ext_p):
    g = GROUP_W
    r_all = nb * ts
    base = HIST - (CONV_W - 1)
    seg_shift = ts.bit_length() - 1
    row2 = _iota((r_all, r_all), 0)
    col2 = _iota((r_all, r_all), 1)
    same = (row2 >> seg_shift) == (col2 >> seg_shift)
    incl = same & (row2 >= col2)
    strict = same & (row2 > col2)
    seq_l = _iota((r_all, LANES), 0) >> seg_shift
    half_l = _iota((r_all, LANES), 1) >> 6
    exp_masks = [((seq_l >> 1) == c) & ((seq_l & 1) == half_l) for c in range(nb // 2)]

    def expand(x):
        x2 = jnp.concatenate([x, x], axis=-1)
        return jnp.concatenate([jnp.where(m, x2, 0.0) for m in exp_masks], axis=-1)

    def stacked(state_ref, h):
        return jnp.concatenate([state_ref[j, h] for j in range(nb)], axis=0)

    def unstack(out_ref, h, sv):
        for j in range(nb):
            out_ref[j, h] = sv[j * HEAD_DIM:(j + 1) * HEAD_DIM, :]

    pv = pv_ref[...]
    ret_gn_g, gdn_norm_g = pv[0:1], pv[1:2]
    lru_conv_b, lru_ba, lru_bx, lru_lambda = pv[2:3], pv[3:4], pv[4:5], pv[5:6]
    pool_b, pool_scale = pv[6:7], pv[7:8]
    lru_conv_w = pv[8:12]
    gcw = gcw_ref[...]
    gabp = gabp_ref[...]

    ext_g[:, 0:HIST, :] = jnp.zeros((nb, HIST, 3 * g), F32)
    ext_g[:, base:HIST, :] = gconv0[...]
    ext_l[:, 0:HIST, :] = jnp.zeros((nb, HIST, g), F32)
    ext_l[:, base:HIST, :] = lconv0[...]
    ext_p[:, 0:HIST, :] = jnp.zeros((nb, HIST, g), F32)
    ext_p[:, HIST - POOL_BUF:HIST, :] = pool0[...]
    y_parts, xc_parts, sum_parts, h0_parts = [], [], [[] for _ in POOL_WINDOWS], []
    for j in range(nb):
        rows = slice(j * ts, (j + 1) * ts)
        eg, el, ep = ext_g.at[j], ext_l.at[j], ext_p.at[j]
        eg[HIST:HIST + ts, :] = z_ref[rows, 4 * g:7 * g]
        el[HIST:HIST + ts, :] = z_ref[rows, 8 * g:9 * g]
        ep[HIST:HIST + ts, :] = z_ref[rows, 10 * g:11 * g]
        y = eg[base:base + ts, :] * gcw[0:1]
        xc = el[base:base + ts, :] * lru_conv_w[0:1]
        for i in range(1, CONV_W):
            y = y + eg[base + i:base + i + ts, :] * gcw[i:i + 1]
            xc = xc + el[base + i:base + i + ts, :] * lru_conv_w[i:i + 1]
        y_parts.append(y)
        xc_parts.append(xc)
        e = ep[...]
        step = 1
        for wi in range(len(POOL_WINDOWS)):
            e = e + pltpu.roll(e, step, 0)
            sum_parts[wi].append(e[HIST:HIST + ts, :])
            step *= 2
        h0_parts.append(jnp.broadcast_to(lru0[j], (ts, g)))
        gconv_o[j] = eg[ts + base:ts + HIST, :]
        lconv_o[j] = el[ts + base:ts + HIST, :]
        pool_o[j] = ep[ts + HIST - POOL_BUF:ts + HIST, :]

    zq, zk, zv = z_ref[:, 0:g], z_ref[:, g:2 * g], z_ref[:, 2 * g:3 * g]
    cos, sin = cos_ref[...], sin_ref[...]
    qdec, kdec, sdec = qdec_ref[...], kdec_ref[...], sdec_ref[...]
    rq = zq * cos + _swap_half_heads(zq) * sin
    rk = (zk * cos + _swap_half_heads(zk) * sin) * (HEAD_DIM ** -0.5)
    kd = rk * kdec
    outs = []
    for h in range(N_HEADS):
        sl = slice(h * HEAD_DIM, (h + 1) * HEAD_DIM)
        q, k, v = rq[:, sl], rk[:, sl], zv[:, sl]
        sv = stacked(sret0, h)
        scores = _mm(q, k, NT) * intra_ref[h]
        o = _mm(scores, v) + _mm(expand(q), sv) * qdec[:, sl]
        unstack(ret_o, h, sv * sdec[h:h + 1, :HEAD_DIM] + _mm(expand(kd[:, sl]), v, TN))
        mu = jnp.mean(o, axis=-1, keepdims=True)
        oc = o - mu
        var = jnp.mean(oc * oc, axis=-1, keepdims=True)
        outs.append(oc * lax.rsqrt(var + EPS))
    mix_ref[:, 0:g] = jnp.concatenate(outs, axis=-1) * ret_gn_g * _silu(z_ref[:, 3 * g:4 * g])

    qkv = _silu(jnp.concatenate(y_parts, axis=0))
    zab = z_ref[:, Z_AB:Z_AB + LANES]
    lane = _iota(zab.shape, 1)
    log_a = -jnp.exp(gabp[0:1]) * _softplus(zab + gabp[1:2])
    gab = jnp.where(lane < N_HEADS, log_a, jnp.where(lane < 2 * N_HEADS, _sigmoid(zab), 0.0))
    gc = _cumsum_rows(gab, seg=ts)
    egc = jnp.exp(gc)
    gct = gc.T
    dotnn = functools.partial(lax.dot_general, dimension_numbers=NN, preferred_element_type=F32)
    same_b = jnp.where(same, 1.0, 0.0).astype(BF16)
    a0, a1, a2 = _split3(gab)
    glast = dotnn(same_b, a0) + (dotnn(same_b, a1) + dotnn(same_b, a2))
    pick = jnp.where(((_iota((nb * HEAD_DIM, r_all), 0) >> 6) == (_iota((nb * HEAD_DIM, r_all), 1) >> seg_shift))
                     & ((_iota((nb * HEAD_DIM, r_all), 1) & (ts - 1)) == 0), 1.0, 0.0).astype(BF16)
    e0, e1, e2 = _split3(jnp.exp(glast))
    eglast_s = dotnn(pick, e0) + (dotnn(pick, e1) + dotnn(pick, e2))
    heads = range(N_HEADS)
    qs, ks, vs, decs = [], [], [], []
    for h in heads:
        qh = qkv[:, h * HEAD_DIM:(h + 1) * HEAD_DIM]
        kh = qkv[:, g + h * HEAD_DIM:g + (h + 1) * HEAD_DIM]
        qs.append(qh * lax.rsqrt(jnp.sum(qh * qh, axis=-1, keepdims=True) + EPS) * (HEAD_DIM ** -0.5))
        ks.append(kh * lax.rsqrt(jnp.sum(kh * kh, axis=-1, keepdims=True) + EPS))
        vs.append(qkv[:, 2 * g + h * HEAD_DIM:2 * g + (h + 1) * HEAD_DIM])
        decs.append(jnp.where(incl, jnp.exp(jnp.where(incl, gc[:, h:h + 1] - gct[h:h + 1, :], 0.0)), 0.0))
    gcols = [gc[:, h:h + 1] for h in heads]
    bcols = [gab[:, N_HEADS + h:N_HEADS + h + 1] for h in heads]
    ecols = [egc[:, h:h + 1] for h in heads]
    svs = [stacked(sgdn0, h) for h in heads]
    mats = [jnp.where(strict, bcols[h] * decs[h] * _mm(ks[h], ks[h], NT), 0.0) for h in heads]
    tinvs = _inv_unit_lower(mats, ts)
    ws = [bcols[h] * (vs[h] - ecols[h] * _mm(expand(ks[h]), svs[h])) for h in heads]
    us = _solve_unit_lower(mats, tinvs, ws)
    os_ = [ecols[h] * _mm(expand(qs[h]), svs[h]) + _mm(_mm(qs[h], ks[h], NT) * decs[h], us[h]) for h in heads]
    for h in heads:
        kdx = jnp.exp(glast[:, h:h + 1] - gcols[h]) * ks[h]
        unstack(gdn_o, h, eglast_s[:, h:h + 1] * svs[h] + _mm(expand(kdx), us[h], TN))
    outs = [o * lax.rsqrt(jnp.mean(o * o, axis=-1, keepdims=True) + EPS) for o in os_]
    mix_ref[:, g:2 * g] = jnp.concatenate(outs, axis=-1) * gdn_norm_g * _silu(z_ref[:, 7 * g:8 * g])

    xc = jnp.concatenate(xc_parts, axis=0) + lru_conv_b
    r = _sigmoid(jnp.dot(xc.astype(BF16), wa_ref[...], preferred_element_type=F32) + lru_ba)
    gate_i = _sigmoid(jnp.dot(xc.astype(BF16), wx_ref[...], preferred_element_type=F32) + lru_bx)
    log_at = -LRU_C * r * _softplus(-lru_lambda)
    av = jnp.exp(log_at)
    bv = jnp.sqrt(-jnp.tanh(log_at) * (av * av + 1.0)) * (gate_i * xc)
    first = (_iota(bv.shape, 0) & (ts - 1)) == 0
    bv = bv + jnp.where(first, av * jnp.concatenate(h0_parts, axis=0), 0.0)
    d = 1
    while d < ts:
        bv = av * _shift_rows(bv, d, 0.0, ts) + bv
        av = av * _shift_rows(av, d, 1.0, ts)
        d *= 2
    for j in range(nb):
        lru_o[j] = bv[(j + 1) * ts - 1:(j + 1) * ts, :]
    mix_ref[:, 2 * g:3 * g] = bv * _gelu_tanh(z_ref[:, 9 * g:10 * g])

    u_in = z_ref[:, 10 * g:11 * g]
    pos1 = (pos_base + 1 + (_iota((r_all, g), 0) & (ts - 1))).astype(F32)
    lane_g = _iota((r_all, g), 1) >> 6
    sums = [jnp.concatenate(parts, axis=0) for parts in sum_parts]
    pooled = sums[-1] / jnp.minimum(pos1, float(POOL_WINDOWS[-1]))
    for gi in range(len(POOL_WINDOWS) - 2, -1, -1):
        pooled = jnp.where(lane_g == gi, sums[gi] / jnp.minimum(pos1, float(POOL_WINDOWS[gi])), pooled)
    pooled = pooled - u_in
    yp = jnp.dot(pooled.astype(BF16), pw_ref[...], preferred_element_type=F32) + pool_b
    mix_ref[:, 3 * g:4 * g] = yp * pool_scale


N_STATES = 6


def _with_state_stack(body, nt, n_prev, n_in, *refs):
    ins = refs[:n_in]
    prevs = refs[n_in:n_in + (N_STATES if n_prev else 0)]
    k = n_in + len(prevs)
    mix_ref, stacks, scratch = refs[k], refs[k + 1:k + 1 + N_STATES], refs[k + 1 + N_STATES:]

    @pl.when(pl.program_id(1) == nt - 1)
    def _():
        for o, p in zip(stacks, prevs):
            o[0:n_prev] = p[...]

    body(*ins, mix_ref, *[o.at[n_prev] for o in stacks], *scratch)


def _ret_consts(c):
    log_g = jnp.log1p(-jnp.exp2(-5.0 - jnp.arange(N_HEADS, dtype=F32)))
    idx = jnp.arange(c, dtype=F32)
    diff = idx[:, None] - idx[None, :]
    intra = jnp.where(diff >= 0, jnp.exp(log_g[:, None, None] * jnp.maximum(diff, 0.0)), 0.0)
    qdec = jnp.exp(log_g[:, None] * (idx + 1.0))
    kdec = jnp.exp(log_g[:, None] * (c - 1.0 - idx))
    sdec = jnp.exp(log_g * c)
    expand = lambda t: jnp.repeat(t.T, HEAD_DIM, axis=1)
    sdec_tab = jnp.zeros((8, LANES), F32).at[:N_HEADS, :].set(sdec[:, None])
    return intra, expand(qdec), expand(kdec), sdec_tab


def _rope_tables(pos):
    half = HEAD_DIM // 2
    inv = ROPE_BASE ** (-jnp.arange(half, dtype=F32) / half)
    ang = pos.astype(F32)[:, None] * inv[None, :]
    cos, sin = jnp.cos(ang), jnp.sin(ang)
    cos_t = jnp.tile(jnp.concatenate([cos, cos], axis=-1), (1, N_HEADS))
    sin_t = jnp.tile(jnp.concatenate([-sin, sin], axis=-1), (1, N_HEADS))
    return cos_t, sin_t


def _mixer(z, z_row0, b, l, ts, nb, pos_base, states, layer, prev, lp):
    nt = l // ts
    assert nb == 1 or (nt == 1 and nb % 2 == 0 and ts == INV_BASE)
    g = GROUP_W
    pos = pos_base + jnp.arange(l, dtype=jnp.int32)
    cos_t, sin_t = _rope_tables(pos)
    intra, qdec, kdec, sdec = _ret_consts(ts)
    rt_ = nb * ts
    if nb > 1:
        cos_t, sin_t, qdec, kdec = (jnp.tile(a, (nb, 1)) for a in (cos_t, sin_t, qdec, kdec))
        intra = jnp.einsum("ab,hij->haibj", jnp.eye(nb, dtype=F32), intra).reshape(N_HEADS, rt_, rt_)
    s_ret0, s_gdn0, gconv0, lru0, lconv0, pool0 = states
    blk0 = z_row0 // rt_
    const2 = lambda bi, n: (0, 0)
    per_b3 = lambda bi, n: (bi, 0, 0)
    per_b4 = lambda bi, n: (bi, 0, 0, 0)
    st3 = lambda bi, n: (layer, bi, 0, 0)
    st4 = lambda bi, n: (layer, bi, 0, 0, 0)
    in_specs = [
        pl.BlockSpec((rt_, NZ), lambda bi, n: (blk0 + bi * nt + n, 0)),
        pl.BlockSpec((rt_, g), lambda bi, n: (n, 0)),
        pl.BlockSpec((rt_, g), lambda bi, n: (n, 0)),
        pl.BlockSpec((N_HEADS, rt_, rt_), lambda bi, n: (0, 0, 0)),
        pl.BlockSpec((rt_, g), const2),
        pl.BlockSpec((rt_, g), const2),
        pl.BlockSpec((8, LANES), const2),
        pl.BlockSpec((16, g), const2),
        pl.BlockSpec((CONV_W, 3 * g), const2),
        pl.BlockSpec((8, LANES), const2),
        pl.BlockSpec((g, g), const2),
        pl.BlockSpec((g, g), const2),
        pl.BlockSpec((g, g), const2),
        pl.BlockSpec((None, nb, N_HEADS, HEAD_DIM, HEAD_DIM), st4),
        pl.BlockSpec((None, nb, N_HEADS, HEAD_DIM, HEAD_DIM), st4),
        pl.BlockSpec((None, nb, CONV_W - 1, 3 * g), st3),
        pl.BlockSpec((None, nb, 1, g), st3),
        pl.BlockSpec((None, nb, CONV_W - 1, g), st3),
        pl.BlockSpec((None, nb, POOL_BUF, g), st3),
    ]
    st_shapes = [(N_HEADS, HEAD_DIM, HEAD_DIM), (N_HEADS, HEAD_DIM, HEAD_DIM), (CONV_W - 1, 3 * g), (1, g),
                 (CONV_W - 1, g), (POOL_BUF, g)]
    n_prev = 0 if prev is None else prev[0].shape[0]
    stack_spec = lambda layers, s: pl.BlockSpec((layers, nb) + s, lambda bi, n: (0, bi) + (0,) * len(s))
    n_in = len(in_specs)
    if n_prev:
        in_specs = in_specs + [stack_spec(n_prev, s) for s in st_shapes]
    out_specs = [pl.BlockSpec((rt_, 4 * g), lambda bi, n: (bi * nt + n, 0))] + [
        stack_spec(n_prev + 1, s) for s in st_shapes]
    out_shape = [jax.ShapeDtypeStruct((b * l, 4 * g), F32)] + [
        jax.ShapeDtypeStruct((n_prev + 1, b) + s, F32) for s in st_shapes]
    ext = [pltpu.VMEM((nb, ts + HIST, 3 * g), F32), pltpu.VMEM((nb, ts + HIST, g), F32),
           pltpu.VMEM((nb, ts + HIST, g), F32)]
    if nb > 1:
        body = functools.partial(_decode_mixer_kernel, ts, nb, pos_base)
        scratch = ext
    else:
        body = functools.partial(_mixer_kernel, nt, ts, nb, pos_base)
        scratch = [pltpu.VMEM((nb, N_HEADS, HEAD_DIM, HEAD_DIM), F32),
                   pltpu.VMEM((nb, N_HEADS, HEAD_DIM, HEAD_DIM), F32)] + ext + [pltpu.VMEM((nb, 1, g), F32)]
    outs = pl.pallas_call(
        functools.partial(_with_state_stack, body, nt, n_prev, n_in),
        grid=(b // nb, nt),
        in_specs=in_specs,
        out_specs=out_specs,
        out_shape=out_shape,
        scratch_shapes=scratch,
        compiler_params=pltpu.CompilerParams(dimension_semantics=("arbitrary", "arbitrary"),
                                             vmem_limit_bytes=VMEM_LIMIT),
        name=f"mixer_ts{ts}",
    )(z, cos_t, sin_t, intra, qdec, kdec, sdec, lp["pvec"], lp["gdn_conv_w"], lp["gabp"],
      lp["lru_wa"], lp["lru_wx"], lp["pool_w"], s_ret0, s_gdn0, gconv0, lru0[:, :, None, :], lconv0, pool0,
      *(prev or ()))
    return outs[0], tuple(outs[1:])


def _to_row_tiles(ref, x):
    m = x.shape[0]
    ref[...] = pltpu.einshape("(gk)(jl)->gkjl", x, g=m // SUBLANES, j=SUBLANES)


def _from_row_tiles(ref):
    return pltpu.einshape("gkjl->(gk)(jl)", ref[...])


def _row_tile_shape(m):
    return (m // SUBLANES, SUBLANES, SUBLANES, LANES)


def _wout_route_kernel(na, ma_ref, mb_ref, xa_ref, xb_ref, wo_ref, g_ref, wr_ref, br_ref, x1_ref, h2_ref, rt_ref):
    mix = _pair_tile(na, ma_ref, mb_ref)
    x1 = _pair_tile(na, xa_ref, xb_ref) + jnp.dot(mix.astype(BF16), wo_ref[...], preferred_element_type=F32)
    x1_ref[...] = x1
    h = x1 * lax.rsqrt(jnp.mean(x1 * x1, axis=-1, keepdims=True) + EPS) * g_ref[...]
    _to_row_tiles(h2_ref, h)
    logits = _mm(h, wr_ref[...], passes=3) + br_ref[...]
    lane = _iota(logits.shape, 1)
    lane_f = lane.astype(F32)
    neg = -1e30
    far = 1e9
    is_g = lane < N_GROUPS
    gl = jnp.where(is_g, logits, neg)
    gmax = jnp.max(gl, axis=-1, keepdims=True)
    gsel = jnp.min(jnp.where(gl == gmax, lane_f, far), axis=-1, keepdims=True)
    p_g = 1.0 / jnp.sum(jnp.where(is_g, jnp.exp(gl - gmax), 0.0), axis=-1, keepdims=True)
    e_group = ((lane - N_GROUPS) >> 3).astype(F32)
    in_group = jnp.where(lane >= N_GROUPS, e_group, -1.0) == gsel
    el = jnp.where(in_group, logits, neg)
    m1 = jnp.max(el, axis=-1, keepdims=True)
    i1 = jnp.min(jnp.where(el == m1, lane_f, far), axis=-1, keepdims=True)
    el2 = jnp.where(lane_f == i1, neg, el)
    m2 = jnp.max(el2, axis=-1, keepdims=True)
    i2 = jnp.min(jnp.where(el2 == m2, lane_f, far), axis=-1, keepdims=True)
    e2 = jnp.exp(m2 - m1)
    w1 = p_g / (1.0 + e2)
    w2 = p_g * e2 / (1.0 + e2)
    rt = jnp.where(lane == 0, i1 - N_GROUPS,
                   jnp.where(lane == 1, i2 - N_GROUPS, jnp.where(lane == 2, w1, jnp.where(lane == 3, w2, 0.0))))
    rt_ref[...] = rt


def _wout_route(mix_pair, x_pair, wo_bf16, g, wr, br, tm):
    d = x_pair[0].shape[1]
    na, t = x_pair[0].shape[0] // tm, x_pair[0].shape[0] + x_pair[1].shape[0]
    row = lambda i: (i, 0)
    const = lambda i: (0, 0)
    return pl.pallas_call(
        functools.partial(_wout_route_kernel, na),
        grid=(t // tm,),
        in_specs=_pair_specs(na, tm, d) + _pair_specs(na, tm, d) + [
            pl.BlockSpec((d, d), const), pl.BlockSpec((1, d), const), pl.BlockSpec((d, LANES), const),
            pl.BlockSpec((1, LANES), const)],
        out_specs=[pl.BlockSpec((tm, d), row), pl.BlockSpec(_row_tile_shape(tm), lambda i: (i, 0, 0, 0)),
                   pl.BlockSpec((tm, LANES), row)],
        out_shape=[jax.ShapeDtypeStruct((t, d), F32), jax.ShapeDtypeStruct(_row_tile_shape(t), F32),
                   jax.ShapeDtypeStruct((t, LANES), F32)],
        compiler_params=pltpu.CompilerParams(dimension_semantics=("arbitrary",), vmem_limit_bytes=VMEM_LIMIT),
        name="wout_route",
    )(*mix_pair, *x_pair, wo_bf16, g, wr, br)


def _tile_of_row(ref4, grp, k):
    return ref4.at[grp, pl.ds(k, 1)]


def _for_each_row(tm, fn):
    def body(grp, c):
        for k in range(SUBLANES):
            fn(grp, k, grp * SUBLANES + k)
        return c

    lax.fori_loop(0, tm // SUBLANES, body, 0)


def _route_plan_kernel(tr, tm, n_tiles, rt_ref, pos_ref, tmeta_ref, emeta_ref, cnt_s, start_s, carry_s):
    ph = pl.program_id(0)
    i = pl.program_id(1)
    rt = rt_ref[...]
    lane = _iota((tr, LANES), 1)
    lane_f = lane.astype(F32)
    oh1 = jnp.where(lane_f == rt[:, 0:1], 1.0, 0.0)
    oh2 = jnp.where(lane_f == rt[:, 1:2], 1.0, 0.0)
    oh = oh1 + oh2
    colsum = jnp.sum(oh, axis=0, keepdims=True)

    @pl.when((ph == 0) & (i == 0))
    def _():
        cnt_s[...] = jnp.zeros((1, LANES), F32)

    @pl.when(ph == 0)
    def _():
        cnt_s[...] = cnt_s[...] + colsum

    @pl.when((ph == 1) & (i == 0))
    def _():
        cnt = cnt_s[...]
        ntile = jnp.floor((cnt + (tm - 1.0)) * (1.0 / tm))
        before = jnp.where(_iota((LANES, LANES), 0) < _iota((LANES, LANES), 1), 1.0, 0.0)
        st_tiles = _mm(jnp.broadcast_to(ntile, (SUBLANES, LANES)), before)[0:1]
        end_tiles = st_tiles + ntile
        start_s[...] = st_tiles * tm
        carry_s[...] = jnp.zeros((1, LANES), F32)
        is_e1 = _iota((1, LANES), 1) < N_EXPERTS
        n_used = jnp.sum(jnp.where(is_e1, ntile, 0.0), axis=-1, keepdims=True)
        last = jnp.sum(jnp.where(is_e1 & (end_tiles <= n_used - 1.0), 1.0, 0.0), axis=-1, keepdims=True)
        ti = _iota((n_tiles, LANES), 0).astype(F32)
        lane_t = _iota((n_tiles, LANES), 1)
        te = jnp.sum(jnp.where((lane_t < N_EXPERTS) & (end_tiles <= ti), 1.0, 0.0), axis=-1, keepdims=True)
        te = jnp.minimum(te, last)
        tmeta_ref[...] = jnp.where(lane_t == 0, te, jnp.where(lane_t == 1, n_used, 0.0))
        srow = _iota((SUBLANES, LANES), 0)
        emeta_ref[...] = jnp.where(srow == 0, ntile, jnp.where(srow == 1, (end_tiles - 1.0) * tm, 0.0))

    @pl.when(ph == 1)
    def _():
        earlier = jnp.where(_iota((tr, tr), 0) > _iota((tr, tr), 1), 1.0, 0.0)
        base = start_s[...] + carry_s[...] + _mm(earlier, oh)
        p1 = jnp.sum(oh1 * base, axis=-1, keepdims=True)
        p2 = jnp.sum(oh2 * base, axis=-1, keepdims=True)
        pos_ref[...] = jnp.where(lane == 0, p1, jnp.where(lane == 1, p2, 0.0))
        carry_s[...] = carry_s[...] + colsum


def _route_plan(rt, tm, n_tiles):
    t = rt.shape[0]
    tr = next(c for c in (1024, 512, 256, tm) if t % c == 0)
    pos, tmeta, emeta = pl.pallas_call(
        functools.partial(_route_plan_kernel, tr, tm, n_tiles),
        grid=(2, t // tr),
        in_specs=[pl.BlockSpec((tr, LANES), lambda ph, i: (i, 0))],
        out_specs=[pl.BlockSpec((tr, LANES), lambda ph, i: (ph * i, 0)),
                   pl.BlockSpec((n_tiles, LANES), lambda ph, i: (0, 0)),
                   pl.BlockSpec((SUBLANES, LANES), lambda ph, i: (0, 0))],
        out_shape=[jax.ShapeDtypeStruct((t, LANES), F32), jax.ShapeDtypeStruct((n_tiles, LANES), F32),
                   jax.ShapeDtypeStruct((SUBLANES, LANES), F32)],
        scratch_shapes=[pltpu.VMEM((1, LANES), F32), pltpu.VMEM((1, LANES), F32), pltpu.VMEM((1, LANES), F32)],
        compiler_params=pltpu.CompilerParams(dimension_semantics=("arbitrary", "arbitrary"),
                                             vmem_limit_bytes=VMEM_LIMIT),
        name="route_plan",
    )(rt)
    pos1 = pos[:, 0].astype(jnp.int32)
    pos2 = pos[:, 1].astype(jnp.int32)
    tile_expert = tmeta[:, 0].astype(jnp.int32)
    n_used = tmeta[0:1, 1].astype(jnp.int32)
    has_rows = (emeta[0, :N_EXPERTS] > 0).astype(jnp.int32)
    last_tile_row = jnp.maximum(emeta[1, :N_EXPERTS], 0.0).astype(jnp.int32)
    return pos1, pos2, tile_expert, n_used, has_rows, last_tile_row


def _scatter_kernel(tm, n_tiles, min_tiles, zon_ref, zrow_ref, nu_ref, p1_ref, p2_ref, h_ref, xs_hbm,
                    zbuf, sem1, sem2, zsem, tsem):
    i = pl.program_id(0)

    @pl.when(i == 0)
    def _():
        zbuf[...] = jnp.zeros(zbuf.shape, F32)

        def zero_copy(e):
            return pltpu.make_async_copy(zbuf, xs_hbm.at[pl.ds(zrow_ref[e], tm)], zsem.at[e])

        def tail_copy(k):
            return pltpu.make_async_copy(zbuf, xs_hbm.at[pl.ds((min_tiles + k) * tm, tm)], tsem.at[k])

        def zstart(e, c):
            @pl.when(zon_ref[e] > 0)
            def _():
                zero_copy(e).start()
            return c

        def zwait(e, c):
            @pl.when(zon_ref[e] > 0)
            def _():
                zero_copy(e).wait()
            return c

        def tstart(k, c):
            @pl.when(min_tiles + k >= nu_ref[0])
            def _():
                tail_copy(k).start()
            return c

        def twait(k, c):
            @pl.when(min_tiles + k >= nu_ref[0])
            def _():
                tail_copy(k).wait()
            return c

        lax.fori_loop(0, N_EXPERTS, zstart, 0)
        lax.fori_loop(0, n_tiles - min_tiles, tstart, 0)
        lax.fori_loop(0, N_EXPERTS, zwait, 0)
        lax.fori_loop(0, n_tiles - min_tiles, twait, 0)

    def copies(grp, k, r):
        src = _tile_of_row(h_ref, grp, k)
        return (pltpu.make_async_copy(src, xs_hbm.at[pl.ds(p1_ref[0, 0, r], 1)], sem1),
                pltpu.make_async_copy(src, xs_hbm.at[pl.ds(p2_ref[0, 0, r], 1)], sem2))

    def start(grp, k, r):
        c1, c2 = copies(grp, k, r)
        c1.start(priority=0)
        c2.start(priority=1)

    def wait(grp, k, r):
        c1, c2 = copies(grp, k, r)
        c1.wait()
        c2.wait()

    _for_each_row(tm, start)
    _for_each_row(tm, wait)


def _scatter_rows(h2, pos1, pos2, has_rows, last_tile_row, n_used, n_tiles, tm):
    t = h2.shape[0] * SUBLANES
    nb = t // tm
    min_tiles = (2 * t) // tm
    smem_blk = pl.BlockSpec((1, 1, tm), lambda i, zon, zrow, nu: (i, 0, 0), memory_space=pltpu.SMEM)
    grid_spec = pltpu.PrefetchScalarGridSpec(
        num_scalar_prefetch=3,
        grid=(nb,),
        in_specs=[smem_blk, smem_blk, pl.BlockSpec(_row_tile_shape(tm), lambda i, zon, zrow, nu: (i, 0, 0, 0))],
        out_specs=pl.BlockSpec(memory_space=pl.ANY),
        scratch_shapes=[pltpu.VMEM((tm, SUBLANES, LANES), F32), pltpu.SemaphoreType.DMA,
                        pltpu.SemaphoreType.DMA, pltpu.SemaphoreType.DMA((N_EXPERTS,)),
                        pltpu.SemaphoreType.DMA((n_tiles - min_tiles,))],
    )
    return pl.pallas_call(
        functools.partial(_scatter_kernel, tm, n_tiles, min_tiles),
        grid_spec=grid_spec,
        out_shape=jax.ShapeDtypeStruct((n_tiles * tm, SUBLANES, LANES), F32),
        compiler_params=pltpu.CompilerParams(dimension_semantics=("arbitrary",), vmem_limit_bytes=VMEM_LIMIT),
        name="moe_scatter",
    )(has_rows, last_tile_row, n_used, pos1.reshape(nb, 1, tm), pos2.reshape(nb, 1, tm), h2)


def _ffn_kernel(te_ref, nu_ref, xs_ref, wg_ref, wu_ref, wd_ref, ys_ref, wg_s, wu_s, wd_s):
    i = pl.program_id(0)
    used = i < nu_ref[0]
    new_expert = (i == 0) | (te_ref[i] != te_ref[jnp.maximum(i - 1, 0)])

    @pl.when(used & new_expert)
    def _():
        wg_s[...] = wg_ref[...].astype(BF16)
        wu_s[...] = wu_ref[...].astype(BF16)
        wd_s[...] = wd_ref[...].astype(BF16)

    @pl.when(used)
    def _():
        x = _from_row_tiles(xs_ref).astype(BF16)
        gate = jnp.dot(x, wg_s[...], preferred_element_type=F32)
        up = jnp.dot(x, wu_s[...], preferred_element_type=F32)
        act = (_silu(gate) * up).astype(BF16)
        _to_row_tiles(ys_ref, jnp.dot(act, wd_s[...], preferred_element_type=F32))

    @pl.when(jnp.logical_not(used))
    def _():
        ys_ref[...] = jnp.zeros(ys_ref.shape, F32)


def _expert_ffn(xs, tile_expert, n_used, w_gate, w_up, w_down, layer, tm):
    p = xs.shape[0]
    xs = xs.reshape(_row_tile_shape(p))
    _, _, d, de = w_gate.shape
    row = lambda i, te, nu: (jnp.minimum(i, nu[0] - 1), 0, 0, 0)
    wsel = lambda i, te, nu: (layer, te[i], 0, 0)
    grid_spec = pltpu.PrefetchScalarGridSpec(
        num_scalar_prefetch=2,
        grid=(p // tm,),
        in_specs=[pl.BlockSpec(_row_tile_shape(tm), row),
                  pl.BlockSpec((None, None, d, de), wsel),
                  pl.BlockSpec((None, None, d, de), wsel),
                  pl.BlockSpec((None, None, de, d), wsel)],
        out_specs=pl.BlockSpec(_row_tile_shape(tm), lambda i, te, nu: (i, 0, 0, 0)),
        scratch_shapes=[pltpu.VMEM((d, de), BF16), pltpu.VMEM((d, de), BF16), pltpu.VMEM((de, d), BF16)],
    )
    return pl.pallas_call(
        _ffn_kernel,
        grid_spec=grid_spec,
        out_shape=jax.ShapeDtypeStruct(_row_tile_shape(p), F32),
        compiler_params=pltpu.CompilerParams(dimension_semantics=("arbitrary",), vmem_limit_bytes=VMEM_LIMIT),
        name="expert_ffn",
    )(tile_expert, n_used, xs, w_gate, w_up, w_down)


def _combine_kernel(tm, n_steps, na, last_layer, p1_ref, p2_ref, p1n_ref, p2n_ref, x1_ref, rt_ref, g_ref, *rest):
    if last_layer:
        ys_hbm, outa_ref, outb_ref, buf1, buf2, sem1, sem2 = rest
    else:
        w_ref, ys_hbm, outa_ref, outb_ref, z_ref, buf1, buf2, sem1, sem2 = rest
    i = pl.program_id(0)
    slot = i % 2

    def copies(pa_ref, pb_ref, s, grp, k, r):
        return (pltpu.make_async_copy(ys_hbm.at[pl.ds(pa_ref[0, 0, r], 1)], _tile_of_row(buf1.at[s], grp, k),
                                      sem1.at[s]),
                pltpu.make_async_copy(ys_hbm.at[pl.ds(pb_ref[0, 0, r], 1)], _tile_of_row(buf2.at[s], grp, k),
                                      sem2.at[s]))

    def start(pa_ref, pb_ref, s):
        def go(grp, k, r):
            c1, c2 = copies(pa_ref, pb_ref, s, grp, k, r)
            c1.start(priority=0)
            c2.start(priority=1)
        _for_each_row(tm, go)

    @pl.when(i == 0)
    def _():
        start(p1_ref, p2_ref, 0)

    @pl.when(i + 1 < n_steps)
    def _():
        start(p1n_ref, p2n_ref, 1 - slot)

    def wait(grp, k, r):
        c1, c2 = copies(p1_ref, p2_ref, slot, grp, k, r)
        c1.wait()
        c2.wait()

    _for_each_row(tm, wait)
    rt = rt_ref[...]
    x = x1_ref[...] + rt[:, 2:3] * _from_row_tiles(buf1.at[slot]) + rt[:, 3:4] * _from_row_tiles(buf2.at[slot])
    h = x * lax.rsqrt(jnp.mean(x * x, axis=-1, keepdims=True) + EPS) * g_ref[...]
    if last_layer:
        x = h
    else:
        z_ref[...] = jnp.dot(h.astype(BF16), w_ref[...], preferred_element_type=F32)

    @pl.when(pl.program_id(0) < na)
    def _():
        outa_ref[...] = x

    @pl.when(pl.program_id(0) >= na)
    def _():
        outb_ref[...] = x


def _combine(x1, rt, ys, pos1, pos2, gain, next_w_in, ta, tm):
    t, d = x1.shape
    nb = t // tm
    na = ta // tm
    last_layer = next_w_in is None
    row = lambda i: (i, 0)
    const = lambda i: (0, 0)
    smem_blk = pl.BlockSpec((1, 1, tm), lambda i: (i, 0, 0), memory_space=pltpu.SMEM)
    smem_next = pl.BlockSpec((1, 1, tm), lambda i: (jnp.minimum(i + 1, nb - 1), 0, 0), memory_space=pltpu.SMEM)
    p1, p2 = pos1.reshape(nb, 1, tm), pos2.reshape(nb, 1, tm)
    in_specs = [smem_blk, smem_blk, smem_next, smem_next, pl.BlockSpec((tm, d), row),
                pl.BlockSpec((tm, LANES), row), pl.BlockSpec((1, d), const)]
    operands = [p1, p2, p1, p2, x1, rt, gain]
    out_specs = [pl.BlockSpec((tm, d), lambda i: (jnp.minimum(i, na - 1), 0)),
                 pl.BlockSpec((tm, d), lambda i: (jnp.maximum(i - na, 0), 0))]
    out_shape = [jax.ShapeDtypeStruct((ta, d), F32), jax.ShapeDtypeStruct((t - ta, d), F32)]
    if not last_layer:
        nz = next_w_in.shape[1]
        in_specs.append(pl.BlockSpec((d, nz), const))
        operands.append(next_w_in)
        out_specs.append(pl.BlockSpec((tm, nz), row))
        out_shape.append(jax.ShapeDtypeStruct((t, nz), F32))
    outs = pl.pallas_call(
        functools.partial(_combine_kernel, tm, nb, na, last_layer),
        grid=(nb,),
        in_specs=in_specs + [pl.BlockSpec(memory_space=pl.ANY)],
        out_specs=out_specs,
        out_shape=out_shape,
        scratch_shapes=[pltpu.VMEM((2,) + _row_tile_shape(tm), F32), pltpu.VMEM((2,) + _row_tile_shape(tm), F32),
                        pltpu.SemaphoreType.DMA((2,)), pltpu.SemaphoreType.DMA((2,))],
        compiler_params=pltpu.CompilerParams(dimension_semantics=("arbitrary",), vmem_limit_bytes=VMEM_LIMIT),
        name="moe_combine",
    )(*operands, ys.reshape(-1, SUBLANES, LANES))
    return (outs[0], outs[1]), (None if last_layer else outs[2])


def _block_diag(w):
    h, a, b = w.shape
    return jnp.einsum("hij,hk->hikj", w, jnp.eye(h, dtype=w.dtype)).reshape(h * a, h * b)


def _layer_params(l, p):
    g = GROUP_W
    w_in = p["w_in"][l]
    d = w_in.shape[0]
    w_perm = jnp.concatenate([w_in[:, :8 * g], w_in[:, 8 * g + 8:], w_in[:, 8 * g:8 * g + 8],
                              jnp.zeros((d, LANES - 8), w_in.dtype)], axis=1)
    rows = [p["ret_gn_g"][l], jnp.tile(p["gdn_norm_g"][l], N_HEADS), p["lru_conv_b"][l], p["lru_ba"][l],
            p["lru_bx"][l], p["lru_lambda"][l], p["pool_b"][l], p["pool_scale"][l]]
    pvec = jnp.concatenate([jnp.stack(rows), p["lru_conv_w"][l], jnp.zeros((4, g), F32)], axis=0)
    lane_pad = lambda v, n: jnp.concatenate([v, jnp.zeros(v.shape[:-1] + (LANES - n,), F32)], axis=-1)
    gabp = jnp.concatenate([lane_pad(jnp.stack([p["gdn_a_log"][l], p["gdn_dt_bias"][l]]), N_HEADS),
                            jnp.zeros((6, LANES), F32)], axis=0)
    n_route = N_GROUPS + N_EXPERTS
    wr = lane_pad(jnp.concatenate([p["router_group_w"][l],
                                   jnp.transpose(p["router_expert_w"][l], (1, 0, 2)).reshape(d, N_EXPERTS)],
                                  axis=1), n_route)
    br = lane_pad(jnp.concatenate([p["router_group_b"][l], p["router_expert_b"][l].reshape(N_EXPERTS)])[None, :],
                  n_route)
    return dict(
        norm1_g=p["norm1_g"][l][None, :], w_in=w_perm.astype(BF16), pvec=pvec, gdn_conv_w=p["gdn_conv_w"][l],
        gabp=gabp, lru_wa=_block_diag(p["lru_wa"][l]).astype(BF16), lru_wx=_block_diag(p["lru_wx"][l]).astype(BF16),
        pool_w=_block_diag(p["pool_w"][l]).astype(BF16), w_out=p["w_out"][l].astype(BF16),
        norm2_g=p["norm2_g"][l][None, :], wr=wr, br=br)


def kernel(x_prompt, x_sample, state_ret, state_gdn, state_gdn_conv, state_lru, state_lru_conv, state_pool,
           norm1_g, w_in, ret_gn_g, gdn_conv_w, gdn_a_log, gdn_dt_bias, gdn_norm_g,
           lru_conv_w, lru_conv_b, lru_wa, lru_ba, lru_wx, lru_bx, lru_lambda,
           pool_w, pool_b, pool_scale, w_out, norm2_g,
           router_group_w, router_group_b, router_expert_w, router_expert_b,
           moe_w_gate, moe_w_up, moe_w_down, final_g):
    params = dict(norm1_g=norm1_g, w_in=w_in, ret_gn_g=ret_gn_g, gdn_conv_w=gdn_conv_w, gdn_a_log=gdn_a_log,
                  gdn_dt_bias=gdn_dt_bias, gdn_norm_g=gdn_norm_g, lru_conv_w=lru_conv_w, lru_conv_b=lru_conv_b,
                  lru_wa=lru_wa, lru_ba=lru_ba, lru_wx=lru_wx, lru_bx=lru_bx, lru_lambda=lru_lambda,
                  pool_w=pool_w, pool_b=pool_b, pool_scale=pool_scale, w_out=w_out, norm2_g=norm2_g,
                  router_group_w=router_group_w, router_group_b=router_group_b,
                  router_expert_w=router_expert_w, router_expert_b=router_expert_b)
    bp, lp_, d = x_prompt.shape
    bs, ls, _ = x_sample.shape
    depth = w_in.shape[0]
    tp, ts_ = bp * lp_, bs * ls
    t = tp + ts_
    tile_p = min(256, lp_)
    tm = 256
    n_tiles = (2 * t + N_EXPERTS * (tm - 1) + tm - 1) // tm

    x = (x_prompt.reshape(tp, d), x_sample.reshape(ts_, d))
    sample_states = (state_ret, state_gdn, state_gdn_conv, state_lru, state_lru_conv, state_pool)
    zero_states = tuple(jnp.zeros((1, bp) + s.shape[2:], F32) for s in sample_states)
    new_p = new_s = None
    lps = [_layer_params(l, params) for l in range(depth)]
    z = _norm_win(x, lps[0]["norm1_g"], lps[0]["w_in"], tm)
    for l in range(depth):
        lp = lps[l]
        last = l == depth - 1
        mix_p, new_p = _mixer(z, 0, bp, lp_, tile_p, 1, 0, zero_states, 0, new_p, lp)
        mix_s, new_s = _mixer(z, tp, bs, ls, ls, SAMPLE_SEQS_PER_STEP, PAST_LEN, sample_states, l, new_s, lp)
        x1, h2, rt = _wout_route((mix_p, mix_s), x, lp["w_out"], lp["norm2_g"], lp["wr"], lp["br"], tm)
        pos1, pos2, tile_expert, n_used, has_rows, last_tile_row = _route_plan(rt, tm, n_tiles)
        xs = _scatter_rows(h2, pos1, pos2, has_rows, last_tile_row, n_used, n_tiles, tm)
        ys = _expert_ffn(xs, tile_expert, n_used, moe_w_gate, moe_w_up, moe_w_down, l, tm)
        x, z = _combine(x1, rt, ys, pos1, pos2, final_g[None, :] if last else lps[l + 1]["norm1_g"],
                        None if last else lps[l + 1]["w_in"], tp, tm)
    y_prompt = x[0].reshape(bp, lp_, d)
    y_sample = x[1].reshape(bs, ls, d)
    unpack = lambda st: (st[0], st[1], st[2], st[3][:, :, 0, :], st[4], st[5])
    return (y_prompt, y_sample, *unpack(new_p), *unpack(new_s))
```

```python
import functools

import jax
import jax.numpy as jnp
from jax import lax
from jax.experimental import pallas as pl
from jax.experimental.pallas import tpu as pltpu

F32 = jnp.float32
BF16 = jnp.bfloat16

EPS = 1e-6
HEAD_DIM = 64
N_HEADS = 4
GROUP_W = 256
CONV_W = 4
LRU_C = 8.0
POOL_WINDOWS = (2, 4, 8, 16)
POOL_BUF = 15
ROPE_BASE = 10000.0
N_GROUPS = 4
EXPERTS_PER_GROUP = 8
N_EXPERTS = 32
PAST_LEN = 16384

LANES = 128
SUBLANES = 8
HIST = 16
Z_AB = 11 * GROUP_W
NZ = Z_AB + LANES
INV_BASE = 8
RESID_PASSES = 3
SAMPLE_SEQS_PER_STEP = 16
VMEM_LIMIT = 56 * 1024 * 1024

NN = (((1,), (0,)), ((), ()))
NT = (((1,), (1,)), ((), ()))
TN = (((0,), (0,)), ((), ()))


def _mm(a, b, dims=NN, passes=1):
    ah = a.astype(BF16)
    bh = b.astype(BF16)
    dot = functools.partial(lax.dot_general, dimension_numbers=dims, preferred_element_type=F32)
    if passes == 1:
        return dot(ah, bh)
    al = (a - ah.astype(F32)).astype(BF16)
    bl = (b - bh.astype(F32)).astype(BF16)
    return dot(ah, bh) + (dot(ah, bl) + dot(al, bh))


def _sigmoid(x):
    return 1.0 / (1.0 + jnp.exp(-x))


def _silu(x):
    return x * _sigmoid(x)


def _softplus(x):
    return jnp.maximum(x, 0.0) + jnp.log1p(jnp.exp(-jnp.abs(x)))


def _gelu_tanh(x):
    return 0.5 * x * (1.0 + jnp.tanh(0.7978845608028654 * (x + 0.044715 * (x * x * x))))


def _iota(shape, axis):
    return lax.broadcasted_iota(jnp.int32, shape, axis)


def _shift_rows(x, d, fill, seg=None):
    t = _iota(x.shape, 0)
    if seg is not None:
        t = t & (seg - 1)
    return jnp.where(t >= d, pltpu.roll(x, d, 0), fill)


def _cumsum_rows(x, seg=None):
    d = 1
    while d < (seg or x.shape[0]):
        x = x + _shift_rows(x, d, 0.0, seg)
        d *= 2
    return x


def _swap_half_heads(x):
    n = x.shape[1]
    lower = (_iota(x.shape, 1) & (HEAD_DIM - 1)) < HEAD_DIM // 2
    return jnp.where(lower, pltpu.roll(x, n - HEAD_DIM // 2, 1), pltpu.roll(x, HEAD_DIM // 2, 1))


def _inv_unit_lower(mats, c=None):
    shape = mats[0].shape
    c = c or shape[0]
    row = _iota(shape, 0)
    col = _iota(shape, 1)
    nb = min(INV_BASE, c)
    shift = nb.bit_length() - 1
    in_base = (row >> shift) == (col >> shift)
    ps = [jnp.where(in_base, a, 0.0) for a in mats]
    eye = jnp.where(row == col, 1.0, 0.0)
    xs = [eye - d for d in ps]
    k = 2
    while k < nb:
        ps = [_mm(p, p) for p in ps]
        xs = [x + _mm(x, p) for x, p in zip(xs, ps)]
        k *= 2
    m = nb
    while m < c:
        s1 = m.bit_length() - 1
        in_off = ((row >> (s1 + 1)) == (col >> (s1 + 1))) & ((row >> s1) != (col >> s1))
        ys = [_mm(x, jnp.where(in_off, a, 0.0)) for x, a in zip(xs, mats)]
        xs = [x - _mm(y, x) for x, y in zip(xs, ys)]
        m *= 2
    return xs


def _solve_unit_lower(mats, tinvs, ws):
    us = [_mm(t, w) for t, w in zip(tinvs, ws)]
    rs = [w - (u + _mm(a, u, passes=RESID_PASSES)) for a, u, w in zip(mats, us, ws)]
    return [u + _mm(t, r) for u, t, r in zip(us, tinvs, rs)]


def _pair_specs(na, tm, d):
    return [pl.BlockSpec((tm, d), lambda i: (jnp.minimum(i, na - 1), 0)),
            pl.BlockSpec((tm, d), lambda i: (jnp.maximum(i - na, 0), 0))]


def _pair_tile(na, a_ref, b_ref):
    return jnp.where(pl.program_id(0) < na, a_ref[...], b_ref[...])


def _norm_win_kernel(na, xa_ref, xb_ref, g_ref, w_ref, z_ref):
    x = _pair_tile(na, xa_ref, xb_ref)
    h = x * lax.rsqrt(jnp.mean(x * x, axis=-1, keepdims=True) + EPS) * g_ref[...]
    z_ref[...] = jnp.dot(h.astype(BF16), w_ref[...], preferred_element_type=F32)


def _norm_win(x_pair, g, w_bf16, tm):
    xa, xb = x_pair
    d = xa.shape[1]
    na, t = xa.shape[0] // tm, xa.shape[0] + xb.shape[0]
    nz = w_bf16.shape[1]
    return pl.pallas_call(
        functools.partial(_norm_win_kernel, na),
        grid=(t // tm,),
        in_specs=_pair_specs(na, tm, d) + [pl.BlockSpec((1, d), lambda i: (0, 0)),
                                           pl.BlockSpec((d, nz), lambda i: (0, 0))],
        out_specs=pl.BlockSpec((tm, nz), lambda i: (i, 0)),
        out_shape=jax.ShapeDtypeStruct((t, nz), F32),
        compiler_params=pltpu.CompilerParams(dimension_semantics=("arbitrary",), vmem_limit_bytes=VMEM_LIMIT),
        name="norm_win",
    )(xa, xb, g, w_bf16)


def _retention_tile(zq, zk, zv, cos, sin, intra_ref, qdec, kdec, sdec, s_ref):
    rq = zq * cos + _swap_half_heads(zq) * sin
    rk = (zk * cos + _swap_half_heads(zk) * sin) * (HEAD_DIM ** -0.5)
    kd = rk * kdec
    heads = range(N_HEADS)
    sls = [slice(h * HEAD_DIM, (h + 1) * HEAD_DIM) for h in heads]
    ss = [s_ref[h] for h in heads]
    scores = [_mm(rq[:, sl], rk[:, sl], NT) * intra_ref[h] for h, sl in zip(heads, sls)]
    os_ = [_mm(sc, zv[:, sl]) + _mm(rq[:, sl], s) * qdec[:, sl] for sc, s, sl in zip(scores, ss, sls)]
    s_new = [s * sdec[h:h + 1, :HEAD_DIM] + _mm(kd[:, sl], zv[:, sl], TN) for h, s, sl in zip(heads, ss, sls)]
    for h in heads:
        s_ref[h] = s_new[h]
    outs = []
    for o in os_:
        mu = jnp.mean(o, axis=-1, keepdims=True)
        oc = o - mu
        var = jnp.mean(oc * oc, axis=-1, keepdims=True)
        outs.append(oc * lax.rsqrt(var + EPS))
    return jnp.concatenate(outs, axis=-1)


def _split3(x):
    x0 = x.astype(BF16)
    r1 = x - x0.astype(F32)
    x1 = r1.astype(BF16)
    x2 = (r1 - x1.astype(F32)).astype(BF16)
    return x0, x1, x2


def _gdn_tile(gq, gk, gv, gab, s_ref):
    c = gq.shape[0]
    gc = _cumsum_rows(gab)
    egc = jnp.exp(gc)
    row = _iota((c, c), 0)
    col = _iota((c, c), 1)
    incl = row >= col
    strict = row > col
    heads = range(N_HEADS)
    sls = [slice(h * HEAD_DIM, (h + 1) * HEAD_DIM) for h in heads]
    ss = [s_ref[h] for h in heads]
    gcols = [gc[:, h:h + 1] for h in heads]
    bcols = [gab[:, N_HEADS + h:N_HEADS + h + 1] for h in heads]
    ecols = [egc[:, h:h + 1] for h in heads]
    gct = gc.T
    decs = [jnp.where(incl, jnp.exp(jnp.where(incl, gcols[h] - gct[h:h + 1, :], 0.0)), 0.0) for h in heads]
    mats = [jnp.where(strict, bcols[h] * decs[h] * _mm(gk[:, sl], gk[:, sl], NT), 0.0) for h, sl in zip(heads, sls)]
    tinvs = _inv_unit_lower(mats)
    ws = [bcols[h] * (gv[:, sl] - ecols[h] * _mm(gk[:, sl], ss[h])) for h, sl in zip(heads, sls)]
    us = _solve_unit_lower(mats, tinvs, ws)
    qks = [_mm(gq[:, sl], gk[:, sl], NT) * decs[h] for h, sl in zip(heads, sls)]
    outs = [ecols[h] * _mm(gq[:, sl], ss[h]) + _mm(qks[h], us[h]) for h, sl in zip(heads, sls)]
    glasts = [gc[c - 1:c, h:h + 1] for h in heads]
    s_new = [jnp.exp(glasts[h]) * ss[h] + _mm(jnp.exp(glasts[h] - gcols[h]) * gk[:, sl], us[h], TN)
             for h, sl in zip(heads, sls)]
    for h in heads:
        s_ref[h] = s_new[h]
    return outs


def _mixer_kernel(nt, ts, nb, pos_base,
                  z_ref, cos_ref, sin_ref, intra_ref, qdec_ref, kdec_ref, sdec_ref, pv_ref, gcw_ref, gabp_ref,
                  wa_ref, wx_ref, pw_ref, sret0, sgdn0, gconv0, lru0, lconv0, pool0,
                  mix_ref, ret_o, gdn_o, gconv_o, lru_o, lconv_o, pool_o,
                  s_ret, s_gdn, ext_g, ext_l, ext_p, h_lru):
    n = pl.program_id(1)
    g = GROUP_W
    base = HIST - (CONV_W - 1)

    @pl.when(n == 0)
    def _():
        s_ret[...] = sret0[...]
        s_gdn[...] = sgdn0[...]
        ext_g[:, 0:HIST, :] = jnp.zeros((nb, HIST, 3 * g), F32)
        ext_g[:, base:HIST, :] = gconv0[...]
        ext_l[:, 0:HIST, :] = jnp.zeros((nb, HIST, g), F32)
        ext_l[:, base:HIST, :] = lconv0[...]
        ext_p[:, 0:HIST, :] = jnp.zeros((nb, HIST, g), F32)
        ext_p[:, HIST - POOL_BUF:HIST, :] = pool0[...]
        h_lru[...] = lru0[...]

    pv = pv_ref[...]
    ret_gn_g, gdn_norm_g = pv[0:1], pv[1:2]
    lru_conv_b, lru_ba, lru_bx, lru_lambda = pv[2:3], pv[3:4], pv[4:5], pv[5:6]
    pool_b, pool_scale = pv[6:7], pv[7:8]
    lru_conv_w = pv[8:12]
    gcw = gcw_ref[...]
    gabp = gabp_ref[...]

    for j in range(nb):
        rows = slice(j * ts, (j + 1) * ts)
        eg, el, ep = ext_g.at[j], ext_l.at[j], ext_p.at[j]

        ro = _retention_tile(z_ref[rows, 0:g], z_ref[rows, g:2 * g], z_ref[rows, 2 * g:3 * g], cos_ref[...],
                             sin_ref[...], intra_ref, qdec_ref[...], kdec_ref[...], sdec_ref[...], s_ret.at[j])
        mix_ref[rows, 0:g] = ro * ret_gn_g * _silu(z_ref[rows, 3 * g:4 * g])

        eg[HIST:HIST + ts, :] = z_ref[rows, 4 * g:7 * g]
        y = eg[base:base + ts, :] * gcw[0:1]
        for i in range(1, CONV_W):
            y = y + eg[base + i:base + i + ts, :] * gcw[i:i + 1]
        qkv = _silu(y)
        zab = z_ref[rows, Z_AB:Z_AB + LANES]
        lane = _iota(zab.shape, 1)
        log_a = -jnp.exp(gabp[0:1]) * _softplus(zab + gabp[1:2])
        gab = jnp.where(lane < N_HEADS, log_a, jnp.where(lane < 2 * N_HEADS, _sigmoid(zab), 0.0))
        gq_parts, gk_parts = [], []
        for h in range(N_HEADS):
            qh = qkv[:, h * HEAD_DIM:(h + 1) * HEAD_DIM]
            kh = qkv[:, g + h * HEAD_DIM:g + (h + 1) * HEAD_DIM]
            gq_parts.append(qh * lax.rsqrt(jnp.sum(qh * qh, axis=-1, keepdims=True) + EPS) * (HEAD_DIM ** -0.5))
            gk_parts.append(kh * lax.rsqrt(jnp.sum(kh * kh, axis=-1, keepdims=True) + EPS))
        go = _gdn_tile(jnp.concatenate(gq_parts, axis=-1), jnp.concatenate(gk_parts, axis=-1),
                       qkv[:, 2 * g:3 * g], gab, s_gdn.at[j])
        go = [o * lax.rsqrt(jnp.mean(o * o, axis=-1, keepdims=True) + EPS) for o in go]
        mix_ref[rows, g:2 * g] = jnp.concatenate(go, axis=-1) * gdn_norm_g * _silu(z_ref[rows, 7 * g:8 * g])

        el[HIST:HIST + ts, :] = z_ref[rows, 8 * g:9 * g]
        xc = el[base:base + ts, :] * lru_conv_w[0:1]
        for i in range(1, CONV_W):
            xc = xc + el[base + i:base + i + ts, :] * lru_conv_w[i:i + 1]
        xc = xc + lru_conv_b
        r = _sigmoid(jnp.dot(xc.astype(BF16), wa_ref[...], preferred_element_type=F32) + lru_ba)
        gate_i = _sigmoid(jnp.dot(xc.astype(BF16), wx_ref[...], preferred_element_type=F32) + lru_bx)
        log_at = -LRU_C * r * _softplus(-lru_lambda)
        av = jnp.exp(log_at)
        bv = jnp.sqrt(-jnp.tanh(log_at) * (av * av + 1.0)) * (gate_i * xc)
        bv = bv + jnp.where(_iota(bv.shape, 0) == 0, av * h_lru[j], 0.0)
        d = 1
        while d < ts:
            bv = av * _shift_rows(bv, d, 0.0) + bv
            av = av * _shift_rows(av, d, 1.0)
            d *= 2
        h_lru[j] = bv[ts - 1:ts, :]
        mix_ref[rows, 2 * g:3 * g] = bv * _gelu_tanh(z_ref[rows, 9 * g:10 * g])

        u = z_ref[rows, 10 * g:11 * g]
        ep[HIST:HIST + ts, :] = u
        e = ep[...]
        sums = []
        step = 1
        for _ in POOL_WINDOWS:
            e = e + pltpu.roll(e, step, 0)
            sums.append(e[HIST:HIST + ts, :])
            step *= 2
        pos1 = (pos_base + 1 + n * ts + _iota((ts, g), 0)).astype(F32)
        lane_g = _iota((ts, g), 1) >> 6
        pooled = sums[-1] / jnp.minimum(pos1, float(POOL_WINDOWS[-1]))
        for gi in range(len(POOL_WINDOWS) - 2, -1, -1):
            pooled = jnp.where(lane_g == gi, sums[gi] / jnp.minimum(pos1, float(POOL_WINDOWS[gi])), pooled)
        pooled = pooled - u
        yp = jnp.dot(pooled.astype(BF16), pw_ref[...], preferred_element_type=F32) + pool_b
        mix_ref[rows, 3 * g:4 * g] = yp * pool_scale

        tail_g = eg[ts:ts + HIST, :]
        tail_l = el[ts:ts + HIST, :]
        tail_p = ep[ts:ts + HIST, :]
        eg[0:HIST, :] = tail_g
        el[0:HIST, :] = tail_l
        ep[0:HIST, :] = tail_p

    @pl.when(n == nt - 1)
    def _():
        ret_o[...] = s_ret[...]
        gdn_o[...] = s_gdn[...]
        gconv_o[...] = ext_g[:, base:HIST, :]
        lconv_o[...] = ext_l[:, base:HIST, :]
        pool_o[...] = ext_p[:, HIST - POOL_BUF:HIST, :]
        lru_o[...] = h_lru[...]


def _decode_mixer_kernel(ts, nb, pos_base,
                         z_ref, cos_ref, sin_ref, intra_ref, qdec_ref, kdec_ref, sdec_ref, pv_ref, gcw_ref, gabp_ref,
                         wa_ref, wx_ref, pw_ref, sret0, sgdn0, gconv0, lru0, lconv0, pool0,
                         mix_ref, ret_o, gdn_o, gconv_o, lru_o, lconv_o, pool_o,
                         ext_g, ext_l, ext_p):
    g = GROUP_W
    r_all = nb * ts
    base = HIST - (CONV_W - 1)
    seg_shift = ts.bit_length() - 1
    row2 = _iota((r_all, r_all), 0)
    col2 = _iota((r_all, r_all), 1)
    same = (row2 >> seg_shift) == (col2 >> seg_shift)
    incl = same & (row2 >= col2)
    strict = same & (row2 > col2)
    seq_l = _iota((r_all, LANES), 0) >> seg_shift
    half_l = _iota((r_all, LANES), 1) >> 6
    exp_masks = [((seq_l >> 1) == c) & ((seq_l & 1) == half_l) for c in range(nb // 2)]

    def expand(x):
        x2 = jnp.concatenate([x, x], axis=-1)
        return jnp.concatenate([jnp.where(m, x2, 0.0) for m in exp_masks], axis=-1)

    def stacked(state_ref, h):
        return jnp.concatenate([state_ref[j, h] for j in range(nb)], axis=0)

    def unstack(out_ref, h, sv):
        for j in range(nb):
            out_ref[j, h] = sv[j * HEAD_DIM:(j + 1) * HEAD_DIM, :]

    pv = pv_ref[...]
    ret_gn_g, gdn_norm_g = pv[0:1], pv[1:2]
    lru_conv_b, lru_ba, lru_bx, lru_lambda = pv[2:3], pv[3:4], pv[4:5], pv[5:6]
    pool_b, pool_scale = pv[6:7], pv[7:8]
    lru_conv_w = pv[8:12]
    gcw = gcw_ref[...]
    gabp = gabp_ref[...]

    ext_g[:, 0:HIST, :] = jnp.zeros((nb, HIST, 3 * g), F32)
    ext_g[:, base:HIST, :] = gconv0[...]
    ext_l[:, 0:HIST, :] = jnp.zeros((nb, HIST, g), F32)
    ext_l[:, base:HIST, :] = lconv0[...]
    ext_p[:, 0:HIST, :] = jnp.zeros((nb, HIST, g), F32)
    ext_p[:, HIST - POOL_BUF:HIST, :] = pool0[...]
    y_parts, xc_parts, sum_parts, h0_parts = [], [], [[] for _ in POOL_WINDOWS], []
    for j in range(nb):
        rows = slice(j * ts, (j + 1) * ts)
        eg, el, ep = ext_g.at[j], ext_l.at[j], ext_p.at[j]
        eg[HIST:HIST + ts, :] = z_ref[rows, 4 * g:7 * g]
        el[HIST:HIST + ts, :] = z_ref[rows, 8 * g:9 * g]
        ep[HIST:HIST + ts, :] = z_ref[rows, 10 * g:11 * g]
        y = eg[base:base + ts, :] * gcw[0:1]
        xc = el[base:base + ts, :] * lru_conv_w[0:1]
        for i in range(1, CONV_W):
            y = y + eg[base + i:base + i + ts, :] * gcw[i:i + 1]
            xc = xc + el[base + i:base + i + ts, :] * lru_conv_w[i:i + 1]
        y_parts.append(y)
        xc_parts.append(xc)
        e = ep[...]
        step = 1
        for wi in range(len(POOL_WINDOWS)):
            e = e + pltpu.roll(e, step, 0)
            sum_parts[wi].append(e[HIST:HIST + ts, :])
            step *= 2
        h0_parts.append(jnp.broadcast_to(lru0[j], (ts, g)))
        gconv_o[j] = eg[ts + base:ts + HIST, :]
        lconv_o[j] = el[ts + base:ts + HIST, :]
        pool_o[j] = ep[ts + HIST - POOL_BUF:ts + HIST, :]

    zq, zk, zv = z_ref[:, 0:g], z_ref[:, g:2 * g], z_ref[:, 2 * g:3 * g]
    cos, sin = cos_ref[...], sin_ref[...]
    qdec, kdec, sdec = qdec_ref[...], kdec_ref[...], sdec_ref[...]
    rq = zq * cos + _swap_half_heads(zq) * sin
    rk = (zk * cos + _swap_half_heads(zk) * sin) * (HEAD_DIM ** -0.5)
    kd = rk * kdec
    outs = []
    for h in range(N_HEADS):
        sl = slice(h * HEAD_DIM, (h + 1) * HEAD_DIM)
        q, k, v = rq[:, sl], rk[:, sl], zv[:, sl]
        sv = stacked(sret0, h)
        scores = _mm(q, k, NT) * intra_ref[h]
        o = _mm(scores, v) + _mm(expand(q), sv) * qdec[:, sl]
        unstack(ret_o, h, sv * sdec[h:h + 1, :HEAD_DIM] + _mm(expand(kd[:, sl]), v, TN))
        mu = jnp.mean(o, axis=-1, keepdims=True)
        oc = o - mu
        var = jnp.mean(oc * oc, axis=-1, keepdims=True)
        outs.append(oc * lax.rsqrt(var + EPS))
    mix_ref[:, 0:g] = jnp.concatenate(outs, axis=-1) * ret_gn_g * _silu(z_ref[:, 3 * g:4 * g])

    qkv = _silu(jnp.concatenate(y_parts, axis=0))
    zab = z_ref[:, Z_AB:Z_AB + LANES]
    lane = _iota(zab.shape, 1)
    log_a = -jnp.exp(gabp[0:1]) * _softplus(zab + gabp[1:2])
    gab = jnp.where(lane < N_HEADS, log_a, jnp.where(lane < 2 * N_HEADS, _sigmoid(zab), 0.0))
    gc = _cumsum_rows(gab, seg=ts)
    egc = jnp.exp(gc)
    gct = gc.T
    dotnn = functools.partial(lax.dot_general, dimension_numbers=NN, preferred_element_type=F32)
    same_b = jnp.where(same, 1.0, 0.0).astype(BF16)
    a0, a1, a2 = _split3(gab)
    glast = dotnn(same_b, a0) + (dotnn(same_b, a1) + dotnn(same_b, a2))
    pick = jnp.where(((_iota((nb * HEAD_DIM, r_all), 0) >> 6) == (_iota((nb * HEAD_DIM, r_all), 1) >> seg_shift))
                     & ((_iota((nb * HEAD_DIM, r_all), 1) & (ts - 1)) == 0), 1.0, 0.0).astype(BF16)
    e0, e1, e2 = _split3(jnp.exp(glast))
    eglast_s = dotnn(pick, e0) + (dotnn(pick, e1) + dotnn(pick, e2))
    heads = range(N_HEADS)
    qs, ks, vs, decs = [], [], [], []
    for h in heads:
        qh = qkv[:, h * HEAD_DIM:(h + 1) * HEAD_DIM]
        kh = qkv[:, g + h * HEAD_DIM:g + (h + 1) * HEAD_DIM]
        qs.append(qh * lax.rsqrt(jnp.sum(qh * qh, axis=-1, keepdims=True) + EPS) * (HEAD_DIM ** -0.5))
        ks.append(kh * lax.rsqrt(jnp.sum(kh * kh, axis=-1, keepdims=True) + EPS))
        vs.append(qkv[:, 2 * g + h * HEAD_DIM:2 * g + (h + 1) * HEAD_DIM])
        decs.append(jnp.where(incl, jnp.exp(jnp.where(incl, gc[:, h:h + 1] - gct[h:h + 1, :], 0.0)), 0.0))
    gcols = [gc[:, h:h + 1] for h in heads]
    bcols = [gab[:, N_HEADS + h:N_HEADS + h + 1] for h in heads]
    ecols = [egc[:, h:h + 1] for h in heads]
    svs = [stacked(sgdn0, h) for h in heads]
    mats = [jnp.where(strict, bcols[h] * decs[h] * _mm(ks[h], ks[h], NT), 0.0) for h in heads]
    tinvs = _inv_unit_lower(mats, ts)
    ws = [bcols[h] * (vs[h] - ecols[h] * _mm(expand(ks[h]), svs[h])) for h in heads]
    us = _solve_unit_lower(mats, tinvs, ws)
    os_ = [ecols[h] * _mm(expand(qs[h]), svs[h]) + _mm(_mm(qs[h], ks[h], NT) * decs[h], us[h]) for h in heads]
    for h in heads:
        kdx = jnp.exp(glast[:, h:h + 1] - gcols[h]) * ks[h]
        unstack(gdn_o, h, eglast_s[:, h:h + 1] * svs[h] + _mm(expand(kdx), us[h], TN))
    outs = [o * lax.rsqrt(jnp.mean(o * o, axis=-1, keepdims=True) + EPS) for o in os_]
    mix_ref[:, g:2 * g] = jnp.concatenate(outs, axis=-1) * gdn_norm_g * _silu(z_ref[:, 7 * g:8 * g])

    xc = jnp.concatenate(xc_parts, axis=0) + lru_conv_b
    r = _sigmoid(jnp.dot(xc.astype(BF16), wa_ref[...], preferred_element_type=F32) + lru_ba)
    gate_i = _sigmoid(jnp.dot(xc.astype(BF16), wx_ref[...], preferred_element_type=F32) + lru_bx)
    log_at = -LRU_C * r * _softplus(-lru_lambda)
    av = jnp.exp(log_at)
    bv = jnp.sqrt(-jnp.tanh(log_at) * (av * av + 1.0)) * (gate_i * xc)
    first = (_iota(bv.shape, 0) & (ts - 1)) == 0
    bv = bv + jnp.where(first, av * jnp.concatenate(h0_parts, axis=0), 0.0)
    d = 1
    while d < ts:
        bv = av * _shift_rows(bv, d, 0.0, ts) + bv
        av = av * _shift_rows(av, d, 1.0, ts)
        d *= 2
    for j in range(nb):
        lru_o[j] = bv[(j + 1) * ts - 1:(j + 1) * ts, :]
    mix_ref[:, 2 * g:3 * g] = bv * _gelu_tanh(z_ref[:, 9 * g:10 * g])

    u_in = z_ref[:, 10 * g:11 * g]
    pos1 = (pos_base + 1 + (_iota((r_all, g), 0) & (ts - 1))).astype(F32)
    lane_g = _iota((r_all, g), 1) >> 6
    sums = [jnp.concatenate(parts, axis=0) for parts in sum_parts]
    pooled = sums[-1] / jnp.minimum(pos1, float(POOL_WINDOWS[-1]))
    for gi in range(len(POOL_WINDOWS) - 2, -1, -1):
        pooled = jnp.where(lane_g == gi, sums[gi] / jnp.minimum(pos1, float(POOL_WINDOWS[gi])), pooled)
    pooled = pooled - u_in
    yp = jnp.dot(pooled.astype(BF16), pw_ref[...], preferred_element_type=F32) + pool_b
    mix_ref[:, 3 * g:4 * g] = yp * pool_scale


N_STATES = 6


def _with_state_stack(body, nt, n_prev, n_in, *refs):
    ins = refs[:n_in]
    prevs = refs[n_in:n_in + (N_STATES if n_prev else 0)]
    k = n_in + len(prevs)
    mix_ref, stacks, scratch = refs[k], refs[k + 1:k + 1 + N_STATES], refs[k + 1 + N_STATES:]

    @pl.when(pl.program_id(1) == nt - 1)
    def _():
        for o, p in zip(stacks, prevs):
            o[0:n_prev] = p[...]

    body(*ins, mix_ref, *[o.at[n_prev] for o in stacks], *scratch)


def _ret_consts(c):
    log_g = jnp.log1p(-jnp.exp2(-5.0 - jnp.arange(N_HEADS, dtype=F32)))
    idx = jnp.arange(c, dtype=F32)
    diff = idx[:, None] - idx[None, :]
    intra = jnp.where(diff >= 0, jnp.exp(log_g[:, None, None] * jnp.maximum(diff, 0.0)), 0.0)
    qdec = jnp.exp(log_g[:, None] * (idx + 1.0))
    kdec = jnp.exp(log_g[:, None] * (c - 1.0 - idx))
    sdec = jnp.exp(log_g * c)
    expand = lambda t: jnp.repeat(t.T, HEAD_DIM, axis=1)
    sdec_tab = jnp.zeros((8, LANES), F32).at[:N_HEADS, :].set(sdec[:, None])
    return intra, expand(qdec), expand(kdec), sdec_tab


def _rope_tables(pos):
    half = HEAD_DIM // 2
    inv = ROPE_BASE ** (-jnp.arange(half, dtype=F32) / half)
    ang = pos.astype(F32)[:, None] * inv[None, :]
    cos, sin = jnp.cos(ang), jnp.sin(ang)
    cos_t = jnp.tile(jnp.concatenate([cos, cos], axis=-1), (1, N_HEADS))
    sin_t = jnp.tile(jnp.concatenate([-sin, sin], axis=-1), (1, N_HEADS))
    return cos_t, sin_t


def _mixer(z, z_row0, b, l, ts, nb, pos_base, states, layer, prev, lp):
    nt = l // ts
    assert nb == 1 or (nt == 1 and nb % 2 == 0 and ts == INV_BASE)
    g = GROUP_W
    pos = pos_base + jnp.arange(l, dtype=jnp.int32)
    cos_t, sin_t = _rope_tables(pos)
    intra, qdec, kdec, sdec = _ret_consts(ts)
    rt_ = nb * ts
    if nb > 1:
        cos_t, sin_t, qdec, kdec = (jnp.tile(a, (nb, 1)) for a in (cos_t, sin_t, qdec, kdec))
        intra = jnp.einsum("ab,hij->haibj", jnp.eye(nb, dtype=F32), intra).reshape(N_HEADS, rt_, rt_)
    s_ret0, s_gdn0, gconv0, lru0, lconv0, pool0 = states
    blk0 = z_row0 // rt_
    const2 = lambda bi, n: (0, 0)
    per_b3 = lambda bi, n: (bi, 0, 0)
    per_b4 = lambda bi, n: (bi, 0, 0, 0)
    st3 = lambda bi, n: (layer, bi, 0, 0)
    st4 = lambda bi, n: (layer, bi, 0, 0, 0)
    in_specs = [
        pl.BlockSpec((rt_, NZ), lambda bi, n: (blk0 + bi * nt + n, 0)),
        pl.BlockSpec((rt_, g), lambda bi, n: (n, 0)),
        pl.BlockSpec((rt_, g), lambda bi, n: (n, 0)),
        pl.BlockSpec((N_HEADS, rt_, rt_), lambda bi, n: (0, 0, 0)),
        pl.BlockSpec((rt_, g), const2),
        pl.BlockSpec((rt_, g), const2),
        pl.BlockSpec((8, LANES), const2),
        pl.BlockSpec((16, g), const2),
        pl.BlockSpec((CONV_W, 3 * g), const2),
        pl.BlockSpec((8, LANES), const2),
        pl.BlockSpec((g, g), const2),
        pl.BlockSpec((g, g), const2),
        pl.BlockSpec((g, g), const2),
        pl.BlockSpec((None, nb, N_HEADS, HEAD_DIM, HEAD_DIM), st4),
        pl.BlockSpec((None, nb, N_HEADS, HEAD_DIM, HEAD_DIM), st4),
        pl.BlockSpec((None, nb, CONV_W - 1, 3 * g), st3),
        pl.BlockSpec((None, nb, 1, g), st3),
        pl.BlockSpec((None, nb, CONV_W - 1, g), st3),
        pl.BlockSpec((None, nb, POOL_BUF, g), st3),
    ]
    st_shapes = [(N_HEADS, HEAD_DIM, HEAD_DIM), (N_HEADS, HEAD_DIM, HEAD_DIM), (CONV_W - 1, 3 * g), (1, g),
                 (CONV_W - 1, g), (POOL_BUF, g)]
    n_prev = 0 if prev is None else prev[0].shape[0]
    stack_spec = lambda layers, s: pl.BlockSpec((layers, nb) + s, lambda bi, n: (0, bi) + (0,) * len(s))
    n_in = len(in_specs)
    if n_prev:
        in_specs = in_specs + [stack_spec(n_prev, s) for s in st_shapes]
    out_specs = [pl.BlockSpec((rt_, 4 * g), lambda bi, n: (bi * nt + n, 0))] + [
        stack_spec(n_prev + 1, s) for s in st_shapes]
    out_shape = [jax.ShapeDtypeStruct((b * l, 4 * g), F32)] + [
        jax.ShapeDtypeStruct((n_prev + 1, b) + s, F32) for s in st_shapes]
    ext = [pltpu.VMEM((nb, ts + HIST, 3 * g), F32), pltpu.VMEM((nb, ts + HIST, g), F32),
           pltpu.VMEM((nb, ts + HIST, g), F32)]
    if nb > 1:
        body = functools.partial(_decode_mixer_kernel, ts, nb, pos_base)
        scratch = ext
    else:
        body = functools.partial(_mixer_kernel, nt, ts, nb, pos_base)
        scratch = [pltpu.VMEM((nb, N_HEADS, HEAD_DIM, HEAD_DIM), F32),
                   pltpu.VMEM((nb, N_HEADS, HEAD_DIM, HEAD_DIM), F32)] + ext + [pltpu.VMEM((nb, 1, g), F32)]
    outs = pl.pallas_call(
        functools.partial(_with_state_stack, body, nt, n_prev, n_in),
        grid=(b // nb, nt),
        in_specs=in_specs,
        out_specs=out_specs,
        out_shape=out_shape,
        scratch_shapes=scratch,
        compiler_params=pltpu.CompilerParams(dimension_semantics=("arbitrary", "arbitrary"),
                                             vmem_limit_bytes=VMEM_LIMIT),
        name=f"mixer_ts{ts}",
    )(z, cos_t, sin_t, intra, qdec, kdec, sdec, lp["pvec"], lp["gdn_conv_w"], lp["gabp"],
      lp["lru_wa"], lp["lru_wx"], lp["pool_w"], s_ret0, s_gdn0, gconv0, lru0[:, :, None, :], lconv0, pool0,
      *(prev or ()))
    return outs[0], tuple(outs[1:])


def _to_row_tiles(ref, x):
    m = x.shape[0]
    ref[...] = x.reshape(m // SUBLANES, SUBLANES, SUBLANES, LANES)


def _from_row_tiles(ref):
    return ref[...].reshape(ref.shape[0] * SUBLANES, SUBLANES * LANES)


def _row_tile_shape(m):
    return (m // SUBLANES, SUBLANES, SUBLANES, LANES)


def _wout_route_kernel(na, ma_ref, mb_ref, xa_ref, xb_ref, wo_ref, g_ref, wr_ref, br_ref, x1_ref, h2_ref, rt_ref):
    mix = _pair_tile(na, ma_ref, mb_ref)
    x1 = _pair_tile(na, xa_ref, xb_ref) + jnp.dot(mix.astype(BF16), wo_ref[...], preferred_element_type=F32)
    x1_ref[...] = x1
    h = x1 * lax.rsqrt(jnp.mean(x1 * x1, axis=-1, keepdims=True) + EPS) * g_ref[...]
    _to_row_tiles(h2_ref, h)
    logits = _mm(h, wr_ref[...], passes=3) + br_ref[...]
    lane = _iota(logits.shape, 1)
    lane_f = lane.astype(F32)
    neg = -1e30
    far = 1e9
    is_g = lane < N_GROUPS
    gl = jnp.where(is_g, logits, neg)
    gmax = jnp.max(gl, axis=-1, keepdims=True)
    gsel = jnp.min(jnp.where(gl == gmax, lane_f, far), axis=-1, keepdims=True)
    p_g = 1.0 / jnp.sum(jnp.where(is_g, jnp.exp(gl - gmax), 0.0), axis=-1, keepdims=True)
    e_group = ((lane - N_GROUPS) >> 3).astype(F32)
    in_group = jnp.where(lane >= N_GROUPS, e_group, -1.0) == gsel
    el = jnp.where(in_group, logits, neg)
    m1 = jnp.max(el, axis=-1, keepdims=True)
    i1 = jnp.min(jnp.where(el == m1, lane_f, far), axis=-1, keepdims=True)
    el2 = jnp.where(lane_f == i1, neg, el)
    m2 = jnp.max(el2, axis=-1, keepdims=True)
    i2 = jnp.min(jnp.where(el2 == m2, lane_f, far), axis=-1, keepdims=True)
    e2 = jnp.exp(m2 - m1)
    w1 = p_g / (1.0 + e2)
    w2 = p_g * e2 / (1.0 + e2)
    rt = jnp.where(lane == 0, i1 - N_GROUPS,
                   jnp.where(lane == 1, i2 - N_GROUPS, jnp.where(lane == 2, w1, jnp.where(lane == 3, w2, 0.0))))
    rt_ref[...] = rt


def _wout_route(mix_pair, x_pair, wo_bf16, g, wr, br, tm):
    d = x_pair[0].shape[1]
    na, t = x_pair[0].shape[0] // tm, x_pair[0].shape[0] + x_pair[1].shape[0]
    row = lambda i: (i, 0)
    const = lambda i: (0, 0)
    return pl.pallas_call(
        functools.partial(_wout_route_kernel, na),
        grid=(t // tm,),
        in_specs=_pair_specs(na, tm, d) + _pair_specs(na, tm, d) + [
            pl.BlockSpec((d, d), const), pl.BlockSpec((1, d), const), pl.BlockSpec((d, LANES), const),
            pl.BlockSpec((1, LANES), const)],
        out_specs=[pl.BlockSpec((tm, d), row), pl.BlockSpec(_row_tile_shape(tm), lambda i: (i, 0, 0, 0)),
                   pl.BlockSpec((tm, LANES), row)],
        out_shape=[jax.ShapeDtypeStruct((t, d), F32), jax.ShapeDtypeStruct(_row_tile_shape(t), F32),
                   jax.ShapeDtypeStruct((t, LANES), F32)],
        compiler_params=pltpu.CompilerParams(dimension_semantics=("arbitrary",), vmem_limit_bytes=VMEM_LIMIT),
        name="wout_route",
    )(*mix_pair, *x_pair, wo_bf16, g, wr, br)


def _tile_of_row(ref4, grp, k):
    return ref4.at[grp, pl.ds(k, 1)]


def _for_each_row(tm, fn):
    def body(grp, c):
        for k in range(SUBLANES):
            fn(grp, k, grp * SUBLANES + k)
        return c

    lax.fori_loop(0, tm // SUBLANES, body, 0)


def _route_plan_kernel(tr, tm, n_tiles, rt_ref, pos_ref, tmeta_ref, emeta_ref, cnt_s, start_s, carry_s):
    ph = pl.program_id(0)
    i = pl.program_id(1)
    rt = rt_ref[...]
    lane = _iota((tr, LANES), 1)
    lane_f = lane.astype(F32)
    oh1 = jnp.where(lane_f == rt[:, 0:1], 1.0, 0.0)
    oh2 = jnp.where(lane_f == rt[:, 1:2], 1.0, 0.0)
    oh = oh1 + oh2
    colsum = jnp.sum(oh, axis=0, keepdims=True)

    @pl.when((ph == 0) & (i == 0))
    def _():
        cnt_s[...] = jnp.zeros((1, LANES), F32)

    @pl.when(ph == 0)
    def _():
        cnt_s[...] = cnt_s[...] + colsum

    @pl.when((ph == 1) & (i == 0))
    def _():
        cnt = cnt_s[...]
        ntile = jnp.floor((cnt + (tm - 1.0)) * (1.0 / tm))
        before = jnp.where(_iota((LANES, LANES), 0) < _iota((LANES, LANES), 1), 1.0, 0.0)
        st_tiles = _mm(jnp.broadcast_to(ntile, (SUBLANES, LANES)), before)[0:1]
        end_tiles = st_tiles + ntile
        start_s[...] = st_tiles * tm
        carry_s[...] = jnp.zeros((1, LANES), F32)
        is_e1 = _iota((1, LANES), 1) < N_EXPERTS
        n_used = jnp.sum(jnp.where(is_e1, ntile, 0.0), axis=-1, keepdims=True)
        last = jnp.sum(jnp.where(is_e1 & (end_tiles <= n_used - 1.0), 1.0, 0.0), axis=-1, keepdims=True)
        ti = _iota((n_tiles, LANES), 0).astype(F32)
        lane_t = _iota((n_tiles, LANES), 1)
        te = jnp.sum(jnp.where((lane_t < N_EXPERTS) & (end_tiles <= ti), 1.0, 0.0), axis=-1, keepdims=True)
        te = jnp.minimum(te, last)
        tmeta_ref[...] = jnp.where(lane_t == 0, te, jnp.where(lane_t == 1, n_used, 0.0))
        srow = _iota((SUBLANES, LANES), 0)
        emeta_ref[...] = jnp.where(srow == 0, ntile, jnp.where(srow == 1, (end_tiles - 1.0) * tm, 0.0))

    @pl.when(ph == 1)
    def _():
        earlier = jnp.where(_iota((tr, tr), 0) > _iota((tr, tr), 1), 1.0, 0.0)
        base = start_s[...] + carry_s[...] + _mm(earlier, oh)
        p1 = jnp.sum(oh1 * base, axis=-1, keepdims=True)
        p2 = jnp.sum(oh2 * base, axis=-1, keepdims=True)
        pos_ref[...] = jnp.where(lane == 0, p1, jnp.where(lane == 1, p2, 0.0))
        carry_s[...] = carry_s[...] + colsum


def _route_plan(rt, tm, n_tiles):
    t = rt.shape[0]
    tr = next(c for c in (1024, 512, 256, tm) if t % c == 0)
    pos, tmeta, emeta = pl.pallas_call(
        functools.partial(_route_plan_kernel, tr, tm, n_tiles),
        grid=(2, t // tr),
        in_specs=[pl.BlockSpec((tr, LANES), lambda ph, i: (i, 0))],
        out_specs=[pl.BlockSpec((tr, LANES), lambda ph, i: (ph * i, 0)),
                   pl.BlockSpec((n_tiles, LANES), lambda ph, i: (0, 0)),
                   pl.BlockSpec((SUBLANES, LANES), lambda ph, i: (0, 0))],
        out_shape=[jax.ShapeDtypeStruct((t, LANES), F32), jax.ShapeDtypeStruct((n_tiles, LANES), F32),
                   jax.ShapeDtypeStruct((SUBLANES, LANES), F32)],
        scratch_shapes=[pltpu.VMEM((1, LANES), F32), pltpu.VMEM((1, LANES), F32), pltpu.VMEM((1, LANES), F32)],
        compiler_params=pltpu.CompilerParams(dimension_semantics=("arbitrary", "arbitrary"),
                                             vmem_limit_bytes=VMEM_LIMIT),
        name="route_plan",
    )(rt)
    pos1 = pos[:, 0].astype(jnp.int32)
    pos2 = pos[:, 1].astype(jnp.int32)
    tile_expert = tmeta[:, 0].astype(jnp.int32)
    n_used = tmeta[0:1, 1].astype(jnp.int32)
    has_rows = (emeta[0, :N_EXPERTS] > 0).astype(jnp.int32)
    last_tile_row = jnp.maximum(emeta[1, :N_EXPERTS], 0.0).astype(jnp.int32)
    return pos1, pos2, tile_expert, n_used, has_rows, last_tile_row


def _scatter_kernel(tm, n_steps, n_tiles, min_tiles, zon_ref, zrow_ref, nu_ref, p1_ref, p2_ref, p1p_ref, p2p_ref,
                    h_ref, xs_hbm, zbuf, stage, sem1, sem2, zsem, tsem):
    i = pl.program_id(0)

    @pl.when(i == 0)
    def _():
        zbuf[...] = jnp.zeros(zbuf.shape, F32)

        def zero_copy(e):
            return pltpu.make_async_copy(zbuf, xs_hbm.at[pl.ds(zrow_ref[e], tm)], zsem.at[e])

        def tail_copy(k):
            return pltpu.make_async_copy(zbuf, xs_hbm.at[pl.ds((min_tiles + k) * tm, tm)], tsem.at[k])

        def zstart(e, c):
            @pl.when(zon_ref[e] > 0)
            def _():
                zero_copy(e).start()
            return c

        def zwait(e, c):
            @pl.when(zon_ref[e] > 0)
            def _():
                zero_copy(e).wait()
            return c

        def tstart(k, c):
            @pl.when(min_tiles + k >= nu_ref[0])
            def _():
                tail_copy(k).start()
            return c

        def twait(k, c):
            @pl.when(min_tiles + k >= nu_ref[0])
            def _():
                tail_copy(k).wait()
            return c

        lax.fori_loop(0, N_EXPERTS, zstart, 0)
        lax.fori_loop(0, n_tiles - min_tiles, tstart, 0)
        lax.fori_loop(0, N_EXPERTS, zwait, 0)
        lax.fori_loop(0, n_tiles - min_tiles, twait, 0)

    slot = i % 2
    stage[slot] = h_ref[...]

    def copies(pa_ref, pb_ref, s, grp, k, r):
        src = _tile_of_row(stage.at[s], grp, k)
        return (pltpu.make_async_copy(src, xs_hbm.at[pl.ds(pa_ref[0, 0, r], 1)], sem1.at[s]),
                pltpu.make_async_copy(src, xs_hbm.at[pl.ds(pb_ref[0, 0, r], 1)], sem2.at[s]))

    def start(grp, k, r):
        c1, c2 = copies(p1_ref, p2_ref, slot, grp, k, r)
        c1.start(priority=0)
        c2.start(priority=1)

    def wait_of(pa_ref, pb_ref, s):
        def wait(grp, k, r):
            c1, c2 = copies(pa_ref, pb_ref, s, grp, k, r)
            c1.wait()
            c2.wait()
        return wait

    _for_each_row(tm, start)

    @pl.when(i >= 1)
    def _():
        _for_each_row(tm, wait_of(p1p_ref, p2p_ref, 1 - slot))

    @pl.when(i == n_steps - 1)
    def _():
        _for_each_row(tm, wait_of(p1_ref, p2_ref, slot))


def _scatter_rows(h2, pos1, pos2, has_rows, last_tile_row, n_used, n_tiles, tm):
    t = h2.shape[0] * SUBLANES
    nb = t // tm
    min_tiles = (2 * t) // tm
    smem_blk = pl.BlockSpec((1, 1, tm), lambda i, zon, zrow, nu: (i, 0, 0), memory_space=pltpu.SMEM)
    smem_prev = pl.BlockSpec((1, 1, tm), lambda i, zon, zrow, nu: (jnp.maximum(i - 1, 0), 0, 0),
                             memory_space=pltpu.SMEM)
    p1, p2 = pos1.reshape(nb, 1, tm), pos2.reshape(nb, 1, tm)
    grid_spec = pltpu.PrefetchScalarGridSpec(
        num_scalar_prefetch=3,
        grid=(nb,),
        in_specs=[smem_blk, smem_blk, smem_prev, smem_prev,
                  pl.BlockSpec(_row_tile_shape(tm), lambda i, zon, zrow, nu: (i, 0, 0, 0))],
        out_specs=pl.BlockSpec(memory_space=pl.ANY),
        scratch_shapes=[pltpu.VMEM((tm, SUBLANES, LANES), F32), pltpu.VMEM((2,) + _row_tile_shape(tm), F32),
                        pltpu.SemaphoreType.DMA((2,)), pltpu.SemaphoreType.DMA((2,)),
                        pltpu.SemaphoreType.DMA((N_EXPERTS,)), pltpu.SemaphoreType.DMA((n_tiles - min_tiles,))],
    )
    return pl.pallas_call(
        functools.partial(_scatter_kernel, tm, nb, n_tiles, min_tiles),
        grid_spec=grid_spec,
        out_shape=jax.ShapeDtypeStruct((n_tiles * tm, SUBLANES, LANES), F32),
        compiler_params=pltpu.CompilerParams(dimension_semantics=("arbitrary",), vmem_limit_bytes=VMEM_LIMIT),
        name="moe_scatter",
    )(has_rows, last_tile_row, n_used, p1, p2, p1, p2, h2)


def _ffn_kernel(te_ref, nu_ref, xs_ref, wg_ref, wu_ref, wd_ref, ys_ref, wg_s, wu_s, wd_s):
    i = pl.program_id(0)
    used = i < nu_ref[0]
    new_expert = (i == 0) | (te_ref[i] != te_ref[jnp.maximum(i - 1, 0)])

    @pl.when(used & new_expert)
    def _():
        wg_s[...] = wg_ref[...].astype(BF16)
        wu_s[...] = wu_ref[...].astype(BF16)
        wd_s[...] = wd_ref[...].astype(BF16)

    @pl.when(used)
    def _():
        x = _from_row_tiles(xs_ref).astype(BF16)
        gate = jnp.dot(x, wg_s[...], preferred_element_type=F32)
        up = jnp.dot(x, wu_s[...], preferred_element_type=F32)
        act = (_silu(gate) * up).astype(BF16)
        _to_row_tiles(ys_ref, jnp.dot(act, wd_s[...], preferred_element_type=F32))

    @pl.when(jnp.logical_not(used))
    def _():
        ys_ref[...] = jnp.zeros(ys_ref.shape, F32)


def _expert_ffn(xs, tile_expert, n_used, w_gate, w_up, w_down, layer, tm):
    p = xs.shape[0]
    xs = xs.reshape(_row_tile_shape(p))
    _, _, d, de = w_gate.shape
    row = lambda i, te, nu: (jnp.minimum(i, nu[0] - 1), 0, 0, 0)
    wsel = lambda i, te, nu: (layer, te[i], 0, 0)
    grid_spec = pltpu.PrefetchScalarGridSpec(
        num_scalar_prefetch=2,
        grid=(p // tm,),
        in_specs=[pl.BlockSpec(_row_tile_shape(tm), row),
                  pl.BlockSpec((None, None, d, de), wsel),
                  pl.BlockSpec((None, None, d, de), wsel),
                  pl.BlockSpec((None, None, de, d), wsel)],
        out_specs=pl.BlockSpec(_row_tile_shape(tm), lambda i, te, nu: (i, 0, 0, 0)),
        scratch_shapes=[pltpu.VMEM((d, de), BF16), pltpu.VMEM((d, de), BF16), pltpu.VMEM((de, d), BF16)],
    )
    return pl.pallas_call(
        _ffn_kernel,
        grid_spec=grid_spec,
        out_shape=jax.ShapeDtypeStruct(_row_tile_shape(p), F32),
        compiler_params=pltpu.CompilerParams(dimension_semantics=("arbitrary",), vmem_limit_bytes=VMEM_LIMIT),
        name="expert_ffn",
    )(tile_expert, n_used, xs, w_gate, w_up, w_down)


def _combine_kernel(tm, n_steps, na, last_layer, p1_ref, p2_ref, p1n_ref, p2n_ref, x1_ref, rt_ref, g_ref, *rest):
    if last_layer:
        ys_hbm, outa_ref, outb_ref, buf1, buf2, sem1, sem2 = rest
    else:
        w_ref, ys_hbm, outa_ref, outb_ref, z_ref, buf1, buf2, sem1, sem2 = rest
    i = pl.program_id(0)
    slot = i % 2

    def copies(pa_ref, pb_ref, s, grp, k, r):
        return (pltpu.make_async_copy(ys_hbm.at[pl.ds(pa_ref[0, 0, r], 1)], _tile_of_row(buf1.at[s], grp, k),
                                      sem1.at[s]),
                pltpu.make_async_copy(ys_hbm.at[pl.ds(pb_ref[0, 0, r], 1)], _tile_of_row(buf2.at[s], grp, k),
                                      sem2.at[s]))

    def start(pa_ref, pb_ref, s):
        def go(grp, k, r):
            c1, c2 = copies(pa_ref, pb_ref, s, grp, k, r)
            c1.start(priority=0)
            c2.start(priority=1)
        _for_each_row(tm, go)

    @pl.when(i == 0)
    def _():
        start(p1_ref, p2_ref, 0)

    @pl.when(i + 1 < n_steps)
    def _():
        start(p1n_ref, p2n_ref, 1 - slot)

    def wait(grp, k, r):
        c1, c2 = copies(p1_ref, p2_ref, slot, grp, k, r)
        c1.wait()
        c2.wait()

    _for_each_row(tm, wait)
    rt = rt_ref[...]
    x = x1_ref[...] + rt[:, 2:3] * _from_row_tiles(buf1.at[slot]) + rt[:, 3:4] * _from_row_tiles(buf2.at[slot])
    h = x * lax.rsqrt(jnp.mean(x * x, axis=-1, keepdims=True) + EPS) * g_ref[...]
    if last_layer:
        x = h
    else:
        z_ref[...] = jnp.dot(h.astype(BF16), w_ref[...], preferred_element_type=F32)

    @pl.when(pl.program_id(0) < na)
    def _():
        outa_ref[...] = x

    @pl.when(pl.program_id(0) >= na)
    def _():
        outb_ref[...] = x


def _combine(x1, rt, ys, pos1, pos2, gain, next_w_in, ta, tm):
    t, d = x1.shape
    nb = t // tm
    na = ta // tm
    last_layer = next_w_in is None
    row = lambda i: (i, 0)
    const = lambda i: (0, 0)
    smem_blk = pl.BlockSpec((1, 1, tm), lambda i: (i, 0, 0), memory_space=pltpu.SMEM)
    smem_next = pl.BlockSpec((1, 1, tm), lambda i: (jnp.minimum(i + 1, nb - 1), 0, 0), memory_space=pltpu.SMEM)
    p1, p2 = pos1.reshape(nb, 1, tm), pos2.reshape(nb, 1, tm)
    in_specs = [smem_blk, smem_blk, smem_next, smem_next, pl.BlockSpec((tm, d), row),
                pl.BlockSpec((tm, LANES), row), pl.BlockSpec((1, d), const)]
    operands = [p1, p2, p1, p2, x1, rt, gain]
    out_specs = [pl.BlockSpec((tm, d), lambda i: (jnp.minimum(i, na - 1), 0)),
                 pl.BlockSpec((tm, d), lambda i: (jnp.maximum(i - na, 0), 0))]
    out_shape = [jax.ShapeDtypeStruct((ta, d), F32), jax.ShapeDtypeStruct((t - ta, d), F32)]
    if not last_layer:
        nz = next_w_in.shape[1]
        in_specs.append(pl.BlockSpec((d, nz), const))
        operands.append(next_w_in)
        out_specs.append(pl.BlockSpec((tm, nz), row))
        out_shape.append(jax.ShapeDtypeStruct((t, nz), F32))
    outs = pl.pallas_call(
        functools.partial(_combine_kernel, tm, nb, na, last_layer),
        grid=(nb,),
        in_specs=in_specs + [pl.BlockSpec(memory_space=pl.ANY)],
        out_specs=out_specs,
        out_shape=out_shape,
        scratch_shapes=[pltpu.VMEM((2,) + _row_tile_shape(tm), F32), pltpu.VMEM((2,) + _row_tile_shape(tm), F32),
                        pltpu.SemaphoreType.DMA((2,)), pltpu.SemaphoreType.DMA((2,))],
        compiler_params=pltpu.CompilerParams(dimension_semantics=("arbitrary",), vmem_limit_bytes=VMEM_LIMIT),
        name="moe_combine",
    )(*operands, ys.reshape(-1, SUBLANES, LANES))
    return (outs[0], outs[1]), (None if last_layer else outs[2])


def _block_diag(w):
    h, a, b = w.shape
    return jnp.einsum("hij,hk->hikj", w, jnp.eye(h, dtype=w.dtype)).reshape(h * a, h * b)


def _layer_params(l, p):
    g = GROUP_W
    w_in = p["w_in"][l]
    d = w_in.shape[0]
    w_perm = jnp.concatenate([w_in[:, :8 * g], w_in[:, 8 * g + 8:], w_in[:, 8 * g:8 * g + 8],
                              jnp.zeros((d, LANES - 8), w_in.dtype)], axis=1)
    rows = [p["ret_gn_g"][l], jnp.tile(p["gdn_norm_g"][l], N_HEADS), p["lru_conv_b"][l], p["lru_ba"][l],
            p["lru_bx"][l], p["lru_lambda"][l], p["pool_b"][l], p["pool_scale"][l]]
    pvec = jnp.concatenate([jnp.stack(rows), p["lru_conv_w"][l], jnp.zeros((4, g), F32)], axis=0)
    lane_pad = lambda v, n: jnp.concatenate([v, jnp.zeros(v.shape[:-1] + (LANES - n,), F32)], axis=-1)
    gabp = jnp.concatenate([lane_pad(jnp.stack([p["gdn_a_log"][l], p["gdn_dt_bias"][l]]), N_HEADS),
                            jnp.zeros((6, LANES), F32)], axis=0)
    n_route = N_GROUPS + N_EXPERTS
    wr = lane_pad(jnp.concatenate([p["router_group_w"][l],
                                   jnp.transpose(p["router_expert_w"][l], (1, 0, 2)).reshape(d, N_EXPERTS)],
                                  axis=1), n_route)
    br = lane_pad(jnp.concatenate([p["router_group_b"][l], p["router_expert_b"][l].reshape(N_EXPERTS)])[None, :],
                  n_route)
    return dict(
        norm1_g=p["norm1_g"][l][None, :], w_in=w_perm.astype(BF16), pvec=pvec, gdn_conv_w=p["gdn_conv_w"][l],
        gabp=gabp, lru_wa=_block_diag(p["lru_wa"][l]).astype(BF16), lru_wx=_block_diag(p["lru_wx"][l]).astype(BF16),
        pool_w=_block_diag(p["pool_w"][l]).astype(BF16), w_out=p["w_out"][l].astype(BF16),
        norm2_g=p["norm2_g"][l][None, :], wr=wr, br=br)


def kernel(x_prompt, x_sample, state_ret, state_gdn, state_gdn_conv, state_lru, state_lru_conv, state_pool,
           norm1_g, w_in, ret_gn_g, gdn_conv_w, gdn_a_log, gdn_dt_bias, gdn_norm_g,
           lru_conv_w, lru_conv_b, lru_wa, lru_ba, lru_wx, lru_bx, lru_lambda,
           pool_w, pool_b, pool_scale, w_out, norm2_g,
           router_group_w, router_group_b, router_expert_w, router_expert_b,
           moe_w_gate, moe_w_up, moe_w_down, final_g):
    params = dict(norm1_g=norm1_g, w_in=w_in, ret_gn_g=ret_gn_g, gdn_conv_w=gdn_conv_w, gdn_a_log=gdn_a_log,
                  gdn_dt_bias=gdn_dt_bias, gdn_norm_g=gdn_norm_g, lru_conv_w=lru_conv_w, lru_conv_b=lru_conv_b,
                  lru_wa=lru_wa, lru_ba=lru_ba, lru_wx=lru_wx, lru_bx=lru_bx, lru_lambda=lru_lambda,
                  pool_w=pool_w, pool_b=pool_b, pool_scale=pool_scale, w_out=w_out, norm2_g=norm2_g,
                  router_group_w=router_group_w, router_group_b=router_group_b,
                  router_expert_w=router_expert_w, router_expert_b=router_expert_b)
    bp, lp_, d = x_prompt.shape
    bs, ls, _ = x_sample.shape
    depth = w_in.shape[0]
    tp, ts_ = bp * lp_, bs * ls
    t = tp + ts_
    tile_p = min(256, lp_)
    tm = 256
    n_tiles = (2 * t + N_EXPERTS * (tm - 1) + tm - 1) // tm

    x = (x_prompt.reshape(tp, d), x_sample.reshape(ts_, d))
    sample_states = (state_ret, state_gdn, state_gdn_conv, state_lru, state_lru_conv, state_pool)
    zero_states = tuple(jnp.zeros((1, bp) + s.shape[2:], F32) for s in sample_states)
    new_p = new_s = None
    lps = [_layer_params(l, params) for l in range(depth)]
    z = _norm_win(x, lps[0]["norm1_g"], lps[0]["w_in"], tm)
    for l in range(depth):
        lp = lps[l]
        last = l == depth - 1
        mix_p, new_p = _mixer(z, 0, bp, lp_, tile_p, 1, 0, zero_states, 0, new_p, lp)
        mix_s, new_s = _mixer(z, tp, bs, ls, ls, SAMPLE_SEQS_PER_STEP, PAST_LEN, sample_states, l, new_s, lp)
        x1, h2, rt = _wout_route((mix_p, mix_s), x, lp["w_out"], lp["norm2_g"], lp["wr"], lp["br"], tm)
        pos1, pos2, tile_expert, n_used, has_rows, last_tile_row = _route_plan(rt, tm, n_tiles)
        xs = _scatter_rows(h2, pos1, pos2, has_rows, last_tile_row, n_used, n_tiles, tm)
        ys = _expert_ffn(xs, tile_expert, n_used, moe_w_gate, moe_w_up, moe_w_down, l, tm)
        x, z = _combine(x1, rt, ys, pos1, pos2, final_g[None, :] if last else lps[l + 1]["norm1_g"],
                        None if last else lps[l + 1]["w_in"], tp, tm)
    y_prompt = x[0].reshape(bp, lp_, d)
    y_sample = x[1].reshape(bs, ls, d)
    unpack = lambda st: (st[0], st[1], st[2], st[3][:, :, 0, :], st[4], st[5])
    return (y_prompt, y_sample, *unpack(new_p), *unpack(new_s))
```

```python
import functools

import jax
import jax.numpy as jnp
from jax import lax
from jax.experimental import pallas as pl
from jax.experimental.pallas import tpu as pltpu

F32 = jnp.float32
BF16 = jnp.bfloat16

EPS = 1e-6
HEAD_DIM = 64
N_HEADS = 4
GROUP_W = 256
CONV_W = 4
LRU_C = 8.0
POOL_WINDOWS = (2, 4, 8, 16)
POOL_BUF = 15
ROPE_BASE = 10000.0
N_GROUPS = 4
EXPERTS_PER_GROUP = 8
N_EXPERTS = 32
PAST_LEN = 16384

LANES = 128
SUBLANES = 8
HIST = 16
Z_AB = 11 * GROUP_W
NZ = Z_AB + LANES
INV_BASE = 8
RESID_PASSES = 3
SAMPLE_SEQS_PER_STEP = 16
VMEM_LIMIT = 56 * 1024 * 1024

NN = (((1,), (0,)), ((), ()))
NT = (((1,), (1,)), ((), ()))
TN = (((0,), (0,)), ((), ()))


def _mm(a, b, dims=NN, passes=1):
    ah = a.astype(BF16)
    bh = b.astype(BF16)
    dot = functools.partial(lax.dot_general, dimension_numbers=dims, preferred_element_type=F32)
    if passes == 1:
        return dot(ah, bh)
    al = (a - ah.astype(F32)).astype(BF16)
    bl = (b - bh.astype(F32)).astype(BF16)
    return dot(ah, bh) + (dot(ah, bl) + dot(al, bh))


def _sigmoid(x):
    return 1.0 / (1.0 + jnp.exp(-x))


def _silu(x):
    return x * _sigmoid(x)


def _softplus(x):
    return jnp.maximum(x, 0.0) + jnp.log1p(jnp.exp(-jnp.abs(x)))


def _gelu_tanh(x):
    return 0.5 * x * (1.0 + jnp.tanh(0.7978845608028654 * (x + 0.044715 * (x * x * x))))


def _iota(shape, axis):
    return lax.broadcasted_iota(jnp.int32, shape, axis)


def _shift_rows(x, d, fill, seg=None):
    t = _iota(x.shape, 0)
    if seg is not None:
        t = t & (seg - 1)
    return jnp.where(t >= d, pltpu.roll(x, d, 0), fill)


def _cumsum_rows(x, seg=None):
    d = 1
    while d < (seg or x.shape[0]):
        x = x + _shift_rows(x, d, 0.0, seg)
        d *= 2
    return x


def _swap_half_heads(x):
    n = x.shape[1]
    lower = (_iota(x.shape, 1) & (HEAD_DIM - 1)) < HEAD_DIM // 2
    return jnp.where(lower, pltpu.roll(x, n - HEAD_DIM // 2, 1), pltpu.roll(x, HEAD_DIM // 2, 1))


def _inv_unit_lower(mats, c=None):
    shape = mats[0].shape
    c = c or shape[0]
    row = _iota(shape, 0)
    col = _iota(shape, 1)
    nb = min(INV_BASE, c)
    shift = nb.bit_length() - 1
    in_base = (row >> shift) == (col >> shift)
    ps = [jnp.where(in_base, a, 0.0) for a in mats]
    eye = jnp.where(row == col, 1.0, 0.0)
    xs = [eye - d for d in ps]
    k = 2
    while k < nb:
        ps = [_mm(p, p) for p in ps]
        xs = [x + _mm(x, p) for x, p in zip(xs, ps)]
        k *= 2
    m = nb
    while m < c:
        s1 = m.bit_length() - 1
        in_off = ((row >> (s1 + 1)) == (col >> (s1 + 1))) & ((row >> s1) != (col >> s1))
        ys = [_mm(x, jnp.where(in_off, a, 0.0)) for x, a in zip(xs, mats)]
        xs = [x - _mm(y, x) for x, y in zip(xs, ys)]
        m *= 2
    return xs


def _solve_unit_lower(mats, tinvs, ws):
    us = [_mm(t, w) for t, w in zip(tinvs, ws)]
    rs = [w - (u + _mm(a, u, passes=RESID_PASSES)) for a, u, w in zip(mats, us, ws)]
    return [u + _mm(t, r) for u, t, r in zip(us, tinvs, rs)]


def _pair_specs(na, tm, d):
    return [pl.BlockSpec((tm, d), lambda i: (jnp.minimum(i, na - 1), 0)),
            pl.BlockSpec((tm, d), lambda i: (jnp.maximum(i - na, 0), 0))]


def _pair_tile(na, a_ref, b_ref):
    return jnp.where(pl.program_id(0) < na, a_ref[...], b_ref[...])


def _norm_win_kernel(na, xa_ref, xb_ref, g_ref, w_ref, z_ref):
    x = _pair_tile(na, xa_ref, xb_ref)
    h = x * lax.rsqrt(jnp.mean(x * x, axis=-1, keepdims=True) + EPS) * g_ref[...]
    z_ref[...] = jnp.dot(h.astype(BF16), w_ref[...], preferred_element_type=F32)


def _norm_win(x_pair, g, w_bf16, tm):
    xa, xb = x_pair
    d = xa.shape[1]
    na, t = xa.shape[0] // tm, xa.shape[0] + xb.shape[0]
    nz = w_bf16.shape[1]
    return pl.pallas_call(
        functools.partial(_norm_win_kernel, na),
        grid=(t // tm,),
        in_specs=_pair_specs(na, tm, d) + [pl.BlockSpec((1, d), lambda i: (0, 0)),
                                           pl.BlockSpec((d, nz), lambda i: (0, 0))],
        out_specs=pl.BlockSpec((tm, nz), lambda i: (i, 0)),
        out_shape=jax.ShapeDtypeStruct((t, nz), F32),
        compiler_params=pltpu.CompilerParams(dimension_semantics=("arbitrary",), vmem_limit_bytes=VMEM_LIMIT),
        name="norm_win",
    )(xa, xb, g, w_bf16)


def _retention_tile(zq, zk, zv, cos, sin, intra_ref, qdec, kdec, sdec, s_ref):
    rq = zq * cos + _swap_half_heads(zq) * sin
    rk = (zk * cos + _swap_half_heads(zk) * sin) * (HEAD_DIM ** -0.5)
    kd = rk * kdec
    heads = range(N_HEADS)
    sls = [slice(h * HEAD_DIM, (h + 1) * HEAD_DIM) for h in heads]
    ss = [s_ref[h] for h in heads]
    scores = [_mm(rq[:, sl], rk[:, sl], NT) * intra_ref[h] for h, sl in zip(heads, sls)]
    os_ = [_mm(sc, zv[:, sl]) + _mm(rq[:, sl], s) * qdec[:, sl] for sc, s, sl in zip(scores, ss, sls)]
    s_new = [s * sdec[h:h + 1, :HEAD_DIM] + _mm(kd[:, sl], zv[:, sl], TN) for h, s, sl in zip(heads, ss, sls)]
    for h in heads:
        s_ref[h] = s_new[h]
    outs = []
    for o in os_:
        mu = jnp.mean(o, axis=-1, keepdims=True)
        oc = o - mu
        var = jnp.mean(oc * oc, axis=-1, keepdims=True)
        outs.append(oc * lax.rsqrt(var + EPS))
    return jnp.concatenate(outs, axis=-1)


def _split3(x):
    x0 = x.astype(BF16)
    r1 = x - x0.astype(F32)
    x1 = r1.astype(BF16)
    x2 = (r1 - x1.astype(F32)).astype(BF16)
    return x0, x1, x2


def _gdn_tile(gq, gk, gv, gab, s_ref):
    c = gq.shape[0]
    gc = _cumsum_rows(gab)
    egc = jnp.exp(gc)
    row = _iota((c, c), 0)
    col = _iota((c, c), 1)
    incl = row >= col
    strict = row > col
    heads = range(N_HEADS)
    sls = [slice(h * HEAD_DIM, (h + 1) * HEAD_DIM) for h in heads]
    ss = [s_ref[h] for h in heads]
    gcols = [gc[:, h:h + 1] for h in heads]
    bcols = [gab[:, N_HEADS + h:N_HEADS + h + 1] for h in heads]
    ecols = [egc[:, h:h + 1] for h in heads]
    gct = gc.T
    decs = [jnp.where(incl, jnp.exp(jnp.where(incl, gcols[h] - gct[h:h + 1, :], 0.0)), 0.0) for h in heads]
    mats = [jnp.where(strict, bcols[h] * decs[h] * _mm(gk[:, sl], gk[:, sl], NT), 0.0) for h, sl in zip(heads, sls)]
    tinvs = _inv_unit_lower(mats)
    ws = [bcols[h] * (gv[:, sl] - ecols[h] * _mm(gk[:, sl], ss[h])) for h, sl in zip(heads, sls)]
    us = _solve_unit_lower(mats, tinvs, ws)
    qks = [_mm(gq[:, sl], gk[:, sl], NT) * decs[h] for h, sl in zip(heads, sls)]
    outs = [ecols[h] * _mm(gq[:, sl], ss[h]) + _mm(qks[h], us[h]) for h, sl in zip(heads, sls)]
    glasts = [gc[c - 1:c, h:h + 1] for h in heads]
    s_new = [jnp.exp(glasts[h]) * ss[h] + _mm(jnp.exp(glasts[h] - gcols[h]) * gk[:, sl], us[h], TN)
             for h, sl in zip(heads, sls)]
    for h in heads:
        s_ref[h] = s_new[h]
    return outs


def _mixer_kernel(nt, ts, nb, pos_base,
                  z_ref, cos_ref, sin_ref, intra_ref, qdec_ref, kdec_ref, sdec_ref, pv_ref, gcw_ref, gabp_ref,
                  wa_ref, wx_ref, pw_ref, sret0, sgdn0, gconv0, lru0, lconv0, pool0,
                  mix_ref, ret_o, gdn_o, gconv_o, lru_o, lconv_o, pool_o,
                  s_ret, s_gdn, ext_g, ext_l, ext_p, h_lru):
    n = pl.program_id(1)
    g = GROUP_W
    base = HIST - (CONV_W - 1)

    @pl.when(n == 0)
    def _():
        s_ret[...] = sret0[...]
        s_gdn[...] = sgdn0[...]
        ext_g[:, 0:HIST, :] = jnp.zeros((nb, HIST, 3 * g), F32)
        ext_g[:, base:HIST, :] = gconv0[...]
        ext_l[:, 0:HIST, :] = jnp.zeros((nb, HIST, g), F32)
        ext_l[:, base:HIST, :] = lconv0[...]
        ext_p[:, 0:HIST, :] = jnp.zeros((nb, HIST, g), F32)
        ext_p[:, HIST - POOL_BUF:HIST, :] = pool0[...]
        h_lru[...] = lru0[...]

    pv = pv_ref[...]
    ret_gn_g, gdn_norm_g = pv[0:1], pv[1:2]
    lru_conv_b, lru_ba, lru_bx, lru_lambda = pv[2:3], pv[3:4], pv[4:5], pv[5:6]
    pool_b, pool_scale = pv[6:7], pv[7:8]
    lru_conv_w = pv[8:12]
    gcw = gcw_ref[...]
    gabp = gabp_ref[...]

    for j in range(nb):
        rows = slice(j * ts, (j + 1) * ts)
        eg, el, ep = ext_g.at[j], ext_l.at[j], ext_p.at[j]

        ro = _retention_tile(z_ref[rows, 0:g], z_ref[rows, g:2 * g], z_ref[rows, 2 * g:3 * g], cos_ref[...],
                             sin_ref[...], intra_ref, qdec_ref[...], kdec_ref[...], sdec_ref[...], s_ret.at[j])
        mix_ref[rows, 0:g] = ro * ret_gn_g * _silu(z_ref[rows, 3 * g:4 * g])

        eg[HIST:HIST + ts, :] = z_ref[rows, 4 * g:7 * g]
        y = eg[base:base + ts, :] * gcw[0:1]
        for i in range(1, CONV_W):
            y = y + eg[base + i:base + i + ts, :] * gcw[i:i + 1]
        qkv = _silu(y)
        zab = z_ref[rows, Z_AB:Z_AB + LANES]
        lane = _iota(zab.shape, 1)
        log_a = -jnp.exp(gabp[0:1]) * _softplus(zab + gabp[1:2])
        gab = jnp.where(lane < N_HEADS, log_a, jnp.where(lane < 2 * N_HEADS, _sigmoid(zab), 0.0))
        gq_parts, gk_parts = [], []
        for h in range(N_HEADS):
            qh = qkv[:, h * HEAD_DIM:(h + 1) * HEAD_DIM]
            kh = qkv[:, g + h * HEAD_DIM:g + (h + 1) * HEAD_DIM]
            gq_parts.append(qh * lax.rsqrt(jnp.sum(qh * qh, axis=-1, keepdims=True) + EPS) * (HEAD_DIM ** -0.5))
            gk_parts.append(kh * lax.rsqrt(jnp.sum(kh * kh, axis=-1, keepdims=True) + EPS))
        go = _gdn_tile(jnp.concatenate(gq_parts, axis=-1), jnp.concatenate(gk_parts, axis=-1),
                       qkv[:, 2 * g:3 * g], gab, s_gdn.at[j])
        go = [o * lax.rsqrt(jnp.mean(o * o, axis=-1, keepdims=True) + EPS) for o in go]
        mix_ref[rows, g:2 * g] = jnp.concatenate(go, axis=-1) * gdn_norm_g * _silu(z_ref[rows, 7 * g:8 * g])

        el[HIST:HIST + ts, :] = z_ref[rows, 8 * g:9 * g]
        xc = el[base:base + ts, :] * lru_conv_w[0:1]
        for i in range(1, CONV_W):
            xc = xc + el[base + i:base + i + ts, :] * lru_conv_w[i:i + 1]
        xc = xc + lru_conv_b
        r = _sigmoid(jnp.dot(xc.astype(BF16), wa_ref[...], preferred_element_type=F32) + lru_ba)
        gate_i = _sigmoid(jnp.dot(xc.astype(BF16), wx_ref[...], preferred_element_type=F32) + lru_bx)
        log_at = -LRU_C * r * _softplus(-lru_lambda)
        av = jnp.exp(log_at)
        bv = jnp.sqrt(-jnp.tanh(log_at) * (av * av + 1.0)) * (gate_i * xc)
        bv = bv + jnp.where(_iota(bv.shape, 0) == 0, av * h_lru[j], 0.0)
        d = 1
        while d < ts:
            bv = av * _shift_rows(bv, d, 0.0) + bv
            av = av * _shift_rows(av, d, 1.0)
            d *= 2
        h_lru[j] = bv[ts - 1:ts, :]
        mix_ref[rows, 2 * g:3 * g] = bv * _gelu_tanh(z_ref[rows, 9 * g:10 * g])

        u = z_ref[rows, 10 * g:11 * g]
        ep[HIST:HIST + ts, :] = u
        e = ep[...]
        sums = []
        step = 1
        for _ in POOL_WINDOWS:
            e = e + pltpu.roll(e, step, 0)
            sums.append(e[HIST:HIST + ts, :])
            step *= 2
        pos1 = (pos_base + 1 + n * ts + _iota((ts, g), 0)).astype(F32)
        lane_g = _iota((ts, g), 1) >> 6
        pooled = sums[-1] / jnp.minimum(pos1, float(POOL_WINDOWS[-1]))
        for gi in range(len(POOL_WINDOWS) - 2, -1, -1):
            pooled = jnp.where(lane_g == gi, sums[gi] / jnp.minimum(pos1, float(POOL_WINDOWS[gi])), pooled)
        pooled = pooled - u
        yp = jnp.dot(pooled.astype(BF16), pw_ref[...], preferred_element_type=F32) + pool_b
        mix_ref[rows, 3 * g:4 * g] = yp * pool_scale

        tail_g = eg[ts:ts + HIST, :]
        tail_l = el[ts:ts + HIST, :]
        tail_p = ep[ts:ts + HIST, :]
        eg[0:HIST, :] = tail_g
        el[0:HIST, :] = tail_l
        ep[0:HIST, :] = tail_p

    @pl.when(n == nt - 1)
    def _():
        ret_o[...] = s_ret[...]
        gdn_o[...] = s_gdn[...]
        gconv_o[...] = ext_g[:, base:HIST, :]
        lconv_o[...] = ext_l[:, base:HIST, :]
        pool_o[...] = ext_p[:, HIST - POOL_BUF:HIST, :]
        lru_o[...] = h_lru[...]


def _decode_mixer_kernel(ts, nb, pos_base,
                         z_ref, cos_ref, sin_ref, intra_ref, qdec_ref, kdec_ref, sdec_ref, pv_ref, gcw_ref, gabp_ref,
                         wa_ref, wx_ref, pw_ref, sret0, sgdn0, gconv0, lru0, lconv0, pool0,
                         mix_ref, ret_o, gdn_o, gconv_o, lru_o, lconv_o, pool_o,
                         ext_g, ext_l, ext_p):
    g = GROUP_W
    r_all = nb * ts
    base = HIST - (CONV_W - 1)
    seg_shift = ts.bit_length() - 1
    row2 = _iota((r_all, r_all), 0)
    col2 = _iota((r_all, r_all), 1)
    same = (row2 >> seg_shift) == (col2 >> seg_shift)
    incl = same & (row2 >= col2)
    strict = same & (row2 > col2)
    seq_l = _iota((r_all, LANES), 0) >> seg_shift
    half_l = _iota((r_all, LANES), 1) >> 6
    exp_masks = [((seq_l >> 1) == c) & ((seq_l & 1) == half_l) for c in range(nb // 2)]

    def expand(x):
        x2 = jnp.concatenate([x, x], axis=-1)
        return jnp.concatenate([jnp.where(m, x2, 0.0) for m in exp_masks], axis=-1)

    def stacked(state_ref, h):
        return jnp.concatenate([state_ref[j, h] for j in range(nb)], axis=0)

    def unstack(out_ref, h, sv):
        for j in range(nb):
            out_ref[j, h] = sv[j * HEAD_DIM:(j + 1) * HEAD_DIM, :]

    pv = pv_ref[...]
    ret_gn_g, gdn_norm_g = pv[0:1], pv[1:2]
    lru_conv_b, lru_ba, lru_bx, lru_lambda = pv[2:3], pv[3:4], pv[4:5], pv[5:6]
    pool_b, pool_scale = pv[6:7], pv[7:8]
    lru_conv_w = pv[8:12]
    gcw = gcw_ref[...]
    gabp = gabp_ref[...]

    ext_g[:, 0:HIST, :] = jnp.zeros((nb, HIST, 3 * g), F32)
    ext_g[:, base:HIST, :] = gconv0[...]
    ext_l[:, 0:HIST, :] = jnp.zeros((nb, HIST, g), F32)
    ext_l[:, base:HIST, :] = lconv0[...]
    ext_p[:, 0:HIST, :] = jnp.zeros((nb, HIST, g), F32)
    ext_p[:, HIST - POOL_BUF:HIST, :] = pool0[...]
    y_parts, xc_parts, sum_parts, h0_parts = [], [], [[] for _ in POOL_WINDOWS], []
    for j in range(nb):
        rows = slice(j * ts, (j + 1) * ts)
        eg, el, ep = ext_g.at[j], ext_l.at[j], ext_p.at[j]
        eg[HIST:HIST + ts, :] = z_ref[rows, 4 * g:7 * g]
        el[HIST:HIST + ts, :] = z_ref[rows, 8 * g:9 * g]
        ep[HIST:HIST + ts, :] = z_ref[rows, 10 * g:11 * g]
        y = eg[base:base + ts, :] * gcw[0:1]
        xc = el[base:base + ts, :] * lru_conv_w[0:1]
        for i in range(1, CONV_W):
            y = y + eg[base + i:base + i + ts, :] * gcw[i:i + 1]
            xc = xc + el[base + i:base + i + ts, :] * lru_conv_w[i:i + 1]
        y_parts.append(y)
        xc_parts.append(xc)
        e = ep[...]
        step = 1
        for wi in range(len(POOL_WINDOWS)):
            e = e + pltpu.roll(e, step, 0)
            sum_parts[wi].append(e[HIST:HIST + ts, :])
            step *= 2
        h0_parts.append(jnp.broadcast_to(lru0[j], (ts, g)))
        gconv_o[j] = eg[ts + base:ts + HIST, :]
        lconv_o[j] = el[ts + base:ts + HIST, :]
        pool_o[j] = ep[ts + HIST - POOL_BUF:ts + HIST, :]

    zq, zk, zv = z_ref[:, 0:g], z_ref[:, g:2 * g], z_ref[:, 2 * g:3 * g]
    cos, sin = cos_ref[...], sin_ref[...]
    qdec, kdec, sdec = qdec_ref[...], kdec_ref[...], sdec_ref[...]
    rq = zq * cos + _swap_half_heads(zq) * sin
    rk = (zk * cos + _swap_half_heads(zk) * sin) * (HEAD_DIM ** -0.5)
    kd = rk * kdec
    outs = []
    for h in range(N_HEADS):
        sl = slice(h * HEAD_DIM, (h + 1) * HEAD_DIM)
        q, k, v = rq[:, sl], rk[:, sl], zv[:, sl]
        sv = stacked(sret0, h)
        scores = _mm(q, k, NT) * intra_ref[h]
        o = _mm(scores, v) + _mm(expand(q), sv) * qdec[:, sl]
        unstack(ret_o, h, sv * sdec[h:h + 1, :HEAD_DIM] + _mm(expand(kd[:, sl]), v, TN))
        mu = jnp.mean(o, axis=-1, keepdims=True)
        oc = o - mu
        var = jnp.mean(oc * oc, axis=-1, keepdims=True)
        outs.append(oc * lax.rsqrt(var + EPS))
    mix_ref[:, 0:g] = jnp.concatenate(outs, axis=-1) * ret_gn_g * _silu(z_ref[:, 3 * g:4 * g])

    qkv = _silu(jnp.concatenate(y_parts, axis=0))
    zab = z_ref[:, Z_AB:Z_AB + LANES]
    lane = _iota(zab.shape, 1)
    log_a = -jnp.exp(gabp[0:1]) * _softplus(zab + gabp[1:2])
    gab = jnp.where(lane < N_HEADS, log_a, jnp.where(lane < 2 * N_HEADS, _sigmoid(zab), 0.0))
    gc = _cumsum_rows(gab, seg=ts)
    egc = jnp.exp(gc)
    gct = gc.T
    dotnn = functools.partial(lax.dot_general, dimension_numbers=NN, preferred_element_type=F32)
    same_b = jnp.where(same, 1.0, 0.0).astype(BF16)
    a0, a1, a2 = _split3(gab)
    glast = dotnn(same_b, a0) + (dotnn(same_b, a1) + dotnn(same_b, a2))
    pick = jnp.where(((_iota((nb * HEAD_DIM, r_all), 0) >> 6) == (_iota((nb * HEAD_DIM, r_all), 1) >> seg_shift))
                     & ((_iota((nb * HEAD_DIM, r_all), 1) & (ts - 1)) == 0), 1.0, 0.0).astype(BF16)
    e0, e1, e2 = _split3(jnp.exp(glast))
    eglast_s = dotnn(pick, e0) + (dotnn(pick, e1) + dotnn(pick, e2))
    heads = range(N_HEADS)
    qs, ks, vs, decs = [], [], [], []
    for h in heads:
        qh = qkv[:, h * HEAD_DIM:(h + 1) * HEAD_DIM]
        kh = qkv[:, g + h * HEAD_DIM:g + (h + 1) * HEAD_DIM]
        qs.append(qh * lax.rsqrt(jnp.sum(qh * qh, axis=-1, keepdims=True) + EPS) * (HEAD_DIM ** -0.5))
        ks.append(kh * lax.rsqrt(jnp.sum(kh * kh, axis=-1, keepdims=True) + EPS))
        vs.append(qkv[:, 2 * g + h * HEAD_DIM:2 * g + (h + 1) * HEAD_DIM])
        decs.append(jnp.where(incl, jnp.exp(jnp.where(incl, gc[:, h:h + 1] - gct[h:h + 1, :], 0.0)), 0.0))
    gcols = [gc[:, h:h + 1] for h in heads]
    bcols = [gab[:, N_HEADS + h:N_HEADS + h + 1] for h in heads]
    ecols = [egc[:, h:h + 1] for h in heads]
    svs = [stacked(sgdn0, h) for h in heads]
    mats = [jnp.where(strict, bcols[h] * decs[h] * _mm(ks[h], ks[h], NT), 0.0) for h in heads]
    tinvs = _inv_unit_lower(mats, ts)
    ws = [bcols[h] * (vs[h] - ecols[h] * _mm(expand(ks[h]), svs[h])) for h in heads]
    us = _solve_unit_lower(mats, tinvs, ws)
    os_ = [ecols[h] * _mm(expand(qs[h]), svs[h]) + _mm(_mm(qs[h], ks[h], NT) * decs[h], us[h]) for h in heads]
    for h in heads:
        kdx = jnp.exp(glast[:, h:h + 1] - gcols[h]) * ks[h]
        unstack(gdn_o, h, eglast_s[:, h:h + 1] * svs[h] + _mm(expand(kdx), us[h], TN))
    outs = [o * lax.rsqrt(jnp.mean(o * o, axis=-1, keepdims=True) + EPS) for o in os_]
    mix_ref[:, g:2 * g] = jnp.concatenate(outs, axis=-1) * gdn_norm_g * _silu(z_ref[:, 7 * g:8 * g])

    xc = jnp.concatenate(xc_parts, axis=0) + lru_conv_b
    r = _sigmoid(jnp.dot(xc.astype(BF16), wa_ref[...], preferred_element_type=F32) + lru_ba)
    gate_i = _sigmoid(jnp.dot(xc.astype(BF16), wx_ref[...], preferred_element_type=F32) + lru_bx)
    log_at = -LRU_C * r * _softplus(-lru_lambda)
    av = jnp.exp(log_at)
    bv = jnp.sqrt(-jnp.tanh(log_at) * (av * av + 1.0)) * (gate_i * xc)
    first = (_iota(bv.shape, 0) & (ts - 1)) == 0
    bv = bv + jnp.where(first, av * jnp.concatenate(h0_parts, axis=0), 0.0)
    d = 1
    while d < ts:
        bv = av * _shift_rows(bv, d, 0.0, ts) + bv
        av = av * _shift_rows(av, d, 1.0, ts)
        d *= 2
    for j in range(nb):
        lru_o[j] = bv[(j + 1) * ts - 1:(j + 1) * ts, :]
    mix_ref[:, 2 * g:3 * g] = bv * _gelu_tanh(z_ref[:, 9 * g:10 * g])

    u_in = z_ref[:, 10 * g:11 * g]
    pos1 = (pos_base + 1 + (_iota((r_all, g), 0) & (ts - 1))).astype(F32)
    lane_g = _iota((r_all, g), 1) >> 6
    sums = [jnp.concatenate(parts, axis=0) for parts in sum_parts]
    pooled = sums[-1] / jnp.minimum(pos1, float(POOL_WINDOWS[-1]))
    for gi in range(len(POOL_WINDOWS) - 2, -1, -1):
        pooled = jnp.where(lane_g == gi, sums[gi] / jnp.minimum(pos1, float(POOL_WINDOWS[gi])), pooled)
    pooled = pooled - u_in
    yp = jnp.dot(pooled.astype(BF16), pw_ref[...], preferred_element_type=F32) + pool_b
    mix_ref[:, 3 * g:4 * g] = yp * pool_scale


N_STATES = 6


def _with_state_stack(body, nt, n_prev, n_in, *refs):
    ins = refs[:n_in]
    prevs = refs[n_in:n_in + (N_STATES if n_prev else 0)]
    k = n_in + len(prevs)
    mix_ref, stacks, scratch = refs[k], refs[k + 1:k + 1 + N_STATES], refs[k + 1 + N_STATES:]

    @pl.when(pl.program_id(1) == nt - 1)
    def _():
        for o, p in zip(stacks, prevs):
            o[0:n_prev] = p[...]

    body(*ins, mix_ref, *[o.at[n_prev] for o in stacks], *scratch)


def _ret_consts(c):
    log_g = jnp.log1p(-jnp.exp2(-5.0 - jnp.arange(N_HEADS, dtype=F32)))
    idx = jnp.arange(c, dtype=F32)
    diff = idx[:, None] - idx[None, :]
    intra = jnp.where(diff >= 0, jnp.exp(log_g[:, None, None] * jnp.maximum(diff, 0.0)), 0.0)
    qdec = jnp.exp(log_g[:, None] * (idx + 1.0))
    kdec = jnp.exp(log_g[:, None] * (c - 1.0 - idx))
    sdec = jnp.exp(log_g * c)
    expand = lambda t: jnp.repeat(t.T, HEAD_DIM, axis=1)
    sdec_tab = jnp.zeros((8, LANES), F32).at[:N_HEADS, :].set(sdec[:, None])
    return intra, expand(qdec), expand(kdec), sdec_tab


def _rope_tables(pos):
    half = HEAD_DIM // 2
    inv = ROPE_BASE ** (-jnp.arange(half, dtype=F32) / half)
    ang = pos.astype(F32)[:, None] * inv[None, :]
    cos, sin = jnp.cos(ang), jnp.sin(ang)
    cos_t = jnp.tile(jnp.concatenate([cos, cos], axis=-1), (1, N_HEADS))
    sin_t = jnp.tile(jnp.concatenate([-sin, sin], axis=-1), (1, N_HEADS))
    return cos_t, sin_t


def _mixer(z, z_row0, b, l, ts, nb, pos_base, states, layer, prev, lp):
    nt = l // ts
    assert nb == 1 or (nt == 1 and nb % 2 == 0 and ts == INV_BASE)
    g = GROUP_W
    pos = pos_base + jnp.arange(l, dtype=jnp.int32)
    cos_t, sin_t = _rope_tables(pos)
    intra, qdec, kdec, sdec = _ret_consts(ts)
    rt_ = nb * ts
    if nb > 1:
        cos_t, sin_t, qdec, kdec = (jnp.tile(a, (nb, 1)) for a in (cos_t, sin_t, qdec, kdec))
        intra = jnp.einsum("ab,hij->haibj", jnp.eye(nb, dtype=F32), intra).reshape(N_HEADS, rt_, rt_)
    s_ret0, s_gdn0, gconv0, lru0, lconv0, pool0 = states
    blk0 = z_row0 // rt_
    const2 = lambda bi, n: (0, 0)
    per_b3 = lambda bi, n: (bi, 0, 0)
    per_b4 = lambda bi, n: (bi, 0, 0, 0)
    st3 = lambda bi, n: (layer, bi, 0, 0)
    st4 = lambda bi, n: (layer, bi, 0, 0, 0)
    in_specs = [
        pl.BlockSpec((rt_, NZ), lambda bi, n: (blk0 + bi * nt + n, 0)),
        pl.BlockSpec((rt_, g), lambda bi, n: (n, 0)),
        pl.BlockSpec((rt_, g), lambda bi, n: (n, 0)),
        pl.BlockSpec((N_HEADS, rt_, rt_), lambda bi, n: (0, 0, 0)),
        pl.BlockSpec((rt_, g), const2),
        pl.BlockSpec((rt_, g), const2),
        pl.BlockSpec((8, LANES), const2),
        pl.BlockSpec((16, g), const2),
        pl.BlockSpec((CONV_W, 3 * g), const2),
        pl.BlockSpec((8, LANES), const2),
        pl.BlockSpec((g, g), const2),
        pl.BlockSpec((g, g), const2),
        pl.BlockSpec((g, g), const2),
        pl.BlockSpec((None, nb, N_HEADS, HEAD_DIM, HEAD_DIM), st4),
        pl.BlockSpec((None, nb, N_HEADS, HEAD_DIM, HEAD_DIM), st4),
        pl.BlockSpec((None, nb, CONV_W - 1, 3 * g), st3),
        pl.BlockSpec((None, nb, 1, g), st3),
        pl.BlockSpec((None, nb, CONV_W - 1, g), st3),
        pl.BlockSpec((None, nb, POOL_BUF, g), st3),
    ]
    st_shapes = [(N_HEADS, HEAD_DIM, HEAD_DIM), (N_HEADS, HEAD_DIM, HEAD_DIM), (CONV_W - 1, 3 * g), (1, g),
                 (CONV_W - 1, g), (POOL_BUF, g)]
    n_prev = 0 if prev is None else prev[0].shape[0]
    stack_spec = lambda layers, s: pl.BlockSpec((layers, nb) + s, lambda bi, n: (0, bi) + (0,) * len(s))
    n_in = len(in_specs)
    if n_prev:
        in_specs = in_specs + [stack_spec(n_prev, s) for s in st_shapes]
    out_specs = [pl.BlockSpec((rt_, 4 * g), lambda bi, n: (bi * nt + n, 0))] + [
        stack_spec(n_prev + 1, s) for s in st_shapes]
    out_shape = [jax.ShapeDtypeStruct((b * l, 4 * g), F32)] + [
        jax.ShapeDtypeStruct((n_prev + 1, b) + s, F32) for s in st_shapes]
    ext = [pltpu.VMEM((nb, ts + HIST, 3 * g), F32), pltpu.VMEM((nb, ts + HIST, g), F32),
           pltpu.VMEM((nb, ts + HIST, g), F32)]
    if nb > 1:
        body = functools.partial(_decode_mixer_kernel, ts, nb, pos_base)
        scratch = ext
    else:
        body = functools.partial(_mixer_kernel, nt, ts, nb, pos_base)
        scratch = [pltpu.VMEM((nb, N_HEADS, HEAD_DIM, HEAD_DIM), F32),
                   pltpu.VMEM((nb, N_HEADS, HEAD_DIM, HEAD_DIM), F32)] + ext + [pltpu.VMEM((nb, 1, g), F32)]
    outs = pl.pallas_call(
        functools.partial(_with_state_stack, body, nt, n_prev, n_in),
        grid=(b // nb, nt),
        in_specs=in_specs,
        out_specs=out_specs,
        out_shape=out_shape,
        scratch_shapes=scratch,
        compiler_params=pltpu.CompilerParams(dimension_semantics=("arbitrary", "arbitrary"),
                                             vmem_limit_bytes=VMEM_LIMIT),
        name=f"mixer_ts{ts}",
    )(z, cos_t, sin_t, intra, qdec, kdec, sdec, lp["pvec"], lp["gdn_conv_w"], lp["gabp"],
      lp["lru_wa"], lp["lru_wx"], lp["pool_w"], s_ret0, s_gdn0, gconv0, lru0[:, :, None, :], lconv0, pool0,
      *(prev or ()))
    return outs[0], tuple(outs[1:])


def _to_row_tiles(ref, x):
    m = x.shape[0]
    ref[...] = x.reshape(m // SUBLANES, SUBLANES, SUBLANES, LANES)


def _from_row_tiles(ref):
    return ref[...].reshape(ref.shape[0] * SUBLANES, SUBLANES * LANES)


def _row_tile_shape(m):
    return (m // SUBLANES, SUBLANES, SUBLANES, LANES)


def _wout_route_kernel(na, ma_ref, mb_ref, xa_ref, xb_ref, wo_ref, g_ref, wr_ref, br_ref, x1_ref, h2_ref, rt_ref,
                       cnt_ref):
    mix = _pair_tile(na, ma_ref, mb_ref)
    x1 = _pair_tile(na, xa_ref, xb_ref) + jnp.dot(mix.astype(BF16), wo_ref[...], preferred_element_type=F32)
    x1_ref[...] = x1
    h = x1 * lax.rsqrt(jnp.mean(x1 * x1, axis=-1, keepdims=True) + EPS) * g_ref[...]
    _to_row_tiles(h2_ref, h)
    logits = _mm(h, wr_ref[...], passes=3) + br_ref[...]
    lane = _iota(logits.shape, 1)
    lane_f = lane.astype(F32)
    neg = -1e30
    far = 1e9
    is_g = lane < N_GROUPS
    gl = jnp.where(is_g, logits, neg)
    gmax = jnp.max(gl, axis=-1, keepdims=True)
    gsel = jnp.min(jnp.where(gl == gmax, lane_f, far), axis=-1, keepdims=True)
    p_g = 1.0 / jnp.sum(jnp.where(is_g, jnp.exp(gl - gmax), 0.0), axis=-1, keepdims=True)
    e_group = ((lane - N_GROUPS) >> 3).astype(F32)
    in_group = jnp.where(lane >= N_GROUPS, e_group, -1.0) == gsel
    el = jnp.where(in_group, logits, neg)
    m1 = jnp.max(el, axis=-1, keepdims=True)
    i1 = jnp.min(jnp.where(el == m1, lane_f, far), axis=-1, keepdims=True)
    el2 = jnp.where(lane_f == i1, neg, el)
    m2 = jnp.max(el2, axis=-1, keepdims=True)
    i2 = jnp.min(jnp.where(el2 == m2, lane_f, far), axis=-1, keepdims=True)
    e2 = jnp.exp(m2 - m1)
    w1 = p_g / (1.0 + e2)
    w2 = p_g * e2 / (1.0 + e2)
    rt = jnp.where(lane == 0, i1 - N_GROUPS,
                   jnp.where(lane == 1, i2 - N_GROUPS, jnp.where(lane == 2, w1, jnp.where(lane == 3, w2, 0.0))))
    rt_ref[...] = rt

    picked = jnp.where((lane_f + N_GROUPS == i1) | (lane_f + N_GROUPS == i2), 1.0, 0.0)

    @pl.when(pl.program_id(0) == 0)
    def _():
        cnt_ref[...] = jnp.zeros(cnt_ref.shape, F32)

    cnt_ref[0:1, :] = cnt_ref[0:1, :] + jnp.sum(picked, axis=0, keepdims=True)


def _wout_route(mix_pair, x_pair, wo_bf16, g, wr, br, tm):
    d = x_pair[0].shape[1]
    na, t = x_pair[0].shape[0] // tm, x_pair[0].shape[0] + x_pair[1].shape[0]
    row = lambda i: (i, 0)
    const = lambda i: (0, 0)
    return pl.pallas_call(
        functools.partial(_wout_route_kernel, na),
        grid=(t // tm,),
        in_specs=_pair_specs(na, tm, d) + _pair_specs(na, tm, d) + [
            pl.BlockSpec((d, d), const), pl.BlockSpec((1, d), const), pl.BlockSpec((d, LANES), const),
            pl.BlockSpec((1, LANES), const)],
        out_specs=[pl.BlockSpec((tm, d), row), pl.BlockSpec(_row_tile_shape(tm), lambda i: (i, 0, 0, 0)),
                   pl.BlockSpec((tm, LANES), row), pl.BlockSpec((SUBLANES, LANES), const)],
        out_shape=[jax.ShapeDtypeStruct((t, d), F32), jax.ShapeDtypeStruct(_row_tile_shape(t), F32),
                   jax.ShapeDtypeStruct((t, LANES), F32), jax.ShapeDtypeStruct((SUBLANES, LANES), F32)],
        compiler_params=pltpu.CompilerParams(dimension_semantics=("arbitrary",), vmem_limit_bytes=VMEM_LIMIT),
        name="wout_route",
    )(*mix_pair, *x_pair, wo_bf16, g, wr, br)


def _tile_of_row(ref4, grp, k):
    return ref4.at[grp, pl.ds(k, 1)]


def _for_each_row(tm, fn):
    def body(grp, c):
        for k in range(SUBLANES):
            fn(grp, k, grp * SUBLANES + k)
        return c

    lax.fori_loop(0, tm // SUBLANES, body, 0)


def _route_plan_kernel(tr, tm, n_tiles, cnt_ref, rt_ref, pos_ref, tmeta_ref, emeta_ref, start_s, carry_s):
    i = pl.program_id(0)
    rt = rt_ref[...]
    lane = _iota((tr, LANES), 1)
    lane_f = lane.astype(F32)
    oh1 = jnp.where(lane_f == rt[:, 0:1], 1.0, 0.0)
    oh2 = jnp.where(lane_f == rt[:, 1:2], 1.0, 0.0)
    oh = oh1 + oh2

    @pl.when(i == 0)
    def _():
        cnt = cnt_ref[0:1, :]
        ntile = jnp.floor((cnt + (tm - 1.0)) * (1.0 / tm))
        before = jnp.where(_iota((LANES, LANES), 0) < _iota((LANES, LANES), 1), 1.0, 0.0)
        st_tiles = _mm(jnp.broadcast_to(ntile, (SUBLANES, LANES)), before)[0:1]
        end_tiles = st_tiles + ntile
        start_s[...] = st_tiles * tm
        carry_s[...] = jnp.zeros((1, LANES), F32)
        is_e1 = _iota((1, LANES), 1) < N_EXPERTS
        n_used = jnp.sum(jnp.where(is_e1, ntile, 0.0), axis=-1, keepdims=True)
        last = jnp.sum(jnp.where(is_e1 & (end_tiles <= n_used - 1.0), 1.0, 0.0), axis=-1, keepdims=True)
        ti = _iota((n_tiles, LANES), 0).astype(F32)
        lane_t = _iota((n_tiles, LANES), 1)
        te = jnp.sum(jnp.where((lane_t < N_EXPERTS) & (end_tiles <= ti), 1.0, 0.0), axis=-1, keepdims=True)
        te = jnp.minimum(te, last)
        tmeta_ref[...] = jnp.where(lane_t == 0, te, jnp.where(lane_t == 1, n_used, 0.0))
        srow = _iota((SUBLANES, LANES), 0)
        emeta_ref[...] = jnp.where(srow == 0, ntile, jnp.where(srow == 1, (end_tiles - 1.0) * tm, 0.0))

    earlier = jnp.where(_iota((tr, tr), 0) > _iota((tr, tr), 1), 1.0, 0.0)
    base = start_s[...] + carry_s[...] + _mm(earlier, oh)
    p1 = jnp.sum(oh1 * base, axis=-1, keepdims=True)
    p2 = jnp.sum(oh2 * base, axis=-1, keepdims=True)
    pos_ref[...] = jnp.where(lane == 0, p1, jnp.where(lane == 1, p2, 0.0))
    carry_s[...] = carry_s[...] + jnp.sum(oh, axis=0, keepdims=True)


def _route_plan(counts, rt, tm, n_tiles):
    t = rt.shape[0]
    tr = next(c for c in (1024, 512, 256, tm) if t % c == 0)
    const = lambda i: (0, 0)
    pos, tmeta, emeta = pl.pallas_call(
        functools.partial(_route_plan_kernel, tr, tm, n_tiles),
        grid=(t // tr,),
        in_specs=[pl.BlockSpec((SUBLANES, LANES), const), pl.BlockSpec((tr, LANES), lambda i: (i, 0))],
        out_specs=[pl.BlockSpec((tr, LANES), lambda i: (i, 0)), pl.BlockSpec((n_tiles, LANES), const),
                   pl.BlockSpec((SUBLANES, LANES), const)],
        out_shape=[jax.ShapeDtypeStruct((t, LANES), F32), jax.ShapeDtypeStruct((n_tiles, LANES), F32),
                   jax.ShapeDtypeStruct((SUBLANES, LANES), F32)],
        scratch_shapes=[pltpu.VMEM((1, LANES), F32), pltpu.VMEM((1, LANES), F32)],
        compiler_params=pltpu.CompilerParams(dimension_semantics=("arbitrary",), vmem_limit_bytes=VMEM_LIMIT),
        name="route_plan",
    )(counts, rt)
    pos1 = pos[:, 0].astype(jnp.int32)
    pos2 = pos[:, 1].astype(jnp.int32)
    tile_expert = tmeta[:, 0].astype(jnp.int32)
    n_used = tmeta[0:1, 1].astype(jnp.int32)
    has_rows = (emeta[0, :N_EXPERTS] > 0).astype(jnp.int32)
    last_tile_row = jnp.maximum(emeta[1, :N_EXPERTS], 0.0).astype(jnp.int32)
    return pos1, pos2, tile_expert, n_used, has_rows, last_tile_row


def _scatter_kernel(tm, n_steps, n_tiles, min_tiles, zon_ref, zrow_ref, nu_ref, p1_ref, p2_ref, p1p_ref, p2p_ref,
                    h_ref, xs_hbm, zbuf, stage, sem1, sem2, zsem, tsem):
    i = pl.program_id(0)

    @pl.when(i == 0)
    def _():
        zbuf[...] = jnp.zeros(zbuf.shape, F32)

        def zero_copy(e):
            return pltpu.make_async_copy(zbuf, xs_hbm.at[pl.ds(zrow_ref[e], tm)], zsem.at[e])

        def tail_copy(k):
            return pltpu.make_async_copy(zbuf, xs_hbm.at[pl.ds((min_tiles + k) * tm, tm)], tsem.at[k])

        def zstart(e, c):
            @pl.when(zon_ref[e] > 0)
            def _():
                zero_copy(e).start()
            return c

        def zwait(e, c):
            @pl.when(zon_ref[e] > 0)
            def _():
                zero_copy(e).wait()
            return c

        def tstart(k, c):
            @pl.when(min_tiles + k >= nu_ref[0])
            def _():
                tail_copy(k).start()
            return c

        def twait(k, c):
            @pl.when(min_tiles + k >= nu_ref[0])
            def _():
                tail_copy(k).wait()
            return c

        lax.fori_loop(0, N_EXPERTS, zstart, 0)
        lax.fori_loop(0, n_tiles - min_tiles, tstart, 0)
        lax.fori_loop(0, N_EXPERTS, zwait, 0)
        lax.fori_loop(0, n_tiles - min_tiles, twait, 0)

    slot = i % 2
    stage[slot] = h_ref[...]

    def copies(pa_ref, pb_ref, s, grp, k, r):
        src = _tile_of_row(stage.at[s], grp, k)
        return (pltpu.make_async_copy(src, xs_hbm.at[pl.ds(pa_ref[0, 0, r], 1)], sem1.at[s]),
                pltpu.make_async_copy(src, xs_hbm.at[pl.ds(pb_ref[0, 0, r], 1)], sem2.at[s]))

    def start(grp, k, r):
        c1, c2 = copies(p1_ref, p2_ref, slot, grp, k, r)
        c1.start(priority=0)
        c2.start(priority=1)

    def wait_of(pa_ref, pb_ref, s):
        def wait(grp, k, r):
            c1, c2 = copies(pa_ref, pb_ref, s, grp, k, r)
            c1.wait()
            c2.wait()
        return wait

    _for_each_row(tm, start)

    @pl.when(i >= 1)
    def _():
        _for_each_row(tm, wait_of(p1p_ref, p2p_ref, 1 - slot))

    @pl.when(i == n_steps - 1)
    def _():
        _for_each_row(tm, wait_of(p1_ref, p2_ref, slot))


def _scatter_rows(h2, pos1, pos2, has_rows, last_tile_row, n_used, n_tiles, tm):
    t = h2.shape[0] * SUBLANES
    nb = t // tm
    min_tiles = (2 * t) // tm
    smem_blk = pl.BlockSpec((1, 1, tm), lambda i, zon, zrow, nu: (i, 0, 0), memory_space=pltpu.SMEM)
    smem_prev = pl.BlockSpec((1, 1, tm), lambda i, zon, zrow, nu: (jnp.maximum(i - 1, 0), 0, 0),
                             memory_space=pltpu.SMEM)
    p1, p2 = pos1.reshape(nb, 1, tm), pos2.reshape(nb, 1, tm)
    grid_spec = pltpu.PrefetchScalarGridSpec(
        num_scalar_prefetch=3,
        grid=(nb,),
        in_specs=[smem_blk, smem_blk, smem_prev, smem_prev,
                  pl.BlockSpec(_row_tile_shape(tm), lambda i, zon, zrow, nu: (i, 0, 0, 0))],
        out_specs=pl.BlockSpec(memory_space=pl.ANY),
        scratch_shapes=[pltpu.VMEM((tm, SUBLANES, LANES), F32), pltpu.VMEM((2,) + _row_tile_shape(tm), F32),
                        pltpu.SemaphoreType.DMA((2,)), pltpu.SemaphoreType.DMA((2,)),
                        pltpu.SemaphoreType.DMA((N_EXPERTS,)), pltpu.SemaphoreType.DMA((n_tiles - min_tiles,))],
    )
    return pl.pallas_call(
        functools.partial(_scatter_kernel, tm, nb, n_tiles, min_tiles),
        grid_spec=grid_spec,
        out_shape=jax.ShapeDtypeStruct((n_tiles * tm, SUBLANES, LANES), F32),
        compiler_params=pltpu.CompilerParams(dimension_semantics=("arbitrary",), vmem_limit_bytes=VMEM_LIMIT),
        name="moe_scatter",
    )(has_rows, last_tile_row, n_used, p1, p2, p1, p2, h2)


def _ffn_kernel(layer, te_ref, nu_ref, slot_ref, nxt_ref, xs_ref, wg_hbm, wu_hbm, wd_hbm, ys_ref,
                wg_s, wu_s, wd_s, stage_g, stage_u, stage_d, wsem):
    i = pl.program_id(0)
    used = i < nu_ref[0]
    new_expert = (i == 0) | (te_ref[i] != te_ref[jnp.maximum(i - 1, 0)])

    def weight_copies(e, s):
        return (pltpu.make_async_copy(wg_hbm.at[layer, e], stage_g.at[s], wsem.at[s, 0]),
                pltpu.make_async_copy(wu_hbm.at[layer, e], stage_u.at[s], wsem.at[s, 1]),
                pltpu.make_async_copy(wd_hbm.at[layer, e], stage_d.at[s], wsem.at[s, 2]))

    @pl.when(i == 0)
    def _():
        for c in weight_copies(te_ref[0], slot_ref[0]):
            c.start()

    @pl.when(used & new_expert)
    def _():
        s = slot_ref[i]
        for c in weight_copies(te_ref[i], s):
            c.wait()

        @pl.when(nxt_ref[i] >= 0)
        def _():
            for c in weight_copies(nxt_ref[i], 1 - s):
                c.start()

        wg_s[...] = stage_g[s].astype(BF16)
        wu_s[...] = stage_u[s].astype(BF16)
        wd_s[...] = stage_d[s].astype(BF16)

    @pl.when(used)
    def _():
        x = _from_row_tiles(xs_ref).astype(BF16)
        gate = jnp.dot(x, wg_s[...], preferred_element_type=F32)
        up = jnp.dot(x, wu_s[...], preferred_element_type=F32)
        act = (_silu(gate) * up).astype(BF16)
        _to_row_tiles(ys_ref, jnp.dot(act, wd_s[...], preferred_element_type=F32))

    @pl.when(jnp.logical_not(used))
    def _():
        ys_ref[...] = jnp.zeros(ys_ref.shape, F32)


def _expert_ffn(xs, tile_expert, n_used, w_gate, w_up, w_down, layer, tm):
    p = xs.shape[0]
    n_tiles = p // tm
    xs = xs.reshape(_row_tile_shape(p))
    _, _, d, de = w_gate.shape
    idx = jnp.arange(n_tiles, dtype=jnp.int32)
    first = jnp.concatenate([jnp.ones((1,), bool), tile_expert[1:] != tile_expert[:-1]]) & (idx < n_used[0])
    slot = ((jnp.cumsum(first.astype(jnp.int32)) - 1) & 1).astype(jnp.int32)
    first_at_or_after = jnp.flip(lax.cummin(jnp.flip(jnp.where(first, idx, n_tiles))))
    nxt_pos = jnp.concatenate([first_at_or_after[1:], jnp.full((1,), n_tiles, jnp.int32)])
    nxt = jnp.where(nxt_pos < n_tiles, tile_expert[jnp.minimum(nxt_pos, n_tiles - 1)], -1).astype(jnp.int32)
    any_spec = pl.BlockSpec(memory_space=pl.ANY)
    grid_spec = pltpu.PrefetchScalarGridSpec(
        num_scalar_prefetch=4,
        grid=(n_tiles,),
        in_specs=[pl.BlockSpec(_row_tile_shape(tm), lambda i, te, nu, sl, nx: (jnp.minimum(i, nu[0] - 1), 0, 0, 0)),
                  any_spec, any_spec, any_spec],
        out_specs=pl.BlockSpec(_row_tile_shape(tm), lambda i, te, nu, sl, nx: (i, 0, 0, 0)),
        scratch_shapes=[pltpu.VMEM((d, de), BF16), pltpu.VMEM((d, de), BF16), pltpu.VMEM((de, d), BF16),
                        pltpu.VMEM((2, d, de), F32), pltpu.VMEM((2, d, de), F32), pltpu.VMEM((2, de, d), F32),
                        pltpu.SemaphoreType.DMA((2, 3))],
    )
    return pl.pallas_call(
        functools.partial(_ffn_kernel, layer),
        grid_spec=grid_spec,
        out_shape=jax.ShapeDtypeStruct(_row_tile_shape(p), F32),
        compiler_params=pltpu.CompilerParams(dimension_semantics=("arbitrary",), vmem_limit_bytes=VMEM_LIMIT),
        name="expert_ffn",
    )(tile_expert, n_used, slot, nxt, xs, w_gate, w_up, w_down)


def _combine_kernel(tm, n_steps, na, last_layer, p1_ref, p2_ref, p1n_ref, p2n_ref, x1_ref, rt_ref, g_ref, *rest):
    if last_layer:
        ys_hbm, outa_ref, outb_ref, buf1, buf2, sem1, sem2 = rest
    else:
        w_ref, ys_hbm, outa_ref, outb_ref, z_ref, buf1, buf2, sem1, sem2 = rest
    i = pl.program_id(0)
    slot = i % 2

    def copies(pa_ref, pb_ref, s, grp, k, r):
        return (pltpu.make_async_copy(ys_hbm.at[pl.ds(pa_ref[0, 0, r], 1)], _tile_of_row(buf1.at[s], grp, k),
                                      sem1.at[s]),
                pltpu.make_async_copy(ys_hbm.at[pl.ds(pb_ref[0, 0, r], 1)], _tile_of_row(buf2.at[s], grp, k),
                                      sem2.at[s]))

    def start(pa_ref, pb_ref, s):
        def go(grp, k, r):
            c1, c2 = copies(pa_ref, pb_ref, s, grp, k, r)
            c1.start(priority=0)
            c2.start(priority=1)
        _for_each_row(tm, go)

    @pl.when(i == 0)
    def _():
        start(p1_ref, p2_ref, 0)

    @pl.when(i + 1 < n_steps)
    def _():
        start(p1n_ref, p2n_ref, 1 - slot)

    def wait(grp, k, r):
        c1, c2 = copies(p1_ref, p2_ref, slot, grp, k, r)
        c1.wait()
        c2.wait()

    _for_each_row(tm, wait)
    rt = rt_ref[...]
    x = x1_ref[...] + rt[:, 2:3] * _from_row_tiles(buf1.at[slot]) + rt[:, 3:4] * _from_row_tiles(buf2.at[slot])
    h = x * lax.rsqrt(jnp.mean(x * x, axis=-1, keepdims=True) + EPS) * g_ref[...]
    if last_layer:
        x = h
    else:
        z_ref[...] = jnp.dot(h.astype(BF16), w_ref[...], preferred_element_type=F32)

    @pl.when(pl.program_id(0) < na)
    def _():
        outa_ref[...] = x

    @pl.when(pl.program_id(0) >= na)
    def _():
        outb_ref[...] = x


def _combine(x1, rt, ys, pos1, pos2, gain, next_w_in, ta, tm):
    t, d = x1.shape
    nb = t // tm
    na = ta // tm
    last_layer = next_w_in is None
    row = lambda i: (i, 0)
    const = lambda i: (0, 0)
    smem_blk = pl.BlockSpec((1, 1, tm), lambda i: (i, 0, 0), memory_space=pltpu.SMEM)
    smem_next = pl.BlockSpec((1, 1, tm), lambda i: (jnp.minimum(i + 1, nb - 1), 0, 0), memory_space=pltpu.SMEM)
    p1, p2 = pos1.reshape(nb, 1, tm), pos2.reshape(nb, 1, tm)
    in_specs = [smem_blk, smem_blk, smem_next, smem_next, pl.BlockSpec((tm, d), row),
                pl.BlockSpec((tm, LANES), row), pl.BlockSpec((1, d), const)]
    operands = [p1, p2, p1, p2, x1, rt, gain]
    out_specs = [pl.BlockSpec((tm, d), lambda i: (jnp.minimum(i, na - 1), 0)),
                 pl.BlockSpec((tm, d), lambda i: (jnp.maximum(i - na, 0), 0))]
    out_shape = [jax.ShapeDtypeStruct((ta, d), F32), jax.ShapeDtypeStruct((t - ta, d), F32)]
    if not last_layer:
        nz = next_w_in.shape[1]
        in_specs.append(pl.BlockSpec((d, nz), const))
        operands.append(next_w_in)
        out_specs.append(pl.BlockSpec((tm, nz), row))
        out_shape.append(jax.ShapeDtypeStruct((t, nz), F32))
    outs = pl.pallas_call(
        functools.partial(_combine_kernel, tm, nb, na, last_layer),
        grid=(nb,),
        in_specs=in_specs + [pl.BlockSpec(memory_space=pl.ANY)],
        out_specs=out_specs,
        out_shape=out_shape,
        scratch_shapes=[pltpu.VMEM((2,) + _row_tile_shape(tm), F32), pltpu.VMEM((2,) + _row_tile_shape(tm), F32),
                        pltpu.SemaphoreType.DMA((2,)), pltpu.SemaphoreType.DMA((2,))],
        compiler_params=pltpu.CompilerParams(dimension_semantics=("arbitrary",), vmem_limit_bytes=VMEM_LIMIT),
        name="moe_combine",
    )(*operands, ys.reshape(-1, SUBLANES, LANES))
    return (outs[0], outs[1]), (None if last_layer else outs[2])


def _block_diag(w):
    h, a, b = w.shape
    return jnp.einsum("hij,hk->hikj", w, jnp.eye(h, dtype=w.dtype)).reshape(h * a, h * b)


def _layer_params(l, p):
    g = GROUP_W
    w_in = p["w_in"][l]
    d = w_in.shape[0]
    w_perm = jnp.concatenate([w_in[:, :8 * g], w_in[:, 8 * g + 8:], w_in[:, 8 * g:8 * g + 8],
                              jnp.zeros((d, LANES - 8), w_in.dtype)], axis=1)
    rows = [p["ret_gn_g"][l], jnp.tile(p["gdn_norm_g"][l], N_HEADS), p["lru_conv_b"][l], p["lru_ba"][l],
            p["lru_bx"][l], p["lru_lambda"][l], p["pool_b"][l], p["pool_scale"][l]]
    pvec = jnp.concatenate([jnp.stack(rows), p["lru_conv_w"][l], jnp.zeros((4, g), F32)], axis=0)
    lane_pad = lambda v, n: jnp.concatenate([v, jnp.zeros(v.shape[:-1] + (LANES - n,), F32)], axis=-1)
    gabp = jnp.concatenate([lane_pad(jnp.stack([p["gdn_a_log"][l], p["gdn_dt_bias"][l]]), N_HEADS),
                            jnp.zeros((6, LANES), F32)], axis=0)
    n_route = N_GROUPS + N_EXPERTS
    wr = lane_pad(jnp.concatenate([p["router_group_w"][l],
                                   jnp.transpose(p["router_expert_w"][l], (1, 0, 2)).reshape(d, N_EXPERTS)],
                                  axis=1), n_route)
    br = lane_pad(jnp.concatenate([p["router_group_b"][l], p["router_expert_b"][l].reshape(N_EXPERTS)])[None, :],
                  n_route)
    return dict(
        norm1_g=p["norm1_g"][l][None, :], w_in=w_perm.astype(BF16), pvec=pvec, gdn_conv_w=p["gdn_conv_w"][l],
        gabp=gabp, lru_wa=_block_diag(p["lru_wa"][l]).astype(BF16), lru_wx=_block_diag(p["lru_wx"][l]).astype(BF16),
        pool_w=_block_diag(p["pool_w"][l]).astype(BF16), w_out=p["w_out"][l].astype(BF16),
        norm2_g=p["norm2_g"][l][None, :], wr=wr, br=br)


def kernel(x_prompt, x_sample, state_ret, state_gdn, state_gdn_conv, state_lru, state_lru_conv, state_pool,
           norm1_g, w_in, ret_gn_g, gdn_conv_w, gdn_a_log, gdn_dt_bias, gdn_norm_g,
           lru_conv_w, lru_conv_b, lru_wa, lru_ba, lru_wx, lru_bx, lru_lambda,
           pool_w, pool_b, pool_scale, w_out, norm2_g,
           router_group_w, router_group_b, router_expert_w, router_expert_b,
           moe_w_gate, moe_w_up, moe_w_down, final_g):
    params = dict(norm1_g=norm1_g, w_in=w_in, ret_gn_g=ret_gn_g, gdn_conv_w=gdn_conv_w, gdn_a_log=gdn_a_log,
                  gdn_dt_bias=gdn_dt_bias, gdn_norm_g=gdn_norm_g, lru_conv_w=lru_conv_w, lru_conv_b=lru_conv_b,
                  lru_wa=lru_wa, lru_ba=lru_ba, lru_wx=lru_wx, lru_bx=lru_bx, lru_lambda=lru_lambda,
                  pool_w=pool_w, pool_b=pool_b, pool_scale=pool_scale, w_out=w_out, norm2_g=norm2_g,
                  router_group_w=router_group_w, router_group_b=router_group_b,
                  router_expert_w=router_expert_w, router_expert_b=router_expert_b)
    bp, lp_, d = x_prompt.shape
    bs, ls, _ = x_sample.shape
    depth = w_in.shape[0]
    tp, ts_ = bp * lp_, bs * ls
    t = tp + ts_
    tile_p = min(256, lp_)
    tm = 256
    n_tiles = (2 * t + N_EXPERTS * (tm - 1) + tm - 1) // tm

    x = (x_prompt.reshape(tp, d), x_sample.reshape(ts_, d))
    sample_states = (state_ret, state_gdn, state_gdn_conv, state_lru, state_lru_conv, state_pool)
    zero_states = tuple(jnp.zeros((1, bp) + s.shape[2:], F32) for s in sample_states)
    new_p = new_s = None
    lps = [_layer_params(l, params) for l in range(depth)]
    z = _norm_win(x, lps[0]["norm1_g"], lps[0]["w_in"], tm)
    for l in range(depth):
        lp = lps[l]
        last = l == depth - 1
        mix_p, new_p = _mixer(z, 0, bp, lp_, tile_p, 1, 0, zero_states, 0, new_p, lp)
        mix_s, new_s = _mixer(z, tp, bs, ls, ls, SAMPLE_SEQS_PER_STEP, PAST_LEN, sample_states, l, new_s, lp)
        x1, h2, rt, counts = _wout_route((mix_p, mix_s), x, lp["w_out"], lp["norm2_g"], lp["wr"], lp["br"], tm)
        pos1, pos2, tile_expert, n_used, has_rows, last_tile_row = _route_plan(counts, rt, tm, n_tiles)
        xs = _scatter_rows(h2, pos1, pos2, has_rows, last_tile_row, n_used, n_tiles, tm)
        ys = _expert_ffn(xs, tile_expert, n_used, moe_w_gate, moe_w_up, moe_w_down, l, tm)
        x, z = _combine(x1, rt, ys, pos1, pos2, final_g[None, :] if last else lps[l + 1]["norm1_g"],
                        None if last else lps[l + 1]["w_in"], tp, tm)
    y_prompt = x[0].reshape(bp, lp_, d)
    y_sample = x[1].reshape(bs, ls, d)
    unpack = lambda st: (st[0], st[1], st[2], st[3][:, :, 0, :], st[4], st[5])
    return (y_prompt, y_sample, *unpack(new_p), *unpack(new_s))
```

```python
import functools

import jax
import jax.numpy as jnp
from jax import lax
from jax.experimental import pallas as pl
from jax.experimental.pallas import tpu as pltpu

F32 = jnp.float32
BF16 = jnp.bfloat16

EPS = 1e-6
HEAD_DIM = 64
N_HEADS = 4
GROUP_W = 256
CONV_W = 4
LRU_C = 8.0
POOL_WINDOWS = (2, 4, 8, 16)
POOL_BUF = 15
ROPE_BASE = 10000.0
N_GROUPS = 4
EXPERTS_PER_GROUP = 8
N_EXPERTS = 32
PAST_LEN = 16384

LANES = 128
SUBLANES = 8
HIST = 16
Z_AB = 11 * GROUP_W
NZ = Z_AB + LANES
INV_BASE = 8
RESID_PASSES = 3
SAMPLE_SEQS_PER_STEP = 16
VMEM_LIMIT = 56 * 1024 * 1024

NN = (((1,), (0,)), ((), ()))
NT = (((1,), (1,)), ((), ()))
TN = (((0,), (0,)), ((), ()))


def _mm(a, b, dims=NN, passes=1):
    ah = a.astype(BF16)
    bh = b.astype(BF16)
    dot = functools.partial(lax.dot_general, dimension_numbers=dims, preferred_element_type=F32)
    if passes == 1:
        return dot(ah, bh)
    al = (a - ah.astype(F32)).astype(BF16)
    bl = (b - bh.astype(F32)).astype(BF16)
    return dot(ah, bh) + (dot(ah, bl) + dot(al, bh))


def _sigmoid(x):
    return 1.0 / (1.0 + jnp.exp(-x))


def _silu(x):
    return x * _sigmoid(x)


def _softplus(x):
    return jnp.maximum(x, 0.0) + jnp.log1p(jnp.exp(-jnp.abs(x)))


def _gelu_tanh(x):
    return 0.5 * x * (1.0 + jnp.tanh(0.7978845608028654 * (x + 0.044715 * (x * x * x))))


def _iota(shape, axis):
    return lax.broadcasted_iota(jnp.int32, shape, axis)


def _shift_rows(x, d, fill, seg=None):
    t = _iota(x.shape, 0)
    if seg is not None:
        t = t & (seg - 1)
    return jnp.where(t >= d, pltpu.roll(x, d, 0), fill)


def _cumsum_rows(x, seg=None):
    d = 1
    while d < (seg or x.shape[0]):
        x = x + _shift_rows(x, d, 0.0, seg)
        d *= 2
    return x


def _swap_half_heads(x):
    n = x.shape[1]
    lower = (_iota(x.shape, 1) & (HEAD_DIM - 1)) < HEAD_DIM // 2
    return jnp.where(lower, pltpu.roll(x, n - HEAD_DIM // 2, 1), pltpu.roll(x, HEAD_DIM // 2, 1))


def _inv_unit_lower(mats, c=None):
    shape = mats[0].shape
    c = c or shape[0]
    row = _iota(shape, 0)
    col = _iota(shape, 1)
    nb = min(INV_BASE, c)
    shift = nb.bit_length() - 1
    in_base = (row >> shift) == (col >> shift)
    ps = [jnp.where(in_base, a, 0.0) for a in mats]
    eye = jnp.where(row == col, 1.0, 0.0)
    xs = [eye - d for d in ps]
    k = 2
    while k < nb:
        ps = [_mm(p, p) for p in ps]
        xs = [x + _mm(x, p) for x, p in zip(xs, ps)]
        k *= 2
    m = nb
    while m < c:
        s1 = m.bit_length() - 1
        in_off = ((row >> (s1 + 1)) == (col >> (s1 + 1))) & ((row >> s1) != (col >> s1))
        ys = [_mm(x, jnp.where(in_off, a, 0.0)) for x, a in zip(xs, mats)]
        xs = [x - _mm(y, x) for x, y in zip(xs, ys)]
        m *= 2
    return xs


def _solve_unit_lower(mats, tinvs, ws):
    us = [_mm(t, w) for t, w in zip(tinvs, ws)]
    rs = [w - (u + _mm(a, u, passes=RESID_PASSES)) for a, u, w in zip(mats, us, ws)]
    return [u + _mm(t, r) for u, t, r in zip(us, tinvs, rs)]


def _pair_specs(na, tm, d):
    return [pl.BlockSpec((tm, d), lambda i: (jnp.minimum(i, na - 1), 0)),
            pl.BlockSpec((tm, d), lambda i: (jnp.maximum(i - na, 0), 0))]


def _pair_tile(na, a_ref, b_ref):
    return jnp.where(pl.program_id(0) < na, a_ref[...], b_ref[...])


def _norm_win_kernel(na, xa_ref, xb_ref, g_ref, w_ref, z_ref):
    x = _pair_tile(na, xa_ref, xb_ref)
    h = x * lax.rsqrt(jnp.mean(x * x, axis=-1, keepdims=True) + EPS) * g_ref[...]
    z_ref[...] = jnp.dot(h.astype(BF16), w_ref[...], preferred_element_type=F32)


def _norm_win(x_pair, g, w_bf16, tm):
    xa, xb = x_pair
    d = xa.shape[1]
    na, t = xa.shape[0] // tm, xa.shape[0] + xb.shape[0]
    nz = w_bf16.shape[1]
    return pl.pallas_call(
        functools.partial(_norm_win_kernel, na),
        grid=(t // tm,),
        in_specs=_pair_specs(na, tm, d) + [pl.BlockSpec((1, d), lambda i: (0, 0)),
                                           pl.BlockSpec((d, nz), lambda i: (0, 0))],
        out_specs=pl.BlockSpec((tm, nz), lambda i: (i, 0)),
        out_shape=jax.ShapeDtypeStruct((t, nz), F32),
        compiler_params=pltpu.CompilerParams(dimension_semantics=("arbitrary",), vmem_limit_bytes=VMEM_LIMIT),
        name="norm_win",
    )(xa, xb, g, w_bf16)


def _retention_tile(zq, zk, zv, cos, sin, intra_ref, qdec, kdec, sdec, s_ref):
    rq = zq * cos + _swap_half_heads(zq) * sin
    rk = (zk * cos + _swap_half_heads(zk) * sin) * (HEAD_DIM ** -0.5)
    kd = rk * kdec
    heads = range(N_HEADS)
    sls = [slice(h * HEAD_DIM, (h + 1) * HEAD_DIM) for h in heads]
    ss = [s_ref[h] for h in heads]
    scores = [_mm(rq[:, sl], rk[:, sl], NT) * intra_ref[h] for h, sl in zip(heads, sls)]
    os_ = [_mm(sc, zv[:, sl]) + _mm(rq[:, sl], s) * qdec[:, sl] for sc, s, sl in zip(scores, ss, sls)]
    s_new = [s * sdec[h:h + 1, :HEAD_DIM] + _mm(kd[:, sl], zv[:, sl], TN) for h, s, sl in zip(heads, ss, sls)]
    for h in heads:
        s_ref[h] = s_new[h]
    outs = []
    for o in os_:
        mu = jnp.mean(o, axis=-1, keepdims=True)
        oc = o - mu
        var = jnp.mean(oc * oc, axis=-1, keepdims=True)
        outs.append(oc * lax.rsqrt(var + EPS))
    return jnp.concatenate(outs, axis=-1)


def _split3(x):
    x0 = x.astype(BF16)
    r1 = x - x0.astype(F32)
    x1 = r1.astype(BF16)
    x2 = (r1 - x1.astype(F32)).astype(BF16)
    return x0, x1, x2


def _gdn_tile(gq, gk, gv, gab, s_ref):
    c = gq.shape[0]
    gc = _cumsum_rows(gab)
    egc = jnp.exp(gc)
    row = _iota((c, c), 0)
    col = _iota((c, c), 1)
    incl = row >= col
    strict = row > col
    heads = range(N_HEADS)
    sls = [slice(h * HEAD_DIM, (h + 1) * HEAD_DIM) for h in heads]
    ss = [s_ref[h] for h in heads]
    gcols = [gc[:, h:h + 1] for h in heads]
    bcols = [gab[:, N_HEADS + h:N_HEADS + h + 1] for h in heads]
    ecols = [egc[:, h:h + 1] for h in heads]
    gct = gc.T
    decs = [jnp.where(incl, jnp.exp(jnp.where(incl, gcols[h] - gct[h:h + 1, :], 0.0)), 0.0) for h in heads]
    mats = [jnp.where(strict, bcols[h] * decs[h] * _mm(gk[:, sl], gk[:, sl], NT), 0.0) for h, sl in zip(heads, sls)]
    tinvs = _inv_unit_lower(mats)
    ws = [bcols[h] * (gv[:, sl] - ecols[h] * _mm(gk[:, sl], ss[h])) for h, sl in zip(heads, sls)]
    us = _solve_unit_lower(mats, tinvs, ws)
    qks = [_mm(gq[:, sl], gk[:, sl], NT) * decs[h] for h, sl in zip(heads, sls)]
    outs = [ecols[h] * _mm(gq[:, sl], ss[h]) + _mm(qks[h], us[h]) for h, sl in zip(heads, sls)]
    glasts = [gc[c - 1:c, h:h + 1] for h in heads]
    s_new = [jnp.exp(glasts[h]) * ss[h] + _mm(jnp.exp(glasts[h] - gcols[h]) * gk[:, sl], us[h], TN)
             for h, sl in zip(heads, sls)]
    for h in heads:
        s_ref[h] = s_new[h]
    return outs


def _mixer_kernel(nt, ts, nb, pos_base,
                  z_ref, cos_ref, sin_ref, intra_ref, qdec_ref, kdec_ref, sdec_ref, pv_ref, gcw_ref, gabp_ref,
                  wa_ref, wx_ref, pw_ref, sret0, sgdn0, gconv0, lru0, lconv0, pool0,
                  mix_ref, ret_o, gdn_o, gconv_o, lru_o, lconv_o, pool_o,
                  s_ret, s_gdn, ext_g, ext_l, ext_p, h_lru):
    n = pl.program_id(1)
    g = GROUP_W
    base = HIST - (CONV_W - 1)

    @pl.when(n == 0)
    def _():
        s_ret[...] = sret0[...]
        s_gdn[...] = sgdn0[...]
        ext_g[:, 0:HIST, :] = jnp.zeros((nb, HIST, 3 * g), F32)
        ext_g[:, base:HIST, :] = gconv0[...]
        ext_l[:, 0:HIST, :] = jnp.zeros((nb, HIST, g), F32)
        ext_l[:, base:HIST, :] = lconv0[...]
        ext_p[:, 0:HIST, :] = jnp.zeros((nb, HIST, g), F32)
        ext_p[:, HIST - POOL_BUF:HIST, :] = pool0[...]
        h_lru[...] = lru0[...]

    pv = pv_ref[...]
    ret_gn_g, gdn_norm_g = pv[0:1], pv[1:2]
    lru_conv_b, lru_ba, lru_bx, lru_lambda = pv[2:3], pv[3:4], pv[4:5], pv[5:6]
    pool_b, pool_scale = pv[6:7], pv[7:8]
    lru_conv_w = pv[8:12]
    gcw = gcw_ref[...]
    gabp = gabp_ref[...]

    for j in range(nb):
        rows = slice(j * ts, (j + 1) * ts)
        eg, el, ep = ext_g.at[j], ext_l.at[j], ext_p.at[j]

        ro = _retention_tile(z_ref[rows, 0:g], z_ref[rows, g:2 * g], z_ref[rows, 2 * g:3 * g], cos_ref[...],
                             sin_ref[...], intra_ref, qdec_ref[...], kdec_ref[...], sdec_ref[...], s_ret.at[j])
        mix_ref[rows, 0:g] = ro * ret_gn_g * _silu(z_ref[rows, 3 * g:4 * g])

        eg[HIST:HIST + ts, :] = z_ref[rows, 4 * g:7 * g]
        y = eg[base:base + ts, :] * gcw[0:1]
        for i in range(1, CONV_W):
            y = y + eg[base + i:base + i + ts, :] * gcw[i:i + 1]
        qkv = _silu(y)
        zab = z_ref[rows, Z_AB:Z_AB + LANES]
        lane = _iota(zab.shape, 1)
        log_a = -jnp.exp(gabp[0:1]) * _softplus(zab + gabp[1:2])
        gab = jnp.where(lane < N_HEADS, log_a, jnp.where(lane < 2 * N_HEADS, _sigmoid(zab), 0.0))
        gq_parts, gk_parts = [], []
        for h in range(N_HEADS):
            qh = qkv[:, h * HEAD_DIM:(h + 1) * HEAD_DIM]
            kh = qkv[:, g + h * HEAD_DIM:g + (h + 1) * HEAD_DIM]
            gq_parts.append(qh * lax.rsqrt(jnp.sum(qh * qh, axis=-1, keepdims=True) + EPS) * (HEAD_DIM ** -0.5))
            gk_parts.append(kh * lax.rsqrt(jnp.sum(kh * kh, axis=-1, keepdims=True) + EPS))
        go = _gdn_tile(jnp.concatenate(gq_parts, axis=-1), jnp.concatenate(gk_parts, axis=-1),
                       qkv[:, 2 * g:3 * g], gab, s_gdn.at[j])
        go = [o * lax.rsqrt(jnp.mean(o * o, axis=-1, keepdims=True) + EPS) for o in go]
        mix_ref[rows, g:2 * g] = jnp.concatenate(go, axis=-1) * gdn_norm_g * _silu(z_ref[rows, 7 * g:8 * g])

        el[HIST:HIST + ts, :] = z_ref[rows, 8 * g:9 * g]
        xc = el[base:base + ts, :] * lru_conv_w[0:1]
        for i in range(1, CONV_W):
            xc = xc + el[base + i:base + i + ts, :] * lru_conv_w[i:i + 1]
        xc = xc + lru_conv_b
        r = _sigmoid(jnp.dot(xc.astype(BF16), wa_ref[...], preferred_element_type=F32) + lru_ba)
        gate_i = _sigmoid(jnp.dot(xc.astype(BF16), wx_ref[...], preferred_element_type=F32) + lru_bx)
        log_at = -LRU_C * r * _softplus(-lru_lambda)
        av = jnp.exp(log_at)
        bv = jnp.sqrt(-jnp.tanh(log_at) * (av * av + 1.0)) * (gate_i * xc)
        bv = bv + jnp.where(_iota(bv.shape, 0) == 0, av * h_lru[j], 0.0)
        d = 1
        while d < ts:
            bv = av * _shift_rows(bv, d, 0.0) + bv
            av = av * _shift_rows(av, d, 1.0)
            d *= 2
        h_lru[j] = bv[ts - 1:ts, :]
        mix_ref[rows, 2 * g:3 * g] = bv * _gelu_tanh(z_ref[rows, 9 * g:10 * g])

        u = z_ref[rows, 10 * g:11 * g]
        ep[HIST:HIST + ts, :] = u
        e = ep[...]
        sums = []
        step = 1
        for _ in POOL_WINDOWS:
            e = e + pltpu.roll(e, step, 0)
            sums.append(e[HIST:HIST + ts, :])
            step *= 2
        pos1 = (pos_base + 1 + n * ts + _iota((ts, g), 0)).astype(F32)
        lane_g = _iota((ts, g), 1) >> 6
        pooled = sums[-1] / jnp.minimum(pos1, float(POOL_WINDOWS[-1]))
        for gi in range(len(POOL_WINDOWS) - 2, -1, -1):
            pooled = jnp.where(lane_g == gi, sums[gi] / jnp.minimum(pos1, float(POOL_WINDOWS[gi])), pooled)
        pooled = pooled - u
        yp = jnp.dot(pooled.astype(BF16), pw_ref[...], preferred_element_type=F32) + pool_b
        mix_ref[rows, 3 * g:4 * g] = yp * pool_scale

        tail_g = eg[ts:ts + HIST, :]
        tail_l = el[ts:ts + HIST, :]
        tail_p = ep[ts:ts + HIST, :]
        eg[0:HIST, :] = tail_g
        el[0:HIST, :] = tail_l
        ep[0:HIST, :] = tail_p

    @pl.when(n == nt - 1)
    def _():
        ret_o[...] = s_ret[...]
        gdn_o[...] = s_gdn[...]
        gconv_o[...] = ext_g[:, base:HIST, :]
        lconv_o[...] = ext_l[:, base:HIST, :]
        pool_o[...] = ext_p[:, HIST - POOL_BUF:HIST, :]
        lru_o[...] = h_lru[...]


def _decode_mixer_kernel(ts, nb, pos_base,
                         z_ref, cos_ref, sin_ref, intra_ref, qdec_ref, kdec_ref, sdec_ref, pv_ref, gcw_ref, gabp_ref,
                         wa_ref, wx_ref, pw_ref, sret0, sgdn0, gconv0, lru0, lconv0, pool0,
                         mix_ref, ret_o, gdn_o, gconv_o, lru_o, lconv_o, pool_o,
                         ext_g, ext_l, ext_p):
    g = GROUP_W
    r_all = nb * ts
    base = HIST - (CONV_W - 1)
    seg_shift = ts.bit_length() - 1
    row2 = _iota((r_all, r_all), 0)
    col2 = _iota((r_all, r_all), 1)
    same = (row2 >> seg_shift) == (col2 >> seg_shift)
    incl = same & (row2 >= col2)
    strict = same & (row2 > col2)
    seq_l = _iota((r_all, LANES), 0) >> seg_shift
    half_l = _iota((r_all, LANES), 1) >> 6
    exp_masks = [((seq_l >> 1) == c) & ((seq_l & 1) == half_l) for c in range(nb // 2)]

    def expand(x):
        x2 = jnp.concatenate([x, x], axis=-1)
        return jnp.concatenate([jnp.where(m, x2, 0.0) for m in exp_masks], axis=-1)

    def stacked(state_ref, h):
        return jnp.concatenate([state_ref[j, h] for j in range(nb)], axis=0)

    def unstack(out_ref, h, sv):
        for j in range(nb):
            out_ref[j, h] = sv[j * HEAD_DIM:(j + 1) * HEAD_DIM, :]

    pv = pv_ref[...]
    ret_gn_g, gdn_norm_g = pv[0:1], pv[1:2]
    lru_conv_b, lru_ba, lru_bx, lru_lambda = pv[2:3], pv[3:4], pv[4:5], pv[5:6]
    pool_b, pool_scale = pv[6:7], pv[7:8]
    lru_conv_w = pv[8:12]
    gcw = gcw_ref[...]
    gabp = gabp_ref[...]

    ext_g[:, 0:HIST, :] = jnp.zeros((nb, HIST, 3 * g), F32)
    ext_g[:, base:HIST, :] = gconv0[...]
    ext_l[:, 0:HIST, :] = jnp.zeros((nb, HIST, g), F32)
    ext_l[:, base:HIST, :] = lconv0[...]
    ext_p[:, 0:HIST, :] = jnp.zeros((nb, HIST, g), F32)
    ext_p[:, HIST - POOL_BUF:HIST, :] = pool0[...]
    y_parts, xc_parts, sum_parts, h0_parts = [], [], [[] for _ in POOL_WINDOWS], []
    for j in range(nb):
        rows = slice(j * ts, (j + 1) * ts)
        eg, el, ep = ext_g.at[j], ext_l.at[j], ext_p.at[j]
        eg[HIST:HIST + ts, :] = z_ref[rows, 4 * g:7 * g]
        el[HIST:HIST + ts, :] = z_ref[rows, 8 * g:9 * g]
        ep[HIST:HIST + ts, :] = z_ref[rows, 10 * g:11 * g]
        y = eg[base:base + ts, :] * gcw[0:1]
        xc = el[base:base + ts, :] * lru_conv_w[0:1]
        for i in range(1, CONV_W):
            y = y + eg[base + i:base + i + ts, :] * gcw[i:i + 1]
            xc = xc + el[base + i:base + i + ts, :] * lru_conv_w[i:i + 1]
        y_parts.append(y)
        xc_parts.append(xc)
        e = ep[...]
        step = 1
        for wi in range(len(POOL_WINDOWS)):
            e = e + pltpu.roll(e, step, 0)
            sum_parts[wi].append(e[HIST:HIST + ts, :])
            step *= 2
        h0_parts.append(jnp.broadcast_to(lru0[j], (ts, g)))
        gconv_o[j] = eg[ts + base:ts + HIST, :]
        lconv_o[j] = el[ts + base:ts + HIST, :]
        pool_o[j] = ep[ts + HIST - POOL_BUF:ts + HIST, :]

    zq, zk, zv = z_ref[:, 0:g], z_ref[:, g:2 * g], z_ref[:, 2 * g:3 * g]
    cos, sin = cos_ref[...], sin_ref[...]
    qdec, kdec, sdec = qdec_ref[...], kdec_ref[...], sdec_ref[...]
    rq = zq * cos + _swap_half_heads(zq) * sin
    rk = (zk * cos + _swap_half_heads(zk) * sin) * (HEAD_DIM ** -0.5)
    kd = rk * kdec
    outs = []
    for h in range(N_HEADS):
        sl = slice(h * HEAD_DIM, (h + 1) * HEAD_DIM)
        q, k, v = rq[:, sl], rk[:, sl], zv[:, sl]
        sv = stacked(sret0, h)
        scores = _mm(q, k, NT) * intra_ref[h]
        o = _mm(scores, v) + _mm(expand(q), sv) * qdec[:, sl]
        unstack(ret_o, h, sv * sdec[h:h + 1, :HEAD_DIM] + _mm(expand(kd[:, sl]), v, TN))
        mu = jnp.mean(o, axis=-1, keepdims=True)
        oc = o - mu
        var = jnp.mean(oc * oc, axis=-1, keepdims=True)
        outs.append(oc * lax.rsqrt(var + EPS))
    mix_ref[:, 0:g] = jnp.concatenate(outs, axis=-1) * ret_gn_g * _silu(z_ref[:, 3 * g:4 * g])

    qkv = _silu(jnp.concatenate(y_parts, axis=0))
    zab = z_ref[:, Z_AB:Z_AB + LANES]
    lane = _iota(zab.shape, 1)
    log_a = -jnp.exp(gabp[0:1]) * _softplus(zab + gabp[1:2])
    gab = jnp.where(lane < N_HEADS, log_a, jnp.where(lane < 2 * N_HEADS, _sigmoid(zab), 0.0))
    gc = _cumsum_rows(gab, seg=ts)
    egc = jnp.exp(gc)
    gct = gc.T
    dotnn = functools.partial(lax.dot_general, dimension_numbers=NN, preferred_element_type=F32)
    same_b = jnp.where(same, 1.0, 0.0).astype(BF16)
    a0, a1, a2 = _split3(gab)
    glast = dotnn(same_b, a0) + (dotnn(same_b, a1) + dotnn(same_b, a2))
    pick = jnp.where(((_iota((nb * HEAD_DIM, r_all), 0) >> 6) == (_iota((nb * HEAD_DIM, r_all), 1) >> seg_shift))
                     & ((_iota((nb * HEAD_DIM, r_all), 1) & (ts - 1)) == 0), 1.0, 0.0).astype(BF16)
    e0, e1, e2 = _split3(jnp.exp(glast))
    eglast_s = dotnn(pick, e0) + (dotnn(pick, e1) + dotnn(pick, e2))
    heads = range(N_HEADS)
    qs, ks, vs, decs = [], [], [], []
    for h in heads:
        qh = qkv[:, h * HEAD_DIM:(h + 1) * HEAD_DIM]
        kh = qkv[:, g + h * HEAD_DIM:g + (h + 1) * HEAD_DIM]
        qs.append(qh * lax.rsqrt(jnp.sum(qh * qh, axis=-1, keepdims=True) + EPS) * (HEAD_DIM ** -0.5))
        ks.append(kh * lax.rsqrt(jnp.sum(kh * kh, axis=-1, keepdims=True) + EPS))
        vs.append(qkv[:, 2 * g + h * HEAD_DIM:2 * g + (h + 1) * HEAD_DIM])
        decs.append(jnp.where(incl, jnp.exp(jnp.where(incl, gc[:, h:h + 1] - gct[h:h + 1, :], 0.0)), 0.0))
    gcols = [gc[:, h:h + 1] for h in heads]
    bcols = [gab[:, N_HEADS + h:N_HEADS + h + 1] for h in heads]
    ecols = [egc[:, h:h + 1] for h in heads]
    svs = [stacked(sgdn0, h) for h in heads]
    mats = [jnp.where(strict, bcols[h] * decs[h] * _mm(ks[h], ks[h], NT), 0.0) for h in heads]
    tinvs = _inv_unit_lower(mats, ts)
    ws = [bcols[h] * (vs[h] - ecols[h] * _mm(expand(ks[h]), svs[h])) for h in heads]
    us = _solve_unit_lower(mats, tinvs, ws)
    os_ = [ecols[h] * _mm(expand(qs[h]), svs[h]) + _mm(_mm(qs[h], ks[h], NT) * decs[h], us[h]) for h in heads]
    for h in heads:
        kdx = jnp.exp(glast[:, h:h + 1] - gcols[h]) * ks[h]
        unstack(gdn_o, h, eglast_s[:, h:h + 1] * svs[h] + _mm(expand(kdx), us[h], TN))
    outs = [o * lax.rsqrt(jnp.mean(o * o, axis=-1, keepdims=True) + EPS) for o in os_]
    mix_ref[:, g:2 * g] = jnp.concatenate(outs, axis=-1) * gdn_norm_g * _silu(z_ref[:, 7 * g:8 * g])

    xc = jnp.concatenate(xc_parts, axis=0) + lru_conv_b
    r = _sigmoid(jnp.dot(xc.astype(BF16), wa_ref[...], preferred_element_type=F32) + lru_ba)
    gate_i = _sigmoid(jnp.dot(xc.astype(BF16), wx_ref[...], preferred_element_type=F32) + lru_bx)
    log_at = -LRU_C * r * _softplus(-lru_lambda)
    av = jnp.exp(log_at)
    bv = jnp.sqrt(-jnp.tanh(log_at) * (av * av + 1.0)) * (gate_i * xc)
    first = (_iota(bv.shape, 0) & (ts - 1)) == 0
    bv = bv + jnp.where(first, av * jnp.concatenate(h0_parts, axis=0), 0.0)
    d = 1
    while d < ts:
        bv = av * _shift_rows(bv, d, 0.0, ts) + bv
        av = av * _shift_rows(av, d, 1.0, ts)
        d *= 2
    for j in range(nb):
        lru_o[j] = bv[(j + 1) * ts - 1:(j + 1) * ts, :]
    mix_ref[:, 2 * g:3 * g] = bv * _gelu_tanh(z_ref[:, 9 * g:10 * g])

    u_in = z_ref[:, 10 * g:11 * g]
    pos1 = (pos_base + 1 + (_iota((r_all, g), 0) & (ts - 1))).astype(F32)
    lane_g = _iota((r_all, g), 1) >> 6
    sums = [jnp.concatenate(parts, axis=0) for parts in sum_parts]
    pooled = sums[-1] / jnp.minimum(pos1, float(POOL_WINDOWS[-1]))
    for gi in range(len(POOL_WINDOWS) - 2, -1, -1):
        pooled = jnp.where(lane_g == gi, sums[gi] / jnp.minimum(pos1, float(POOL_WINDOWS[gi])), pooled)
    pooled = pooled - u_in
    yp = jnp.dot(pooled.astype(BF16), pw_ref[...], preferred_element_type=F32) + pool_b
    mix_ref[:, 3 * g:4 * g] = yp * pool_scale


N_STATES = 6


def _with_state_stack(body, nt, n_prev, n_in, *refs):
    ins = refs[:n_in]
    prevs = refs[n_in:n_in + (N_STATES if n_prev else 0)]
    k = n_in + len(prevs)
    mix_ref, stacks, scratch = refs[k], refs[k + 1:k + 1 + N_STATES], refs[k + 1 + N_STATES:]

    @pl.when(pl.program_id(1) == nt - 1)
    def _():
        for o, p in zip(stacks, prevs):
            o[0:n_prev] = p[...]

    body(*ins, mix_ref, *[o.at[n_prev] for o in stacks], *scratch)


def _ret_consts(c):
    log_g = jnp.log1p(-jnp.exp2(-5.0 - jnp.arange(N_HEADS, dtype=F32)))
    idx = jnp.arange(c, dtype=F32)
    diff = idx[:, None] - idx[None, :]
    intra = jnp.where(diff >= 0, jnp.exp(log_g[:, None, None] * jnp.maximum(diff, 0.0)), 0.0)
    qdec = jnp.exp(log_g[:, None] * (idx + 1.0))
    kdec = jnp.exp(log_g[:, None] * (c - 1.0 - idx))
    sdec = jnp.exp(log_g * c)
    expand = lambda t: jnp.repeat(t.T, HEAD_DIM, axis=1)
    sdec_tab = jnp.zeros((8, LANES), F32).at[:N_HEADS, :].set(sdec[:, None])
    return intra, expand(qdec), expand(kdec), sdec_tab


def _rope_tables(pos):
    half = HEAD_DIM // 2
    inv = ROPE_BASE ** (-jnp.arange(half, dtype=F32) / half)
    ang = pos.astype(F32)[:, None] * inv[None, :]
    cos, sin = jnp.cos(ang), jnp.sin(ang)
    cos_t = jnp.tile(jnp.concatenate([cos, cos], axis=-1), (1, N_HEADS))
    sin_t = jnp.tile(jnp.concatenate([-sin, sin], axis=-1), (1, N_HEADS))
    return cos_t, sin_t


def _mixer(z, z_row0, b, l, ts, nb, pos_base, states, layer, prev, lp):
    nt = l // ts
    assert nb == 1 or (nt == 1 and nb % 2 == 0 and ts == INV_BASE)
    g = GROUP_W
    pos = pos_base + jnp.arange(l, dtype=jnp.int32)
    cos_t, sin_t = _rope_tables(pos)
    intra, qdec, kdec, sdec = _ret_consts(ts)
    rt_ = nb * ts
    if nb > 1:
        cos_t, sin_t, qdec, kdec = (jnp.tile(a, (nb, 1)) for a in (cos_t, sin_t, qdec, kdec))
        intra = jnp.einsum("ab,hij->haibj", jnp.eye(nb, dtype=F32), intra).reshape(N_HEADS, rt_, rt_)
    s_ret0, s_gdn0, gconv0, lru0, lconv0, pool0 = states
    blk0 = z_row0 // rt_
    const2 = lambda bi, n: (0, 0)
    per_b3 = lambda bi, n: (bi, 0, 0)
    per_b4 = lambda bi, n: (bi, 0, 0, 0)
    st3 = lambda bi, n: (layer, bi, 0, 0)
    st4 = lambda bi, n: (layer, bi, 0, 0, 0)
    in_specs = [
        pl.BlockSpec((rt_, NZ), lambda bi, n: (blk0 + bi * nt + n, 0)),
        pl.BlockSpec((rt_, g), lambda bi, n: (n, 0)),
        pl.BlockSpec((rt_, g), lambda bi, n: (n, 0)),
        pl.BlockSpec((N_HEADS, rt_, rt_), lambda bi, n: (0, 0, 0)),
        pl.BlockSpec((rt_, g), const2),
        pl.BlockSpec((rt_, g), const2),
        pl.BlockSpec((8, LANES), const2),
        pl.BlockSpec((16, g), const2),
        pl.BlockSpec((CONV_W, 3 * g), const2),
        pl.BlockSpec((8, LANES), const2),
        pl.BlockSpec((g, g), const2),
        pl.BlockSpec((g, g), const2),
        pl.BlockSpec((g, g), const2),
        pl.BlockSpec((None, nb, N_HEADS, HEAD_DIM, HEAD_DIM), st4),
        pl.BlockSpec((None, nb, N_HEADS, HEAD_DIM, HEAD_DIM), st4),
        pl.BlockSpec((None, nb, CONV_W - 1, 3 * g), st3),
        pl.BlockSpec((None, nb, 1, g), st3),
        pl.BlockSpec((None, nb, CONV_W - 1, g), st3),
        pl.BlockSpec((None, nb, POOL_BUF, g), st3),
    ]
    st_shapes = [(N_HEADS, HEAD_DIM, HEAD_DIM), (N_HEADS, HEAD_DIM, HEAD_DIM), (CONV_W - 1, 3 * g), (1, g),
                 (CONV_W - 1, g), (POOL_BUF, g)]
    n_prev = 0 if prev is None else prev[0].shape[0]
    stack_spec = lambda layers, s: pl.BlockSpec((layers, nb) + s, lambda bi, n: (0, bi) + (0,) * len(s))
    n_in = len(in_specs)
    if n_prev:
        in_specs = in_specs + [stack_spec(n_prev, s) for s in st_shapes]
    out_specs = [pl.BlockSpec((rt_, 4 * g), lambda bi, n: (bi * nt + n, 0))] + [
        stack_spec(n_prev + 1, s) for s in st_shapes]
    out_shape = [jax.ShapeDtypeStruct((b * l, 4 * g), F32)] + [
        jax.ShapeDtypeStruct((n_prev + 1, b) + s, F32) for s in st_shapes]
    ext = [pltpu.VMEM((nb, ts + HIST, 3 * g), F32), pltpu.VMEM((nb, ts + HIST, g), F32),
           pltpu.VMEM((nb, ts + HIST, g), F32)]
    if nb > 1:
        body = functools.partial(_decode_mixer_kernel, ts, nb, pos_base)
        scratch = ext
    else:
        body = functools.partial(_mixer_kernel, nt, ts, nb, pos_base)
        scratch = [pltpu.VMEM((nb, N_HEADS, HEAD_DIM, HEAD_DIM), F32),
                   pltpu.VMEM((nb, N_HEADS, HEAD_DIM, HEAD_DIM), F32)] + ext + [pltpu.VMEM((nb, 1, g), F32)]
    outs = pl.pallas_call(
        functools.partial(_with_state_stack, body, nt, n_prev, n_in),
        grid=(b // nb, nt),
        in_specs=in_specs,
        out_specs=out_specs,
        out_shape=out_shape,
        scratch_shapes=scratch,
        compiler_params=pltpu.CompilerParams(dimension_semantics=("arbitrary", "arbitrary"),
                                             vmem_limit_bytes=VMEM_LIMIT),
        name=f"mixer_ts{ts}",
    )(z, cos_t, sin_t, intra, qdec, kdec, sdec, lp["pvec"], lp["gdn_conv_w"], lp["gabp"],
      lp["lru_wa"], lp["lru_wx"], lp["pool_w"], s_ret0, s_gdn0, gconv0, lru0[:, :, None, :], lconv0, pool0,
      *(prev or ()))
    return outs[0], tuple(outs[1:])


def _to_row_tiles(ref, x):
    m = x.shape[0]
    ref[...] = x.reshape(m // SUBLANES, SUBLANES, SUBLANES, LANES)


def _from_row_tiles(ref):
    return ref[...].reshape(ref.shape[0] * SUBLANES, SUBLANES * LANES)


def _row_tile_shape(m):
    return (m // SUBLANES, SUBLANES, SUBLANES, LANES)


def _wout_route_kernel(na, ma_ref, mb_ref, xa_ref, xb_ref, wo_ref, g_ref, wr_ref, br_ref, x1_ref, h2_ref, rt_ref,
                       cnt_ref):
    mix = _pair_tile(na, ma_ref, mb_ref)
    x1 = _pair_tile(na, xa_ref, xb_ref) + jnp.dot(mix.astype(BF16), wo_ref[...], preferred_element_type=F32)
    x1_ref[...] = x1
    h = x1 * lax.rsqrt(jnp.mean(x1 * x1, axis=-1, keepdims=True) + EPS) * g_ref[...]
    _to_row_tiles(h2_ref, h)
    logits = _mm(h, wr_ref[...], passes=3) + br_ref[...]
    lane = _iota(logits.shape, 1)
    lane_f = lane.astype(F32)
    neg = -1e30
    far = 1e9
    is_g = lane < N_GROUPS
    gl = jnp.where(is_g, logits, neg)
    gmax = jnp.max(gl, axis=-1, keepdims=True)
    gsel = jnp.min(jnp.where(gl == gmax, lane_f, far), axis=-1, keepdims=True)
    p_g = 1.0 / jnp.sum(jnp.where(is_g, jnp.exp(gl - gmax), 0.0), axis=-1, keepdims=True)
    e_group = ((lane - N_GROUPS) >> 3).astype(F32)
    in_group = jnp.where(lane >= N_GROUPS, e_group, -1.0) == gsel
    el = jnp.where(in_group, logits, neg)
    m1 = jnp.max(el, axis=-1, keepdims=True)
    i1 = jnp.min(jnp.where(el == m1, lane_f, far), axis=-1, keepdims=True)
    el2 = jnp.where(lane_f == i1, neg, el)
    m2 = jnp.max(el2, axis=-1, keepdims=True)
    i2 = jnp.min(jnp.where(el2 == m2, lane_f, far), axis=-1, keepdims=True)
    e2 = jnp.exp(m2 - m1)
    w1 = p_g / (1.0 + e2)
    w2 = p_g * e2 / (1.0 + e2)
    rt = jnp.where(lane == 0, i1 - N_GROUPS,
                   jnp.where(lane == 1, i2 - N_GROUPS, jnp.where(lane == 2, w1, jnp.where(lane == 3, w2, 0.0))))
    rt_ref[...] = rt

    picked = jnp.where((lane_f + N_GROUPS == i1) | (lane_f + N_GROUPS == i2), 1.0, 0.0)

    @pl.when(pl.program_id(0) == 0)
    def _():
        cnt_ref[...] = jnp.zeros(cnt_ref.shape, F32)

    cnt_ref[0:1, :] = cnt_ref[0:1, :] + jnp.sum(picked, axis=0, keepdims=True)


def _wout_route(mix_pair, x_pair, wo_bf16, g, wr, br, tm):
    d = x_pair[0].shape[1]
    na, t = x_pair[0].shape[0] // tm, x_pair[0].shape[0] + x_pair[1].shape[0]
    row = lambda i: (i, 0)
    const = lambda i: (0, 0)
    return pl.pallas_call(
        functools.partial(_wout_route_kernel, na),
        grid=(t // tm,),
        in_specs=_pair_specs(na, tm, d) + _pair_specs(na, tm, d) + [
            pl.BlockSpec((d, d), const), pl.BlockSpec((1, d), const), pl.BlockSpec((d, LANES), const),
            pl.BlockSpec((1, LANES), const)],
        out_specs=[pl.BlockSpec((tm, d), row), pl.BlockSpec(_row_tile_shape(tm), lambda i: (i, 0, 0, 0)),
                   pl.BlockSpec((tm, LANES), row), pl.BlockSpec((SUBLANES, LANES), const)],
        out_shape=[jax.ShapeDtypeStruct((t, d), F32), jax.ShapeDtypeStruct(_row_tile_shape(t), F32),
                   jax.ShapeDtypeStruct((t, LANES), F32), jax.ShapeDtypeStruct((SUBLANES, LANES), F32)],
        compiler_params=pltpu.CompilerParams(dimension_semantics=("arbitrary",), vmem_limit_bytes=VMEM_LIMIT),
        name="wout_route",
    )(*mix_pair, *x_pair, wo_bf16, g, wr, br)


def _tile_of_row(ref4, grp, k):
    return ref4.at[grp, pl.ds(k, 1)]


def _for_each_row(tm, fn):
    def body(grp, c):
        for k in range(SUBLANES):
            fn(grp, k, grp * SUBLANES + k)
        return c

    lax.fori_loop(0, tm // SUBLANES, body, 0)


def _route_plan_kernel(tr, tm, n_tiles, cnt_ref, rt_ref, pos_ref, tmeta_ref, emeta_ref, start_s, carry_s):
    i = pl.program_id(0)
    rt = rt_ref[...]
    lane = _iota((tr, LANES), 1)
    lane_f = lane.astype(F32)
    oh1 = jnp.where(lane_f == rt[:, 0:1], 1.0, 0.0)
    oh2 = jnp.where(lane_f == rt[:, 1:2], 1.0, 0.0)
    oh = oh1 + oh2

    @pl.when(i == 0)
    def _():
        cnt = cnt_ref[0:1, :]
        ntile = jnp.floor((cnt + (tm - 1.0)) * (1.0 / tm))
        before = jnp.where(_iota((LANES, LANES), 0) < _iota((LANES, LANES), 1), 1.0, 0.0)
        st_tiles = _mm(jnp.broadcast_to(ntile, (SUBLANES, LANES)), before)[0:1]
        end_tiles = st_tiles + ntile
        start_s[...] = st_tiles * tm
        carry_s[...] = jnp.zeros((1, LANES), F32)
        is_e1 = _iota((1, LANES), 1) < N_EXPERTS
        n_used = jnp.sum(jnp.where(is_e1, ntile, 0.0), axis=-1, keepdims=True)
        last = jnp.sum(jnp.where(is_e1 & (end_tiles <= n_used - 1.0), 1.0, 0.0), axis=-1, keepdims=True)
        ti = _iota((n_tiles, LANES), 0).astype(F32)
        lane_t = _iota((n_tiles, LANES), 1)
        te = jnp.sum(jnp.where((lane_t < N_EXPERTS) & (end_tiles <= ti), 1.0, 0.0), axis=-1, keepdims=True)
        te = jnp.minimum(te, last)
        tmeta_ref[...] = jnp.where(lane_t == 0, te, jnp.where(lane_t == 1, n_used, 0.0))
        srow = _iota((SUBLANES, LANES), 0)
        emeta_ref[...] = jnp.where(srow == 0, ntile, jnp.where(srow == 1, (end_tiles - 1.0) * tm, 0.0))

    earlier = jnp.where(_iota((tr, tr), 0) > _iota((tr, tr), 1), 1.0, 0.0)
    base = start_s[...] + carry_s[...] + _mm(earlier, oh)
    p1 = jnp.sum(oh1 * base, axis=-1, keepdims=True)
    p2 = jnp.sum(oh2 * base, axis=-1, keepdims=True)
    pos_ref[...] = jnp.where(lane == 0, p1, jnp.where(lane == 1, p2, 0.0))
    carry_s[...] = carry_s[...] + jnp.sum(oh, axis=0, keepdims=True)


def _route_plan(counts, rt, tm, n_tiles):
    t = rt.shape[0]
    tr = next(c for c in (1024, 512, 256, tm) if t % c == 0)
    const = lambda i: (0, 0)
    pos, tmeta, emeta = pl.pallas_call(
        functools.partial(_route_plan_kernel, tr, tm, n_tiles),
        grid=(t // tr,),
        in_specs=[pl.BlockSpec((SUBLANES, LANES), const), pl.BlockSpec((tr, LANES), lambda i: (i, 0))],
        out_specs=[pl.BlockSpec((tr, LANES), lambda i: (i, 0)), pl.BlockSpec((n_tiles, LANES), const),
                   pl.BlockSpec((SUBLANES, LANES), const)],
        out_shape=[jax.ShapeDtypeStruct((t, LANES), F32), jax.ShapeDtypeStruct((n_tiles, LANES), F32),
                   jax.ShapeDtypeStruct((SUBLANES, LANES), F32)],
        scratch_shapes=[pltpu.VMEM((1, LANES), F32), pltpu.VMEM((1, LANES), F32)],
        compiler_params=pltpu.CompilerParams(dimension_semantics=("arbitrary",), vmem_limit_bytes=VMEM_LIMIT),
        name="route_plan",
    )(counts, rt)
    pos1 = pos[:, 0].astype(jnp.int32)
    pos2 = pos[:, 1].astype(jnp.int32)
    tile_expert = tmeta[:, 0].astype(jnp.int32)
    n_used = tmeta[0:1, 1].astype(jnp.int32)
    has_rows = (emeta[0, :N_EXPERTS] > 0).astype(jnp.int32)
    last_tile_row = jnp.maximum(emeta[1, :N_EXPERTS], 0.0).astype(jnp.int32)
    return pos1, pos2, tile_expert, n_used, has_rows, last_tile_row


def _scatter_kernel(tm, n_steps, n_tiles, min_tiles, zon_ref, zrow_ref, nu_ref, p1_ref, p2_ref, p1p_ref, p2p_ref,
                    h_ref, xs_hbm, zbuf, stage, sem1, sem2, zsem, tsem):
    i = pl.program_id(0)

    @pl.when(i == 0)
    def _():
        zbuf[...] = jnp.zeros(zbuf.shape, F32)

        def zero_copy(e):
            return pltpu.make_async_copy(zbuf, xs_hbm.at[pl.ds(zrow_ref[e], tm)], zsem.at[e])

        def tail_copy(k):
            return pltpu.make_async_copy(zbuf, xs_hbm.at[pl.ds((min_tiles + k) * tm, tm)], tsem.at[k])

        def zstart(e, c):
            @pl.when(zon_ref[e] > 0)
            def _():
                zero_copy(e).start()
            return c

        def zwait(e, c):
            @pl.when(zon_ref[e] > 0)
            def _():
                zero_copy(e).wait()
            return c

        def tstart(k, c):
            @pl.when(min_tiles + k >= nu_ref[0])
            def _():
                tail_copy(k).start()
            return c

        def twait(k, c):
            @pl.when(min_tiles + k >= nu_ref[0])
            def _():
                tail_copy(k).wait()
            return c

        lax.fori_loop(0, N_EXPERTS, zstart, 0)
        lax.fori_loop(0, n_tiles - min_tiles, tstart, 0)
        lax.fori_loop(0, N_EXPERTS, zwait, 0)
        lax.fori_loop(0, n_tiles - min_tiles, twait, 0)

    slot = i % 2
    stage[slot] = h_ref[...]

    def copies(pa_ref, pb_ref, s, grp, k, r):
        src = _tile_of_row(stage.at[s], grp, k)
        return (pltpu.make_async_copy(src, xs_hbm.at[pl.ds(pa_ref[0, 0, r], 1)], sem1.at[s]),
                pltpu.make_async_copy(src, xs_hbm.at[pl.ds(pb_ref[0, 0, r], 1)], sem2.at[s]))

    def start(grp, k, r):
        c1, c2 = copies(p1_ref, p2_ref, slot, grp, k, r)
        c1.start(priority=0)
        c2.start(priority=1)

    def wait_of(pa_ref, pb_ref, s):
        def wait(grp, k, r):
            c1, c2 = copies(pa_ref, pb_ref, s, grp, k, r)
            c1.wait()
            c2.wait()
        return wait

    _for_each_row(tm, start)

    @pl.when(i >= 1)
    def _():
        _for_each_row(tm, wait_of(p1p_ref, p2p_ref, 1 - slot))

    @pl.when(i == n_steps - 1)
    def _():
        _for_each_row(tm, wait_of(p1_ref, p2_ref, slot))


def _scatter_rows(h2, pos1, pos2, has_rows, last_tile_row, n_used, n_tiles, tm):
    t = h2.shape[0] * SUBLANES
    nb = t // tm
    min_tiles = (2 * t) // tm
    smem_blk = pl.BlockSpec((1, 1, tm), lambda i, zon, zrow, nu: (i, 0, 0), memory_space=pltpu.SMEM)
    smem_prev = pl.BlockSpec((1, 1, tm), lambda i, zon, zrow, nu: (jnp.maximum(i - 1, 0), 0, 0),
                             memory_space=pltpu.SMEM)
    p1, p2 = pos1.reshape(nb, 1, tm), pos2.reshape(nb, 1, tm)
    grid_spec = pltpu.PrefetchScalarGridSpec(
        num_scalar_prefetch=3,
        grid=(nb,),
        in_specs=[smem_blk, smem_blk, smem_prev, smem_prev,
                  pl.BlockSpec(_row_tile_shape(tm), lambda i, zon, zrow, nu: (i, 0, 0, 0))],
        out_specs=pl.BlockSpec(memory_space=pl.ANY),
        scratch_shapes=[pltpu.VMEM((tm, SUBLANES, LANES), F32), pltpu.VMEM((2,) + _row_tile_shape(tm), F32),
                        pltpu.SemaphoreType.DMA((2,)), pltpu.SemaphoreType.DMA((2,)),
                        pltpu.SemaphoreType.DMA((N_EXPERTS,)), pltpu.SemaphoreType.DMA((n_tiles - min_tiles,))],
    )
    return pl.pallas_call(
        functools.partial(_scatter_kernel, tm, nb, n_tiles, min_tiles),
        grid_spec=grid_spec,
        out_shape=jax.ShapeDtypeStruct((n_tiles * tm, SUBLANES, LANES), F32),
        compiler_params=pltpu.CompilerParams(dimension_semantics=("arbitrary",), vmem_limit_bytes=VMEM_LIMIT),
        name="moe_scatter",
    )(has_rows, last_tile_row, n_used, p1, p2, p1, p2, h2)


def _ffn_kernel(layer, te_ref, nu_ref, slot_ref, nxt_ref, xs_ref, wg_hbm, wu_hbm, wd_hbm, ys_ref,
                wg_s, wu_s, wd_s, stage_g, stage_u, stage_d, wsem):
    i = pl.program_id(0)
    used = i < nu_ref[0]
    new_expert = (i == 0) | (te_ref[i] != te_ref[jnp.maximum(i - 1, 0)])

    def weight_copies(e, s):
        return (pltpu.make_async_copy(wg_hbm.at[layer, e], stage_g.at[s], wsem.at[s, 0]),
                pltpu.make_async_copy(wu_hbm.at[layer, e], stage_u.at[s], wsem.at[s, 1]),
                pltpu.make_async_copy(wd_hbm.at[layer, e], stage_d.at[s], wsem.at[s, 2]))

    @pl.when(i == 0)
    def _():
        for c in weight_copies(te_ref[0], slot_ref[0]):
            c.start()

    @pl.when(used & new_expert)
    def _():
        s = slot_ref[i]
        for c in weight_copies(te_ref[i], s):
            c.wait()

        @pl.when(nxt_ref[i] >= 0)
        def _():
            for c in weight_copies(nxt_ref[i], 1 - s):
                c.start()

        wg_s[...] = stage_g[s].astype(BF16)
        wu_s[...] = stage_u[s].astype(BF16)
        wd_s[...] = stage_d[s].astype(BF16)

    @pl.when(used)
    def _():
        x = _from_row_tiles(xs_ref).astype(BF16)
        gate = jnp.dot(x, wg_s[...], preferred_element_type=F32)
        up = jnp.dot(x, wu_s[...], preferred_element_type=F32)
        act = (_silu(gate) * up).astype(BF16)
        _to_row_tiles(ys_ref, jnp.dot(act, wd_s[...], preferred_element_type=F32))

    @pl.when(jnp.logical_not(used))
    def _():
        ys_ref[...] = jnp.zeros(ys_ref.shape, F32)


def _expert_ffn(xs, tile_expert, n_used, w_gate, w_up, w_down, layer, tm):
    p = xs.shape[0]
    n_tiles = p // tm
    xs = xs.reshape(_row_tile_shape(p))
    _, _, d, de = w_gate.shape
    idx = jnp.arange(n_tiles, dtype=jnp.int32)
    first = jnp.concatenate([jnp.ones((1,), bool), tile_expert[1:] != tile_expert[:-1]]) & (idx < n_used[0])
    slot = ((jnp.cumsum(first.astype(jnp.int32)) - 1) & 1).astype(jnp.int32)
    first_at_or_after = jnp.flip(lax.cummin(jnp.flip(jnp.where(first, idx, n_tiles))))
    nxt_pos = jnp.concatenate([first_at_or_after[1:], jnp.full((1,), n_tiles, jnp.int32)])
    nxt = jnp.where(nxt_pos < n_tiles, tile_expert[jnp.minimum(nxt_pos, n_tiles - 1)], -1).astype(jnp.int32)
    any_spec = pl.BlockSpec(memory_space=pl.ANY)
    grid_spec = pltpu.PrefetchScalarGridSpec(
        num_scalar_prefetch=4,
        grid=(n_tiles,),
        in_specs=[pl.BlockSpec(_row_tile_shape(tm), lambda i, te, nu, sl, nx: (jnp.minimum(i, nu[0] - 1), 0, 0, 0)),
                  any_spec, any_spec, any_spec],
        out_specs=pl.BlockSpec(_row_tile_shape(tm), lambda i, te, nu, sl, nx: (i, 0, 0, 0)),
        scratch_shapes=[pltpu.VMEM((d, de), BF16), pltpu.VMEM((d, de), BF16), pltpu.VMEM((de, d), BF16),
                        pltpu.VMEM((2, d, de), F32), pltpu.VMEM((2, d, de), F32), pltpu.VMEM((2, de, d), F32),
                        pltpu.SemaphoreType.DMA((2, 3))],
    )
    return pl.pallas_call(
        functools.partial(_ffn_kernel, layer),
        grid_spec=grid_spec,
        out_shape=jax.ShapeDtypeStruct(_row_tile_shape(p), F32),
        compiler_params=pltpu.CompilerParams(dimension_semantics=("arbitrary",), vmem_limit_bytes=VMEM_LIMIT),
        name="expert_ffn",
    )(tile_expert, n_used, slot, nxt, xs, w_gate, w_up, w_down)


def _combine_kernel(tm, n_steps, na, last_layer, p1_ref, p2_ref, p1n_ref, p2n_ref, x1_ref, rt_ref, g_ref, *rest):
    if last_layer:
        ys_hbm, outa_ref, outb_ref, buf1, buf2, sem1, sem2 = rest
    else:
        w_ref, ys_hbm, outa_ref, outb_ref, z_ref, buf1, buf2, sem1, sem2, h_s = rest
    i = pl.program_id(0)
    slot = i % 2

    def copies(pa_ref, pb_ref, s, grp, k, r):
        return (pltpu.make_async_copy(ys_hbm.at[pl.ds(pa_ref[0, 0, r], 1)], _tile_of_row(buf1.at[s], grp, k),
                                      sem1.at[s]),
                pltpu.make_async_copy(ys_hbm.at[pl.ds(pb_ref[0, 0, r], 1)], _tile_of_row(buf2.at[s], grp, k),
                                      sem2.at[s]))

    def start(pa_ref, pb_ref, s):
        def go(grp, k, r):
            c1, c2 = copies(pa_ref, pb_ref, s, grp, k, r)
            c1.start(priority=0)
            c2.start(priority=1)
        _for_each_row(tm, go)

    @pl.when(i == 0)
    def _():
        start(p1_ref, p2_ref, 0)

    @pl.when(i + 1 < n_steps)
    def _():
        start(p1n_ref, p2n_ref, 1 - slot)

    if not last_layer:
        @pl.when(i >= 1)
        def _():
            z_ref[...] = jnp.dot(h_s[...], w_ref[...], preferred_element_type=F32)

    def wait(grp, k, r):
        c1, c2 = copies(p1_ref, p2_ref, slot, grp, k, r)
        c1.wait()
        c2.wait()

    @pl.when(i < n_steps)
    def _():
        _for_each_row(tm, wait)
        rt = rt_ref[...]
        x = x1_ref[...] + rt[:, 2:3] * _from_row_tiles(buf1.at[slot]) + rt[:, 3:4] * _from_row_tiles(buf2.at[slot])
        h = x * lax.rsqrt(jnp.mean(x * x, axis=-1, keepdims=True) + EPS) * g_ref[...]
        if last_layer:
            x = h
        else:
            h_s[...] = h.astype(BF16)

        @pl.when(i < na)
        def _():
            outa_ref[...] = x

        @pl.when(i >= na)
        def _():
            outb_ref[...] = x


def _combine(x1, rt, ys, pos1, pos2, gain, next_w_in, ta, tm):
    t, d = x1.shape
    nb = t // tm
    na = ta // tm
    last_layer = next_w_in is None
    cur = lambda i: jnp.minimum(i, nb - 1)
    row = lambda i: (cur(i), 0)
    const = lambda i: (0, 0)
    smem_blk = pl.BlockSpec((1, 1, tm), lambda i: (cur(i), 0, 0), memory_space=pltpu.SMEM)
    smem_next = pl.BlockSpec((1, 1, tm), lambda i: (jnp.minimum(i + 1, nb - 1), 0, 0), memory_space=pltpu.SMEM)
    p1, p2 = pos1.reshape(nb, 1, tm), pos2.reshape(nb, 1, tm)
    in_specs = [smem_blk, smem_blk, smem_next, smem_next, pl.BlockSpec((tm, d), row),
                pl.BlockSpec((tm, LANES), row), pl.BlockSpec((1, d), const)]
    operands = [p1, p2, p1, p2, x1, rt, gain]
    out_specs = [pl.BlockSpec((tm, d), lambda i: (jnp.minimum(i, na - 1), 0)),
                 pl.BlockSpec((tm, d), lambda i: (jnp.clip(i - na, 0, nb - na - 1), 0))]
    out_shape = [jax.ShapeDtypeStruct((ta, d), F32), jax.ShapeDtypeStruct((t - ta, d), F32)]
    scratch = [pltpu.VMEM((2,) + _row_tile_shape(tm), F32), pltpu.VMEM((2,) + _row_tile_shape(tm), F32),
               pltpu.SemaphoreType.DMA((2,)), pltpu.SemaphoreType.DMA((2,))]
    if not last_layer:
        nz = next_w_in.shape[1]
        in_specs.append(pl.BlockSpec((d, nz), const))
        operands.append(next_w_in)
        out_specs.append(pl.BlockSpec((tm, nz), lambda i: (jnp.maximum(i - 1, 0), 0)))
        out_shape.append(jax.ShapeDtypeStruct((t, nz), F32))
        scratch.append(pltpu.VMEM((tm, d), BF16))
    outs = pl.pallas_call(
        functools.partial(_combine_kernel, tm, nb, na, last_layer),
        grid=(nb if last_layer else nb + 1,),
        in_specs=in_specs + [pl.BlockSpec(memory_space=pl.ANY)],
        out_specs=out_specs,
        out_shape=out_shape,
        scratch_shapes=scratch,
        compiler_params=pltpu.CompilerParams(dimension_semantics=("arbitrary",), vmem_limit_bytes=VMEM_LIMIT),
        name="moe_combine",
    )(*operands, ys.reshape(-1, SUBLANES, LANES))
    return (outs[0], outs[1]), (None if last_layer else outs[2])


def _block_diag(w):
    h, a, b = w.shape
    return jnp.einsum("hij,hk->hikj", w, jnp.eye(h, dtype=w.dtype)).reshape(h * a, h * b)


def _layer_params(l, p):
    g = GROUP_W
    w_in = p["w_in"][l]
    d = w_in.shape[0]
    w_perm = jnp.concatenate([w_in[:, :8 * g], w_in[:, 8 * g + 8:], w_in[:, 8 * g:8 * g + 8],
                              jnp.zeros((d, LANES - 8), w_in.dtype)], axis=1)
    rows = [p["ret_gn_g"][l], jnp.tile(p["gdn_norm_g"][l], N_HEADS), p["lru_conv_b"][l], p["lru_ba"][l],
            p["lru_bx"][l], p["lru_lambda"][l], p["pool_b"][l], p["pool_scale"][l]]
    pvec = jnp.concatenate([jnp.stack(rows), p["lru_conv_w"][l], jnp.zeros((4, g), F32)], axis=0)
    lane_pad = lambda v, n: jnp.concatenate([v, jnp.zeros(v.shape[:-1] + (LANES - n,), F32)], axis=-1)
    gabp = jnp.concatenate([lane_pad(jnp.stack([p["gdn_a_log"][l], p["gdn_dt_bias"][l]]), N_HEADS),
                            jnp.zeros((6, LANES), F32)], axis=0)
    n_route = N_GROUPS + N_EXPERTS
    wr = lane_pad(jnp.concatenate([p["router_group_w"][l],
                                   jnp.transpose(p["router_expert_w"][l], (1, 0, 2)).reshape(d, N_EXPERTS)],
                                  axis=1), n_route)
    br = lane_pad(jnp.concatenate([p["router_group_b"][l], p["router_expert_b"][l].reshape(N_EXPERTS)])[None, :],
                  n_route)
    return dict(
        norm1_g=p["norm1_g"][l][None, :], w_in=w_perm.astype(BF16), pvec=pvec, gdn_conv_w=p["gdn_conv_w"][l],
        gabp=gabp, lru_wa=_block_diag(p["lru_wa"][l]).astype(BF16), lru_wx=_block_diag(p["lru_wx"][l]).astype(BF16),
        pool_w=_block_diag(p["pool_w"][l]).astype(BF16), w_out=p["w_out"][l].astype(BF16),
        norm2_g=p["norm2_g"][l][None, :], wr=wr, br=br)


def kernel(x_prompt, x_sample, state_ret, state_gdn, state_gdn_conv, state_lru, state_lru_conv, state_pool,
           norm1_g, w_in, ret_gn_g, gdn_conv_w, gdn_a_log, gdn_dt_bias, gdn_norm_g,
           lru_conv_w, lru_conv_b, lru_wa, lru_ba, lru_wx, lru_bx, lru_lambda,
           pool_w, pool_b, pool_scale, w_out, norm2_g,
           router_group_w, router_group_b, router_expert_w, router_expert_b,
           moe_w_gate, moe_w_up, moe_w_down, final_g):
    params = dict(norm1_g=norm1_g, w_in=w_in, ret_gn_g=ret_gn_g, gdn_conv_w=gdn_conv_w, gdn_a_log=gdn_a_log,
                  gdn_dt_bias=gdn_dt_bias, gdn_norm_g=gdn_norm_g, lru_conv_w=lru_conv_w, lru_conv_b=lru_conv_b,
                  lru_wa=lru_wa, lru_ba=lru_ba, lru_wx=lru_wx, lru_bx=lru_bx, lru_lambda=lru_lambda,
                  pool_w=pool_w, pool_b=pool_b, pool_scale=pool_scale, w_out=w_out, norm2_g=norm2_g,
                  router_group_w=router_group_w, router_group_b=router_group_b,
                  router_expert_w=router_expert_w, router_expert_b=router_expert_b)
    bp, lp_, d = x_prompt.shape
    bs, ls, _ = x_sample.shape
    depth = w_in.shape[0]
    tp, ts_ = bp * lp_, bs * ls
    t = tp + ts_
    tile_p = min(256, lp_)
    tm = 256
    n_tiles = (2 * t + N_EXPERTS * (tm - 1) + tm - 1) // tm

    x = (x_prompt.reshape(tp, d), x_sample.reshape(ts_, d))
    sample_states = (state_ret, state_gdn, state_gdn_conv, state_lru, state_lru_conv, state_pool)
    zero_states = tuple(jnp.zeros((1, bp) + s.shape[2:], F32) for s in sample_states)
    new_p = new_s = None
    lps = [_layer_params(l, params) for l in range(depth)]
    z = _norm_win(x, lps[0]["norm1_g"], lps[0]["w_in"], tm)
    for l in range(depth):
        lp = lps[l]
        last = l == depth - 1
        mix_p, new_p = _mixer(z, 0, bp, lp_, tile_p, 1, 0, zero_states, 0, new_p, lp)
        mix_s, new_s = _mixer(z, tp, bs, ls, ls, SAMPLE_SEQS_PER_STEP, PAST_LEN, sample_states, l, new_s, lp)
        x1, h2, rt, counts = _wout_route((mix_p, mix_s), x, lp["w_out"], lp["norm2_g"], lp["wr"], lp["br"], tm)
        pos1, pos2, tile_expert, n_used, has_rows, last_tile_row = _route_plan(counts, rt, tm, n_tiles)
        xs = _scatter_rows(h2, pos1, pos2, has_rows, last_tile_row, n_used, n_tiles, tm)
        ys = _expert_ffn(xs, tile_expert, n_used, moe_w_gate, moe_w_up, moe_w_down, l, tm)
        x, z = _combine(x1, rt, ys, pos1, pos2, final_g[None, :] if last else lps[l + 1]["norm1_g"],
                        None if last else lps[l + 1]["w_in"], tp, tm)
    y_prompt = x[0].reshape(bp, lp_, d)
    y_sample = x[1].reshape(bs, ls, d)
    unpack = lambda st: (st[0], st[1], st[2], st[3][:, :, 0, :], st[4], st[5])
    return (y_prompt, y_sample, *unpack(new_p), *unpack(new_s))
```

```python
import functools

import jax
import jax.numpy as jnp
from jax import lax
from jax.experimental import pallas as pl
from jax.experimental.pallas import tpu as pltpu

F32 = jnp.float32
BF16 = jnp.bfloat16

EPS = 1e-6
HEAD_DIM = 64
N_HEADS = 4
GROUP_W = 256
CONV_W = 4
LRU_C = 8.0
POOL_WINDOWS = (2, 4, 8, 16)
POOL_BUF = 15
ROPE_BASE = 10000.0
N_GROUPS = 4
EXPERTS_PER_GROUP = 8
N_EXPERTS = 32
PAST_LEN = 16384

LANES = 128
SUBLANES = 8
HIST = 16
Z_AB = 11 * GROUP_W
NZ = Z_AB + LANES
INV_BASE = 8
RESID_PASSES = 3
SAMPLE_SEQS_PER_STEP = 16
VMEM_LIMIT = 56 * 1024 * 1024

NN = (((1,), (0,)), ((), ()))
NT = (((1,), (1,)), ((), ()))
TN = (((0,), (0,)), ((), ()))


def _mm(a, b, dims=NN, passes=1):
    ah = a.astype(BF16)
    bh = b.astype(BF16)
    dot = functools.partial(lax.dot_general, dimension_numbers=dims, preferred_element_type=F32)
    if passes == 1:
        return dot(ah, bh)
    al = (a - ah.astype(F32)).astype(BF16)
    bl = (b - bh.astype(F32)).astype(BF16)
    return dot(ah, bh) + (dot(ah, bl) + dot(al, bh))


def _sigmoid(x):
    return 1.0 / (1.0 + jnp.exp(-x))


def _silu(x):
    return x * _sigmoid(x)


def _softplus(x):
    return jnp.maximum(x, 0.0) + jnp.log1p(jnp.exp(-jnp.abs(x)))


def _gelu_tanh(x):
    return 0.5 * x * (1.0 + jnp.tanh(0.7978845608028654 * (x + 0.044715 * (x * x * x))))


def _iota(shape, axis):
    return lax.broadcasted_iota(jnp.int32, shape, axis)


def _shift_rows(x, d, fill, seg=None):
    t = _iota(x.shape, 0)
    if seg is not None:
        t = t & (seg - 1)
    return jnp.where(t >= d, pltpu.roll(x, d, 0), fill)


def _cumsum_rows(x, seg=None):
    d = 1
    while d < (seg or x.shape[0]):
        x = x + _shift_rows(x, d, 0.0, seg)
        d *= 2
    return x


def _swap_half_heads(x):
    n = x.shape[1]
    lower = (_iota(x.shape, 1) & (HEAD_DIM - 1)) < HEAD_DIM // 2
    return jnp.where(lower, pltpu.roll(x, n - HEAD_DIM // 2, 1), pltpu.roll(x, HEAD_DIM // 2, 1))


def _inv_unit_lower(mats, c=None):
    shape = mats[0].shape
    c = c or shape[0]
    row = _iota(shape, 0)
    col = _iota(shape, 1)
    nb = min(INV_BASE, c)
    shift = nb.bit_length() - 1
    in_base = (row >> shift) == (col >> shift)
    ps = [jnp.where(in_base, a, 0.0) for a in mats]
    eye = jnp.where(row == col, 1.0, 0.0)
    xs = [eye - d for d in ps]
    k = 2
    while k < nb:
        ps = [_mm(p, p) for p in ps]
        xs = [x + _mm(x, p) for x, p in zip(xs, ps)]
        k *= 2
    m = nb
    while m < c:
        s1 = m.bit_length() - 1
        in_off = ((row >> (s1 + 1)) == (col >> (s1 + 1))) & ((row >> s1) != (col >> s1))
        ys = [_mm(x, jnp.where(in_off, a, 0.0)) for x, a in zip(xs, mats)]
        xs = [x - _mm(y, x) for x, y in zip(xs, ys)]
        m *= 2
    return xs


def _solve_unit_lower(mats, tinvs, ws):
    us = [_mm(t, w) for t, w in zip(tinvs, ws)]
    rs = [w - (u + _mm(a, u, passes=RESID_PASSES)) for a, u, w in zip(mats, us, ws)]
    return [u + _mm(t, r) for u, t, r in zip(us, tinvs, rs)]


def _pair_specs(na, tm, d):
    return [pl.BlockSpec((tm, d), lambda i: (jnp.minimum(i, na - 1), 0)),
            pl.BlockSpec((tm, d), lambda i: (jnp.maximum(i - na, 0), 0))]


def _pair_tile(na, a_ref, b_ref):
    return jnp.where(pl.program_id(0) < na, a_ref[...], b_ref[...])


def _norm_win_kernel(na, xa_ref, xb_ref, g_ref, w_ref, z_ref):
    x = _pair_tile(na, xa_ref, xb_ref)
    h = x * lax.rsqrt(jnp.mean(x * x, axis=-1, keepdims=True) + EPS) * g_ref[...]
    z_ref[...] = jnp.dot(h.astype(BF16), w_ref[...], preferred_element_type=F32)


def _norm_win(x_pair, g, w_bf16, tm):
    xa, xb = x_pair
    d = xa.shape[1]
    na, t = xa.shape[0] // tm, xa.shape[0] + xb.shape[0]
    nz = w_bf16.shape[1]
    return pl.pallas_call(
        functools.partial(_norm_win_kernel, na),
        grid=(t // tm,),
        in_specs=_pair_specs(na, tm, d) + [pl.BlockSpec((1, d), lambda i: (0, 0)),
                                           pl.BlockSpec((d, nz), lambda i: (0, 0))],
        out_specs=pl.BlockSpec((tm, nz), lambda i: (i, 0)),
        out_shape=jax.ShapeDtypeStruct((t, nz), F32),
        compiler_params=pltpu.CompilerParams(dimension_semantics=("arbitrary",), vmem_limit_bytes=VMEM_LIMIT),
        name="norm_win",
    )(xa, xb, g, w_bf16)


def _retention_tile(zq, zk, zv, cos, sin, intra_ref, qdec, kdec, sdec, s_ref):
    rq = zq * cos + _swap_half_heads(zq) * sin
    rk = (zk * cos + _swap_half_heads(zk) * sin) * (HEAD_DIM ** -0.5)
    kd = rk * kdec
    heads = range(N_HEADS)
    sls = [slice(h * HEAD_DIM, (h + 1) * HEAD_DIM) for h in heads]
    ss = [s_ref[h] for h in heads]
    scores = [_mm(rq[:, sl], rk[:, sl], NT) * intra_ref[h] for h, sl in zip(heads, sls)]
    os_ = [_mm(sc, zv[:, sl]) + _mm(rq[:, sl], s) * qdec[:, sl] for sc, s, sl in zip(scores, ss, sls)]
    s_new = [s * sdec[h:h + 1, :HEAD_DIM] + _mm(kd[:, sl], zv[:, sl], TN) for h, s, sl in zip(heads, ss, sls)]
    for h in heads:
        s_ref[h] = s_new[h]
    outs = []
    for o in os_:
        mu = jnp.mean(o, axis=-1, keepdims=True)
        oc = o - mu
        var = jnp.mean(oc * oc, axis=-1, keepdims=True)
        outs.append(oc * lax.rsqrt(var + EPS))
    return jnp.concatenate(outs, axis=-1)


def _split3(x):
    x0 = x.astype(BF16)
    r1 = x - x0.astype(F32)
    x1 = r1.astype(BF16)
    x2 = (r1 - x1.astype(F32)).astype(BF16)
    return x0, x1, x2


def _gdn_tile(gq, gk, gv, gab, s_ref):
    c = gq.shape[0]
    gc = _cumsum_rows(gab)
    egc = jnp.exp(gc)
    row = _iota((c, c), 0)
    col = _iota((c, c), 1)
    incl = row >= col
    strict = row > col
    heads = range(N_HEADS)
    sls = [slice(h * HEAD_DIM, (h + 1) * HEAD_DIM) for h in heads]
    ss = [s_ref[h] for h in heads]
    gcols = [gc[:, h:h + 1] for h in heads]
    bcols = [gab[:, N_HEADS + h:N_HEADS + h + 1] for h in heads]
    ecols = [egc[:, h:h + 1] for h in heads]
    gct = gc.T
    decs = [jnp.where(incl, jnp.exp(jnp.where(incl, gcols[h] - gct[h:h + 1, :], 0.0)), 0.0) for h in heads]
    mats = [jnp.where(strict, bcols[h] * decs[h] * _mm(gk[:, sl], gk[:, sl], NT), 0.0) for h, sl in zip(heads, sls)]
    tinvs = _inv_unit_lower(mats)
    ws = [bcols[h] * (gv[:, sl] - ecols[h] * _mm(gk[:, sl], ss[h])) for h, sl in zip(heads, sls)]
    us = _solve_unit_lower(mats, tinvs, ws)
    qks = [_mm(gq[:, sl], gk[:, sl], NT) * decs[h] for h, sl in zip(heads, sls)]
    outs = [ecols[h] * _mm(gq[:, sl], ss[h]) + _mm(qks[h], us[h]) for h, sl in zip(heads, sls)]
    glasts = [gc[c - 1:c, h:h + 1] for h in heads]
    s_new = [jnp.exp(glasts[h]) * ss[h] + _mm(jnp.exp(glasts[h] - gcols[h]) * gk[:, sl], us[h], TN)
             for h, sl in zip(heads, sls)]
    for h in heads:
        s_ref[h] = s_new[h]
    return outs


def _mixer_kernel(nt, ts, nb, pos_base,
                  z_ref, cos_ref, sin_ref, intra_ref, qdec_ref, kdec_ref, sdec_ref, pv_ref, gcw_ref, gabp_ref,
                  wa_ref, wx_ref, pw_ref, sret0, sgdn0, gconv0, lru0, lconv0, pool0,
                  mix_ref, ret_o, gdn_o, gconv_o, lru_o, lconv_o, pool_o,
                  s_ret, s_gdn, ext_g, ext_l, ext_p, h_lru):
    n = pl.program_id(1)
    g = GROUP_W
    base = HIST - (CONV_W - 1)

    @pl.when(n == 0)
    def _():
        s_ret[...] = sret0[...]
        s_gdn[...] = sgdn0[...]
        ext_g[:, 0:HIST, :] = jnp.zeros((nb, HIST, 3 * g), F32)
        ext_g[:, base:HIST, :] = gconv0[...]
        ext_l[:, 0:HIST, :] = jnp.zeros((nb, HIST, g), F32)
        ext_l[:, base:HIST, :] = lconv0[...]
        ext_p[:, 0:HIST, :] = jnp.zeros((nb, HIST, g), F32)
        ext_p[:, HIST - POOL_BUF:HIST, :] = pool0[...]
        h_lru[...] = lru0[...]

    pv = pv_ref[...]
    ret_gn_g, gdn_norm_g = pv[0:1], pv[1:2]
    lru_conv_b, lru_ba, lru_bx, lru_lambda = pv[2:3], pv[3:4], pv[4:5], pv[5:6]
    pool_b, pool_scale = pv[6:7], pv[7:8]
    lru_conv_w = pv[8:12]
    gcw = gcw_ref[...]
    gabp = gabp_ref[...]

    for j in range(nb):
        rows = slice(j * ts, (j + 1) * ts)
        eg, el, ep = ext_g.at[j], ext_l.at[j], ext_p.at[j]

        ro = _retention_tile(z_ref[rows, 0:g], z_ref[rows, g:2 * g], z_ref[rows, 2 * g:3 * g], cos_ref[...],
                             sin_ref[...], intra_ref, qdec_ref[...], kdec_ref[...], sdec_ref[...], s_ret.at[j])
        mix_ref[rows, 0:g] = ro * ret_gn_g * _silu(z_ref[rows, 3 * g:4 * g])

        eg[HIST:HIST + ts, :] = z_ref[rows, 4 * g:7 * g]
        y = eg[base:base + ts, :] * gcw[0:1]
        for i in range(1, CONV_W):
            y = y + eg[base + i:base + i + ts, :] * gcw[i:i + 1]
        qkv = _silu(y)
        zab = z_ref[rows, Z_AB:Z_AB + LANES]
        lane = _iota(zab.shape, 1)
        log_a = -jnp.exp(gabp[0:1]) * _softplus(zab + gabp[1:2])
        gab = jnp.where(lane < N_HEADS, log_a, jnp.where(lane < 2 * N_HEADS, _sigmoid(zab), 0.0))
        gq_parts, gk_parts = [], []
        for h in range(N_HEADS):
            qh = qkv[:, h * HEAD_DIM:(h + 1) * HEAD_DIM]
            kh = qkv[:, g + h * HEAD_DIM:g + (h + 1) * HEAD_DIM]
            gq_parts.append(qh * lax.rsqrt(jnp.sum(qh * qh, axis=-1, keepdims=True) + EPS) * (HEAD_DIM ** -0.5))
            gk_parts.append(kh * lax.rsqrt(jnp.sum(kh * kh, axis=-1, keepdims=True) + EPS))
        go = _gdn_tile(jnp.concatenate(gq_parts, axis=-1), jnp.concatenate(gk_parts, axis=-1),
                       qkv[:, 2 * g:3 * g], gab, s_gdn.at[j])
        go = [o * lax.rsqrt(jnp.mean(o * o, axis=-1, keepdims=True) + EPS) for o in go]
        mix_ref[rows, g:2 * g] = jnp.concatenate(go, axis=-1) * gdn_norm_g * _silu(z_ref[rows, 7 * g:8 * g])

        el[HIST:HIST + ts, :] = z_ref[rows, 8 * g:9 * g]
        xc = el[base:base + ts, :] * lru_conv_w[0:1]
        for i in range(1, CONV_W):
            xc = xc + el[base + i:base + i + ts, :] * lru_conv_w[i:i + 1]
        xc = xc + lru_conv_b
        r = _sigmoid(jnp.dot(xc.astype(BF16), wa_ref[...], preferred_element_type=F32) + lru_ba)
        gate_i = _sigmoid(jnp.dot(xc.astype(BF16), wx_ref[...], preferred_element_type=F32) + lru_bx)
        log_at = -LRU_C * r * _softplus(-lru_lambda)
        av = jnp.exp(log_at)
        bv = jnp.sqrt(-jnp.tanh(log_at) * (av * av + 1.0)) * (gate_i * xc)
        bv = bv + jnp.where(_iota(bv.shape, 0) == 0, av * h_lru[j], 0.0)
        d = 1
        while d < ts:
            bv = av * _shift_rows(bv, d, 0.0) + bv
            av = av * _shift_rows(av, d, 1.0)
            d *= 2
        h_lru[j] = bv[ts - 1:ts, :]
        mix_ref[rows, 2 * g:3 * g] = bv * _gelu_tanh(z_ref[rows, 9 * g:10 * g])

        u = z_ref[rows, 10 * g:11 * g]
        ep[HIST:HIST + ts, :] = u
        e = ep[...]
        sums = []
        step = 1
        for _ in POOL_WINDOWS:
            e = e + pltpu.roll(e, step, 0)
            sums.append(e[HIST:HIST + ts, :])
            step *= 2
        pos1 = (pos_base + 1 + n * ts + _iota((ts, g), 0)).astype(F32)
        lane_g = _iota((ts, g), 1) >> 6
        pooled = sums[-1] / jnp.minimum(pos1, float(POOL_WINDOWS[-1]))
        for gi in range(len(POOL_WINDOWS) - 2, -1, -1):
            pooled = jnp.where(lane_g == gi, sums[gi] / jnp.minimum(pos1, float(POOL_WINDOWS[gi])), pooled)
        pooled = pooled - u
        yp = jnp.dot(pooled.astype(BF16), pw_ref[...], preferred_element_type=F32) + pool_b
        mix_ref[rows, 3 * g:4 * g] = yp * pool_scale

        tail_g = eg[ts:ts + HIST, :]
        tail_l = el[ts:ts + HIST, :]
        tail_p = ep[ts:ts + HIST, :]
        eg[0:HIST, :] = tail_g
        el[0:HIST, :] = tail_l
        ep[0:HIST, :] = tail_p

    @pl.when(n == nt - 1)
    def _():
        ret_o[...] = s_ret[...]
        gdn_o[...] = s_gdn[...]
        gconv_o[...] = ext_g[:, base:HIST, :]
        lconv_o[...] = ext_l[:, base:HIST, :]
        pool_o[...] = ext_p[:, HIST - POOL_BUF:HIST, :]
        lru_o[...] = h_lru[...]


def _decode_mixer_kernel(ts, nb, pos_base,
                         z_ref, cos_ref, sin_ref, intra_ref, qdec_ref, kdec_ref, sdec_ref, pv_ref, gcw_ref, gabp_ref,
                         wa_ref, wx_ref, pw_ref, sret0, sgdn0, gconv0, lru0, lconv0, pool0,
                         mix_ref, ret_o, gdn_o, gconv_o, lru_o, lconv_o, pool_o,
                         ext_g, ext_l, ext_p):
    g = GROUP_W
    r_all = nb * ts
    base = HIST - (CONV_W - 1)
    seg_shift = ts.bit_length() - 1
    row2 = _iota((r_all, r_all), 0)
    col2 = _iota((r_all, r_all), 1)
    same = (row2 >> seg_shift) == (col2 >> seg_shift)
    incl = same & (row2 >= col2)
    strict = same & (row2 > col2)
    seq_l = _iota((r_all, LANES), 0) >> seg_shift
    half_l = _iota((r_all, LANES), 1) >> 6
    exp_masks = [((seq_l >> 1) == c) & ((seq_l & 1) == half_l) for c in range(nb // 2)]

    def expand(x):
        x2 = jnp.concatenate([x, x], axis=-1)
        return jnp.concatenate([jnp.where(m, x2, 0.0) for m in exp_masks], axis=-1)

    def stacked(state_ref, h):
        return jnp.concatenate([state_ref[j, h] for j in range(nb)], axis=0)

    def unstack(out_ref, h, sv):
        for j in range(nb):
            out_ref[j, h] = sv[j * HEAD_DIM:(j + 1) * HEAD_DIM, :]

    pv = pv_ref[...]
    ret_gn_g, gdn_norm_g = pv[0:1], pv[1:2]
    lru_conv_b, lru_ba, lru_bx, lru_lambda = pv[2:3], pv[3:4], pv[4:5], pv[5:6]
    pool_b, pool_scale = pv[6:7], pv[7:8]
    lru_conv_w = pv[8:12]
    gcw = gcw_ref[...]
    gabp = gabp_ref[...]

    ext_g[:, 0:HIST, :] = jnp.zeros((nb, HIST, 3 * g), F32)
    ext_g[:, base:HIST, :] = gconv0[...]
    ext_l[:, 0:HIST, :] = jnp.zeros((nb, HIST, g), F32)
    ext_l[:, base:HIST, :] = lconv0[...]
    ext_p[:, 0:HIST, :] = jnp.zeros((nb, HIST, g), F32)
    ext_p[:, HIST - POOL_BUF:HIST, :] = pool0[...]
    y_parts, xc_parts, sum_parts, h0_parts = [], [], [[] for _ in POOL_WINDOWS], []
    for j in range(nb):
        rows = slice(j * ts, (j + 1) * ts)
        eg, el, ep = ext_g.at[j], ext_l.at[j], ext_p.at[j]
        eg[HIST:HIST + ts, :] = z_ref[rows, 4 * g:7 * g]
        el[HIST:HIST + ts, :] = z_ref[rows, 8 * g:9 * g]
        ep[HIST:HIST + ts, :] = z_ref[rows, 10 * g:11 * g]
        y = eg[base:base + ts, :] * gcw[0:1]
        xc = el[base:base + ts, :] * lru_conv_w[0:1]
        for i in range(1, CONV_W):
            y = y + eg[base + i:base + i + ts, :] * gcw[i:i + 1]
            xc = xc + el[base + i:base + i + ts, :] * lru_conv_w[i:i + 1]
        y_parts.append(y)
        xc_parts.append(xc)
        e = ep[...]
        step = 1
        for wi in range(len(POOL_WINDOWS)):
            e = e + pltpu.roll(e, step, 0)
            sum_parts[wi].append(e[HIST:HIST + ts, :])
            step *= 2
        h0_parts.append(jnp.broadcast_to(lru0[j], (ts, g)))
        gconv_o[j] = eg[ts + base:ts + HIST, :]
        lconv_o[j] = el[ts + base:ts + HIST, :]
        pool_o[j] = ep[ts + HIST - POOL_BUF:ts + HIST, :]

    zq, zk, zv = z_ref[:, 0:g], z_ref[:, g:2 * g], z_ref[:, 2 * g:3 * g]
    cos, sin = cos_ref[...], sin_ref[...]
    qdec, kdec, sdec = qdec_ref[...], kdec_ref[...], sdec_ref[...]
    rq = zq * cos + _swap_half_heads(zq) * sin
    rk = (zk * cos + _swap_half_heads(zk) * sin) * (HEAD_DIM ** -0.5)
    kd = rk * kdec
    outs = []
    for h in range(N_HEADS):
        sl = slice(h * HEAD_DIM, (h + 1) * HEAD_DIM)
        q, k, v = rq[:, sl], rk[:, sl], zv[:, sl]
        sv = stacked(sret0, h)
        scores = _mm(q, k, NT) * intra_ref[h]
        o = _mm(scores, v) + _mm(expand(q), sv) * qdec[:, sl]
        unstack(ret_o, h, sv * sdec[h:h + 1, :HEAD_DIM] + _mm(expand(kd[:, sl]), v, TN))
        mu = jnp.mean(o, axis=-1, keepdims=True)
        oc = o - mu
        var = jnp.mean(oc * oc, axis=-1, keepdims=True)
        outs.append(oc * lax.rsqrt(var + EPS))
    mix_ref[:, 0:g] = jnp.concatenate(outs, axis=-1) * ret_gn_g * _silu(z_ref[:, 3 * g:4 * g])

    qkv = _silu(jnp.concatenate(y_parts, axis=0))
    zab = z_ref[:, Z_AB:Z_AB + LANES]
    lane = _iota(zab.shape, 1)
    log_a = -jnp.exp(gabp[0:1]) * _softplus(zab + gabp[1:2])
    gab = jnp.where(lane < N_HEADS, log_a, jnp.where(lane < 2 * N_HEADS, _sigmoid(zab), 0.0))
    gc = _cumsum_rows(gab, seg=ts)
    egc = jnp.exp(gc)
    gct = gc.T
    dotnn = functools.partial(lax.dot_general, dimension_numbers=NN, preferred_element_type=F32)
    same_b = jnp.where(same, 1.0, 0.0).astype(BF16)
    a0, a1, a2 = _split3(gab)
    glast = dotnn(same_b, a0) + (dotnn(same_b, a1) + dotnn(same_b, a2))
    pick = jnp.where(((_iota((nb * HEAD_DIM, r_all), 0) >> 6) == (_iota((nb * HEAD_DIM, r_all), 1) >> seg_shift))
                     & ((_iota((nb * HEAD_DIM, r_all), 1) & (ts - 1)) == 0), 1.0, 0.0).astype(BF16)
    e0, e1, e2 = _split3(jnp.exp(glast))
    eglast_s = dotnn(pick, e0) + (dotnn(pick, e1) + dotnn(pick, e2))
    heads = range(N_HEADS)
    qs, ks, vs, decs = [], [], [], []
    for h in heads:
        qh = qkv[:, h * HEAD_DIM:(h + 1) * HEAD_DIM]
        kh = qkv[:, g + h * HEAD_DIM:g + (h + 1) * HEAD_DIM]
        qs.append(qh * lax.rsqrt(jnp.sum(qh * qh, axis=-1, keepdims=True) + EPS) * (HEAD_DIM ** -0.5))
        ks.append(kh * lax.rsqrt(jnp.sum(kh * kh, axis=-1, keepdims=True) + EPS))
        vs.append(qkv[:, 2 * g + h * HEAD_DIM:2 * g + (h + 1) * HEAD_DIM])
        decs.append(jnp.where(incl, jnp.exp(jnp.where(incl, gc[:, h:h + 1] - gct[h:h + 1, :], 0.0)), 0.0))
    gcols = [gc[:, h:h + 1] for h in heads]
    bcols = [gab[:, N_HEADS + h:N_HEADS + h + 1] for h in heads]
    ecols = [egc[:, h:h + 1] for h in heads]
    svs = [stacked(sgdn0, h) for h in heads]
    mats = [jnp.where(strict, bcols[h] * decs[h] * _mm(ks[h], ks[h], NT), 0.0) for h in heads]
    tinvs = _inv_unit_lower(mats, ts)
    ws = [bcols[h] * (vs[h] - ecols[h] * _mm(expand(ks[h]), svs[h])) for h in heads]
    us = _solve_unit_lower(mats, tinvs, ws)
    os_ = [ecols[h] * _mm(expand(qs[h]), svs[h]) + _mm(_mm(qs[h], ks[h], NT) * decs[h], us[h]) for h in heads]
    for h in heads:
        kdx = jnp.exp(glast[:, h:h + 1] - gcols[h]) * ks[h]
        unstack(gdn_o, h, eglast_s[:, h:h + 1] * svs[h] + _mm(expand(kdx), us[h], TN))
    outs = [o * lax.rsqrt(jnp.mean(o * o, axis=-1, keepdims=True) + EPS) for o in os_]
    mix_ref[:, g:2 * g] = jnp.concatenate(outs, axis=-1) * gdn_norm_g * _silu(z_ref[:, 7 * g:8 * g])

    xc = jnp.concatenate(xc_parts, axis=0) + lru_conv_b
    r = _sigmoid(jnp.dot(xc.astype(BF16), wa_ref[...], preferred_element_type=F32) + lru_ba)
    gate_i = _sigmoid(jnp.dot(xc.astype(BF16), wx_ref[...], preferred_element_type=F32) + lru_bx)
    log_at = -LRU_C * r * _softplus(-lru_lambda)
    av = jnp.exp(log_at)
    bv = jnp.sqrt(-jnp.tanh(log_at) * (av * av + 1.0)) * (gate_i * xc)
    first = (_iota(bv.shape, 0) & (ts - 1)) == 0
    bv = bv + jnp.where(first, av * jnp.concatenate(h0_parts, axis=0), 0.0)
    d = 1
    while d < ts:
        bv = av * _shift_rows(bv, d, 0.0, ts) + bv
        av = av * _shift_rows(av, d, 1.0, ts)
        d *= 2
    for j in range(nb):
        lru_o[j] = bv[(j + 1) * ts - 1:(j + 1) * ts, :]
    mix_ref[:, 2 * g:3 * g] = bv * _gelu_tanh(z_ref[:, 9 * g:10 * g])

    u_in = z_ref[:, 10 * g:11 * g]
    pos1 = (pos_base + 1 + (_iota((r_all, g), 0) & (ts - 1))).astype(F32)
    lane_g = _iota((r_all, g), 1) >> 6
    sums = [jnp.concatenate(parts, axis=0) for parts in sum_parts]
    pooled = sums[-1] / jnp.minimum(pos1, float(POOL_WINDOWS[-1]))
    for gi in range(len(POOL_WINDOWS) - 2, -1, -1):
        pooled = jnp.where(lane_g == gi, sums[gi] / jnp.minimum(pos1, float(POOL_WINDOWS[gi])), pooled)
    pooled = pooled - u_in
    yp = jnp.dot(pooled.astype(BF16), pw_ref[...], preferred_element_type=F32) + pool_b
    mix_ref[:, 3 * g:4 * g] = yp * pool_scale


N_STATES = 6


def _with_state_stack(body, nt, n_prev, n_in, *refs):
    ins = refs[:n_in]
    prevs = refs[n_in:n_in + (N_STATES if n_prev else 0)]
    k = n_in + len(prevs)
    mix_ref, stacks, scratch = refs[k], refs[k + 1:k + 1 + N_STATES], refs[k + 1 + N_STATES:]

    @pl.when(pl.program_id(1) == nt - 1)
    def _():
        for o, p in zip(stacks, prevs):
            o[0:n_prev] = p[...]

    body(*ins, mix_ref, *[o.at[n_prev] for o in stacks], *scratch)


def _ret_consts(c):
    log_g = jnp.log1p(-jnp.exp2(-5.0 - jnp.arange(N_HEADS, dtype=F32)))
    idx = jnp.arange(c, dtype=F32)
    diff = idx[:, None] - idx[None, :]
    intra = jnp.where(diff >= 0, jnp.exp(log_g[:, None, None] * jnp.maximum(diff, 0.0)), 0.0)
    qdec = jnp.exp(log_g[:, None] * (idx + 1.0))
    kdec = jnp.exp(log_g[:, None] * (c - 1.0 - idx))
    sdec = jnp.exp(log_g * c)
    expand = lambda t: jnp.repeat(t.T, HEAD_DIM, axis=1)
    sdec_tab = jnp.zeros((8, LANES), F32).at[:N_HEADS, :].set(sdec[:, None])
    return intra, expand(qdec), expand(kdec), sdec_tab


def _rope_tables(pos):
    half = HEAD_DIM // 2
    inv = ROPE_BASE ** (-jnp.arange(half, dtype=F32) / half)
    ang = pos.astype(F32)[:, None] * inv[None, :]
    cos, sin = jnp.cos(ang), jnp.sin(ang)
    cos_t = jnp.tile(jnp.concatenate([cos, cos], axis=-1), (1, N_HEADS))
    sin_t = jnp.tile(jnp.concatenate([-sin, sin], axis=-1), (1, N_HEADS))
    return cos_t, sin_t


def _mixer(z, z_row0, b, l, ts, nb, pos_base, states, layer, prev, lp):
    nt = l // ts
    assert nb == 1 or (nt == 1 and nb % 2 == 0 and ts == INV_BASE)
    g = GROUP_W
    pos = pos_base + jnp.arange(l, dtype=jnp.int32)
    cos_t, sin_t = _rope_tables(pos)
    intra, qdec, kdec, sdec = _ret_consts(ts)
    rt_ = nb * ts
    if nb > 1:
        cos_t, sin_t, qdec, kdec = (jnp.tile(a, (nb, 1)) for a in (cos_t, sin_t, qdec, kdec))
        intra = jnp.einsum("ab,hij->haibj", jnp.eye(nb, dtype=F32), intra).reshape(N_HEADS, rt_, rt_)
    s_ret0, s_gdn0, gconv0, lru0, lconv0, pool0 = states
    blk0 = z_row0 // rt_
    const2 = lambda bi, n: (0, 0)
    per_b3 = lambda bi, n: (bi, 0, 0)
    per_b4 = lambda bi, n: (bi, 0, 0, 0)
    st3 = lambda bi, n: (layer, bi, 0, 0)
    st4 = lambda bi, n: (layer, bi, 0, 0, 0)
    in_specs = [
        pl.BlockSpec((rt_, NZ), lambda bi, n: (blk0 + bi * nt + n, 0)),
        pl.BlockSpec((rt_, g), lambda bi, n: (n, 0)),
        pl.BlockSpec((rt_, g), lambda bi, n: (n, 0)),
        pl.BlockSpec((N_HEADS, rt_, rt_), lambda bi, n: (0, 0, 0)),
        pl.BlockSpec((rt_, g), const2),
        pl.BlockSpec((rt_, g), const2),
        pl.BlockSpec((8, LANES), const2),
        pl.BlockSpec((16, g), const2),
        pl.BlockSpec((CONV_W, 3 * g), const2),
        pl.BlockSpec((8, LANES), const2),
        pl.BlockSpec((g, g), const2),
        pl.BlockSpec((g, g), const2),
        pl.BlockSpec((g, g), const2),
        pl.BlockSpec((None, nb, N_HEADS, HEAD_DIM, HEAD_DIM), st4),
        pl.BlockSpec((None, nb, N_HEADS, HEAD_DIM, HEAD_DIM), st4),
        pl.BlockSpec((None, nb, CONV_W - 1, 3 * g), st3),
        pl.BlockSpec((None, nb, 1, g), st3),
        pl.BlockSpec((None, nb, CONV_W - 1, g), st3),
        pl.BlockSpec((None, nb, POOL_BUF, g), st3),
    ]
    st_shapes = [(N_HEADS, HEAD_DIM, HEAD_DIM), (N_HEADS, HEAD_DIM, HEAD_DIM), (CONV_W - 1, 3 * g), (1, g),
                 (CONV_W - 1, g), (POOL_BUF, g)]
    n_prev = 0 if prev is None else prev[0].shape[0]
    stack_spec = lambda layers, s: pl.BlockSpec((layers, nb) + s, lambda bi, n: (0, bi) + (0,) * len(s))
    n_in = len(in_specs)
    if n_prev:
        in_specs = in_specs + [stack_spec(n_prev, s) for s in st_shapes]
    out_specs = [pl.BlockSpec((rt_, 4 * g), lambda bi, n: (bi * nt + n, 0))] + [
        stack_spec(n_prev + 1, s) for s in st_shapes]
    out_shape = [jax.ShapeDtypeStruct((b * l, 4 * g), F32)] + [
        jax.ShapeDtypeStruct((n_prev + 1, b) + s, F32) for s in st_shapes]
    ext = [pltpu.VMEM((nb, ts + HIST, 3 * g), F32), pltpu.VMEM((nb, ts + HIST, g), F32),
           pltpu.VMEM((nb, ts + HIST, g), F32)]
    if nb > 1:
        body = functools.partial(_decode_mixer_kernel, ts, nb, pos_base)
        scratch = ext
    else:
        body = functools.partial(_mixer_kernel, nt, ts, nb, pos_base)
        scratch = [pltpu.VMEM((nb, N_HEADS, HEAD_DIM, HEAD_DIM), F32),
                   pltpu.VMEM((nb, N_HEADS, HEAD_DIM, HEAD_DIM), F32)] + ext + [pltpu.VMEM((nb, 1, g), F32)]
    outs = pl.pallas_call(
        functools.partial(_with_state_stack, body, nt, n_prev, n_in),
        grid=(b // nb, nt),
        in_specs=in_specs,
        out_specs=out_specs,
        out_shape=out_shape,
        scratch_shapes=scratch,
        compiler_params=pltpu.CompilerParams(dimension_semantics=("arbitrary", "arbitrary"),
                                             vmem_limit_bytes=VMEM_LIMIT),
        name=f"mixer_ts{ts}",
    )(z, cos_t, sin_t, intra, qdec, kdec, sdec, lp["pvec"], lp["gdn_conv_w"], lp["gabp"],
      lp["lru_wa"], lp["lru_wx"], lp["pool_w"], s_ret0, s_gdn0, gconv0, lru0[:, :, None, :], lconv0, pool0,
      *(prev or ()))
    return outs[0], tuple(outs[1:])


def _to_row_tiles(ref, x):
    m = x.shape[0]
    ref[...] = x.reshape(m // SUBLANES, SUBLANES, SUBLANES, LANES)


def _from_row_tiles(ref):
    return ref[...].reshape(ref.shape[0] * SUBLANES, SUBLANES * LANES)


def _row_tile_shape(m):
    return (m // SUBLANES, SUBLANES, SUBLANES, LANES)


def _wout_route_kernel(na, ma_ref, mb_ref, xa_ref, xb_ref, wo_ref, g_ref, wr_ref, br_ref, x1_ref, h2_ref, rt_ref,
                       cnt_ref):
    mix = _pair_tile(na, ma_ref, mb_ref)
    x1 = _pair_tile(na, xa_ref, xb_ref) + jnp.dot(mix.astype(BF16), wo_ref[...], preferred_element_type=F32)
    x1_ref[...] = x1
    h = x1 * lax.rsqrt(jnp.mean(x1 * x1, axis=-1, keepdims=True) + EPS) * g_ref[...]
    _to_row_tiles(h2_ref, h)
    logits = _mm(h, wr_ref[...], passes=3) + br_ref[...]
    lane = _iota(logits.shape, 1)
    lane_f = lane.astype(F32)
    neg = -1e30
    far = 1e9
    is_g = lane < N_GROUPS
    gl = jnp.where(is_g, logits, neg)
    gmax = jnp.max(gl, axis=-1, keepdims=True)
    gsel = jnp.min(jnp.where(gl == gmax, lane_f, far), axis=-1, keepdims=True)
    p_g = 1.0 / jnp.sum(jnp.where(is_g, jnp.exp(gl - gmax), 0.0), axis=-1, keepdims=True)
    e_group = ((lane - N_GROUPS) >> 3).astype(F32)
    in_group = jnp.where(lane >= N_GROUPS, e_group, -1.0) == gsel
    el = jnp.where(in_group, logits, neg)
    m1 = jnp.max(el, axis=-1, keepdims=True)
    i1 = jnp.min(jnp.where(el == m1, lane_f, far), axis=-1, keepdims=True)
    el2 = jnp.where(lane_f == i1, neg, el)
    m2 = jnp.max(el2, axis=-1, keepdims=True)
    i2 = jnp.min(jnp.where(el2 == m2, lane_f, far), axis=-1, keepdims=True)
    e2 = jnp.exp(m2 - m1)
    w1 = p_g / (1.0 + e2)
    w2 = p_g * e2 / (1.0 + e2)
    rt = jnp.where(lane == 0, i1 - N_GROUPS,
                   jnp.where(lane == 1, i2 - N_GROUPS, jnp.where(lane == 2, w1, jnp.where(lane == 3, w2, 0.0))))
    rt_ref[...] = rt

    picked = jnp.where((lane_f + N_GROUPS == i1) | (lane_f + N_GROUPS == i2), 1.0, 0.0)

    @pl.when(pl.program_id(0) == 0)
    def _():
        cnt_ref[...] = jnp.zeros(cnt_ref.shape, F32)

    cnt_ref[0:1, :] = cnt_ref[0:1, :] + jnp.sum(picked, axis=0, keepdims=True)


def _wout_route(mix_pair, x_pair, wo_bf16, g, wr, br, tm):
    d = x_pair[0].shape[1]
    na, t = x_pair[0].shape[0] // tm, x_pair[0].shape[0] + x_pair[1].shape[0]
    row = lambda i: (i, 0)
    const = lambda i: (0, 0)
    return pl.pallas_call(
        functools.partial(_wout_route_kernel, na),
        grid=(t // tm,),
        in_specs=_pair_specs(na, tm, d) + _pair_specs(na, tm, d) + [
            pl.BlockSpec((d, d), const), pl.BlockSpec((1, d), const), pl.BlockSpec((d, LANES), const),
            pl.BlockSpec((1, LANES), const)],
        out_specs=[pl.BlockSpec((tm, d), row), pl.BlockSpec(_row_tile_shape(tm), lambda i: (i, 0, 0, 0)),
                   pl.BlockSpec((tm, LANES), row), pl.BlockSpec((SUBLANES, LANES), const)],
        out_shape=[jax.ShapeDtypeStruct((t, d), F32), jax.ShapeDtypeStruct(_row_tile_shape(t), F32),
                   jax.ShapeDtypeStruct((t, LANES), F32), jax.ShapeDtypeStruct((SUBLANES, LANES), F32)],
        compiler_params=pltpu.CompilerParams(dimension_semantics=("arbitrary",), vmem_limit_bytes=VMEM_LIMIT),
        name="wout_route",
    )(*mix_pair, *x_pair, wo_bf16, g, wr, br)


def _tile_of_row(ref4, grp, k):
    return ref4.at[grp, pl.ds(k, 1)]


def _for_each_row(tm, fn):
    def body(grp, c):
        for k in range(SUBLANES):
            fn(grp, k, grp * SUBLANES + k)
        return c

    lax.fori_loop(0, tm // SUBLANES, body, 0)


def _route_plan_kernel(tr, tm, n_tiles, cnt_ref, rt_ref, pos_ref, tmeta_ref, emeta_ref, start_s, carry_s):
    i = pl.program_id(0)
    rt = rt_ref[...]
    lane = _iota((tr, LANES), 1)
    lane_f = lane.astype(F32)
    oh1 = jnp.where(lane_f == rt[:, 0:1], 1.0, 0.0)
    oh2 = jnp.where(lane_f == rt[:, 1:2], 1.0, 0.0)
    oh = oh1 + oh2

    @pl.when(i == 0)
    def _():
        cnt = cnt_ref[0:1, :]
        ntile = jnp.floor((cnt + (tm - 1.0)) * (1.0 / tm))
        before = jnp.where(_iota((LANES, LANES), 0) < _iota((LANES, LANES), 1), 1.0, 0.0)
        st_tiles = _mm(jnp.broadcast_to(ntile, (SUBLANES, LANES)), before)[0:1]
        end_tiles = st_tiles + ntile
        start_s[...] = st_tiles * tm
        carry_s[...] = jnp.zeros((1, LANES), F32)
        is_e1 = _iota((1, LANES), 1) < N_EXPERTS
        n_used = jnp.sum(jnp.where(is_e1, ntile, 0.0), axis=-1, keepdims=True)
        last = jnp.sum(jnp.where(is_e1 & (end_tiles <= n_used - 1.0), 1.0, 0.0), axis=-1, keepdims=True)
        ti = _iota((n_tiles, LANES), 0).astype(F32)
        lane_t = _iota((n_tiles, LANES), 1)
        te = jnp.sum(jnp.where((lane_t < N_EXPERTS) & (end_tiles <= ti), 1.0, 0.0), axis=-1, keepdims=True)
        te = jnp.minimum(te, last)
        tmeta_ref[...] = jnp.where(lane_t == 0, te, jnp.where(lane_t == 1, n_used, 0.0))
        srow = _iota((SUBLANES, LANES), 0)
        emeta_ref[...] = jnp.where(srow == 0, ntile, jnp.where(srow == 1, (end_tiles - 1.0) * tm, 0.0))

    earlier = jnp.where(_iota((tr, tr), 0) > _iota((tr, tr), 1), 1.0, 0.0)
    base = start_s[...] + carry_s[...] + _mm(earlier, oh)
    p1 = jnp.sum(oh1 * base, axis=-1, keepdims=True)
    p2 = jnp.sum(oh2 * base, axis=-1, keepdims=True)
    pos_ref[...] = jnp.where(lane == 0, p1, jnp.where(lane == 1, p2, 0.0))
    carry_s[...] = carry_s[...] + jnp.sum(oh, axis=0, keepdims=True)


def _route_plan(counts, rt, tm, n_tiles):
    t = rt.shape[0]
    tr = next(c for c in (1024, 512, 256, tm) if t % c == 0)
    const = lambda i: (0, 0)
    pos, tmeta, emeta = pl.pallas_call(
        functools.partial(_route_plan_kernel, tr, tm, n_tiles),
        grid=(t // tr,),
        in_specs=[pl.BlockSpec((SUBLANES, LANES), const), pl.BlockSpec((tr, LANES), lambda i: (i, 0))],
        out_specs=[pl.BlockSpec((tr, LANES), lambda i: (i, 0)), pl.BlockSpec((n_tiles, LANES), const),
                   pl.BlockSpec((SUBLANES, LANES), const)],
        out_shape=[jax.ShapeDtypeStruct((t, LANES), F32), jax.ShapeDtypeStruct((n_tiles, LANES), F32),
                   jax.ShapeDtypeStruct((SUBLANES, LANES), F32)],
        scratch_shapes=[pltpu.VMEM((1, LANES), F32), pltpu.VMEM((1, LANES), F32)],
        compiler_params=pltpu.CompilerParams(dimension_semantics=("arbitrary",), vmem_limit_bytes=VMEM_LIMIT),
        name="route_plan",
    )(counts, rt)
    pos1 = pos[:, 0].astype(jnp.int32)
    pos2 = pos[:, 1].astype(jnp.int32)
    tile_expert = tmeta[:, 0].astype(jnp.int32)
    n_used = tmeta[0:1, 1].astype(jnp.int32)
    has_rows = (emeta[0, :N_EXPERTS] > 0).astype(jnp.int32)
    last_tile_row = jnp.maximum(emeta[1, :N_EXPERTS], 0.0).astype(jnp.int32)
    return pos1, pos2, tile_expert, n_used, has_rows, last_tile_row


def _scatter_kernel(tm, n_steps, n_tiles, min_tiles, zon_ref, zrow_ref, nu_ref, p1_ref, p2_ref, p1p_ref, p2p_ref,
                    h_ref, xs_hbm, zbuf, stage, sem1, sem2, zsem, tsem):
    i = pl.program_id(0)

    @pl.when(i == 0)
    def _():
        zbuf[...] = jnp.zeros(zbuf.shape, F32)

        def zero_copy(e):
            return pltpu.make_async_copy(zbuf, xs_hbm.at[pl.ds(zrow_ref[e], tm)], zsem.at[e])

        def tail_copy(k):
            return pltpu.make_async_copy(zbuf, xs_hbm.at[pl.ds((min_tiles + k) * tm, tm)], tsem.at[k])

        def zstart(e, c):
            @pl.when(zon_ref[e] > 0)
            def _():
                zero_copy(e).start()
            return c

        def zwait(e, c):
            @pl.when(zon_ref[e] > 0)
            def _():
                zero_copy(e).wait()
            return c

        def tstart(k, c):
            @pl.when(min_tiles + k >= nu_ref[0])
            def _():
                tail_copy(k).start()
            return c

        def twait(k, c):
            @pl.when(min_tiles + k >= nu_ref[0])
            def _():
                tail_copy(k).wait()
            return c

        lax.fori_loop(0, N_EXPERTS, zstart, 0)
        lax.fori_loop(0, n_tiles - min_tiles, tstart, 0)
        lax.fori_loop(0, N_EXPERTS, zwait, 0)
        lax.fori_loop(0, n_tiles - min_tiles, twait, 0)

    slot = i % 2
    stage[slot] = h_ref[...]

    def copies(pa_ref, pb_ref, s, grp, k, r):
        src = _tile_of_row(stage.at[s], grp, k)
        return (pltpu.make_async_copy(src, xs_hbm.at[pl.ds(pa_ref[0, 0, r], 1)], sem1.at[s]),
                pltpu.make_async_copy(src, xs_hbm.at[pl.ds(pb_ref[0, 0, r], 1)], sem2.at[s]))

    def start(grp, k, r):
        c1, c2 = copies(p1_ref, p2_ref, slot, grp, k, r)
        c1.start(priority=0)
        c2.start(priority=1)

    def wait_of(pa_ref, pb_ref, s):
        def wait(grp, k, r):
            c1, c2 = copies(pa_ref, pb_ref, s, grp, k, r)
            c1.wait()
            c2.wait()
        return wait

    _for_each_row(tm, start)

    @pl.when(i >= 1)
    def _():
        _for_each_row(tm, wait_of(p1p_ref, p2p_ref, 1 - slot))

    @pl.when(i == n_steps - 1)
    def _():
        _for_each_row(tm, wait_of(p1_ref, p2_ref, slot))


def _scatter_rows(h2, pos1, pos2, has_rows, last_tile_row, n_used, n_tiles, tm):
    t = h2.shape[0] * SUBLANES
    nb = t // tm
    min_tiles = (2 * t) // tm
    smem_blk = pl.BlockSpec((1, 1, tm), lambda i, zon, zrow, nu: (i, 0, 0), memory_space=pltpu.SMEM)
    smem_prev = pl.BlockSpec((1, 1, tm), lambda i, zon, zrow, nu: (jnp.maximum(i - 1, 0), 0, 0),
                             memory_space=pltpu.SMEM)
    p1, p2 = pos1.reshape(nb, 1, tm), pos2.reshape(nb, 1, tm)
    grid_spec = pltpu.PrefetchScalarGridSpec(
        num_scalar_prefetch=3,
        grid=(nb,),
        in_specs=[smem_blk, smem_blk, smem_prev, smem_prev,
                  pl.BlockSpec(_row_tile_shape(tm), lambda i, zon, zrow, nu: (i, 0, 0, 0))],
        out_specs=pl.BlockSpec(memory_space=pl.ANY),
        scratch_shapes=[pltpu.VMEM((tm, SUBLANES, LANES), F32), pltpu.VMEM((2,) + _row_tile_shape(tm), F32),
                        pltpu.SemaphoreType.DMA((2,)), pltpu.SemaphoreType.DMA((2,)),
                        pltpu.SemaphoreType.DMA((N_EXPERTS,)), pltpu.SemaphoreType.DMA((n_tiles - min_tiles,))],
    )
    return pl.pallas_call(
        functools.partial(_scatter_kernel, tm, nb, n_tiles, min_tiles),
        grid_spec=grid_spec,
        out_shape=jax.ShapeDtypeStruct((n_tiles * tm, SUBLANES, LANES), F32),
        compiler_params=pltpu.CompilerParams(dimension_semantics=("arbitrary",), vmem_limit_bytes=VMEM_LIMIT),
        name="moe_scatter",
    )(has_rows, last_tile_row, n_used, p1, p2, p1, p2, h2)


def _ffn_kernel(layer, te_ref, nu_ref, slot_ref, nxt_ref, xs_ref, wg_hbm, wu_hbm, wd_hbm, ys_ref,
                wg_s, wu_s, wd_s, stage_g, stage_u, stage_d, wsem):
    i = pl.program_id(0)
    used = i < nu_ref[0]
    new_expert = (i == 0) | (te_ref[i] != te_ref[jnp.maximum(i - 1, 0)])

    def weight_copies(e, s):
        return (pltpu.make_async_copy(wg_hbm.at[layer, e], stage_g.at[s], wsem.at[s, 0]),
                pltpu.make_async_copy(wu_hbm.at[layer, e], stage_u.at[s], wsem.at[s, 1]),
                pltpu.make_async_copy(wd_hbm.at[layer, e], stage_d.at[s], wsem.at[s, 2]))

    @pl.when(i == 0)
    def _():
        for c in weight_copies(te_ref[0], slot_ref[0]):
            c.start()

    @pl.when(used & new_expert)
    def _():
        s = slot_ref[i]
        for c in weight_copies(te_ref[i], s):
            c.wait()

        @pl.when(nxt_ref[i] >= 0)
        def _():
            for c in weight_copies(nxt_ref[i], 1 - s):
                c.start()

        wg_s[...] = stage_g[s].astype(BF16)
        wu_s[...] = stage_u[s].astype(BF16)
        wd_s[...] = stage_d[s].astype(BF16)

    @pl.when(used)
    def _():
        x = _from_row_tiles(xs_ref).astype(BF16)
        gate = jnp.dot(x, wg_s[...], preferred_element_type=F32)
        up = jnp.dot(x, wu_s[...], preferred_element_type=F32)
        act = (_silu(gate) * up).astype(BF16)
        _to_row_tiles(ys_ref, jnp.dot(act, wd_s[...], preferred_element_type=F32))

    @pl.when(jnp.logical_not(used))
    def _():
        ys_ref[...] = jnp.zeros(ys_ref.shape, F32)


def _expert_ffn(xs, tile_expert, n_used, w_gate, w_up, w_down, layer, tm):
    p = xs.shape[0]
    n_tiles = p // tm
    xs = xs.reshape(_row_tile_shape(p))
    _, _, d, de = w_gate.shape
    idx = jnp.arange(n_tiles, dtype=jnp.int32)
    first = jnp.concatenate([jnp.ones((1,), bool), tile_expert[1:] != tile_expert[:-1]]) & (idx < n_used[0])
    slot = ((jnp.cumsum(first.astype(jnp.int32)) - 1) & 1).astype(jnp.int32)
    first_at_or_after = jnp.flip(lax.cummin(jnp.flip(jnp.where(first, idx, n_tiles))))
    nxt_pos = jnp.concatenate([first_at_or_after[1:], jnp.full((1,), n_tiles, jnp.int32)])
    nxt = jnp.where(nxt_pos < n_tiles, tile_expert[jnp.minimum(nxt_pos, n_tiles - 1)], -1).astype(jnp.int32)
    any_spec = pl.BlockSpec(memory_space=pl.ANY)
    grid_spec = pltpu.PrefetchScalarGridSpec(
        num_scalar_prefetch=4,
        grid=(n_tiles,),
        in_specs=[pl.BlockSpec(_row_tile_shape(tm), lambda i, te, nu, sl, nx: (jnp.minimum(i, nu[0] - 1), 0, 0, 0)),
                  any_spec, any_spec, any_spec],
        out_specs=pl.BlockSpec(_row_tile_shape(tm), lambda i, te, nu, sl, nx: (i, 0, 0, 0)),
        scratch_shapes=[pltpu.VMEM((d, de), BF16), pltpu.VMEM((d, de), BF16), pltpu.VMEM((de, d), BF16),
                        pltpu.VMEM((2, d, de), F32), pltpu.VMEM((2, d, de), F32), pltpu.VMEM((2, de, d), F32),
                        pltpu.SemaphoreType.DMA((2, 3))],
    )
    return pl.pallas_call(
        functools.partial(_ffn_kernel, layer),
        grid_spec=grid_spec,
        out_shape=jax.ShapeDtypeStruct(_row_tile_shape(p), F32),
        compiler_params=pltpu.CompilerParams(dimension_semantics=("arbitrary",), vmem_limit_bytes=VMEM_LIMIT),
        name="expert_ffn",
    )(tile_expert, n_used, slot, nxt, xs, w_gate, w_up, w_down)


def _combine_kernel(tm, n_steps, na, last_layer, p1_ref, p2_ref, p1n_ref, p2n_ref, x1_ref, rt_ref, g_ref, *rest):
    if last_layer:
        ys_hbm, outa_ref, outb_ref, buf1, buf2, sem1, sem2 = rest
    else:
        w_ref, ys_hbm, outa_ref, outb_ref, z_ref, buf1, buf2, sem1, sem2 = rest
    i = pl.program_id(0)
    slot = i % 2

    def copies(pa_ref, pb_ref, s, grp, k, r):
        return (pltpu.make_async_copy(ys_hbm.at[pl.ds(pa_ref[0, 0, r], 1)], _tile_of_row(buf1.at[s], grp, k),
                                      sem1.at[s]),
                pltpu.make_async_copy(ys_hbm.at[pl.ds(pb_ref[0, 0, r], 1)], _tile_of_row(buf2.at[s], grp, k),
                                      sem2.at[s]))

    def start(pa_ref, pb_ref, s):
        def go(grp, k, r):
            c1, c2 = copies(pa_ref, pb_ref, s, grp, k, r)
            c1.start(priority=0)
            c2.start(priority=1)
        _for_each_row(tm, go)

    @pl.when(i == 0)
    def _():
        start(p1_ref, p2_ref, 0)

    @pl.when(i + 1 < n_steps)
    def _():
        start(p1n_ref, p2n_ref, 1 - slot)

    def wait(grp, k, r):
        c1, c2 = copies(p1_ref, p2_ref, slot, grp, k, r)
        c1.wait()
        c2.wait()

    _for_each_row(tm, wait)
    rt = rt_ref[...]
    x = x1_ref[...] + rt[:, 2:3] * _from_row_tiles(buf1.at[slot]) + rt[:, 3:4] * _from_row_tiles(buf2.at[slot])
    h = x * lax.rsqrt(jnp.mean(x * x, axis=-1, keepdims=True) + EPS) * g_ref[...]
    if last_layer:
        x = h
    else:
        z_ref[...] = jnp.dot(h.astype(BF16), w_ref[...], preferred_element_type=F32)

    @pl.when(pl.program_id(0) < na)
    def _():
        outa_ref[...] = x

    @pl.when(pl.program_id(0) >= na)
    def _():
        outb_ref[...] = x


def _combine(x1, rt, ys, pos1, pos2, gain, next_w_in, ta, tm):
    t, d = x1.shape
    nb = t // tm
    na = ta // tm
    last_layer = next_w_in is None
    row = lambda i: (i, 0)
    const = lambda i: (0, 0)
    smem_blk = pl.BlockSpec((1, 1, tm), lambda i: (i, 0, 0), memory_space=pltpu.SMEM)
    smem_next = pl.BlockSpec((1, 1, tm), lambda i: (jnp.minimum(i + 1, nb - 1), 0, 0), memory_space=pltpu.SMEM)
    p1, p2 = pos1.reshape(nb, 1, tm), pos2.reshape(nb, 1, tm)
    in_specs = [smem_blk, smem_blk, smem_next, smem_next, pl.BlockSpec((tm, d), row),
                pl.BlockSpec((tm, LANES), row), pl.BlockSpec((1, d), const)]
    operands = [p1, p2, p1, p2, x1, rt, gain]
    out_specs = [pl.BlockSpec((tm, d), lambda i: (jnp.minimum(i, na - 1), 0)),
                 pl.BlockSpec((tm, d), lambda i: (jnp.maximum(i - na, 0), 0))]
    out_shape = [jax.ShapeDtypeStruct((ta, d), F32), jax.ShapeDtypeStruct((t - ta, d), F32)]
    if not last_layer:
        nz = next_w_in.shape[1]
        in_specs.append(pl.BlockSpec((d, nz), const))
        operands.append(next_w_in)
        out_specs.append(pl.BlockSpec((tm, nz), row))
        out_shape.append(jax.ShapeDtypeStruct((t, nz), F32))
    outs = pl.pallas_call(
        functools.partial(_combine_kernel, tm, nb, na, last_layer),
        grid=(nb,),
        in_specs=in_specs + [pl.BlockSpec(memory_space=pl.ANY)],
        out_specs=out_specs,
        out_shape=out_shape,
        scratch_shapes=[pltpu.VMEM((2,) + _row_tile_shape(tm), F32), pltpu.VMEM((2,) + _row_tile_shape(tm), F32),
                        pltpu.SemaphoreType.DMA((2,)), pltpu.SemaphoreType.DMA((2,))],
        compiler_params=pltpu.CompilerParams(dimension_semantics=("arbitrary",), vmem_limit_bytes=VMEM_LIMIT),
        name="moe_combine",
    )(*operands, ys.reshape(-1, SUBLANES, LANES))
    return (outs[0], outs[1]), (None if last_layer else outs[2])


def _block_diag(w):
    h, a, b = w.shape
    return jnp.einsum("hij,hk->hikj", w, jnp.eye(h, dtype=w.dtype)).reshape(h * a, h * b)


def _layer_params(l, p):
    g = GROUP_W
    w_in = p["w_in"][l]
    d = w_in.shape[0]
    w_perm = jnp.concatenate([w_in[:, :8 * g], w_in[:, 8 * g + 8:], w_in[:, 8 * g:8 * g + 8],
                              jnp.zeros((d, LANES - 8), w_in.dtype)], axis=1)
    rows = [p["ret_gn_g"][l], jnp.tile(p["gdn_norm_g"][l], N_HEADS), p["lru_conv_b"][l], p["lru_ba"][l],
            p["lru_bx"][l], p["lru_lambda"][l], p["pool_b"][l], p["pool_scale"][l]]
    pvec = jnp.concatenate([jnp.stack(rows), p["lru_conv_w"][l], jnp.zeros((4, g), F32)], axis=0)
    lane_pad = lambda v, n: jnp.concatenate([v, jnp.zeros(v.shape[:-1] + (LANES - n,), F32)], axis=-1)
    gabp = jnp.concatenate([lane_pad(jnp.stack([p["gdn_a_log"][l], p["gdn_dt_bias"][l]]), N_HEADS),
                            jnp.zeros((6, LANES), F32)], axis=0)
    n_route = N_GROUPS + N_EXPERTS
    wr = lane_pad(jnp.concatenate([p["router_group_w"][l],
                                   jnp.transpose(p["router_expert_w"][l], (1, 0, 2)).reshape(d, N_EXPERTS)],
                                  axis=1), n_route)
    br = lane_pad(jnp.concatenate([p["router_group_b"][l], p["router_expert_b"][l].reshape(N_EXPERTS)])[None, :],
                  n_route)
    return dict(
        norm1_g=p["norm1_g"][l][None, :], w_in=w_perm.astype(BF16), pvec=pvec, gdn_conv_w=p["gdn_conv_w"][l],
        gabp=gabp, lru_wa=_block_diag(p["lru_wa"][l]).astype(BF16), lru_wx=_block_diag(p["lru_wx"][l]).astype(BF16),
        pool_w=_block_diag(p["pool_w"][l]).astype(BF16), w_out=p["w_out"][l].astype(BF16),
        norm2_g=p["norm2_g"][l][None, :], wr=wr, br=br)


def kernel(x_prompt, x_sample, state_ret, state_gdn, state_gdn_conv, state_lru, state_lru_conv, state_pool,
           norm1_g, w_in, ret_gn_g, gdn_conv_w, gdn_a_log, gdn_dt_bias, gdn_norm_g,
           lru_conv_w, lru_conv_b, lru_wa, lru_ba, lru_wx, lru_bx, lru_lambda,
           pool_w, pool_b, pool_scale, w_out, norm2_g,
           router_group_w, router_group_b, router_expert_w, router_expert_b,
           moe_w_gate, moe_w_up, moe_w_down, final_g):
    params = dict(norm1_g=norm1_g, w_in=w_in, ret_gn_g=ret_gn_g, gdn_conv_w=gdn_conv_w, gdn_a_log=gdn_a_log,
                  gdn_dt_bias=gdn_dt_bias, gdn_norm_g=gdn_norm_g, lru_conv_w=lru_conv_w, lru_conv_b=lru_conv_b,
                  lru_wa=lru_wa, lru_ba=lru_ba, lru_wx=lru_wx, lru_bx=lru_bx, lru_lambda=lru_lambda,
                  pool_w=pool_w, pool_b=pool_b, pool_scale=pool_scale, w_out=w_out, norm2_g=norm2_g,
                  router_group_w=router_group_w, router_group_b=router_group_b,
                  router_expert_w=router_expert_w, router_expert_b=router_expert_b)
    bp, lp_, d = x_prompt.shape
    bs, ls, _ = x_sample.shape
    depth = w_in.shape[0]
    tp, ts_ = bp * lp_, bs * ls
    t = tp + ts_
    tile_p = min(256, lp_)
    tm = 256
    tt = 512 if (tp % 512 == 0 and ts_ % 512 == 0) else tm
    n_tiles = (2 * t + N_EXPERTS * (tm - 1) + tm - 1) // tm

    x = (x_prompt.reshape(tp, d), x_sample.reshape(ts_, d))
    sample_states = (state_ret, state_gdn, state_gdn_conv, state_lru, state_lru_conv, state_pool)
    zero_states = tuple(jnp.zeros((1, bp) + s.shape[2:], F32) for s in sample_states)
    new_p = new_s = None
    lps = [_layer_params(l, params) for l in range(depth)]
    z = _norm_win(x, lps[0]["norm1_g"], lps[0]["w_in"], tt)
    for l in range(depth):
        lp = lps[l]
        last = l == depth - 1
        mix_p, new_p = _mixer(z, 0, bp, lp_, tile_p, 1, 0, zero_states, 0, new_p, lp)
        mix_s, new_s = _mixer(z, tp, bs, ls, ls, SAMPLE_SEQS_PER_STEP, PAST_LEN, sample_states, l, new_s, lp)
        x1, h2, rt, counts = _wout_route((mix_p, mix_s), x, lp["w_out"], lp["norm2_g"], lp["wr"], lp["br"], tt)
        pos1, pos2, tile_expert, n_used, has_rows, last_tile_row = _route_plan(counts, rt, tm, n_tiles)
        xs = _scatter_rows(h2, pos1, pos2, has_rows, last_tile_row, n_used, n_tiles, tm)
        ys = _expert_ffn(xs, tile_expert, n_used, moe_w_gate, moe_w_up, moe_w_down, l, tm)
        x, z = _combine(x1, rt, ys, pos1, pos2, final_g[None, :] if last else lps[l + 1]["norm1_g"],
                        None if last else lps[l + 1]["w_in"], tp, tm)
    y_prompt = x[0].reshape(bp, lp_, d)
    y_sample = x[1].reshape(bs, ls, d)
    unpack = lambda st: (st[0], st[1], st[2], st[3][:, :, 0, :], st[4], st[5])
    return (y_prompt, y_sample, *unpack(new_p), *unpack(new_s))
```

```python
import functools

import jax
import jax.numpy as jnp
from jax import lax
from jax.experimental import pallas as pl
from jax.experimental.pallas import tpu as pltpu

F32 = jnp.float32
BF16 = jnp.bfloat16

EPS = 1e-6
HEAD_DIM = 64
N_HEADS = 4
GROUP_W = 256
CONV_W = 4
LRU_C = 8.0
POOL_WINDOWS = (2, 4, 8, 16)
POOL_BUF = 15
ROPE_BASE = 10000.0
N_GROUPS = 4
EXPERTS_PER_GROUP = 8
N_EXPERTS = 32
PAST_LEN = 16384

LANES = 128
SUBLANES = 8
HIST = 16
Z_AB = 11 * GROUP_W
NZ = Z_AB + LANES
INV_BASE = 8
RESID_PASSES = 3
HEAD_SHIFT = 6
GROUP_SHIFT = 3
MASKED_LOGIT = -1e30
NO_LANE = 1e9
ROUTE_PLAN_STEPS = (1024, 512, 256)
SAMPLE_SEQS_PER_STEP = 16
VMEM_LIMIT = 56 * 1024 * 1024

NN = (((1,), (0,)), ((), ()))
NT = (((1,), (1,)), ((), ()))
TN = (((0,), (0,)), ((), ()))


def _mm(a, b, dims=NN, passes=1):
    ah = a.astype(BF16)
    bh = b.astype(BF16)
    dot = functools.partial(lax.dot_general, dimension_numbers=dims, preferred_element_type=F32)
    if passes == 1:
        return dot(ah, bh)
    al = (a - ah.astype(F32)).astype(BF16)
    bl = (b - bh.astype(F32)).astype(BF16)
    return dot(ah, bh) + (dot(ah, bl) + dot(al, bh))


def _sigmoid(x):
    return 1.0 / (1.0 + jnp.exp(-x))


def _silu(x):
    return x * _sigmoid(x)


def _softplus(x):
    return jnp.maximum(x, 0.0) + jnp.log1p(jnp.exp(-jnp.abs(x)))


def _gelu_tanh(x):
    return 0.5 * x * (1.0 + jnp.tanh(0.7978845608028654 * (x + 0.044715 * (x * x * x))))


def _iota(shape, axis):
    return lax.broadcasted_iota(jnp.int32, shape, axis)


def _shift_rows(x, d, fill, seg=None):
    t = _iota(x.shape, 0)
    if seg is not None:
        t = t & (seg - 1)
    return jnp.where(t >= d, pltpu.roll(x, d, 0), fill)


def _cumsum_rows(x, seg=None):
    d = 1
    while d < (seg or x.shape[0]):
        x = x + _shift_rows(x, d, 0.0, seg)
        d *= 2
    return x


def _swap_half_heads(x):
    n = x.shape[1]
    lower = (_iota(x.shape, 1) & (HEAD_DIM - 1)) < HEAD_DIM // 2
    return jnp.where(lower, pltpu.roll(x, n - HEAD_DIM // 2, 1), pltpu.roll(x, HEAD_DIM // 2, 1))


def _inv_unit_lower(mats, c=None):
    shape = mats[0].shape
    c = c or shape[0]
    row = _iota(shape, 0)
    col = _iota(shape, 1)
    nb = min(INV_BASE, c)
    shift = nb.bit_length() - 1
    in_base = (row >> shift) == (col >> shift)
    ps = [jnp.where(in_base, a, 0.0) for a in mats]
    eye = jnp.where(row == col, 1.0, 0.0)
    xs = [eye - d for d in ps]
    k = 2
    while k < nb:
        ps = [_mm(p, p) for p in ps]
        xs = [x + _mm(x, p) for x, p in zip(xs, ps)]
        k *= 2
    m = nb
    while m < c:
        s1 = m.bit_length() - 1
        in_off = ((row >> (s1 + 1)) == (col >> (s1 + 1))) & ((row >> s1) != (col >> s1))
        ys = [_mm(x, jnp.where(in_off, a, 0.0)) for x, a in zip(xs, mats)]
        xs = [x - _mm(y, x) for x, y in zip(xs, ys)]
        m *= 2
    return xs


def _solve_unit_lower(mats, tinvs, ws):
    us = [_mm(t, w) for t, w in zip(tinvs, ws)]
    rs = [w - (u + _mm(a, u, passes=RESID_PASSES)) for a, u, w in zip(mats, us, ws)]
    return [u + _mm(t, r) for u, t, r in zip(us, tinvs, rs)]


def _pair_specs(na, tm, d):
    return [pl.BlockSpec((tm, d), lambda i: (jnp.minimum(i, na - 1), 0)),
            pl.BlockSpec((tm, d), lambda i: (jnp.maximum(i - na, 0), 0))]


def _pair_tile(na, a_ref, b_ref):
    return jnp.where(pl.program_id(0) < na, a_ref[...], b_ref[...])


def _norm_win_kernel(na, xa_ref, xb_ref, g_ref, w_ref, z_ref):
    x = _pair_tile(na, xa_ref, xb_ref)
    h = x * lax.rsqrt(jnp.mean(x * x, axis=-1, keepdims=True) + EPS) * g_ref[...]
    z_ref[...] = jnp.dot(h.astype(BF16), w_ref[...], preferred_element_type=F32)


def _norm_win(x_pair, g, w_bf16, tm):
    xa, xb = x_pair
    d = xa.shape[1]
    na, t = xa.shape[0] // tm, xa.shape[0] + xb.shape[0]
    nz = w_bf16.shape[1]
    return pl.pallas_call(
        functools.partial(_norm_win_kernel, na),
        grid=(t // tm,),
        in_specs=_pair_specs(na, tm, d) + [pl.BlockSpec((1, d), lambda i: (0, 0)),
                                           pl.BlockSpec((d, nz), lambda i: (0, 0))],
        out_specs=pl.BlockSpec((tm, nz), lambda i: (i, 0)),
        out_shape=jax.ShapeDtypeStruct((t, nz), F32),
        compiler_params=pltpu.CompilerParams(dimension_semantics=("arbitrary",), vmem_limit_bytes=VMEM_LIMIT),
        name="norm_win",
    )(xa, xb, g, w_bf16)


def _retention_tile(zq, zk, zv, cos, sin, intra_ref, qdec, kdec, sdec, s_ref):
    rq = zq * cos + _swap_half_heads(zq) * sin
    rk = (zk * cos + _swap_half_heads(zk) * sin) * (HEAD_DIM ** -0.5)
    kd = rk * kdec
    heads = range(N_HEADS)
    sls = [slice(h * HEAD_DIM, (h + 1) * HEAD_DIM) for h in heads]
    ss = [s_ref[h] for h in heads]
    scores = [_mm(rq[:, sl], rk[:, sl], NT) * intra_ref[h] for h, sl in zip(heads, sls)]
    os_ = [_mm(sc, zv[:, sl]) + _mm(rq[:, sl], s) * qdec[:, sl] for sc, s, sl in zip(scores, ss, sls)]
    s_new = [s * sdec[h:h + 1, :HEAD_DIM] + _mm(kd[:, sl], zv[:, sl], TN) for h, s, sl in zip(heads, ss, sls)]
    for h in heads:
        s_ref[h] = s_new[h]
    outs = []
    for o in os_:
        mu = jnp.mean(o, axis=-1, keepdims=True)
        oc = o - mu
        var = jnp.mean(oc * oc, axis=-1, keepdims=True)
        outs.append(oc * lax.rsqrt(var + EPS))
    return jnp.concatenate(outs, axis=-1)


def _split3(x):
    x0 = x.astype(BF16)
    r1 = x - x0.astype(F32)
    x1 = r1.astype(BF16)
    x2 = (r1 - x1.astype(F32)).astype(BF16)
    return x0, x1, x2


def _gdn_tile(gq, gk, gv, gab, s_ref):
    c = gq.shape[0]
    gc = _cumsum_rows(gab)
    egc = jnp.exp(gc)
    row = _iota((c, c), 0)
    col = _iota((c, c), 1)
    incl = row >= col
    strict = row > col
    heads = range(N_HEADS)
    sls = [slice(h * HEAD_DIM, (h + 1) * HEAD_DIM) for h in heads]
    ss = [s_ref[h] for h in heads]
    gcols = [gc[:, h:h + 1] for h in heads]
    bcols = [gab[:, N_HEADS + h:N_HEADS + h + 1] for h in heads]
    ecols = [egc[:, h:h + 1] for h in heads]
    gct = gc.T
    decs = [jnp.where(incl, jnp.exp(jnp.where(incl, gcols[h] - gct[h:h + 1, :], 0.0)), 0.0) for h in heads]
    mats = [jnp.where(strict, bcols[h] * decs[h] * _mm(gk[:, sl], gk[:, sl], NT), 0.0) for h, sl in zip(heads, sls)]
    tinvs = _inv_unit_lower(mats)
    ws = [bcols[h] * (gv[:, sl] - ecols[h] * _mm(gk[:, sl], ss[h])) for h, sl in zip(heads, sls)]
    us = _solve_unit_lower(mats, tinvs, ws)
    qks = [_mm(gq[:, sl], gk[:, sl], NT) * decs[h] for h, sl in zip(heads, sls)]
    outs = [ecols[h] * _mm(gq[:, sl], ss[h]) + _mm(qks[h], us[h]) for h, sl in zip(heads, sls)]
    glasts = [gc[c - 1:c, h:h + 1] for h in heads]
    s_new = [jnp.exp(glasts[h]) * ss[h] + _mm(jnp.exp(glasts[h] - gcols[h]) * gk[:, sl], us[h], TN)
             for h, sl in zip(heads, sls)]
    for h in heads:
        s_ref[h] = s_new[h]
    return outs


def _mixer_kernel(nt, ts, nb, pos_base,
                  z_ref, cos_ref, sin_ref, intra_ref, qdec_ref, kdec_ref, sdec_ref, pv_ref, gcw_ref, gabp_ref,
                  wa_ref, wx_ref, pw_ref, sret0, sgdn0, gconv0, lru0, lconv0, pool0,
                  mix_ref, ret_o, gdn_o, gconv_o, lru_o, lconv_o, pool_o,
                  s_ret, s_gdn, ext_g, ext_l, ext_p, h_lru):
    n = pl.program_id(1)
    g = GROUP_W
    base = HIST - (CONV_W - 1)

    @pl.when(n == 0)
    def _():
        s_ret[...] = sret0[...]
        s_gdn[...] = sgdn0[...]
        ext_g[:, 0:HIST, :] = jnp.zeros((nb, HIST, 3 * g), F32)
        ext_g[:, base:HIST, :] = gconv0[...]
        ext_l[:, 0:HIST, :] = jnp.zeros((nb, HIST, g), F32)
        ext_l[:, base:HIST, :] = lconv0[...]
        ext_p[:, 0:HIST, :] = jnp.zeros((nb, HIST, g), F32)
        ext_p[:, HIST - POOL_BUF:HIST, :] = pool0[...]
        h_lru[...] = lru0[...]

    pv = pv_ref[...]
    ret_gn_g, gdn_norm_g = pv[0:1], pv[1:2]
    lru_conv_b, lru_ba, lru_bx, lru_lambda = pv[2:3], pv[3:4], pv[4:5], pv[5:6]
    pool_b, pool_scale = pv[6:7], pv[7:8]
    lru_conv_w = pv[8:12]
    gcw = gcw_ref[...]
    gabp = gabp_ref[...]

    for j in range(nb):
        rows = slice(j * ts, (j + 1) * ts)
        eg, el, ep = ext_g.at[j], ext_l.at[j], ext_p.at[j]

        ro = _retention_tile(z_ref[rows, 0:g], z_ref[rows, g:2 * g], z_ref[rows, 2 * g:3 * g], cos_ref[...],
                             sin_ref[...], intra_ref, qdec_ref[...], kdec_ref[...], sdec_ref[...], s_ret.at[j])
        mix_ref[rows, 0:g] = ro * ret_gn_g * _silu(z_ref[rows, 3 * g:4 * g])

        eg[HIST:HIST + ts, :] = z_ref[rows, 4 * g:7 * g]
        y = eg[base:base + ts, :] * gcw[0:1]
        for i in range(1, CONV_W):
            y = y + eg[base + i:base + i + ts, :] * gcw[i:i + 1]
        qkv = _silu(y)
        zab = z_ref[rows, Z_AB:Z_AB + LANES]
        lane = _iota(zab.shape, 1)
        log_a = -jnp.exp(gabp[0:1]) * _softplus(zab + gabp[1:2])
        gab = jnp.where(lane < N_HEADS, log_a, jnp.where(lane < 2 * N_HEADS, _sigmoid(zab), 0.0))
        gq_parts, gk_parts = [], []
        for h in range(N_HEADS):
            qh = qkv[:, h * HEAD_DIM:(h + 1) * HEAD_DIM]
            kh = qkv[:, g + h * HEAD_DIM:g + (h + 1) * HEAD_DIM]
            gq_parts.append(qh * lax.rsqrt(jnp.sum(qh * qh, axis=-1, keepdims=True) + EPS) * (HEAD_DIM ** -0.5))
            gk_parts.append(kh * lax.rsqrt(jnp.sum(kh * kh, axis=-1, keepdims=True) + EPS))
        go = _gdn_tile(jnp.concatenate(gq_parts, axis=-1), jnp.concatenate(gk_parts, axis=-1),
                       qkv[:, 2 * g:3 * g], gab, s_gdn.at[j])
        go = [o * lax.rsqrt(jnp.mean(o * o, axis=-1, keepdims=True) + EPS) for o in go]
        mix_ref[rows, g:2 * g] = jnp.concatenate(go, axis=-1) * gdn_norm_g * _silu(z_ref[rows, 7 * g:8 * g])

        el[HIST:HIST + ts, :] = z_ref[rows, 8 * g:9 * g]
        xc = el[base:base + ts, :] * lru_conv_w[0:1]
        for i in range(1, CONV_W):
            xc = xc + el[base + i:base + i + ts, :] * lru_conv_w[i:i + 1]
        xc = xc + lru_conv_b
        r = _sigmoid(jnp.dot(xc.astype(BF16), wa_ref[...], preferred_element_type=F32) + lru_ba)
        gate_i = _sigmoid(jnp.dot(xc.astype(BF16), wx_ref[...], preferred_element_type=F32) + lru_bx)
        log_at = -LRU_C * r * _softplus(-lru_lambda)
        av = jnp.exp(log_at)
        bv = jnp.sqrt(-jnp.tanh(log_at) * (av * av + 1.0)) * (gate_i * xc)
        bv = bv + jnp.where(_iota(bv.shape, 0) == 0, av * h_lru[j], 0.0)
        d = 1
        while d < ts:
            bv = av * _shift_rows(bv, d, 0.0) + bv
            av = av * _shift_rows(av, d, 1.0)
            d *= 2
        h_lru[j] = bv[ts - 1:ts, :]
        mix_ref[rows, 2 * g:3 * g] = bv * _gelu_tanh(z_ref[rows, 9 * g:10 * g])

        u = z_ref[rows, 10 * g:11 * g]
        ep[HIST:HIST + ts, :] = u
        e = ep[...]
        sums = []
        step = 1
        for _ in POOL_WINDOWS:
            e = e + pltpu.roll(e, step, 0)
            sums.append(e[HIST:HIST + ts, :])
            step *= 2
        pos1 = (pos_base + 1 + n * ts + _iota((ts, g), 0)).astype(F32)
        lane_g = _iota((ts, g), 1) >> HEAD_SHIFT
        pooled = sums[-1] / jnp.minimum(pos1, float(POOL_WINDOWS[-1]))
        for gi in range(len(POOL_WINDOWS) - 2, -1, -1):
            pooled = jnp.where(lane_g == gi, sums[gi] / jnp.minimum(pos1, float(POOL_WINDOWS[gi])), pooled)
        pooled = pooled - u
        yp = jnp.dot(pooled.astype(BF16), pw_ref[...], preferred_element_type=F32) + pool_b
        mix_ref[rows, 3 * g:4 * g] = yp * pool_scale

        tail_g = eg[ts:ts + HIST, :]
        tail_l = el[ts:ts + HIST, :]
        tail_p = ep[ts:ts + HIST, :]
        eg[0:HIST, :] = tail_g
        el[0:HIST, :] = tail_l
        ep[0:HIST, :] = tail_p

    @pl.when(n == nt - 1)
    def _():
        ret_o[...] = s_ret[...]
        gdn_o[...] = s_gdn[...]
        gconv_o[...] = ext_g[:, base:HIST, :]
        lconv_o[...] = ext_l[:, base:HIST, :]
        pool_o[...] = ext_p[:, HIST - POOL_BUF:HIST, :]
        lru_o[...] = h_lru[...]


def _decode_mixer_kernel(ts, nb, pos_base,
                         z_ref, cos_ref, sin_ref, intra_ref, qdec_ref, kdec_ref, sdec_ref, pv_ref, gcw_ref, gabp_ref,
                         wa_ref, wx_ref, pw_ref, sret0, sgdn0, gconv0, lru0, lconv0, pool0,
                         mix_ref, ret_o, gdn_o, gconv_o, lru_o, lconv_o, pool_o,
                         ext_g, ext_l, ext_p):
    g = GROUP_W
    r_all = nb * ts
    base = HIST - (CONV_W - 1)
    seg_shift = ts.bit_length() - 1
    row2 = _iota((r_all, r_all), 0)
    col2 = _iota((r_all, r_all), 1)
    same = (row2 >> seg_shift) == (col2 >> seg_shift)
    incl = same & (row2 >= col2)
    strict = same & (row2 > col2)
    seq_l = _iota((r_all, LANES), 0) >> seg_shift
    half_l = _iota((r_all, LANES), 1) >> HEAD_SHIFT
    exp_masks = [((seq_l >> 1) == c) & ((seq_l & 1) == half_l) for c in range(nb // 2)]

    def expand(x):
        x2 = jnp.concatenate([x, x], axis=-1)
        return jnp.concatenate([jnp.where(m, x2, 0.0) for m in exp_masks], axis=-1)

    def stacked(state_ref, h):
        return jnp.concatenate([state_ref[j, h] for j in range(nb)], axis=0)

    def unstack(out_ref, h, sv):
        for j in range(nb):
            out_ref[j, h] = sv[j * HEAD_DIM:(j + 1) * HEAD_DIM, :]

    pv = pv_ref[...]
    ret_gn_g, gdn_norm_g = pv[0:1], pv[1:2]
    lru_conv_b, lru_ba, lru_bx, lru_lambda = pv[2:3], pv[3:4], pv[4:5], pv[5:6]
    pool_b, pool_scale = pv[6:7], pv[7:8]
    lru_conv_w = pv[8:12]
    gcw = gcw_ref[...]
    gabp = gabp_ref[...]

    ext_g[:, 0:HIST, :] = jnp.zeros((nb, HIST, 3 * g), F32)
    ext_g[:, base:HIST, :] = gconv0[...]
    ext_l[:, 0:HIST, :] = jnp.zeros((nb, HIST, g), F32)
    ext_l[:, base:HIST, :] = lconv0[...]
    ext_p[:, 0:HIST, :] = jnp.zeros((nb, HIST, g), F32)
    ext_p[:, HIST - POOL_BUF:HIST, :] = pool0[...]
    y_parts, xc_parts, sum_parts, h0_parts = [], [], [[] for _ in POOL_WINDOWS], []
    for j in range(nb):
        rows = slice(j * ts, (j + 1) * ts)
        eg, el, ep = ext_g.at[j], ext_l.at[j], ext_p.at[j]
        eg[HIST:HIST + ts, :] = z_ref[rows, 4 * g:7 * g]
        el[HIST:HIST + ts, :] = z_ref[rows, 8 * g:9 * g]
        ep[HIST:HIST + ts, :] = z_ref[rows, 10 * g:11 * g]
        y = eg[base:base + ts, :] * gcw[0:1]
        xc = el[base:base + ts, :] * lru_conv_w[0:1]
        for i in range(1, CONV_W):
            y = y + eg[base + i:base + i + ts, :] * gcw[i:i + 1]
            xc = xc + el[base + i:base + i + ts, :] * lru_conv_w[i:i + 1]
        y_parts.append(y)
        xc_parts.append(xc)
        e = ep[...]
        step = 1
        for wi in range(len(POOL_WINDOWS)):
            e = e + pltpu.roll(e, step, 0)
            sum_parts[wi].append(e[HIST:HIST + ts, :])
            step *= 2
        h0_parts.append(jnp.broadcast_to(lru0[j], (ts, g)))
        gconv_o[j] = eg[ts + base:ts + HIST, :]
        lconv_o[j] = el[ts + base:ts + HIST, :]
        pool_o[j] = ep[ts + HIST - POOL_BUF:ts + HIST, :]

    zq, zk, zv = z_ref[:, 0:g], z_ref[:, g:2 * g], z_ref[:, 2 * g:3 * g]
    cos, sin = cos_ref[...], sin_ref[...]
    qdec, kdec, sdec = qdec_ref[...], kdec_ref[...], sdec_ref[...]
    rq = zq * cos + _swap_half_heads(zq) * sin
    rk = (zk * cos + _swap_half_heads(zk) * sin) * (HEAD_DIM ** -0.5)
    kd = rk * kdec
    outs = []
    for h in range(N_HEADS):
        sl = slice(h * HEAD_DIM, (h + 1) * HEAD_DIM)
        q, k, v = rq[:, sl], rk[:, sl], zv[:, sl]
        sv = stacked(sret0, h)
        scores = _mm(q, k, NT) * intra_ref[h]
        o = _mm(scores, v) + _mm(expand(q), sv) * qdec[:, sl]
        unstack(ret_o, h, sv * sdec[h:h + 1, :HEAD_DIM] + _mm(expand(kd[:, sl]), v, TN))
        mu = jnp.mean(o, axis=-1, keepdims=True)
        oc = o - mu
        var = jnp.mean(oc * oc, axis=-1, keepdims=True)
        outs.append(oc * lax.rsqrt(var + EPS))
    mix_ref[:, 0:g] = jnp.concatenate(outs, axis=-1) * ret_gn_g * _silu(z_ref[:, 3 * g:4 * g])

    qkv = _silu(jnp.concatenate(y_parts, axis=0))
    zab = z_ref[:, Z_AB:Z_AB + LANES]
    lane = _iota(zab.shape, 1)
    log_a = -jnp.exp(gabp[0:1]) * _softplus(zab + gabp[1:2])
    gab = jnp.where(lane < N_HEADS, log_a, jnp.where(lane < 2 * N_HEADS, _sigmoid(zab), 0.0))
    gc = _cumsum_rows(gab, seg=ts)
    egc = jnp.exp(gc)
    gct = gc.T
    dotnn = functools.partial(lax.dot_general, dimension_numbers=NN, preferred_element_type=F32)
    same_b = jnp.where(same, 1.0, 0.0).astype(BF16)
    a0, a1, a2 = _split3(gab)
    glast = dotnn(same_b, a0) + (dotnn(same_b, a1) + dotnn(same_b, a2))
    pick = jnp.where(((_iota((nb * HEAD_DIM, r_all), 0) >> HEAD_SHIFT)
                      == (_iota((nb * HEAD_DIM, r_all), 1) >> seg_shift))
                     & ((_iota((nb * HEAD_DIM, r_all), 1) & (ts - 1)) == 0), 1.0, 0.0).astype(BF16)
    e0, e1, e2 = _split3(jnp.exp(glast))
    eglast_s = dotnn(pick, e0) + (dotnn(pick, e1) + dotnn(pick, e2))
    heads = range(N_HEADS)
    qs, ks, vs, decs = [], [], [], []
    for h in heads:
        qh = qkv[:, h * HEAD_DIM:(h + 1) * HEAD_DIM]
        kh = qkv[:, g + h * HEAD_DIM:g + (h + 1) * HEAD_DIM]
        qs.append(qh * lax.rsqrt(jnp.sum(qh * qh, axis=-1, keepdims=True) + EPS) * (HEAD_DIM ** -0.5))
        ks.append(kh * lax.rsqrt(jnp.sum(kh * kh, axis=-1, keepdims=True) + EPS))
        vs.append(qkv[:, 2 * g + h * HEAD_DIM:2 * g + (h + 1) * HEAD_DIM])
        decs.append(jnp.where(incl, jnp.exp(jnp.where(incl, gc[:, h:h + 1] - gct[h:h + 1, :], 0.0)), 0.0))
    gcols = [gc[:, h:h + 1] for h in heads]
    bcols = [gab[:, N_HEADS + h:N_HEADS + h + 1] for h in heads]
    ecols = [egc[:, h:h + 1] for h in heads]
    svs = [stacked(sgdn0, h) for h in heads]
    mats = [jnp.where(strict, bcols[h] * decs[h] * _mm(ks[h], ks[h], NT), 0.0) for h in heads]
    tinvs = _inv_unit_lower(mats, ts)
    ws = [bcols[h] * (vs[h] - ecols[h] * _mm(expand(ks[h]), svs[h])) for h in heads]
    us = _solve_unit_lower(mats, tinvs, ws)
    os_ = [ecols[h] * _mm(expand(qs[h]), svs[h]) + _mm(_mm(qs[h], ks[h], NT) * decs[h], us[h]) for h in heads]
    for h in heads:
        kdx = jnp.exp(glast[:, h:h + 1] - gcols[h]) * ks[h]
        unstack(gdn_o, h, eglast_s[:, h:h + 1] * svs[h] + _mm(expand(kdx), us[h], TN))
    outs = [o * lax.rsqrt(jnp.mean(o * o, axis=-1, keepdims=True) + EPS) for o in os_]
    mix_ref[:, g:2 * g] = jnp.concatenate(outs, axis=-1) * gdn_norm_g * _silu(z_ref[:, 7 * g:8 * g])

    xc = jnp.concatenate(xc_parts, axis=0) + lru_conv_b
    r = _sigmoid(jnp.dot(xc.astype(BF16), wa_ref[...], preferred_element_type=F32) + lru_ba)
    gate_i = _sigmoid(jnp.dot(xc.astype(BF16), wx_ref[...], preferred_element_type=F32) + lru_bx)
    log_at = -LRU_C * r * _softplus(-lru_lambda)
    av = jnp.exp(log_at)
    bv = jnp.sqrt(-jnp.tanh(log_at) * (av * av + 1.0)) * (gate_i * xc)
    first = (_iota(bv.shape, 0) & (ts - 1)) == 0
    bv = bv + jnp.where(first, av * jnp.concatenate(h0_parts, axis=0), 0.0)
    d = 1
    while d < ts:
        bv = av * _shift_rows(bv, d, 0.0, ts) + bv
        av = av * _shift_rows(av, d, 1.0, ts)
        d *= 2
    for j in range(nb):
        lru_o[j] = bv[(j + 1) * ts - 1:(j + 1) * ts, :]
    mix_ref[:, 2 * g:3 * g] = bv * _gelu_tanh(z_ref[:, 9 * g:10 * g])

    u_in = z_ref[:, 10 * g:11 * g]
    pos1 = (pos_base + 1 + (_iota((r_all, g), 0) & (ts - 1))).astype(F32)
    lane_g = _iota((r_all, g), 1) >> HEAD_SHIFT
    sums = [jnp.concatenate(parts, axis=0) for parts in sum_parts]
    pooled = sums[-1] / jnp.minimum(pos1, float(POOL_WINDOWS[-1]))
    for gi in range(len(POOL_WINDOWS) - 2, -1, -1):
        pooled = jnp.where(lane_g == gi, sums[gi] / jnp.minimum(pos1, float(POOL_WINDOWS[gi])), pooled)
    pooled = pooled - u_in
    yp = jnp.dot(pooled.astype(BF16), pw_ref[...], preferred_element_type=F32) + pool_b
    mix_ref[:, 3 * g:4 * g] = yp * pool_scale


N_STATES = 6


def _with_state_stack(body, nt, n_prev, n_in, *refs):
    ins = refs[:n_in]
    prevs = refs[n_in:n_in + (N_STATES if n_prev else 0)]
    k = n_in + len(prevs)
    mix_ref, stacks, scratch = refs[k], refs[k + 1:k + 1 + N_STATES], refs[k + 1 + N_STATES:]

    @pl.when(pl.program_id(1) == nt - 1)
    def _():
        for o, p in zip(stacks, prevs):
            o[0:n_prev] = p[...]

    body(*ins, mix_ref, *[o.at[n_prev] for o in stacks], *scratch)


def _ret_consts(c):
    log_g = jnp.log1p(-jnp.exp2(-5.0 - jnp.arange(N_HEADS, dtype=F32)))
    idx = jnp.arange(c, dtype=F32)
    diff = idx[:, None] - idx[None, :]
    intra = jnp.where(diff >= 0, jnp.exp(log_g[:, None, None] * jnp.maximum(diff, 0.0)), 0.0)
    qdec = jnp.exp(log_g[:, None] * (idx + 1.0))
    kdec = jnp.exp(log_g[:, None] * (c - 1.0 - idx))
    sdec = jnp.exp(log_g * c)
    expand = lambda t: jnp.repeat(t.T, HEAD_DIM, axis=1)
    sdec_tab = jnp.zeros((8, LANES), F32).at[:N_HEADS, :].set(sdec[:, None])
    return intra, expand(qdec), expand(kdec), sdec_tab


def _rope_tables(pos):
    half = HEAD_DIM // 2
    inv = ROPE_BASE ** (-jnp.arange(half, dtype=F32) / half)
    ang = pos.astype(F32)[:, None] * inv[None, :]
    cos, sin = jnp.cos(ang), jnp.sin(ang)
    cos_t = jnp.tile(jnp.concatenate([cos, cos], axis=-1), (1, N_HEADS))
    sin_t = jnp.tile(jnp.concatenate([-sin, sin], axis=-1), (1, N_HEADS))
    return cos_t, sin_t


def _mixer(z, z_row0, b, l, ts, nb, pos_base, states, layer, prev, lp):
    nt = l // ts
    assert nb == 1 or (nt == 1 and nb % 2 == 0 and ts == INV_BASE)
    g = GROUP_W
    pos = pos_base + jnp.arange(l, dtype=jnp.int32)
    cos_t, sin_t = _rope_tables(pos)
    intra, qdec, kdec, sdec = _ret_consts(ts)
    rt_ = nb * ts
    if nb > 1:
        cos_t, sin_t, qdec, kdec = (jnp.tile(a, (nb, 1)) for a in (cos_t, sin_t, qdec, kdec))
        intra = jnp.einsum("ab,hij->haibj", jnp.eye(nb, dtype=F32), intra).reshape(N_HEADS, rt_, rt_)
    s_ret0, s_gdn0, gconv0, lru0, lconv0, pool0 = states
    blk0 = z_row0 // rt_
    const2 = lambda bi, n: (0, 0)
    per_b3 = lambda bi, n: (bi, 0, 0)
    per_b4 = lambda bi, n: (bi, 0, 0, 0)
    st3 = lambda bi, n: (layer, bi, 0, 0)
    st4 = lambda bi, n: (layer, bi, 0, 0, 0)
    in_specs = [
        pl.BlockSpec((rt_, NZ), lambda bi, n: (blk0 + bi * nt + n, 0)),
        pl.BlockSpec((rt_, g), lambda bi, n: (n, 0)),
        pl.BlockSpec((rt_, g), lambda bi, n: (n, 0)),
        pl.BlockSpec((N_HEADS, rt_, rt_), lambda bi, n: (0, 0, 0)),
        pl.BlockSpec((rt_, g), const2),
        pl.BlockSpec((rt_, g), const2),
        pl.BlockSpec((8, LANES), const2),
        pl.BlockSpec((16, g), const2),
        pl.BlockSpec((CONV_W, 3 * g), const2),
        pl.BlockSpec((8, LANES), const2),
        pl.BlockSpec((g, g), const2),
        pl.BlockSpec((g, g), const2),
        pl.BlockSpec((g, g), const2),
        pl.BlockSpec((None, nb, N_HEADS, HEAD_DIM, HEAD_DIM), st4),
        pl.BlockSpec((None, nb, N_HEADS, HEAD_DIM, HEAD_DIM), st4),
        pl.BlockSpec((None, nb, CONV_W - 1, 3 * g), st3),
        pl.BlockSpec((None, nb, 1, g), st3),
        pl.BlockSpec((None, nb, CONV_W - 1, g), st3),
        pl.BlockSpec((None, nb, POOL_BUF, g), st3),
    ]
    st_shapes = [(N_HEADS, HEAD_DIM, HEAD_DIM), (N_HEADS, HEAD_DIM, HEAD_DIM), (CONV_W - 1, 3 * g), (1, g),
                 (CONV_W - 1, g), (POOL_BUF, g)]
    n_prev = 0 if prev is None else prev[0].shape[0]
    stack_spec = lambda layers, s: pl.BlockSpec((layers, nb) + s, lambda bi, n: (0, bi) + (0,) * len(s))
    n_in = len(in_specs)
    if n_prev:
        in_specs = in_specs + [stack_spec(n_prev, s) for s in st_shapes]
    out_specs = [pl.BlockSpec((rt_, 4 * g), lambda bi, n: (bi * nt + n, 0))] + [
        stack_spec(n_prev + 1, s) for s in st_shapes]
    out_shape = [jax.ShapeDtypeStruct((b * l, 4 * g), F32)] + [
        jax.ShapeDtypeStruct((n_prev + 1, b) + s, F32) for s in st_shapes]
    ext = [pltpu.VMEM((nb, ts + HIST, 3 * g), F32), pltpu.VMEM((nb, ts + HIST, g), F32),
           pltpu.VMEM((nb, ts + HIST, g), F32)]
    if nb > 1:
        body = functools.partial(_decode_mixer_kernel, ts, nb, pos_base)
        scratch = ext
    else:
        body = functools.partial(_mixer_kernel, nt, ts, nb, pos_base)
        scratch = [pltpu.VMEM((nb, N_HEADS, HEAD_DIM, HEAD_DIM), F32),
                   pltpu.VMEM((nb, N_HEADS, HEAD_DIM, HEAD_DIM), F32)] + ext + [pltpu.VMEM((nb, 1, g), F32)]
    outs = pl.pallas_call(
        functools.partial(_with_state_stack, body, nt, n_prev, n_in),
        grid=(b // nb, nt),
        in_specs=in_specs,
        out_specs=out_specs,
        out_shape=out_shape,
        scratch_shapes=scratch,
        compiler_params=pltpu.CompilerParams(dimension_semantics=("arbitrary", "arbitrary"),
                                             vmem_limit_bytes=VMEM_LIMIT),
        name=f"mixer_ts{ts}",
    )(z, cos_t, sin_t, intra, qdec, kdec, sdec, lp["pvec"], lp["gdn_conv_w"], lp["gabp"],
      lp["lru_wa"], lp["lru_wx"], lp["pool_w"], s_ret0, s_gdn0, gconv0, lru0[:, :, None, :], lconv0, pool0,
      *(prev or ()))
    return outs[0], tuple(outs[1:])


def _to_row_tiles(ref, x):
    m = x.shape[0]
    ref[...] = x.reshape(m // SUBLANES, SUBLANES, SUBLANES, LANES)


def _from_row_tiles(ref):
    return ref[...].reshape(ref.shape[0] * SUBLANES, SUBLANES * LANES)


def _row_tile_shape(m):
    return (m // SUBLANES, SUBLANES, SUBLANES, LANES)


def _wout_route_kernel(na, ma_ref, mb_ref, xa_ref, xb_ref, wo_ref, g_ref, wr_ref, br_ref, x1_ref, h2_ref, rt_ref,
                       cnt_ref):
    mix = _pair_tile(na, ma_ref, mb_ref)
    x1 = _pair_tile(na, xa_ref, xb_ref) + jnp.dot(mix.astype(BF16), wo_ref[...], preferred_element_type=F32)
    x1_ref[...] = x1
    h = x1 * lax.rsqrt(jnp.mean(x1 * x1, axis=-1, keepdims=True) + EPS) * g_ref[...]
    _to_row_tiles(h2_ref, h)
    logits = _mm(h, wr_ref[...], passes=3) + br_ref[...]
    lane = _iota(logits.shape, 1)
    lane_f = lane.astype(F32)
    neg = MASKED_LOGIT
    far = NO_LANE
    is_g = lane < N_GROUPS
    gl = jnp.where(is_g, logits, neg)
    gmax = jnp.max(gl, axis=-1, keepdims=True)
    gsel = jnp.min(jnp.where(gl == gmax, lane_f, far), axis=-1, keepdims=True)
    p_g = 1.0 / jnp.sum(jnp.where(is_g, jnp.exp(gl - gmax), 0.0), axis=-1, keepdims=True)
    e_group = ((lane - N_GROUPS) >> GROUP_SHIFT).astype(F32)
    in_group = jnp.where(lane >= N_GROUPS, e_group, -1.0) == gsel
    el = jnp.where(in_group, logits, neg)
    m1 = jnp.max(el, axis=-1, keepdims=True)
    i1 = jnp.min(jnp.where(el == m1, lane_f, far), axis=-1, keepdims=True)
    el2 = jnp.where(lane_f == i1, neg, el)
    m2 = jnp.max(el2, axis=-1, keepdims=True)
    i2 = jnp.min(jnp.where(el2 == m2, lane_f, far), axis=-1, keepdims=True)
    e2 = jnp.exp(m2 - m1)
    w1 = p_g / (1.0 + e2)
    w2 = p_g * e2 / (1.0 + e2)
    rt = jnp.where(lane == 0, i1 - N_GROUPS,
                   jnp.where(lane == 1, i2 - N_GROUPS, jnp.where(lane == 2, w1, jnp.where(lane == 3, w2, 0.0))))
    rt_ref[...] = rt

    picked = jnp.where((lane_f + N_GROUPS == i1) | (lane_f + N_GROUPS == i2), 1.0, 0.0)

    @pl.when(pl.program_id(0) == 0)
    def _():
        cnt_ref[...] = jnp.zeros(cnt_ref.shape, F32)

    cnt_ref[0:1, :] = cnt_ref[0:1, :] + jnp.sum(picked, axis=0, keepdims=True)


def _wout_route(mix_pair, x_pair, wo_bf16, g, wr, br, tm):
    d = x_pair[0].shape[1]
    na, t = x_pair[0].shape[0] // tm, x_pair[0].shape[0] + x_pair[1].shape[0]
    row = lambda i: (i, 0)
    const = lambda i: (0, 0)
    return pl.pallas_call(
        functools.partial(_wout_route_kernel, na),
        grid=(t // tm,),
        in_specs=_pair_specs(na, tm, d) + _pair_specs(na, tm, d) + [
            pl.BlockSpec((d, d), const), pl.BlockSpec((1, d), const), pl.BlockSpec((d, LANES), const),
            pl.BlockSpec((1, LANES), const)],
        out_specs=[pl.BlockSpec((tm, d), row), pl.BlockSpec(_row_tile_shape(tm), lambda i: (i, 0, 0, 0)),
                   pl.BlockSpec((tm, LANES), row), pl.BlockSpec((SUBLANES, LANES), const)],
        out_shape=[jax.ShapeDtypeStruct((t, d), F32), jax.ShapeDtypeStruct(_row_tile_shape(t), F32),
                   jax.ShapeDtypeStruct((t, LANES), F32), jax.ShapeDtypeStruct((SUBLANES, LANES), F32)],
        compiler_params=pltpu.CompilerParams(dimension_semantics=("arbitrary",), vmem_limit_bytes=VMEM_LIMIT),
        name="wout_route",
    )(*mix_pair, *x_pair, wo_bf16, g, wr, br)


def _tile_of_row(ref4, grp, k):
    return ref4.at[grp, pl.ds(k, 1)]


def _for_each_row(tm, fn):
    def body(grp, c):
        for k in range(SUBLANES):
            fn(grp, k, grp * SUBLANES + k)
        return c

    lax.fori_loop(0, tm // SUBLANES, body, 0)


def _route_plan_kernel(tr, tm, n_tiles, cnt_ref, rt_ref, pos_ref, tmeta_ref, emeta_ref, start_s, carry_s):
    i = pl.program_id(0)
    rt = rt_ref[...]
    lane = _iota((tr, LANES), 1)
    lane_f = lane.astype(F32)
    oh1 = jnp.where(lane_f == rt[:, 0:1], 1.0, 0.0)
    oh2 = jnp.where(lane_f == rt[:, 1:2], 1.0, 0.0)
    oh = oh1 + oh2

    @pl.when(i == 0)
    def _():
        cnt = cnt_ref[0:1, :]
        ntile = jnp.floor((cnt + (tm - 1.0)) * (1.0 / tm))
        before = jnp.where(_iota((LANES, LANES), 0) < _iota((LANES, LANES), 1), 1.0, 0.0)
        st_tiles = _mm(jnp.broadcast_to(ntile, (SUBLANES, LANES)), before)[0:1]
        end_tiles = st_tiles + ntile
        start_s[...] = st_tiles * tm
        carry_s[...] = jnp.zeros((1, LANES), F32)
        is_e1 = _iota((1, LANES), 1) < N_EXPERTS
        n_used = jnp.sum(jnp.where(is_e1, ntile, 0.0), axis=-1, keepdims=True)
        last = jnp.sum(jnp.where(is_e1 & (end_tiles <= n_used - 1.0), 1.0, 0.0), axis=-1, keepdims=True)
        ti = _iota((n_tiles, LANES), 0).astype(F32)
        lane_t = _iota((n_tiles, LANES), 1)
        te = jnp.sum(jnp.where((lane_t < N_EXPERTS) & (end_tiles <= ti), 1.0, 0.0), axis=-1, keepdims=True)
        te = jnp.minimum(te, last)
        tmeta_ref[...] = jnp.where(lane_t == 0, te, jnp.where(lane_t == 1, n_used, 0.0))
        srow = _iota((SUBLANES, LANES), 0)
        emeta_ref[...] = jnp.where(srow == 0, ntile, jnp.where(srow == 1, (end_tiles - 1.0) * tm, 0.0))

    earlier = jnp.where(_iota((tr, tr), 0) > _iota((tr, tr), 1), 1.0, 0.0)
    base = start_s[...] + carry_s[...] + _mm(earlier, oh)
    p1 = jnp.sum(oh1 * base, axis=-1, keepdims=True)
    p2 = jnp.sum(oh2 * base, axis=-1, keepdims=True)
    pos_ref[...] = jnp.where(lane == 0, p1, jnp.where(lane == 1, p2, 0.0))
    carry_s[...] = carry_s[...] + jnp.sum(oh, axis=0, keepdims=True)


def _route_plan(counts, rt, tm, n_tiles):
    t = rt.shape[0]
    tr = next(c for c in ROUTE_PLAN_STEPS + (tm,) if t % c == 0)
    const = lambda i: (0, 0)
    pos, tmeta, emeta = pl.pallas_call(
        functools.partial(_route_plan_kernel, tr, tm, n_tiles),
        grid=(t // tr,),
        in_specs=[pl.BlockSpec((SUBLANES, LANES), const), pl.BlockSpec((tr, LANES), lambda i: (i, 0))],
        out_specs=[pl.BlockSpec((tr, LANES), lambda i: (i, 0)), pl.BlockSpec((n_tiles, LANES), const),
                   pl.BlockSpec((SUBLANES, LANES), const)],
        out_shape=[jax.ShapeDtypeStruct((t, LANES), F32), jax.ShapeDtypeStruct((n_tiles, LANES), F32),
                   jax.ShapeDtypeStruct((SUBLANES, LANES), F32)],
        scratch_shapes=[pltpu.VMEM((1, LANES), F32), pltpu.VMEM((1, LANES), F32)],
        compiler_params=pltpu.CompilerParams(dimension_semantics=("arbitrary",), vmem_limit_bytes=VMEM_LIMIT),
        name="route_plan",
    )(counts, rt)
    pos1 = pos[:, 0].astype(jnp.int32)
    pos2 = pos[:, 1].astype(jnp.int32)
    tile_expert = tmeta[:, 0].astype(jnp.int32)
    n_used = tmeta[0:1, 1].astype(jnp.int32)
    has_rows = (emeta[0, :N_EXPERTS] > 0).astype(jnp.int32)
    last_tile_row = jnp.maximum(emeta[1, :N_EXPERTS], 0.0).astype(jnp.int32)
    return pos1, pos2, tile_expert, n_used, has_rows, last_tile_row


def _scatter_kernel(tm, n_steps, n_tiles, min_tiles, zon_ref, zrow_ref, nu_ref, p1_ref, p2_ref, p1p_ref, p2p_ref,
                    h_ref, xs_hbm, zbuf, stage, sem1, sem2, zsem, tsem):
    i = pl.program_id(0)

    @pl.when(i == 0)
    def _():
        zbuf[...] = jnp.zeros(zbuf.shape, F32)

        def zero_copy(e):
            return pltpu.make_async_copy(zbuf, xs_hbm.at[pl.ds(zrow_ref[e], tm)], zsem.at[e])

        def tail_copy(k):
            return pltpu.make_async_copy(zbuf, xs_hbm.at[pl.ds((min_tiles + k) * tm, tm)], tsem.at[k])

        def zstart(e, c):
            @pl.when(zon_ref[e] > 0)
            def _():
                zero_copy(e).start()
            return c

        def zwait(e, c):
            @pl.when(zon_ref[e] > 0)
            def _():
                zero_copy(e).wait()
            return c

        def tstart(k, c):
            @pl.when(min_tiles + k >= nu_ref[0])
            def _():
                tail_copy(k).start()
            return c

        def twait(k, c):
            @pl.when(min_tiles + k >= nu_ref[0])
            def _():
                tail_copy(k).wait()
            return c

        lax.fori_loop(0, N_EXPERTS, zstart, 0)
        lax.fori_loop(0, n_tiles - min_tiles, tstart, 0)
        lax.fori_loop(0, N_EXPERTS, zwait, 0)
        lax.fori_loop(0, n_tiles - min_tiles, twait, 0)

    slot = i % 2
    stage[slot] = h_ref[...]

    def copies(pa_ref, pb_ref, s, grp, k, r):
        src = _tile_of_row(stage.at[s], grp, k)
        return (pltpu.make_async_copy(src, xs_hbm.at[pl.ds(pa_ref[0, 0, r], 1)], sem1.at[s]),
                pltpu.make_async_copy(src, xs_hbm.at[pl.ds(pb_ref[0, 0, r], 1)], sem2.at[s]))

    def start(grp, k, r):
        c1, c2 = copies(p1_ref, p2_ref, slot, grp, k, r)
        c1.start(priority=0)
        c2.start(priority=1)

    def wait_of(pa_ref, pb_ref, s):
        def wait(grp, k, r):
            c1, c2 = copies(pa_ref, pb_ref, s, grp, k, r)
            c1.wait()
            c2.wait()
        return wait

    _for_each_row(tm, start)

    @pl.when(i >= 1)
    def _():
        _for_each_row(tm, wait_of(p1p_ref, p2p_ref, 1 - slot))

    @pl.when(i == n_steps - 1)
    def _():
        _for_each_row(tm, wait_of(p1_ref, p2_ref, slot))


def _scatter_rows(h2, pos1, pos2, has_rows, last_tile_row, n_used, n_tiles, tm):
    t = h2.shape[0] * SUBLANES
    nb = t // tm
    min_tiles = (2 * t) // tm
    smem_blk = pl.BlockSpec((1, 1, tm), lambda i, zon, zrow, nu: (i, 0, 0), memory_space=pltpu.SMEM)
    smem_prev = pl.BlockSpec((1, 1, tm), lambda i, zon, zrow, nu: (jnp.maximum(i - 1, 0), 0, 0),
                             memory_space=pltpu.SMEM)
    p1, p2 = pos1.reshape(nb, 1, tm), pos2.reshape(nb, 1, tm)
    grid_spec = pltpu.PrefetchScalarGridSpec(
        num_scalar_prefetch=3,
        grid=(nb,),
        in_specs=[smem_blk, smem_blk, smem_prev, smem_prev,
                  pl.BlockSpec(_row_tile_shape(tm), lambda i, zon, zrow, nu: (i, 0, 0, 0))],
        out_specs=pl.BlockSpec(memory_space=pl.ANY),
        scratch_shapes=[pltpu.VMEM((tm, SUBLANES, LANES), F32), pltpu.VMEM((2,) + _row_tile_shape(tm), F32),
                        pltpu.SemaphoreType.DMA((2,)), pltpu.SemaphoreType.DMA((2,)),
                        pltpu.SemaphoreType.DMA((N_EXPERTS,)), pltpu.SemaphoreType.DMA((n_tiles - min_tiles,))],
    )
    return pl.pallas_call(
        functools.partial(_scatter_kernel, tm, nb, n_tiles, min_tiles),
        grid_spec=grid_spec,
        out_shape=jax.ShapeDtypeStruct((n_tiles * tm, SUBLANES, LANES), F32),
        compiler_params=pltpu.CompilerParams(dimension_semantics=("arbitrary",), vmem_limit_bytes=VMEM_LIMIT),
        name="moe_scatter",
    )(has_rows, last_tile_row, n_used, p1, p2, p1, p2, h2)


def _ffn_kernel(layer, te_ref, nu_ref, slot_ref, nxt_ref, xs_ref, wg_hbm, wu_hbm, wd_hbm, ys_ref,
                wg_s, wu_s, wd_s, stage_g, stage_u, stage_d, wsem):
    i = pl.program_id(0)
    used = i < nu_ref[0]
    new_expert = (i == 0) | (te_ref[i] != te_ref[jnp.maximum(i - 1, 0)])

    def weight_copies(e, s):
        return (pltpu.make_async_copy(wg_hbm.at[layer, e], stage_g.at[s], wsem.at[s, 0]),
                pltpu.make_async_copy(wu_hbm.at[layer, e], stage_u.at[s], wsem.at[s, 1]),
                pltpu.make_async_copy(wd_hbm.at[layer, e], stage_d.at[s], wsem.at[s, 2]))

    @pl.when(i == 0)
    def _():
        for c in weight_copies(te_ref[0], slot_ref[0]):
            c.start()

    @pl.when(used & new_expert)
    def _():
        s = slot_ref[i]
        for c in weight_copies(te_ref[i], s):
            c.wait()

        @pl.when(nxt_ref[i] >= 0)
        def _():
            for c in weight_copies(nxt_ref[i], 1 - s):
                c.start()

        wg_s[...] = stage_g[s].astype(BF16)
        wu_s[...] = stage_u[s].astype(BF16)
        wd_s[...] = stage_d[s].astype(BF16)

    @pl.when(used)
    def _():
        x = _from_row_tiles(xs_ref).astype(BF16)
        gate = jnp.dot(x, wg_s[...], preferred_element_type=F32)
        up = jnp.dot(x, wu_s[...], preferred_element_type=F32)
        act = (_silu(gate) * up).astype(BF16)
        _to_row_tiles(ys_ref, jnp.dot(act, wd_s[...], preferred_element_type=F32))

    @pl.when(jnp.logical_not(used))
    def _():
        ys_ref[...] = jnp.zeros(ys_ref.shape, F32)


def _expert_ffn(xs, tile_expert, n_used, w_gate, w_up, w_down, layer, tm):
    p = xs.shape[0]
    n_tiles = p // tm
    xs = xs.reshape(_row_tile_shape(p))
    _, _, d, de = w_gate.shape
    idx = jnp.arange(n_tiles, dtype=jnp.int32)
    first = jnp.concatenate([jnp.ones((1,), bool), tile_expert[1:] != tile_expert[:-1]]) & (idx < n_used[0])
    slot = ((jnp.cumsum(first.astype(jnp.int32)) - 1) & 1).astype(jnp.int32)
    first_at_or_after = jnp.flip(lax.cummin(jnp.flip(jnp.where(first, idx, n_tiles))))
    nxt_pos = jnp.concatenate([first_at_or_after[1:], jnp.full((1,), n_tiles, jnp.int32)])
    nxt = jnp.where(nxt_pos < n_tiles, tile_expert[jnp.minimum(nxt_pos, n_tiles - 1)], -1).astype(jnp.int32)
    any_spec = pl.BlockSpec(memory_space=pl.ANY)
    grid_spec = pltpu.PrefetchScalarGridSpec(
        num_scalar_prefetch=4,
        grid=(n_tiles,),
        in_specs=[pl.BlockSpec(_row_tile_shape(tm), lambda i, te, nu, sl, nx: (jnp.minimum(i, nu[0] - 1), 0, 0, 0)),
                  any_spec, any_spec, any_spec],
        out_specs=pl.BlockSpec(_row_tile_shape(tm), lambda i, te, nu, sl, nx: (i, 0, 0, 0)),
        scratch_shapes=[pltpu.VMEM((d, de), BF16), pltpu.VMEM((d, de), BF16), pltpu.VMEM((de, d), BF16),
                        pltpu.VMEM((2, d, de), F32), pltpu.VMEM((2, d, de), F32), pltpu.VMEM((2, de, d), F32),
                        pltpu.SemaphoreType.DMA((2, 3))],
    )
    return pl.pallas_call(
        functools.partial(_ffn_kernel, layer),
        grid_spec=grid_spec,
        out_shape=jax.ShapeDtypeStruct(_row_tile_shape(p), F32),
        compiler_params=pltpu.CompilerParams(dimension_semantics=("arbitrary",), vmem_limit_bytes=VMEM_LIMIT),
        name="expert_ffn",
    )(tile_expert, n_used, slot, nxt, xs, w_gate, w_up, w_down)


def _combine_kernel(tm, n_steps, na, last_layer, p1_ref, p2_ref, p1n_ref, p2n_ref, x1_ref, rt_ref, g_ref, *rest):
    if last_layer:
        ys_hbm, outa_ref, outb_ref, buf1, buf2, sem1, sem2 = rest
    else:
        w_ref, ys_hbm, outa_ref, outb_ref, z_ref, buf1, buf2, sem1, sem2 = rest
    i = pl.program_id(0)
    slot = i % 2

    def copies(pa_ref, pb_ref, s, grp, k, r):
        return (pltpu.make_async_copy(ys_hbm.at[pl.ds(pa_ref[0, 0, r], 1)], _tile_of_row(buf1.at[s], grp, k),
                                      sem1.at[s]),
                pltpu.make_async_copy(ys_hbm.at[pl.ds(pb_ref[0, 0, r], 1)], _tile_of_row(buf2.at[s], grp, k),
                                      sem2.at[s]))

    def start(pa_ref, pb_ref, s):
        def go(grp, k, r):
            c1, c2 = copies(pa_ref, pb_ref, s, grp, k, r)
            c1.start(priority=0)
            c2.start(priority=1)
        _for_each_row(tm, go)

    @pl.when(i == 0)
    def _():
        start(p1_ref, p2_ref, 0)

    @pl.when(i + 1 < n_steps)
    def _():
        start(p1n_ref, p2n_ref, 1 - slot)

    def wait(grp, k, r):
        c1, c2 = copies(p1_ref, p2_ref, slot, grp, k, r)
        c1.wait()
        c2.wait()

    _for_each_row(tm, wait)
    rt = rt_ref[...]
    x = x1_ref[...] + rt[:, 2:3] * _from_row_tiles(buf1.at[slot]) + rt[:, 3:4] * _from_row_tiles(buf2.at[slot])
    h = x * lax.rsqrt(jnp.mean(x * x, axis=-1, keepdims=True) + EPS) * g_ref[...]
    if last_layer:
        x = h
    else:
        z_ref[...] = jnp.dot(h.astype(BF16), w_ref[...], preferred_element_type=F32)

    @pl.when(pl.program_id(0) < na)
    def _():
        outa_ref[...] = x

    @pl.when(pl.program_id(0) >= na)
    def _():
        outb_ref[...] = x


def _combine(x1, rt, ys, pos1, pos2, gain, next_w_in, ta, tm):
    t, d = x1.shape
    nb = t // tm
    na = ta // tm
    last_layer = next_w_in is None
    row = lambda i: (i, 0)
    const = lambda i: (0, 0)
    smem_blk = pl.BlockSpec((1, 1, tm), lambda i: (i, 0, 0), memory_space=pltpu.SMEM)
    smem_next = pl.BlockSpec((1, 1, tm), lambda i: (jnp.minimum(i + 1, nb - 1), 0, 0), memory_space=pltpu.SMEM)
    p1, p2 = pos1.reshape(nb, 1, tm), pos2.reshape(nb, 1, tm)
    in_specs = [smem_blk, smem_blk, smem_next, smem_next, pl.BlockSpec((tm, d), row),
                pl.BlockSpec((tm, LANES), row), pl.BlockSpec((1, d), const)]
    operands = [p1, p2, p1, p2, x1, rt, gain]
    out_specs = [pl.BlockSpec((tm, d), lambda i: (jnp.minimum(i, na - 1), 0)),
                 pl.BlockSpec((tm, d), lambda i: (jnp.maximum(i - na, 0), 0))]
    out_shape = [jax.ShapeDtypeStruct((ta, d), F32), jax.ShapeDtypeStruct((t - ta, d), F32)]
    if not last_layer:
        nz = next_w_in.shape[1]
        in_specs.append(pl.BlockSpec((d, nz), const))
        operands.append(next_w_in)
        out_specs.append(pl.BlockSpec((tm, nz), row))
        out_shape.append(jax.ShapeDtypeStruct((t, nz), F32))
    outs = pl.pallas_call(
        functools.partial(_combine_kernel, tm, nb, na, last_layer),
        grid=(nb,),
        in_specs=in_specs + [pl.BlockSpec(memory_space=pl.ANY)],
        out_specs=out_specs,
        out_shape=out_shape,
        scratch_shapes=[pltpu.VMEM((2,) + _row_tile_shape(tm), F32), pltpu.VMEM((2,) + _row_tile_shape(tm), F32),
                        pltpu.SemaphoreType.DMA((2,)), pltpu.SemaphoreType.DMA((2,))],
        compiler_params=pltpu.CompilerParams(dimension_semantics=("arbitrary",), vmem_limit_bytes=VMEM_LIMIT),
        name="moe_combine",
    )(*operands, ys.reshape(-1, SUBLANES, LANES))
    return (outs[0], outs[1]), (None if last_layer else outs[2])


def _block_diag(w):
    h, a, b = w.shape
    return jnp.einsum("hij,hk->hikj", w, jnp.eye(h, dtype=w.dtype)).reshape(h * a, h * b)


def _layer_params(l, p):
    g = GROUP_W
    w_in = p["w_in"][l]
    d = w_in.shape[0]
    w_perm = jnp.concatenate([w_in[:, :8 * g], w_in[:, 8 * g + 8:], w_in[:, 8 * g:8 * g + 8],
                              jnp.zeros((d, LANES - 8), w_in.dtype)], axis=1)
    rows = [p["ret_gn_g"][l], jnp.tile(p["gdn_norm_g"][l], N_HEADS), p["lru_conv_b"][l], p["lru_ba"][l],
            p["lru_bx"][l], p["lru_lambda"][l], p["pool_b"][l], p["pool_scale"][l]]
    pvec = jnp.concatenate([jnp.stack(rows), p["lru_conv_w"][l], jnp.zeros((4, g), F32)], axis=0)
    lane_pad = lambda v, n: jnp.concatenate([v, jnp.zeros(v.shape[:-1] + (LANES - n,), F32)], axis=-1)
    gabp = jnp.concatenate([lane_pad(jnp.stack([p["gdn_a_log"][l], p["gdn_dt_bias"][l]]), N_HEADS),
                            jnp.zeros((6, LANES), F32)], axis=0)
    n_route = N_GROUPS + N_EXPERTS
    wr = lane_pad(jnp.concatenate([p["router_group_w"][l],
                                   jnp.transpose(p["router_expert_w"][l], (1, 0, 2)).reshape(d, N_EXPERTS)],
                                  axis=1), n_route)
    br = lane_pad(jnp.concatenate([p["router_group_b"][l], p["router_expert_b"][l].reshape(N_EXPERTS)])[None, :],
                  n_route)
    return dict(
        norm1_g=p["norm1_g"][l][None, :], w_in=w_perm.astype(BF16), pvec=pvec, gdn_conv_w=p["gdn_conv_w"][l],
        gabp=gabp, lru_wa=_block_diag(p["lru_wa"][l]).astype(BF16), lru_wx=_block_diag(p["lru_wx"][l]).astype(BF16),
        pool_w=_block_diag(p["pool_w"][l]).astype(BF16), w_out=p["w_out"][l].astype(BF16),
        norm2_g=p["norm2_g"][l][None, :], wr=wr, br=br)


def kernel(x_prompt, x_sample, state_ret, state_gdn, state_gdn_conv, state_lru, state_lru_conv, state_pool,
           norm1_g, w_in, ret_gn_g, gdn_conv_w, gdn_a_log, gdn_dt_bias, gdn_norm_g,
           lru_conv_w, lru_conv_b, lru_wa, lru_ba, lru_wx, lru_bx, lru_lambda,
           pool_w, pool_b, pool_scale, w_out, norm2_g,
           router_group_w, router_group_b, router_expert_w, router_expert_b,
           moe_w_gate, moe_w_up, moe_w_down, final_g):
    params = dict(norm1_g=norm1_g, w_in=w_in, ret_gn_g=ret_gn_g, gdn_conv_w=gdn_conv_w, gdn_a_log=gdn_a_log,
                  gdn_dt_bias=gdn_dt_bias, gdn_norm_g=gdn_norm_g, lru_conv_w=lru_conv_w, lru_conv_b=lru_conv_b,
                  lru_wa=lru_wa, lru_ba=lru_ba, lru_wx=lru_wx, lru_bx=lru_bx, lru_lambda=lru_lambda,
                  pool_w=pool_w, pool_b=pool_b, pool_scale=pool_scale, w_out=w_out, norm2_g=norm2_g,
                  router_group_w=router_group_w, router_group_b=router_group_b,
                  router_expert_w=router_expert_w, router_expert_b=router_expert_b)
    bp, lp_, d = x_prompt.shape
    bs, ls, _ = x_sample.shape
    depth = w_in.shape[0]
    tp, ts_ = bp * lp_, bs * ls
    t = tp + ts_
    tile_p = min(256, lp_)
    tm = 256
    tt = 512 if (tp % 512 == 0 and ts_ % 512 == 0) else tm
    n_tiles = (2 * t + N_EXPERTS * (tm - 1) + tm - 1) // tm

    x = (x_prompt.reshape(tp, d), x_sample.reshape(ts_, d))
    sample_states = (state_ret, state_gdn, state_gdn_conv, state_lru, state_lru_conv, state_pool)
    zero_states = tuple(jnp.zeros((1, bp) + s.shape[2:], F32) for s in sample_states)
    new_p = new_s = None
    lps = [_layer_params(l, params) for l in range(depth)]
    z = _norm_win(x, lps[0]["norm1_g"], lps[0]["w_in"], tt)
    for l in range(depth):
        lp = lps[l]
        last = l == depth - 1
        mix_p, new_p = _mixer(z, 0, bp, lp_, tile_p, 1, 0, zero_states, 0, new_p, lp)
        mix_s, new_s = _mixer(z, tp, bs, ls, ls, SAMPLE_SEQS_PER_STEP, PAST_LEN, sample_states, l, new_s, lp)
        x1, h2, rt, counts = _wout_route((mix_p, mix_s), x, lp["w_out"], lp["norm2_g"], lp["wr"], lp["br"], tt)
        pos1, pos2, tile_expert, n_used, has_rows, last_tile_row = _route_plan(counts, rt, tm, n_tiles)
        xs = _scatter_rows(h2, pos1, pos2, has_rows, last_tile_row, n_used, n_tiles, tm)
        ys = _expert_ffn(xs, tile_expert, n_used, moe_w_gate, moe_w_up, moe_w_down, l, tm)
        x, z = _combine(x1, rt, ys, pos1, pos2, final_g[None, :] if last else lps[l + 1]["norm1_g"],
                        None if last else lps[l + 1]["w_in"], tp, tm)
    y_prompt = x[0].reshape(bp, lp_, d)
    y_sample = x[1].reshape(bs, ls, d)
    unpack = lambda st: (st[0], st[1], st[2], st[3][:, :, 0, :], st[4], st[5])
    return (y_prompt, y_sample, *unpack(new_p), *unpack(new_s))
```

```python
import functools

import jax
import jax.numpy as jnp
from jax import lax
from jax.experimental import pallas as pl
from jax.experimental.pallas import tpu as pltpu

F32 = jnp.float32
BF16 = jnp.bfloat16

EPS = 1e-6
HEAD_DIM = 64
N_HEADS = 4
GROUP_W = 256
CONV_W = 4
LRU_C = 8.0
POOL_WINDOWS = (2, 4, 8, 16)
POOL_BUF = 15
ROPE_BASE = 10000.0
N_GROUPS = 4
EXPERTS_PER_GROUP = 8
N_EXPERTS = 32
PAST_LEN = 16384

LANES = 128
SUBLANES = 8
HIST = 16
Z_AB = 11 * GROUP_W
NZ = Z_AB + LANES
INV_BASE = 8
RESID_PASSES = 3
HEAD_SHIFT = 6
GROUP_SHIFT = 3
MASKED_LOGIT = -1e30
NO_LANE = 1e9
ROUTE_PLAN_STEPS = (1024, 512, 256)
SAMPLE_SEQS_PER_STEP = 16
VMEM_LIMIT = 56 * 1024 * 1024

NN = (((1,), (0,)), ((), ()))
NT = (((1,), (1,)), ((), ()))
TN = (((0,), (0,)), ((), ()))


def _mm(a, b, dims=NN, passes=1):
    ah = a.astype(BF16)
    bh = b.astype(BF16)
    dot = functools.partial(lax.dot_general, dimension_numbers=dims, preferred_element_type=F32)
    if passes == 1:
        return dot(ah, bh)
    al = (a - ah.astype(F32)).astype(BF16)
    bl = (b - bh.astype(F32)).astype(BF16)
    return dot(ah, bh) + (dot(ah, bl) + dot(al, bh))


def _sigmoid(x):
    return 1.0 / (1.0 + jnp.exp(-x))


def _silu(x):
    return x * _sigmoid(x)


def _softplus(x):
    return jnp.maximum(x, 0.0) + jnp.log1p(jnp.exp(-jnp.abs(x)))


def _gelu_tanh(x):
    return 0.5 * x * (1.0 + jnp.tanh(0.7978845608028654 * (x + 0.044715 * (x * x * x))))


def _iota(shape, axis):
    return lax.broadcasted_iota(jnp.int32, shape, axis)


def _shift_rows(x, d, fill, seg=None):
    t = _iota(x.shape, 0)
    if seg is not None:
        t = t & (seg - 1)
    return jnp.where(t >= d, pltpu.roll(x, d, 0), fill)


def _cumsum_rows(x, seg=None):
    d = 1
    while d < (seg or x.shape[0]):
        x = x + _shift_rows(x, d, 0.0, seg)
        d *= 2
    return x


def _swap_half_heads(x):
    n = x.shape[1]
    lower = (_iota(x.shape, 1) & (HEAD_DIM - 1)) < HEAD_DIM // 2
    return jnp.where(lower, pltpu.roll(x, n - HEAD_DIM // 2, 1), pltpu.roll(x, HEAD_DIM // 2, 1))


def _inv_unit_lower(mats, c=None):
    shape = mats[0].shape
    c = c or shape[0]
    row = _iota(shape, 0)
    col = _iota(shape, 1)
    nb = min(INV_BASE, c)
    shift = nb.bit_length() - 1
    in_base = (row >> shift) == (col >> shift)
    ps = [jnp.where(in_base, a, 0.0) for a in mats]
    eye = jnp.where(row == col, 1.0, 0.0)
    xs = [eye - d for d in ps]
    k = 2
    while k < nb:
        ps = [_mm(p, p) for p in ps]
        xs = [x + _mm(x, p) for x, p in zip(xs, ps)]
        k *= 2
    m = nb
    while m < c:
        s1 = m.bit_length() - 1
        in_off = ((row >> (s1 + 1)) == (col >> (s1 + 1))) & ((row >> s1) != (col >> s1))
        ys = [_mm(x, jnp.where(in_off, a, 0.0)) for x, a in zip(xs, mats)]
        xs = [x - _mm(y, x) for x, y in zip(xs, ys)]
        m *= 2
    return xs


def _solve_unit_lower(mats, tinvs, ws):
    us = [_mm(t, w) for t, w in zip(tinvs, ws)]
    rs = [w - (u + _mm(a, u, passes=RESID_PASSES)) for a, u, w in zip(mats, us, ws)]
    return [u + _mm(t, r) for u, t, r in zip(us, tinvs, rs)]


def _pair_specs(na, tm, d):
    return [pl.BlockSpec((tm, d), lambda i: (jnp.minimum(i, na - 1), 0)),
            pl.BlockSpec((tm, d), lambda i: (jnp.maximum(i - na, 0), 0))]


def _pair_tile(na, a_ref, b_ref):
    return jnp.where(pl.program_id(0) < na, a_ref[...], b_ref[...])


def _norm_win_kernel(na, xa_ref, xb_ref, g_ref, w_ref, z_ref):
    x = _pair_tile(na, xa_ref, xb_ref)
    h = x * lax.rsqrt(jnp.mean(x * x, axis=-1, keepdims=True) + EPS) * g_ref[...]
    z_ref[...] = jnp.dot(h.astype(BF16), w_ref[...], preferred_element_type=F32)


def _norm_win(x_pair, g, w_bf16, tm):
    xa, xb = x_pair
    d = xa.shape[1]
    na, t = xa.shape[0] // tm, xa.shape[0] + xb.shape[0]
    nz = w_bf16.shape[1]
    return pl.pallas_call(
        functools.partial(_norm_win_kernel, na),
        grid=(t // tm,),
        in_specs=_pair_specs(na, tm, d) + [pl.BlockSpec((1, d), lambda i: (0, 0)),
                                           pl.BlockSpec((d, nz), lambda i: (0, 0))],
        out_specs=pl.BlockSpec((tm, nz), lambda i: (i, 0)),
        out_shape=jax.ShapeDtypeStruct((t, nz), F32),
        compiler_params=pltpu.CompilerParams(dimension_semantics=("arbitrary",), vmem_limit_bytes=VMEM_LIMIT),
        name="norm_win",
    )(xa, xb, g, w_bf16)


def _retention_tile(zq, zk, zv, cos, sin, intra_ref, qdec, kdec, sdec, s_ref):
    rq = zq * cos + _swap_half_heads(zq) * sin
    rk = (zk * cos + _swap_half_heads(zk) * sin) * (HEAD_DIM ** -0.5)
    kd = rk * kdec
    heads = range(N_HEADS)
    sls = [slice(h * HEAD_DIM, (h + 1) * HEAD_DIM) for h in heads]
    ss = [s_ref[h] for h in heads]
    scores = [_mm(rq[:, sl], rk[:, sl], NT) * intra_ref[h] for h, sl in zip(heads, sls)]
    os_ = [_mm(sc, zv[:, sl]) + _mm(rq[:, sl], s) * qdec[:, sl] for sc, s, sl in zip(scores, ss, sls)]
    s_new = [s * sdec[h:h + 1, :HEAD_DIM] + _mm(kd[:, sl], zv[:, sl], TN) for h, s, sl in zip(heads, ss, sls)]
    for h in heads:
        s_ref[h] = s_new[h]
    outs = []
    for o in os_:
        mu = jnp.mean(o, axis=-1, keepdims=True)
        oc = o - mu
        var = jnp.mean(oc * oc, axis=-1, keepdims=True)
        outs.append(oc * lax.rsqrt(var + EPS))
    return jnp.concatenate(outs, axis=-1)


def _split3(x):
    x0 = x.astype(BF16)
    r1 = x - x0.astype(F32)
    x1 = r1.astype(BF16)
    x2 = (r1 - x1.astype(F32)).astype(BF16)
    return x0, x1, x2


def _gdn_tile(gq, gk, gv, gab, s_ref):
    c = gq.shape[0]
    gc = _cumsum_rows(gab)
    egc = jnp.exp(gc)
    row = _iota((c, c), 0)
    col = _iota((c, c), 1)
    incl = row >= col
    strict = row > col
    heads = range(N_HEADS)
    sls = [slice(h * HEAD_DIM, (h + 1) * HEAD_DIM) for h in heads]
    ss = [s_ref[h] for h in heads]
    gcols = [gc[:, h:h + 1] for h in heads]
    bcols = [gab[:, N_HEADS + h:N_HEADS + h + 1] for h in heads]
    ecols = [egc[:, h:h + 1] for h in heads]
    gct = gc.T
    decs = [jnp.where(incl, jnp.exp(jnp.where(incl, gcols[h] - gct[h:h + 1, :], 0.0)), 0.0) for h in heads]
    mats = [jnp.where(strict, bcols[h] * decs[h] * _mm(gk[:, sl], gk[:, sl], NT), 0.0) for h, sl in zip(heads, sls)]
    tinvs = _inv_unit_lower(mats)
    ws = [bcols[h] * (gv[:, sl] - ecols[h] * _mm(gk[:, sl], ss[h])) for h, sl in zip(heads, sls)]
    us = _solve_unit_lower(mats, tinvs, ws)
    qks = [_mm(gq[:, sl], gk[:, sl], NT) * decs[h] for h, sl in zip(heads, sls)]
    outs = [ecols[h] * _mm(gq[:, sl], ss[h]) + _mm(qks[h], us[h]) for h, sl in zip(heads, sls)]
    glasts = [gc[c - 1:c, h:h + 1] for h in heads]
    s_new = [jnp.exp(glasts[h]) * ss[h] + _mm(jnp.exp(glasts[h] - gcols[h]) * gk[:, sl], us[h], TN)
             for h, sl in zip(heads, sls)]
    for h in heads:
        s_ref[h] = s_new[h]
    return outs


def _mixer_kernel(nt, ts, nb, pos_base,
                  z_ref, cos_ref, sin_ref, intra_ref, qdec_ref, kdec_ref, sdec_ref, pv_ref, gcw_ref, gabp_ref,
                  wa_ref, wx_ref, pw_ref, sret0, sgdn0, gconv0, lru0, lconv0, pool0,
                  mix_ref, ret_o, gdn_o, gconv_o, lru_o, lconv_o, pool_o,
                  s_ret, s_gdn, ext_g, ext_l, ext_p, h_lru):
    n = pl.program_id(1)
    g = GROUP_W
    base = HIST - (CONV_W - 1)

    @pl.when(n == 0)
    def _():
        s_ret[...] = sret0[...]
        s_gdn[...] = sgdn0[...]
        ext_g[:, 0:HIST, :] = jnp.zeros((nb, HIST, 3 * g), F32)
        ext_g[:, base:HIST, :] = gconv0[...]
        ext_l[:, 0:HIST, :] = jnp.zeros((nb, HIST, g), F32)
        ext_l[:, base:HIST, :] = lconv0[...]
        ext_p[:, 0:HIST, :] = jnp.zeros((nb, HIST, g), F32)
        ext_p[:, HIST - POOL_BUF:HIST, :] = pool0[...]
        h_lru[...] = lru0[...]

    pv = pv_ref[...]
    ret_gn_g, gdn_norm_g = pv[0:1], pv[1:2]
    lru_conv_b, lru_ba, lru_bx, lru_lambda = pv[2:3], pv[3:4], pv[4:5], pv[5:6]
    pool_b, pool_scale = pv[6:7], pv[7:8]
    lru_conv_w = pv[8:12]
    gcw = gcw_ref[...]
    gabp = gabp_ref[...]

    for j in range(nb):
        rows = slice(j * ts, (j + 1) * ts)
        eg, el, ep = ext_g.at[j], ext_l.at[j], ext_p.at[j]

        ro = _retention_tile(z_ref[rows, 0:g], z_ref[rows, g:2 * g], z_ref[rows, 2 * g:3 * g], cos_ref[...],
                             sin_ref[...], intra_ref, qdec_ref[...], kdec_ref[...], sdec_ref[...], s_ret.at[j])
        mix_ref[rows, 0:g] = ro * ret_gn_g * _silu(z_ref[rows, 3 * g:4 * g])

        eg[HIST:HIST + ts, :] = z_ref[rows, 4 * g:7 * g]
        y = eg[base:base + ts, :] * gcw[0:1]
        for i in range(1, CONV_W):
            y = y + eg[base + i:base + i + ts, :] * gcw[i:i + 1]
        qkv = _silu(y)
        zab = z_ref[rows, Z_AB:Z_AB + LANES]
        lane = _iota(zab.shape, 1)
        log_a = -jnp.exp(gabp[0:1]) * _softplus(zab + gabp[1:2])
        gab = jnp.where(lane < N_HEADS, log_a, jnp.where(lane < 2 * N_HEADS, _sigmoid(zab), 0.0))
        gq_parts, gk_parts = [], []
        for h in range(N_HEADS):
            qh = qkv[:, h * HEAD_DIM:(h + 1) * HEAD_DIM]
            kh = qkv[:, g + h * HEAD_DIM:g + (h + 1) * HEAD_DIM]
            gq_parts.append(qh * lax.rsqrt(jnp.sum(qh * qh, axis=-1, keepdims=True) + EPS) * (HEAD_DIM ** -0.5))
            gk_parts.append(kh * lax.rsqrt(jnp.sum(kh * kh, axis=-1, keepdims=True) + EPS))
        go = _gdn_tile(jnp.concatenate(gq_parts, axis=-1), jnp.concatenate(gk_parts, axis=-1),
                       qkv[:, 2 * g:3 * g], gab, s_gdn.at[j])
        go = [o * lax.rsqrt(jnp.mean(o * o, axis=-1, keepdims=True) + EPS) for o in go]
        mix_ref[rows, g:2 * g] = jnp.concatenate(go, axis=-1) * gdn_norm_g * _silu(z_ref[rows, 7 * g:8 * g])

        el[HIST:HIST + ts, :] = z_ref[rows, 8 * g:9 * g]
        xc = el[base:base + ts, :] * lru_conv_w[0:1]
        for i in range(1, CONV_W):
            xc = xc + el[base + i:base + i + ts, :] * lru_conv_w[i:i + 1]
        xc = xc + lru_conv_b
        r = _sigmoid(jnp.dot(xc.astype(BF16), wa_ref[...], preferred_element_type=F32) + lru_ba)
        gate_i = _sigmoid(jnp.dot(xc.astype(BF16), wx_ref[...], preferred_element_type=F32) + lru_bx)
        log_at = -LRU_C * r * _softplus(-lru_lambda)
        av = jnp.exp(log_at)
        bv = jnp.sqrt(-jnp.tanh(log_at) * (av * av + 1.0)) * (gate_i * xc)
        bv = bv + jnp.where(_iota(bv.shape, 0) == 0, av * h_lru[j], 0.0)
        d = 1
        while d < ts:
            bv = av * _shift_rows(bv, d, 0.0) + bv
            av = av * _shift_rows(av, d, 1.0)
            d *= 2
        h_lru[j] = bv[ts - 1:ts, :]
        mix_ref[rows, 2 * g:3 * g] = bv * _gelu_tanh(z_ref[rows, 9 * g:10 * g])

        u = z_ref[rows, 10 * g:11 * g]
        ep[HIST:HIST + ts, :] = u
        e = ep[...]
        sums = []
        step = 1
        for _ in POOL_WINDOWS:
            e = e + pltpu.roll(e, step, 0)
            sums.append(e[HIST:HIST + ts, :])
            step *= 2
        pos1 = (pos_base + 1 + n * ts + _iota((ts, g), 0)).astype(F32)
        lane_g = _iota((ts, g), 1) >> HEAD_SHIFT
        pooled = sums[-1] / jnp.minimum(pos1, float(POOL_WINDOWS[-1]))
        for gi in range(len(POOL_WINDOWS) - 2, -1, -1):
            pooled = jnp.where(lane_g == gi, sums[gi] / jnp.minimum(pos1, float(POOL_WINDOWS[gi])), pooled)
        pooled = pooled - u
        yp = jnp.dot(pooled.astype(BF16), pw_ref[...], preferred_element_type=F32) + pool_b
        mix_ref[rows, 3 * g:4 * g] = yp * pool_scale

        tail_g = eg[ts:ts + HIST, :]
        tail_l = el[ts:ts + HIST, :]
        tail_p = ep[ts:ts + HIST, :]
        eg[0:HIST, :] = tail_g
        el[0:HIST, :] = tail_l
        ep[0:HIST, :] = tail_p

    @pl.when(n == nt - 1)
    def _():
        ret_o[...] = s_ret[...]
        gdn_o[...] = s_gdn[...]
        gconv_o[...] = ext_g[:, base:HIST, :]
        lconv_o[...] = ext_l[:, base:HIST, :]
        pool_o[...] = ext_p[:, HIST - POOL_BUF:HIST, :]
        lru_o[...] = h_lru[...]


def _decode_mixer_kernel(ts, nb, pos_base,
                         z_ref, cos_ref, sin_ref, intra_ref, qdec_ref, kdec_ref, sdec_ref, pv_ref, gcw_ref, gabp_ref,
                         wa_ref, wx_ref, pw_ref, sret0, sgdn0, gconv0, lru0, lconv0, pool0,
                         mix_ref, ret_o, gdn_o, gconv_o, lru_o, lconv_o, pool_o,
                         ext_g, ext_l, ext_p):
    g = GROUP_W
    r_all = nb * ts
    base = HIST - (CONV_W - 1)
    seg_shift = ts.bit_length() - 1
    row2 = _iota((r_all, r_all), 0)
    col2 = _iota((r_all, r_all), 1)
    same = (row2 >> seg_shift) == (col2 >> seg_shift)
    incl = same & (row2 >= col2)
    strict = same & (row2 > col2)
    seq_l = _iota((r_all, LANES), 0) >> seg_shift
    half_l = _iota((r_all, LANES), 1) >> HEAD_SHIFT
    exp_masks = [((seq_l >> 1) == c) & ((seq_l & 1) == half_l) for c in range(nb // 2)]

    def expand(x):
        x2 = jnp.concatenate([x, x], axis=-1)
        return jnp.concatenate([jnp.where(m, x2, 0.0) for m in exp_masks], axis=-1)

    def stacked(state_ref, h):
        return jnp.concatenate([state_ref[j, h] for j in range(nb)], axis=0)

    def unstack(out_ref, h, sv):
        for j in range(nb):
            out_ref[j, h] = sv[j * HEAD_DIM:(j + 1) * HEAD_DIM, :]

    pv = pv_ref[...]
    ret_gn_g, gdn_norm_g = pv[0:1], pv[1:2]
    lru_conv_b, lru_ba, lru_bx, lru_lambda = pv[2:3], pv[3:4], pv[4:5], pv[5:6]
    pool_b, pool_scale = pv[6:7], pv[7:8]
    lru_conv_w = pv[8:12]
    gcw = gcw_ref[...]
    gabp = gabp_ref[...]

    ext_g[:, 0:HIST, :] = jnp.zeros((nb, HIST, 3 * g), F32)
    ext_g[:, base:HIST, :] = gconv0[...]
    ext_l[:, 0:HIST, :] = jnp.zeros((nb, HIST, g), F32)
    ext_l[:, base:HIST, :] = lconv0[...]
    ext_p[:, 0:HIST, :] = jnp.zeros((nb, HIST, g), F32)
    ext_p[:, HIST - POOL_BUF:HIST, :] = pool0[...]
    y_parts, xc_parts, sum_parts, h0_parts = [], [], [[] for _ in POOL_WINDOWS], []
    for j in range(nb):
        rows = slice(j * ts, (j + 1) * ts)
        eg, el, ep = ext_g.at[j], ext_l.at[j], ext_p.at[j]
        eg[HIST:HIST + ts, :] = z_ref[rows, 4 * g:7 * g]
        el[HIST:HIST + ts, :] = z_ref[rows, 8 * g:9 * g]
        ep[HIST:HIST + ts, :] = z_ref[rows, 10 * g:11 * g]
        y = eg[base:base + ts, :] * gcw[0:1]
        xc = el[base:base + ts, :] * lru_conv_w[0:1]
        for i in range(1, CONV_W):
            y = y + eg[base + i:base + i + ts, :] * gcw[i:i + 1]
            xc = xc + el[base + i:base + i + ts, :] * lru_conv_w[i:i + 1]
        y_parts.append(y)
        xc_parts.append(xc)
        e = ep[...]
        step = 1
        for wi in range(len(POOL_WINDOWS)):
            e = e + pltpu.roll(e, step, 0)
            sum_parts[wi].append(e[HIST:HIST + ts, :])
            step *= 2
        h0_parts.append(jnp.broadcast_to(lru0[j], (ts, g)))
        gconv_o[j] = eg[ts + base:ts + HIST, :]
        lconv_o[j] = el[ts + base:ts + HIST, :]
        pool_o[j] = ep[ts + HIST - POOL_BUF:ts + HIST, :]

    zq, zk, zv = z_ref[:, 0:g], z_ref[:, g:2 * g], z_ref[:, 2 * g:3 * g]
    cos, sin = cos_ref[...], sin_ref[...]
    qdec, kdec, sdec = qdec_ref[...], kdec_ref[...], sdec_ref[...]
    rq = zq * cos + _swap_half_heads(zq) * sin
    rk = (zk * cos + _swap_half_heads(zk) * sin) * (HEAD_DIM ** -0.5)
    kd = rk * kdec
    outs = []
    for h in range(N_HEADS):
        sl = slice(h * HEAD_DIM, (h + 1) * HEAD_DIM)
        q, k, v = rq[:, sl], rk[:, sl], zv[:, sl]
        sv = stacked(sret0, h)
        scores = _mm(q, k, NT) * intra_ref[h]
        o = _mm(scores, v) + _mm(expand(q), sv) * qdec[:, sl]
        unstack(ret_o, h, sv * sdec[h:h + 1, :HEAD_DIM] + _mm(expand(kd[:, sl]), v, TN))
        mu = jnp.mean(o, axis=-1, keepdims=True)
        oc = o - mu
        var = jnp.mean(oc * oc, axis=-1, keepdims=True)
        outs.append(oc * lax.rsqrt(var + EPS))
    mix_ref[:, 0:g] = jnp.concatenate(outs, axis=-1) * ret_gn_g * _silu(z_ref[:, 3 * g:4 * g])

    qkv = _silu(jnp.concatenate(y_parts, axis=0))
    zab = z_ref[:, Z_AB:Z_AB + LANES]
    lane = _iota(zab.shape, 1)
    log_a = -jnp.exp(gabp[0:1]) * _softplus(zab + gabp[1:2])
    gab = jnp.where(lane < N_HEADS, log_a, jnp.where(lane < 2 * N_HEADS, _sigmoid(zab), 0.0))
    gc = _cumsum_rows(gab, seg=ts)
    egc = jnp.exp(gc)
    gct = gc.T
    dotnn = functools.partial(lax.dot_general, dimension_numbers=NN, preferred_element_type=F32)
    same_b = jnp.where(same, 1.0, 0.0).astype(BF16)
    a0, a1, a2 = _split3(gab)
    glast = dotnn(same_b, a0) + (dotnn(same_b, a1) + dotnn(same_b, a2))
    pick = jnp.where(((_iota((nb * HEAD_DIM, r_all), 0) >> HEAD_SHIFT)
                      == (_iota((nb * HEAD_DIM, r_all), 1) >> seg_shift))
                     & ((_iota((nb * HEAD_DIM, r_all), 1) & (ts - 1)) == 0), 1.0, 0.0).astype(BF16)
    e0, e1, e2 = _split3(jnp.exp(glast))
    eglast_s = dotnn(pick, e0) + (dotnn(pick, e1) + dotnn(pick, e2))
    heads = range(N_HEADS)
    qs, ks, vs, decs = [], [], [], []
    for h in heads:
        qh = qkv[:, h * HEAD_DIM:(h + 1) * HEAD_DIM]
        kh = qkv[:, g + h * HEAD_DIM:g + (h + 1) * HEAD_DIM]
        qs.append(qh * lax.rsqrt(jnp.sum(qh * qh, axis=-1, keepdims=True) + EPS) * (HEAD_DIM ** -0.5))
        ks.append(kh * lax.rsqrt(jnp.sum(kh * kh, axis=-1, keepdims=True) + EPS))
        vs.append(qkv[:, 2 * g + h * HEAD_DIM:2 * g + (h + 1) * HEAD_DIM])
        decs.append(jnp.where(incl, jnp.exp(jnp.where(incl, gc[:, h:h + 1] - gct[h:h + 1, :], 0.0)), 0.0))
    gcols = [gc[:, h:h + 1] for h in heads]
    bcols = [gab[:, N_HEADS + h:N_HEADS + h + 1] for h in heads]
    ecols = [egc[:, h:h + 1] for h in heads]
    svs = [stacked(sgdn0, h) for h in heads]
    mats = [jnp.where(strict, bcols[h] * decs[h] * _mm(ks[h], ks[h], NT), 0.0) for h in heads]
    tinvs = _inv_unit_lower(mats, ts)
    ws = [bcols[h] * (vs[h] - ecols[h] * _mm(expand(ks[h]), svs[h])) for h in heads]
    us = _solve_unit_lower(mats, tinvs, ws)
    os_ = [ecols[h] * _mm(expand(qs[h]), svs[h]) + _mm(_mm(qs[h], ks[h], NT) * decs[h], us[h]) for h in heads]
    for h in heads:
        kdx = jnp.exp(glast[:, h:h + 1] - gcols[h]) * ks[h]
        unstack(gdn_o, h, eglast_s[:, h:h + 1] * svs[h] + _mm(expand(kdx), us[h], TN))
    outs = [o * lax.rsqrt(jnp.mean(o * o, axis=-1, keepdims=True) + EPS) for o in os_]
    mix_ref[:, g:2 * g] = jnp.concatenate(outs, axis=-1) * gdn_norm_g * _silu(z_ref[:, 7 * g:8 * g])

    xc = jnp.concatenate(xc_parts, axis=0) + lru_conv_b
    r = _sigmoid(jnp.dot(xc.astype(BF16), wa_ref[...], preferred_element_type=F32) + lru_ba)
    gate_i = _sigmoid(jnp.dot(xc.astype(BF16), wx_ref[...], preferred_element_type=F32) + lru_bx)
    log_at = -LRU_C * r * _softplus(-lru_lambda)
    av = jnp.exp(log_at)
    bv = jnp.sqrt(-jnp.tanh(log_at) * (av * av + 1.0)) * (gate_i * xc)
    first = (_iota(bv.shape, 0) & (ts - 1)) == 0
    bv = bv + jnp.where(first, av * jnp.concatenate(h0_parts, axis=0), 0.0)
    d = 1
    while d < ts:
        bv = av * _shift_rows(bv, d, 0.0, ts) + bv
        av = av * _shift_rows(av, d, 1.0, ts)
        d *= 2
    for j in range(nb):
        lru_o[j] = bv[(j + 1) * ts - 1:(j + 1) * ts, :]
    mix_ref[:, 2 * g:3 * g] = bv * _gelu_tanh(z_ref[:, 9 * g:10 * g])

    u_in = z_ref[:, 10 * g:11 * g]
    pos1 = (pos_base + 1 + (_iota((r_all, g), 0) & (ts - 1))).astype(F32)
    lane_g = _iota((r_all, g), 1) >> HEAD_SHIFT
    sums = [jnp.concatenate(parts, axis=0) for parts in sum_parts]
    pooled = sums[-1] / jnp.minimum(pos1, float(POOL_WINDOWS[-1]))
    for gi in range(len(POOL_WINDOWS) - 2, -1, -1):
        pooled = jnp.where(lane_g == gi, sums[gi] / jnp.minimum(pos1, float(POOL_WINDOWS[gi])), pooled)
    pooled = pooled - u_in
    yp = jnp.dot(pooled.astype(BF16), pw_ref[...], preferred_element_type=F32) + pool_b
    mix_ref[:, 3 * g:4 * g] = yp * pool_scale


N_STATES = 6


def _with_state_stack(body, nt, n_prev, n_in, *refs):
    ins = refs[:n_in]
    prevs = refs[n_in:n_in + (N_STATES if n_prev else 0)]
    k = n_in + len(prevs)
    mix_ref, stacks, scratch = refs[k], refs[k + 1:k + 1 + N_STATES], refs[k + 1 + N_STATES:]

    @pl.when(pl.program_id(1) == nt - 1)
    def _():
        for o, p in zip(stacks, prevs):
            o[0:n_prev] = p[...]

    body(*ins, mix_ref, *[o.at[n_prev] for o in stacks], *scratch)


def _ret_consts(c):
    log_g = jnp.log1p(-jnp.exp2(-5.0 - jnp.arange(N_HEADS, dtype=F32)))
    idx = jnp.arange(c, dtype=F32)
    diff = idx[:, None] - idx[None, :]
    intra = jnp.where(diff >= 0, jnp.exp(log_g[:, None, None] * jnp.maximum(diff, 0.0)), 0.0)
    qdec = jnp.exp(log_g[:, None] * (idx + 1.0))
    kdec = jnp.exp(log_g[:, None] * (c - 1.0 - idx))
    sdec = jnp.exp(log_g * c)
    expand = lambda t: jnp.repeat(t.T, HEAD_DIM, axis=1)
    sdec_tab = jnp.zeros((8, LANES), F32).at[:N_HEADS, :].set(sdec[:, None])
    return intra, expand(qdec), expand(kdec), sdec_tab


def _rope_tables(pos):
    half = HEAD_DIM // 2
    inv = ROPE_BASE ** (-jnp.arange(half, dtype=F32) / half)
    ang = pos.astype(F32)[:, None] * inv[None, :]
    cos, sin = jnp.cos(ang), jnp.sin(ang)
    cos_t = jnp.tile(jnp.concatenate([cos, cos], axis=-1), (1, N_HEADS))
    sin_t = jnp.tile(jnp.concatenate([-sin, sin], axis=-1), (1, N_HEADS))
    return cos_t, sin_t


def _mixer(z, z_row0, b, l, ts, nb, pos_base, states, layer, prev, lp):
    nt = l // ts
    assert nb == 1 or (nt == 1 and nb % 2 == 0 and ts == INV_BASE)
    g = GROUP_W
    pos = pos_base + jnp.arange(l, dtype=jnp.int32)
    cos_t, sin_t = _rope_tables(pos)
    intra, qdec, kdec, sdec = _ret_consts(ts)
    rt_ = nb * ts
    if nb > 1:
        cos_t, sin_t, qdec, kdec = (jnp.tile(a, (nb, 1)) for a in (cos_t, sin_t, qdec, kdec))
        intra = jnp.einsum("ab,hij->haibj", jnp.eye(nb, dtype=F32), intra).reshape(N_HEADS, rt_, rt_)
    s_ret0, s_gdn0, gconv0, lru0, lconv0, pool0 = states
    blk0 = z_row0 // rt_
    const2 = lambda bi, n: (0, 0)
    per_b3 = lambda bi, n: (bi, 0, 0)
    per_b4 = lambda bi, n: (bi, 0, 0, 0)
    st3 = lambda bi, n: (layer, bi, 0, 0)
    st4 = lambda bi, n: (layer, bi, 0, 0, 0)
    in_specs = [
        pl.BlockSpec((rt_, NZ), lambda bi, n: (blk0 + bi * nt + n, 0)),
        pl.BlockSpec((rt_, g), lambda bi, n: (n, 0)),
        pl.BlockSpec((rt_, g), lambda bi, n: (n, 0)),
        pl.BlockSpec((N_HEADS, rt_, rt_), lambda bi, n: (0, 0, 0)),
        pl.BlockSpec((rt_, g), const2),
        pl.BlockSpec((rt_, g), const2),
        pl.BlockSpec((8, LANES), const2),
        pl.BlockSpec((16, g), const2),
        pl.BlockSpec((CONV_W, 3 * g), const2),
        pl.BlockSpec((8, LANES), const2),
        pl.BlockSpec((g, g), const2),
        pl.BlockSpec((g, g), const2),
        pl.BlockSpec((g, g), const2),
        pl.BlockSpec((None, nb, N_HEADS, HEAD_DIM, HEAD_DIM), st4),
        pl.BlockSpec((None, nb, N_HEADS, HEAD_DIM, HEAD_DIM), st4),
        pl.BlockSpec((None, nb, CONV_W - 1, 3 * g), st3),
        pl.BlockSpec((None, nb, 1, g), st3),
        pl.BlockSpec((None, nb, CONV_W - 1, g), st3),
        pl.BlockSpec((None, nb, POOL_BUF, g), st3),
    ]
    st_shapes = [(N_HEADS, HEAD_DIM, HEAD_DIM), (N_HEADS, HEAD_DIM, HEAD_DIM), (CONV_W - 1, 3 * g), (1, g),
                 (CONV_W - 1, g), (POOL_BUF, g)]
    n_prev = 0 if prev is None else prev[0].shape[0]
    stack_spec = lambda layers, s: pl.BlockSpec((layers, nb) + s, lambda bi, n: (0, bi) + (0,) * len(s))
    n_in = len(in_specs)
    if n_prev:
        in_specs = in_specs + [stack_spec(n_prev, s) for s in st_shapes]
    out_specs = [pl.BlockSpec((rt_, 4 * g), lambda bi, n: (bi * nt + n, 0))] + [
        stack_spec(n_prev + 1, s) for s in st_shapes]
    out_shape = [jax.ShapeDtypeStruct((b * l, 4 * g), F32)] + [
        jax.ShapeDtypeStruct((n_prev + 1, b) + s, F32) for s in st_shapes]
    ext = [pltpu.VMEM((nb, ts + HIST, 3 * g), F32), pltpu.VMEM((nb, ts + HIST, g), F32),
           pltpu.VMEM((nb, ts + HIST, g), F32)]
    if nb > 1:
        body = functools.partial(_decode_mixer_kernel, ts, nb, pos_base)
        scratch = ext
    else:
        body = functools.partial(_mixer_kernel, nt, ts, nb, pos_base)
        scratch = [pltpu.VMEM((nb, N_HEADS, HEAD_DIM, HEAD_DIM), F32),
                   pltpu.VMEM((nb, N_HEADS, HEAD_DIM, HEAD_DIM), F32)] + ext + [pltpu.VMEM((nb, 1, g), F32)]
    outs = pl.pallas_call(
        functools.partial(_with_state_stack, body, nt, n_prev, n_in),
        grid=(b // nb, nt),
        in_specs=in_specs,
        out_specs=out_specs,
        out_shape=out_shape,
        scratch_shapes=scratch,
        compiler_params=pltpu.CompilerParams(dimension_semantics=("arbitrary", "arbitrary"),
                                             vmem_limit_bytes=VMEM_LIMIT),
        name=f"mixer_ts{ts}",
    )(z, cos_t, sin_t, intra, qdec, kdec, sdec, lp["pvec"], lp["gdn_conv_w"], lp["gabp"],
      lp["lru_wa"], lp["lru_wx"], lp["pool_w"], s_ret0, s_gdn0, gconv0, lru0[:, :, None, :], lconv0, pool0,
      *(prev or ()))
    return outs[0], tuple(outs[1:])


def _to_row_tiles(ref, x):
    m = x.shape[0]
    ref[...] = x.reshape(m // SUBLANES, SUBLANES, SUBLANES, LANES)


def _from_row_tiles(ref):
    return ref[...].reshape(ref.shape[0] * SUBLANES, SUBLANES * LANES)


def _row_tile_shape(m):
    return (m // SUBLANES, SUBLANES, SUBLANES, LANES)


def _wout_route_kernel(na, ma_ref, mb_ref, xa_ref, xb_ref, wo_ref, g_ref, wr_ref, br_ref, x1_ref, h2_ref, rt_ref,
                       cnt_ref):
    mix = _pair_tile(na, ma_ref, mb_ref)
    x1 = _pair_tile(na, xa_ref, xb_ref) + jnp.dot(mix.astype(BF16), wo_ref[...], preferred_element_type=F32)
    x1_ref[...] = x1
    h = x1 * lax.rsqrt(jnp.mean(x1 * x1, axis=-1, keepdims=True) + EPS) * g_ref[...]
    _to_row_tiles(h2_ref, h)
    logits = _mm(h, wr_ref[...], passes=3) + br_ref[...]
    lane = _iota(logits.shape, 1)
    lane_f = lane.astype(F32)
    neg = MASKED_LOGIT
    far = NO_LANE
    is_g = lane < N_GROUPS
    gl = jnp.where(is_g, logits, neg)
    gmax = jnp.max(gl, axis=-1, keepdims=True)
    gsel = jnp.min(jnp.where(gl == gmax, lane_f, far), axis=-1, keepdims=True)
    p_g = 1.0 / jnp.sum(jnp.where(is_g, jnp.exp(gl - gmax), 0.0), axis=-1, keepdims=True)
    e_group = ((lane - N_GROUPS) >> GROUP_SHIFT).astype(F32)
    in_group = jnp.where(lane >= N_GROUPS, e_group, -1.0) == gsel
    el = jnp.where(in_group, logits, neg)
    m1 = jnp.max(el, axis=-1, keepdims=True)
    i1 = jnp.min(jnp.where(el == m1, lane_f, far), axis=-1, keepdims=True)
    el2 = jnp.where(lane_f == i1, neg, el)
    m2 = jnp.max(el2, axis=-1, keepdims=True)
    i2 = jnp.min(jnp.where(el2 == m2, lane_f, far), axis=-1, keepdims=True)
    e2 = jnp.exp(m2 - m1)
    w1 = p_g / (1.0 + e2)
    w2 = p_g * e2 / (1.0 + e2)
    rt = jnp.where(lane == 0, i1 - N_GROUPS,
                   jnp.where(lane == 1, i2 - N_GROUPS, jnp.where(lane == 2, w1, jnp.where(lane == 3, w2, 0.0))))
    rt_ref[...] = rt

    picked = jnp.where((lane_f + N_GROUPS == i1) | (lane_f + N_GROUPS == i2), 1.0, 0.0)

    @pl.when(pl.program_id(0) == 0)
    def _():
        cnt_ref[...] = jnp.zeros(cnt_ref.shape, F32)

    cnt_ref[0:1, :] = cnt_ref[0:1, :] + jnp.sum(picked, axis=0, keepdims=True)


def _wout_route(mix_pair, x_pair, wo_bf16, g, wr, br, tm):
    d = x_pair[0].shape[1]
    na, t = x_pair[0].shape[0] // tm, x_pair[0].shape[0] + x_pair[1].shape[0]
    row = lambda i: (i, 0)
    const = lambda i: (0, 0)
    return pl.pallas_call(
        functools.partial(_wout_route_kernel, na),
        grid=(t // tm,),
        in_specs=_pair_specs(na, tm, d) + _pair_specs(na, tm, d) + [
            pl.BlockSpec((d, d), const), pl.BlockSpec((1, d), const), pl.BlockSpec((d, LANES), const),
            pl.BlockSpec((1, LANES), const)],
        out_specs=[pl.BlockSpec((tm, d), row), pl.BlockSpec(_row_tile_shape(tm), lambda i: (i, 0, 0, 0)),
                   pl.BlockSpec((tm, LANES), row), pl.BlockSpec((SUBLANES, LANES), const)],
        out_shape=[jax.ShapeDtypeStruct((t, d), F32), jax.ShapeDtypeStruct(_row_tile_shape(t), F32),
                   jax.ShapeDtypeStruct((t, LANES), F32), jax.ShapeDtypeStruct((SUBLANES, LANES), F32)],
        compiler_params=pltpu.CompilerParams(dimension_semantics=("arbitrary",), vmem_limit_bytes=VMEM_LIMIT),
        name="wout_route",
    )(*mix_pair, *x_pair, wo_bf16, g, wr, br)


def _tile_of_row(ref4, grp, k):
    return ref4.at[grp, pl.ds(k, 1)]


def _for_each_row(tm, fn):
    def body(grp, c):
        for k in range(SUBLANES):
            fn(grp, k, grp * SUBLANES + k)
        return c

    lax.fori_loop(0, tm // SUBLANES, body, 0)


def _route_plan_kernel(tr, tm, n_tiles, cnt_ref, rt_ref, pos_ref, tmeta_ref, emeta_ref, start_s, carry_s):
    i = pl.program_id(0)
    rt = rt_ref[...]
    lane = _iota((tr, LANES), 1)
    lane_f = lane.astype(F32)
    oh1 = jnp.where(lane_f == rt[:, 0:1], 1.0, 0.0)
    oh2 = jnp.where(lane_f == rt[:, 1:2], 1.0, 0.0)
    oh = oh1 + oh2

    @pl.when(i == 0)
    def _():
        cnt = cnt_ref[0:1, :]
        ntile = jnp.floor((cnt + (tm - 1.0)) * (1.0 / tm))
        before = jnp.where(_iota((LANES, LANES), 0) < _iota((LANES, LANES), 1), 1.0, 0.0)
        st_tiles = _mm(jnp.broadcast_to(ntile, (SUBLANES, LANES)), before)[0:1]
        end_tiles = st_tiles + ntile
        start_s[...] = st_tiles * tm
        carry_s[...] = jnp.zeros((1, LANES), F32)
        is_e1 = _iota((1, LANES), 1) < N_EXPERTS
        n_used = jnp.sum(jnp.where(is_e1, ntile, 0.0), axis=-1, keepdims=True)
        last = jnp.sum(jnp.where(is_e1 & (end_tiles <= n_used - 1.0), 1.0, 0.0), axis=-1, keepdims=True)
        ti = _iota((n_tiles, LANES), 0).astype(F32)
        lane_t = _iota((n_tiles, LANES), 1)
        te = jnp.sum(jnp.where((lane_t < N_EXPERTS) & (end_tiles <= ti), 1.0, 0.0), axis=-1, keepdims=True)
        te = jnp.minimum(te, last)
        tmeta_ref[...] = jnp.where(lane_t == 0, te, jnp.where(lane_t == 1, n_used, 0.0))
        srow = _iota((SUBLANES, LANES), 0)
        emeta_ref[...] = jnp.where(srow == 0, ntile, jnp.where(srow == 1, (end_tiles - 1.0) * tm, 0.0))

    earlier = jnp.where(_iota((tr, tr), 0) > _iota((tr, tr), 1), 1.0, 0.0)
    base = start_s[...] + carry_s[...] + _mm(earlier, oh)
    p1 = jnp.sum(oh1 * base, axis=-1, keepdims=True)
    p2 = jnp.sum(oh2 * base, axis=-1, keepdims=True)
    pos_ref[...] = jnp.where(lane == 0, p1, jnp.where(lane == 1, p2, 0.0))
    carry_s[...] = carry_s[...] + jnp.sum(oh, axis=0, keepdims=True)


def _route_plan(counts, rt, tm, n_tiles):
    t = rt.shape[0]
    tr = next(c for c in ROUTE_PLAN_STEPS + (tm,) if t % c == 0)
    const = lambda i: (0, 0)
    pos, tmeta, emeta = pl.pallas_call(
        functools.partial(_route_plan_kernel, tr, tm, n_tiles),
        grid=(t // tr,),
        in_specs=[pl.BlockSpec((SUBLANES, LANES), const), pl.BlockSpec((tr, LANES), lambda i: (i, 0))],
        out_specs=[pl.BlockSpec((tr, LANES), lambda i: (i, 0)), pl.BlockSpec((n_tiles, LANES), const),
                   pl.BlockSpec((SUBLANES, LANES), const)],
        out_shape=[jax.ShapeDtypeStruct((t, LANES), F32), jax.ShapeDtypeStruct((n_tiles, LANES), F32),
                   jax.ShapeDtypeStruct((SUBLANES, LANES), F32)],
        scratch_shapes=[pltpu.VMEM((1, LANES), F32), pltpu.VMEM((1, LANES), F32)],
        compiler_params=pltpu.CompilerParams(dimension_semantics=("arbitrary",), vmem_limit_bytes=VMEM_LIMIT),
        name="route_plan",
    )(counts, rt)
    pos1 = pos[:, 0].astype(jnp.int32)
    pos2 = pos[:, 1].astype(jnp.int32)
    tile_expert = tmeta[:, 0].astype(jnp.int32)
    n_used = tmeta[0:1, 1].astype(jnp.int32)
    has_rows = (emeta[0, :N_EXPERTS] > 0).astype(jnp.int32)
    last_tile_row = jnp.maximum(emeta[1, :N_EXPERTS], 0.0).astype(jnp.int32)
    return pos1, pos2, tile_expert, n_used, has_rows, last_tile_row


def _scatter_kernel(tm, n_tiles, min_tiles, zon_ref, zrow_ref, nu_ref, p1_ref, p2_ref, h_ref, xs_hbm,
                    zbuf, sem1, sem2, zsem, tsem):
    i = pl.program_id(0)

    @pl.when(i == 0)
    def _():
        zbuf[...] = jnp.zeros(zbuf.shape, F32)

        def zero_copy(e):
            return pltpu.make_async_copy(zbuf, xs_hbm.at[pl.ds(zrow_ref[e], tm)], zsem.at[e])

        def tail_copy(k):
            return pltpu.make_async_copy(zbuf, xs_hbm.at[pl.ds((min_tiles + k) * tm, tm)], tsem.at[k])

        def zstart(e, c):
            @pl.when(zon_ref[e] > 0)
            def _():
                zero_copy(e).start()
            return c

        def zwait(e, c):
            @pl.when(zon_ref[e] > 0)
            def _():
                zero_copy(e).wait()
            return c

        def tstart(k, c):
            @pl.when(min_tiles + k >= nu_ref[0])
            def _():
                tail_copy(k).start()
            return c

        def twait(k, c):
            @pl.when(min_tiles + k >= nu_ref[0])
            def _():
                tail_copy(k).wait()
            return c

        lax.fori_loop(0, N_EXPERTS, zstart, 0)
        lax.fori_loop(0, n_tiles - min_tiles, tstart, 0)
        lax.fori_loop(0, N_EXPERTS, zwait, 0)
        lax.fori_loop(0, n_tiles - min_tiles, twait, 0)

    def copies(grp, k, r):
        src = _tile_of_row(h_ref, grp, k)
        return (pltpu.make_async_copy(src, xs_hbm.at[pl.ds(p1_ref[0, 0, r], 1)], sem1),
                pltpu.make_async_copy(src, xs_hbm.at[pl.ds(p2_ref[0, 0, r], 1)], sem2))

    def start(grp, k, r):
        c1, c2 = copies(grp, k, r)
        c1.start(priority=0)
        c2.start(priority=1)

    def wait(grp, k, r):
        c1, c2 = copies(grp, k, r)
        c1.wait()
        c2.wait()

    _for_each_row(tm, start)
    _for_each_row(tm, wait)


def _scatter_rows(h2, pos1, pos2, has_rows, last_tile_row, n_used, n_tiles, tm):
    t = h2.shape[0] * SUBLANES
    nb = t // tm
    min_tiles = (2 * t) // tm
    smem_blk = pl.BlockSpec((1, 1, tm), lambda i, zon, zrow, nu: (i, 0, 0), memory_space=pltpu.SMEM)
    grid_spec = pltpu.PrefetchScalarGridSpec(
        num_scalar_prefetch=3,
        grid=(nb,),
        in_specs=[smem_blk, smem_blk, pl.BlockSpec(_row_tile_shape(tm), lambda i, zon, zrow, nu: (i, 0, 0, 0))],
        out_specs=pl.BlockSpec(memory_space=pl.ANY),
        scratch_shapes=[pltpu.VMEM((tm, SUBLANES, LANES), F32), pltpu.SemaphoreType.DMA,
                        pltpu.SemaphoreType.DMA, pltpu.SemaphoreType.DMA((N_EXPERTS,)),
                        pltpu.SemaphoreType.DMA((n_tiles - min_tiles,))],
    )
    return pl.pallas_call(
        functools.partial(_scatter_kernel, tm, n_tiles, min_tiles),
        grid_spec=grid_spec,
        out_shape=jax.ShapeDtypeStruct((n_tiles * tm, SUBLANES, LANES), F32),
        compiler_params=pltpu.CompilerParams(dimension_semantics=("arbitrary",), vmem_limit_bytes=VMEM_LIMIT),
        name="moe_scatter",
    )(has_rows, last_tile_row, n_used, pos1.reshape(nb, 1, tm), pos2.reshape(nb, 1, tm), h2)


def _ffn_kernel(layer, te_ref, nu_ref, slot_ref, nxt_ref, xs_ref, wg_hbm, wu_hbm, wd_hbm, ys_ref,
                wg_s, wu_s, wd_s, stage_g, stage_u, stage_d, wsem):
    i = pl.program_id(0)
    used = i < nu_ref[0]
    new_expert = (i == 0) | (te_ref[i] != te_ref[jnp.maximum(i - 1, 0)])

    def weight_copies(e, s):
        return (pltpu.make_async_copy(wg_hbm.at[layer, e], stage_g.at[s], wsem.at[s, 0]),
                pltpu.make_async_copy(wu_hbm.at[layer, e], stage_u.at[s], wsem.at[s, 1]),
                pltpu.make_async_copy(wd_hbm.at[layer, e], stage_d.at[s], wsem.at[s, 2]))

    @pl.when(i == 0)
    def _():
        for c in weight_copies(te_ref[0], slot_ref[0]):
            c.start()

    @pl.when(used & new_expert)
    def _():
        s = slot_ref[i]
        for c in weight_copies(te_ref[i], s):
            c.wait()

        @pl.when(nxt_ref[i] >= 0)
        def _():
            for c in weight_copies(nxt_ref[i], 1 - s):
                c.start()

        wg_s[...] = stage_g[s].astype(BF16)
        wu_s[...] = stage_u[s].astype(BF16)
        wd_s[...] = stage_d[s].astype(BF16)

    @pl.when(used)
    def _():
        x = _from_row_tiles(xs_ref).astype(BF16)
        gate = jnp.dot(x, wg_s[...], preferred_element_type=F32)
        up = jnp.dot(x, wu_s[...], preferred_element_type=F32)
        act = (_silu(gate) * up).astype(BF16)
        _to_row_tiles(ys_ref, jnp.dot(act, wd_s[...], preferred_element_type=F32))

    @pl.when(jnp.logical_not(used))
    def _():
        ys_ref[...] = jnp.zeros(ys_ref.shape, F32)


def _expert_ffn(xs, tile_expert, n_used, w_gate, w_up, w_down, layer, tm):
    p = xs.shape[0]
    n_tiles = p // tm
    xs = xs.reshape(_row_tile_shape(p))
    _, _, d, de = w_gate.shape
    idx = jnp.arange(n_tiles, dtype=jnp.int32)
    first = jnp.concatenate([jnp.ones((1,), bool), tile_expert[1:] != tile_expert[:-1]]) & (idx < n_used[0])
    slot = ((jnp.cumsum(first.astype(jnp.int32)) - 1) & 1).astype(jnp.int32)
    first_at_or_after = jnp.flip(lax.cummin(jnp.flip(jnp.where(first, idx, n_tiles))))
    nxt_pos = jnp.concatenate([first_at_or_after[1:], jnp.full((1,), n_tiles, jnp.int32)])
    nxt = jnp.where(nxt_pos < n_tiles, tile_expert[jnp.minimum(nxt_pos, n_tiles - 1)], -1).astype(jnp.int32)
    any_spec = pl.BlockSpec(memory_space=pl.ANY)
    grid_spec = pltpu.PrefetchScalarGridSpec(
        num_scalar_prefetch=4,
        grid=(n_tiles,),
        in_specs=[pl.BlockSpec(_row_tile_shape(tm), lambda i, te, nu, sl, nx: (jnp.minimum(i, nu[0] - 1), 0, 0, 0)),
                  any_spec, any_spec, any_spec],
        out_specs=pl.BlockSpec(_row_tile_shape(tm), lambda i, te, nu, sl, nx: (i, 0, 0, 0)),
        scratch_shapes=[pltpu.VMEM((d, de), BF16), pltpu.VMEM((d, de), BF16), pltpu.VMEM((de, d), BF16),
                        pltpu.VMEM((2, d, de), F32), pltpu.VMEM((2, d, de), F32), pltpu.VMEM((2, de, d), F32),
                        pltpu.SemaphoreType.DMA((2, 3))],
    )
    return pl.pallas_call(
        functools.partial(_ffn_kernel, layer),
        grid_spec=grid_spec,
        out_shape=jax.ShapeDtypeStruct(_row_tile_shape(p), F32),
        compiler_params=pltpu.CompilerParams(dimension_semantics=("arbitrary",), vmem_limit_bytes=VMEM_LIMIT),
        name="expert_ffn",
    )(tile_expert, n_used, slot, nxt, xs, w_gate, w_up, w_down)


def _combine_kernel(tm, n_steps, na, last_layer, p1_ref, p2_ref, p1n_ref, p2n_ref, x1_ref, rt_ref, g_ref, *rest):
    if last_layer:
        ys_hbm, outa_ref, outb_ref, buf1, buf2, sem1, sem2 = rest
    else:
        w_ref, ys_hbm, outa_ref, outb_ref, z_ref, buf1, buf2, sem1, sem2 = rest
    i = pl.program_id(0)
    slot = i % 2

    def copies(pa_ref, pb_ref, s, grp, k, r):
        return (pltpu.make_async_copy(ys_hbm.at[pl.ds(pa_ref[0, 0, r], 1)], _tile_of_row(buf1.at[s], grp, k),
                                      sem1.at[s]),
                pltpu.make_async_copy(ys_hbm.at[pl.ds(pb_ref[0, 0, r], 1)], _tile_of_row(buf2.at[s], grp, k),
                                      sem2.at[s]))

    def start(pa_ref, pb_ref, s):
        def go(grp, k, r):
            c1, c2 = copies(pa_ref, pb_ref, s, grp, k, r)
            c1.start(priority=0)
            c2.start(priority=1)
        _for_each_row(tm, go)

    @pl.when(i == 0)
    def _():
        start(p1_ref, p2_ref, 0)

    @pl.when(i + 1 < n_steps)
    def _():
        start(p1n_ref, p2n_ref, 1 - slot)

    def wait(grp, k, r):
        c1, c2 = copies(p1_ref, p2_ref, slot, grp, k, r)
        c1.wait()
        c2.wait()

    _for_each_row(tm, wait)
    rt = rt_ref[...]
    x = x1_ref[...] + rt[:, 2:3] * _from_row_tiles(buf1.at[slot]) + rt[:, 3:4] * _from_row_tiles(buf2.at[slot])
    h = x * lax.rsqrt(jnp.mean(x * x, axis=-1, keepdims=True) + EPS) * g_ref[...]
    if last_layer:
        x = h
    else:
        z_ref[...] = jnp.dot(h.astype(BF16), w_ref[...], preferred_element_type=F32)

    @pl.when(pl.program_id(0) < na)
    def _():
        outa_ref[...] = x

    @pl.when(pl.program_id(0) >= na)
    def _():
        outb_ref[...] = x


def _combine(x1, rt, ys, pos1, pos2, gain, next_w_in, ta, tm):
    t, d = x1.shape
    nb = t // tm
    na = ta // tm
    last_layer = next_w_in is None
    row = lambda i: (i, 0)
    const = lambda i: (0, 0)
    smem_blk = pl.BlockSpec((1, 1, tm), lambda i: (i, 0, 0), memory_space=pltpu.SMEM)
    smem_next = pl.BlockSpec((1, 1, tm), lambda i: (jnp.minimum(i + 1, nb - 1), 0, 0), memory_space=pltpu.SMEM)
    p1, p2 = pos1.reshape(nb, 1, tm), pos2.reshape(nb, 1, tm)
    in_specs = [smem_blk, smem_blk, smem_next, smem_next, pl.BlockSpec((tm, d), row),
                pl.BlockSpec((tm, LANES), row), pl.BlockSpec((1, d), const)]
    operands = [p1, p2, p1, p2, x1, rt, gain]
    out_specs = [pl.BlockSpec((tm, d), lambda i: (jnp.minimum(i, na - 1), 0)),
                 pl.BlockSpec((tm, d), lambda i: (jnp.maximum(i - na, 0), 0))]
    out_shape = [jax.ShapeDtypeStruct((ta, d), F32), jax.ShapeDtypeStruct((t - ta, d), F32)]
    if not last_layer:
        nz = next_w_in.shape[1]
        in_specs.append(pl.BlockSpec((d, nz), const))
        operands.append(next_w_in)
        out_specs.append(pl.BlockSpec((tm, nz), row))
        out_shape.append(jax.ShapeDtypeStruct((t, nz), F32))
    outs = pl.pallas_call(
        functools.partial(_combine_kernel, tm, nb, na, last_layer),
        grid=(nb,),
        in_specs=in_specs + [pl.BlockSpec(memory_space=pl.ANY)],
        out_specs=out_specs,
        out_shape=out_shape,
        scratch_shapes=[pltpu.VMEM((2,) + _row_tile_shape(tm), F32), pltpu.VMEM((2,) + _row_tile_shape(tm), F32),
                        pltpu.SemaphoreType.DMA((2,)), pltpu.SemaphoreType.DMA((2,))],
        compiler_params=pltpu.CompilerParams(dimension_semantics=("arbitrary",), vmem_limit_bytes=VMEM_LIMIT),
        name="moe_combine",
    )(*operands, ys.reshape(-1, SUBLANES, LANES))
    return (outs[0], outs[1]), (None if last_layer else outs[2])


def _block_diag(w):
    h, a, b = w.shape
    return jnp.einsum("hij,hk->hikj", w, jnp.eye(h, dtype=w.dtype)).reshape(h * a, h * b)


def _layer_params(l, p):
    g = GROUP_W
    w_in = p["w_in"][l]
    d = w_in.shape[0]
    w_perm = jnp.concatenate([w_in[:, :8 * g], w_in[:, 8 * g + 8:], w_in[:, 8 * g:8 * g + 8],
                              jnp.zeros((d, LANES - 8), w_in.dtype)], axis=1)
    rows = [p["ret_gn_g"][l], jnp.tile(p["gdn_norm_g"][l], N_HEADS), p["lru_conv_b"][l], p["lru_ba"][l],
            p["lru_bx"][l], p["lru_lambda"][l], p["pool_b"][l], p["pool_scale"][l]]
    pvec = jnp.concatenate([jnp.stack(rows), p["lru_conv_w"][l], jnp.zeros((4, g), F32)], axis=0)
    lane_pad = lambda v, n: jnp.concatenate([v, jnp.zeros(v.shape[:-1] + (LANES - n,), F32)], axis=-1)
    gabp = jnp.concatenate([lane_pad(jnp.stack([p["gdn_a_log"][l], p["gdn_dt_bias"][l]]), N_HEADS),
                            jnp.zeros((6, LANES), F32)], axis=0)
    n_route = N_GROUPS + N_EXPERTS
    wr = lane_pad(jnp.concatenate([p["router_group_w"][l],
                                   jnp.transpose(p["router_expert_w"][l], (1, 0, 2)).reshape(d, N_EXPERTS)],
                                  axis=1), n_route)
    br = lane_pad(jnp.concatenate([p["router_group_b"][l], p["router_expert_b"][l].reshape(N_EXPERTS)])[None, :],
                  n_route)
    return dict(
        norm1_g=p["norm1_g"][l][None, :], w_in=w_perm.astype(BF16), pvec=pvec, gdn_conv_w=p["gdn_conv_w"][l],
        gabp=gabp, lru_wa=_block_diag(p["lru_wa"][l]).astype(BF16), lru_wx=_block_diag(p["lru_wx"][l]).astype(BF16),
        pool_w=_block_diag(p["pool_w"][l]).astype(BF16), w_out=p["w_out"][l].astype(BF16),
        norm2_g=p["norm2_g"][l][None, :], wr=wr, br=br)


def kernel(x_prompt, x_sample, state_ret, state_gdn, state_gdn_conv, state_lru, state_lru_conv, state_pool,
           norm1_g, w_in, ret_gn_g, gdn_conv_w, gdn_a_log, gdn_dt_bias, gdn_norm_g,
           lru_conv_w, lru_conv_b, lru_wa, lru_ba, lru_wx, lru_bx, lru_lambda,
           pool_w, pool_b, pool_scale, w_out, norm2_g,
           router_group_w, router_group_b, router_expert_w, router_expert_b,
           moe_w_gate, moe_w_up, moe_w_down, final_g):
    params = dict(norm1_g=norm1_g, w_in=w_in, ret_gn_g=ret_gn_g, gdn_conv_w=gdn_conv_w, gdn_a_log=gdn_a_log,
                  gdn_dt_bias=gdn_dt_bias, gdn_norm_g=gdn_norm_g, lru_conv_w=lru_conv_w, lru_conv_b=lru_conv_b,
                  lru_wa=lru_wa, lru_ba=lru_ba, lru_wx=lru_wx, lru_bx=lru_bx, lru_lambda=lru_lambda,
                  pool_w=pool_w, pool_b=pool_b, pool_scale=pool_scale, w_out=w_out, norm2_g=norm2_g,
                  router_group_w=router_group_w, router_group_b=router_group_b,
                  router_expert_w=router_expert_w, router_expert_b=router_expert_b)
    bp, lp_, d = x_prompt.shape
    bs, ls, _ = x_sample.shape
    depth = w_in.shape[0]
    tp, ts_ = bp * lp_, bs * ls
    t = tp + ts_
    tile_p = min(256, lp_)
    tm = 256
    tt = 512 if (tp % 512 == 0 and ts_ % 512 == 0) else tm
    n_tiles = (2 * t + N_EXPERTS * (tm - 1) + tm - 1) // tm

    x = (x_prompt.reshape(tp, d), x_sample.reshape(ts_, d))
    sample_states = (state_ret, state_gdn, state_gdn_conv, state_lru, state_lru_conv, state_pool)
    zero_states = tuple(jnp.zeros((1, bp) + s.shape[2:], F32) for s in sample_states)
    new_p = new_s = None
    lps = [_layer_params(l, params) for l in range(depth)]
    z = _norm_win(x, lps[0]["norm1_g"], lps[0]["w_in"], tt)
    for l in range(depth):
        lp = lps[l]
        last = l == depth - 1
        mix_p, new_p = _mixer(z, 0, bp, lp_, tile_p, 1, 0, zero_states, 0, new_p, lp)
        mix_s, new_s = _mixer(z, tp, bs, ls, ls, SAMPLE_SEQS_PER_STEP, PAST_LEN, sample_states, l, new_s, lp)
        x1, h2, rt, counts = _wout_route((mix_p, mix_s), x, lp["w_out"], lp["norm2_g"], lp["wr"], lp["br"], tt)
        pos1, pos2, tile_expert, n_used, has_rows, last_tile_row = _route_plan(counts, rt, tm, n_tiles)
        xs = _scatter_rows(h2, pos1, pos2, has_rows, last_tile_row, n_used, n_tiles, tm)
        ys = _expert_ffn(xs, tile_expert, n_used, moe_w_gate, moe_w_up, moe_w_down, l, tm)
        x, z = _combine(x1, rt, ys, pos1, pos2, final_g[None, :] if last else lps[l + 1]["norm1_g"],
                        None if last else lps[l + 1]["w_in"], tp, tm)
    y_prompt = x[0].reshape(bp, lp_, d)
    y_sample = x[1].reshape(bs, ls, d)
    unpack = lambda st: (st[0], st[1], st[2], st[3][:, :, 0, :], st[4], st[5])
    return (y_prompt, y_sample, *unpack(new_p), *unpack(new_s))
```
